```python
import jax
import jax.numpy as jnp
from jax import lax
import numpy as np

D_MODEL = 1024
BATCH = 4
SEQ = 8192
DEPTH = 2

GRID_W = 64
CTX_LEN = 256
HEAD_DIM = 64
ROPE_BASE = 10000.0
EPS = 1e-6

A_HEADS = 8
A_KV_HEADS = 2
A_GROUP = A_HEADS // A_KV_HEADS
A_WINDOW = 128
A_BLOCK = A_WINDOW
B_HEADS = 8
B_DK = 64
B_DV = 64
B_CHUNK = 64
C_HEADS = 8
C_NOPE = 64
C_ROPE = 32
C_V = 64
C_Q_LORA = 256
C_KV_LORA = 128
C_QBLOCK = 128
N_BRANCH = 3
BRANCH_W = 512
N_GROUPS = 4
EXPERTS_PER_GROUP = 8
N_EXPERTS = N_GROUPS * EXPERTS_PER_GROUP
TOP_K_IN_GROUP = 2
D_EXPERT = 256

A_Q_W = A_HEADS * HEAD_DIM
A_KV_W = A_KV_HEADS * HEAD_DIM
B_W = B_HEADS * B_DK
GATE_W = N_BRANCH * D_MODEL
IN_SPLITS = (A_Q_W, A_KV_W, A_KV_W, B_W, B_W, B_W, B_W, B_W, C_Q_LORA, C_KV_LORA, C_ROPE, GATE_W)
IN_W = A_Q_W + 2 * A_KV_W + 5 * B_W + C_Q_LORA + C_KV_LORA + C_ROPE + GATE_W

kernel_name = 'hybrid_dit_trunk_swa_hgrn2_mla_hmoe'

F32 = jnp.float32


def rmsnorm(x, g):
    xf = x.astype(F32)
    y = xf * lax.rsqrt(jnp.mean(xf * xf, axis=-1, keepdims=True) + EPS)
    return (y * g.astype(F32)).astype(x.dtype)


def modulate(x, g, shift, scale):
    return rmsnorm(x, g) * (1 + scale) + shift


def axial_rope(rows, rot_dim):
    row = jnp.repeat(jnp.arange(rows, dtype=F32), GRID_W)
    col = jnp.tile(jnp.arange(GRID_W, dtype=F32), rows)
    n_freq = rot_dim // 4
    inv = ROPE_BASE ** (-jnp.arange(n_freq, dtype=F32) / n_freq)
    ang = jnp.concatenate([row[:, None] * inv, col[:, None] * inv], axis=-1)
    return jnp.cos(ang), jnp.sin(ang)


def apply_rope(x, cos, sin):
    half = x.shape[-1] // 2
    x1, x2 = x[..., :half], x[..., half:]
    c = cos[None, :, None, :].astype(x.dtype)
    s = sin[None, :, None, :].astype(x.dtype)
    return jnp.concatenate([x1 * c - x2 * s, x1 * s + x2 * c], axis=-1)


def multi_softmax(parts, sink=None):
    m = parts[0].max(-1, keepdims=True)
    for p in parts[1:]:
        m = jnp.maximum(m, p.max(-1, keepdims=True))
    if sink is not None:
        m = jnp.maximum(m, sink)
    es = [jnp.exp(p - m) for p in parts]
    denom = es[0].sum(-1, keepdims=True)
    for e in es[1:]:
        denom = denom + e.sum(-1, keepdims=True)
    if sink is not None:
        denom = denom + jnp.exp(sink - m)
    return [e / denom for e in es]


def split_in(p):
    idx = np.cumsum(IN_SPLITS)[:-1].tolist()
    return jnp.split(p, idx, axis=-1)


def window_gqa(q, k, v, qc, kc, vc, sink, cos, sin, need_ctx):
    b, s = q.shape[:2]
    nb = s // A_BLOCK
    scale = HEAD_DIM ** -0.5
    q = apply_rope(q, cos, sin)
    k = apply_rope(k, cos, sin)
    qb = q.reshape(b, nb, A_BLOCK, A_KV_HEADS, A_GROUP, HEAD_DIM)

    def band(t):
        tp = jnp.pad(t, ((0, 0), (A_BLOCK, A_BLOCK), (0, 0), (0, 0)))
        tp = tp.reshape(b, nb + 2, A_BLOCK, A_KV_HEADS, HEAD_DIM)
        return jnp.concatenate([tp[:, :-2], tp[:, 1:-1], tp[:, 2:]], axis=2)

    kb, vb = band(k), band(v)
    blk = jnp.arange(nb)[:, None, None] * A_BLOCK
    qpos = blk + jnp.arange(A_BLOCK)[None, :, None]
    kpos = blk - A_BLOCK + jnp.arange(3 * A_BLOCK)[None, None, :]
    valid = (kpos >= 0) & (kpos < s) & (jnp.abs(qpos - kpos) <= A_WINDOW)
    s_lat = jnp.einsum('bnqhgd,bnkhd->bnhgqk', qb, kb, preferred_element_type=F32) * scale
    s_lat = jnp.where(valid[None, :, None, None], s_lat, -jnp.inf)
    s_ctx = jnp.einsum('bnqhgd,blhd->bnhgql', qb, kc, preferred_element_type=F32) * scale
    snk = sink.astype(F32).reshape(1, 1, A_KV_HEADS, A_GROUP, 1, 1)
    p_lat, p_ctx = multi_softmax([s_lat, s_ctx], snk)
    o = (jnp.einsum('bnhgqk,bnkhd->bnqhgd', p_lat.astype(v.dtype), vb)
         + jnp.einsum('bnhgql,blhd->bnqhgd', p_ctx.astype(v.dtype), vc))
    o = o.reshape(b, s, A_Q_W)
    oc = None
    if need_ctx:
        lc = qc.shape[1]
        qcg = qc.reshape(b, lc, A_KV_HEADS, A_GROUP, HEAD_DIM)
        sc = jnp.einsum('blhgd,bmhd->bhglm', qcg, kc, preferred_element_type=F32) * scale
        (pc,) = multi_softmax([sc], sink.astype(F32).reshape(1, A_KV_HEADS, A_GROUP, 1, 1))
        oc = jnp.einsum('bhglm,bmhd->blhgd', pc.astype(vc.dtype), vc).reshape(b, lc, A_Q_W)
    return o, oc


def gla_chunks(q, k, v, logf, s0):
    b, t = q.shape[:2]
    nc = t // B_CHUNK

    def chunks(a):
        return a.astype(F32).reshape(b, nc, B_CHUNK, B_HEADS, a.shape[-1]).transpose(1, 0, 3, 2, 4)

    tri = jnp.tril(jnp.ones((B_CHUNK, B_CHUNK), dtype=bool))

    def step(S, inp):
        qc, kc, vc, gc = inp
        bc = jnp.cumsum(gc, axis=2)
        inter = jnp.einsum('bhtd,bhde->bhte', qc * jnp.exp(bc), S)
        diff = jnp.where(tri[:, :, None], bc[:, :, :, None, :] - bc[:, :, None, :, :], -jnp.inf)
        att = jnp.einsum('bhtsd,bhsd->bhts', qc[:, :, :, None, :] * jnp.exp(diff), kc)
        o = inter + jnp.einsum('bhts,bhse->bhte', att, vc)
        blast = bc[:, :, -1:, :]
        S = jnp.exp(blast[:, :, 0, :, None]) * S + jnp.einsum('bhsd,bhse->bhde', kc * jnp.exp(blast - bc), vc)
        return S, o

    S, o = lax.scan(step, s0, (chunks(q), chunks(k), chunks(v), chunks(logf)))
    o = o.transpose(1, 0, 3, 2, 4).reshape(b, t, B_HEADS, B_DV)
    return o, S


def hgrn2(q, i, zf, zb, g, qc, ic, zfc, zbc, gc, lb, g_onorm, need_ctx):
    def heads(a):
        return a.reshape(a.shape[0], a.shape[1], B_HEADS, B_DK)

    def forget(z, lbd):
        z = z.astype(F32)
        logf = jnp.logaddexp(jnp.log(lbd), jnp.log1p(-lbd) + jax.nn.log_sigmoid(z))
        key = (1 - lbd) * jax.nn.sigmoid(-z)
        return heads(key), heads(logf)

    def flip(a):
        return a[:, ::-1]

    b = q.shape[0]
    s0 = jnp.zeros((b, B_HEADS, B_DK, B_DV), F32)
    qh, vh, qch, vch = heads(q), heads(i), heads(qc), heads(ic)
    kfc, lfc = forget(zfc, lb[0])
    kf, lf = forget(zf, lb[0])
    oc_f, S_f = gla_chunks(qch, kfc, vch, lfc, s0)
    o_f, _ = gla_chunks(qh, kf, vh, lf, S_f)
    kbc, lbc = forget(zbc, lb[1])
    kbw, lbw = forget(zb, lb[1])
    oc_b, S_b = gla_chunks(flip(qch), flip(kbc), flip(vch), flip(lbc), s0)
    o_b, _ = gla_chunks(flip(qh), flip(kbw), flip(vh), flip(lbw), S_b)
    gn = g_onorm.reshape(B_HEADS, B_DV)
    o = rmsnorm(o_f + flip(o_b), gn).reshape(q.shape[0], q.shape[1], B_W).astype(g.dtype) * jax.nn.silu(g)
    oc = None
    if need_ctx:
        oc = rmsnorm(oc_f + flip(oc_b), gn).reshape(qc.shape[0], qc.shape[1], B_W).astype(gc.dtype) * jax.nn.silu(gc)
    return o, oc


def mla(cq, ckv, kr, cqc, ckvc, krc, g_q, g_kv, w_uq, w_ukv, cos, sin, need_ctx):
    def expand(cq_, ckv_, kr_, rope):
        b_, n_ = cq_.shape[:2]
        qh = (rmsnorm(cq_, g_q) @ w_uq).reshape(b_, n_, C_HEADS, C_NOPE + C_ROPE)
        kvh = (rmsnorm(ckv_, g_kv) @ w_ukv).reshape(b_, n_, C_HEADS, C_NOPE + C_V)
        q_nope, q_rope = qh[..., :C_NOPE], qh[..., C_NOPE:]
        k_nope, vv = kvh[..., :C_NOPE], kvh[..., C_NOPE:]
        k_rope = kr_[:, :, None, :]
        if rope:
            q_rope = apply_rope(q_rope, cos, sin)
            k_rope = apply_rope(k_rope, cos, sin)
        qf = jnp.concatenate([q_nope, q_rope], axis=-1)
        kf = jnp.concatenate([k_nope, jnp.broadcast_to(k_rope, (b_, n_, C_HEADS, C_ROPE))], axis=-1)
        return qf, kf, vv

    q, k, v = expand(cq, ckv, kr, True)
    qc, kc, vc = expand(cqc, ckvc, krc, False)
    scale = (C_NOPE + C_ROPE) ** -0.5
    b, s = q.shape[:2]
    nb = s // C_QBLOCK
    qb = q.reshape(b, nb, C_QBLOCK, C_HEADS, C_NOPE + C_ROPE).transpose(1, 0, 2, 3, 4)

    def block(qi):
        s_lat = jnp.einsum('bqhd,bkhd->bhqk', qi, k, preferred_element_type=F32) * scale
        s_ctx = jnp.einsum('bqhd,blhd->bhql', qi, kc, preferred_element_type=F32) * scale
        p_lat, p_ctx = multi_softmax([s_lat, s_ctx])
        return (jnp.einsum('bhqk,bkhd->bqhd', p_lat.astype(v.dtype), v)
                + jnp.einsum('bhql,blhd->bqhd', p_ctx.astype(vc.dtype), vc))

    o = lax.map(block, qb).transpose(1, 0, 2, 3, 4).reshape(b, s, C_HEADS * C_V)
    oc = None
    if need_ctx:
        sc = jnp.einsum('blhd,bmhd->bhlm', qc, kc, preferred_element_type=F32) * scale
        (pc,) = multi_softmax([sc])
        oc = jnp.einsum('bhlm,bmhd->blhd', pc.astype(vc.dtype), vc).reshape(qc.shape[0], qc.shape[1], C_HEADS * C_V)
    return o, oc


def merge_branches(branches, gate_logits, w_br, w_out):
    gl = gate_logits.reshape(gate_logits.shape[0], gate_logits.shape[1], N_BRANCH, D_MODEL)
    y = jax.nn.sigmoid(gl[:, :, 0]) * (branches[0] @ w_br[0])
    for n in range(1, N_BRANCH):
        y = y + jax.nn.sigmoid(gl[:, :, n]) * (branches[n] @ w_br[n])
    return y @ w_out


def hier_moe(h, w_rg, w_re, w1, w3, w2):
    shp = h.shape
    hf = h.reshape(-1, D_MODEL)
    n = hf.shape[0]
    g_logits = jnp.dot(hf, w_rg, preferred_element_type=F32)
    g_prob = jax.nn.softmax(g_logits, axis=-1)
    g_sel = jnp.argmax(g_logits, axis=-1)
    g_w = jnp.take_along_axis(g_prob, g_sel[:, None], axis=-1)
    e_logits = jnp.dot(hf, w_re, preferred_element_type=F32).reshape(n, N_GROUPS, EXPERTS_PER_GROUP)
    e_logits = jnp.take_along_axis(e_logits, g_sel[:, None, None], axis=1)[:, 0]
    e_prob = jax.nn.softmax(e_logits, axis=-1)
    top_v, top_i = lax.top_k(e_prob, TOP_K_IN_GROUP)
    wts = g_w * top_v / top_v.sum(-1, keepdims=True)
    ids = g_sel[:, None] * EXPERTS_PER_GROUP + top_i
    combine = (jax.nn.one_hot(ids, N_EXPERTS, dtype=F32) * wts[..., None]).sum(1)

    def expert(y, inp):
        w1e, w3e, w2e, ce = inp
        a = jax.nn.silu(hf @ w1e) * (hf @ w3e)
        return y + ce[:, None].astype(hf.dtype) * (a @ w2e), None

    y, _ = lax.scan(expert, jnp.zeros_like(hf), (w1, w3, w2, combine.T))
    return y.reshape(shp)


def trunk_layer(x, xc, mod, mod_c, lb, cos_a, sin_a, cos_c, sin_c, g_n1, g_n2, w_in, sink, g_onorm,
                g_q, g_kv, w_uq, w_ukv, w_br, w_out, w_rg, w_re, w1, w3, w2, need_ctx):
    sh1, sc1, gt1, sh2, sc2, gt2 = jnp.split(mod[:, None, :], 6, axis=-1)
    sh1c, sc1c, gt1c, sh2c, sc2c, gt2c = jnp.split(mod_c, 6, axis=-1)
    h = modulate(x, g_n1, sh1, sc1)
    hc = modulate(xc, g_n1, sh1c, sc1c)
    aq, ak, av, bq, bi, bzf, bzb, bg, cq, ckv, ckr, gl = split_in(h @ w_in)
    aqc, akc, avc, bqc, bic, bzfc, bzbc, bgc, cqc, ckvc, ckrc, glc = split_in(hc @ w_in)

    def hd(t, nh):
        return t.reshape(t.shape[0], t.shape[1], nh, HEAD_DIM)

    o_a, oc_a = window_gqa(hd(aq, A_HEADS), hd(ak, A_KV_HEADS), hd(av, A_KV_HEADS),
                           hd(aqc, A_HEADS), hd(akc, A_KV_HEADS), hd(avc, A_KV_HEADS),
                           sink, cos_a, sin_a, need_ctx)
    o_b, oc_b = hgrn2(bq, bi, bzf, bzb, bg, bqc, bic, bzfc, bzbc, bgc, lb, g_onorm, need_ctx)
    o_c, oc_c = mla(cq, ckv, ckr, cqc, ckvc, ckrc, g_q, g_kv, w_uq, w_ukv, cos_c, sin_c, need_ctx)
    x = x + gt1 * merge_branches((o_a, o_b, o_c), gl, w_br, w_out)
    x = x + gt2 * hier_moe(modulate(x, g_n2, sh2, sc2), w_rg, w_re, w1, w3, w2)
    if need_ctx:
        xc = xc + gt1c * merge_branches((oc_a, oc_b, oc_c), glc, w_br, w_out)
        xc = xc + gt2c * hier_moe(modulate(xc, g_n2, sh2c, sc2c), w_rg, w_re, w1, w3, w2)
    return x, xc


def setup_inputs(seed: int = 0) -> dict:
    key = jax.random.key(seed)
    ks = jax.random.split(key, 24)

    def nrm(k, shape, s):
        return jax.random.normal(k, shape, F32) * s

    return {
        'x': nrm(ks[0], (BATCH, SEQ, D_MODEL), 1.0),
        'c': nrm(ks[1], (BATCH, D_MODEL), 1.0),
        'ctx': nrm(ks[2], (BATCH, CTX_LEN, D_MODEL), 1.0),
        'c_ctx': nrm(ks[3], (D_MODEL,), 1.0),
        'w_mod': nrm(ks[4], (DEPTH, D_MODEL, 6 * D_MODEL), 0.5 * D_MODEL ** -0.5),
        'b_mod': nrm(ks[5], (DEPTH, 6 * D_MODEL), 0.02),
        'g_norm1': 1.0 + nrm(ks[6], (DEPTH, D_MODEL), 0.02),
        'g_norm2': 1.0 + nrm(ks[7], (DEPTH, D_MODEL), 0.02),
        'w_in': nrm(ks[8], (DEPTH, D_MODEL, IN_W), D_MODEL ** -0.5),
        'a_sink': nrm(ks[9], (DEPTH, A_HEADS), 0.5),
        'b_lb_logits': nrm(ks[10], (DEPTH, 2, B_W), 0.5),
        'b_onorm': 1.0 + nrm(ks[11], (DEPTH, B_W), 0.02),
        'c_qnorm': 1.0 + nrm(ks[12], (DEPTH, C_Q_LORA), 0.02),
        'c_kvnorm': 1.0 + nrm(ks[13], (DEPTH, C_KV_LORA), 0.02),
        'w_uq': nrm(ks[14], (DEPTH, C_Q_LORA, C_HEADS * (C_NOPE + C_ROPE)), C_Q_LORA ** -0.5),
        'w_ukv': nrm(ks[15], (DEPTH, C_KV_LORA, C_HEADS * (C_NOPE + C_V)), C_KV_LORA ** -0.5),
        'w_br': nrm(ks[16], (DEPTH, N_BRANCH, BRANCH_W, D_MODEL), BRANCH_W ** -0.5),
        'w_out': nrm(ks[17], (DEPTH, D_MODEL, D_MODEL), D_MODEL ** -0.5),
        'w_rg': nrm(ks[18], (DEPTH, D_MODEL, N_GROUPS), D_MODEL ** -0.5),
        'w_re': nrm(ks[19], (DEPTH, D_MODEL, N_EXPERTS), D_MODEL ** -0.5),
        'w1': nrm(ks[20], (DEPTH, N_EXPERTS, D_MODEL, D_EXPERT), D_MODEL ** -0.5),
        'w3': nrm(ks[21], (DEPTH, N_EXPERTS, D_MODEL, D_EXPERT), D_MODEL ** -0.5),
        'w2': nrm(ks[22], (DEPTH, N_EXPERTS, D_EXPERT, D_MODEL), D_EXPERT ** -0.5),
        'g_final': 1.0 + nrm(ks[23], (D_MODEL,), 0.02),
    }


def reference(x, c, ctx, c_ctx, w_mod, b_mod, g_norm1, g_norm2, w_in, a_sink, b_lb_logits, b_onorm,
              c_qnorm, c_kvnorm, w_uq, w_ukv, w_br, w_out, w_rg, w_re, w1, w3, w2, g_final):
    rows = x.shape[1] // GRID_W
    cos_a, sin_a = axial_rope(rows, HEAD_DIM)
    cos_c, sin_c = axial_rope(rows, C_ROPE)
    lb_all = jnp.cumsum(jax.nn.softmax(b_lb_logits.astype(F32), axis=0), axis=0)
    lb_all = lb_all - lb_all[0:1]
    xc = ctx
    for l in range(DEPTH):
        mod = jax.nn.silu(c) @ w_mod[l] + b_mod[l]
        mod_c = jax.nn.silu(c_ctx) @ w_mod[l] + b_mod[l]
        x, xc = trunk_layer(x, xc, mod, mod_c, lb_all[l], cos_a, sin_a, cos_c, sin_c,
                            g_norm1[l], g_norm2[l], w_in[l], a_sink[l], b_onorm[l],
                            c_qnorm[l], c_kvnorm[l], w_uq[l], w_ukv[l], w_br[l], w_out[l],
                            w_rg[l], w_re[l], w1[l], w3[l], w2[l], l < DEPTH - 1)
    return rmsnorm(x, g_final)
```

```python
import functools

import jax
import jax.numpy as jnp
import numpy as np
from jax import lax
from jax.experimental import pallas as pl
from jax.experimental.pallas import tpu as pltpu

F32 = jnp.float32
BF16 = jnp.bfloat16
HIGHEST = lax.Precision.HIGHEST

D_MODEL = 1024
GRID_W = 64
HEAD_DIM = 64
ROPE_BASE = 10000.0
EPS = 1e-6
A_HEADS = 8
A_KV_HEADS = 2
A_GROUP = A_HEADS // A_KV_HEADS
A_WINDOW = 128
A_BLOCK = A_WINDOW
B_HEADS = 8
B_DK = 64
B_W = B_HEADS * B_DK
B_CHUNK = 64
C_HEADS = 8
C_NOPE = 64
C_ROPE = 32
C_V = 64
C_Q_LORA = 256
C_KV_LORA = 128
C_HEAD_PAD = 128
N_BRANCH = 3
BRANCH_W = 512
N_GROUPS = 4
EXPERTS_PER_GROUP = 8
N_EXPERTS = N_GROUPS * EXPERTS_PER_GROUP
D_EXPERT = 256
ROUTER_W = 128

OFF_AQ, OFF_AK, OFF_AV = 0, 512, 640
OFF_BQ, OFF_BI, OFF_BZF, OFF_BZB, OFF_BG = 768, 1280, 1792, 2304, 2816
OFF_CQ, OFF_CKV, OFF_CKR, OFF_GL = 3328, 3584, 3712, 3840
IN_W_PACKED = OFF_GL + N_BRANCH * D_MODEL

TOKEN_BLOCK = 256
MOE_TOKEN_BLOCK = 1024
MLA_Q_BLOCK = 512
MLA_K_BLOCK = 1024
VMEM_LIMIT = 56 * 1024 * 1024


def _cparams(sem):
    return pltpu.CompilerParams(dimension_semantics=sem, vmem_limit_bytes=VMEM_LIMIT)


def _mod_kernel(cc_ref, w_ref, b_ref, o_ref):
    cc = cc_ref[...]
    a = cc * jax.nn.sigmoid(cc)
    o_ref[...] = jnp.dot(a, w_ref[...], preferred_element_type=F32, precision=HIGHEST) + b_ref[...]


def _modulation(cc, w_mod, b_mod):
    depth = w_mod.shape[0]
    nj = 6
    return pl.pallas_call(
        _mod_kernel,
        grid=(depth, nj),
        in_specs=[
            pl.BlockSpec((8, D_MODEL), lambda l, j: (0, 0)),
            pl.BlockSpec((None, D_MODEL, D_MODEL), lambda l, j: (l, 0, j)),
            pl.BlockSpec((None, 1, D_MODEL), lambda l, j: (l, 0, j)),
        ],
        out_specs=pl.BlockSpec((None, 8, D_MODEL), lambda l, j: (l, 0, j)),
        out_shape=jax.ShapeDtypeStruct((depth, 8, 6 * D_MODEL), F32),
        compiler_params=_cparams(("arbitrary", "arbitrary")),
        name="adaln_mod",
    )(cc, w_mod, b_mod.reshape(depth, 1, 6 * D_MODEL))


def _rms(x, g):
    return x * lax.rsqrt(jnp.mean(x * x, axis=-1, keepdims=True) + EPS) * g


def _rope(v, tab_ref, half):
    n = v.shape[-1]
    cos = tab_ref[:, 0:128]
    s_lo = tab_ref[:, 128:256]
    s_hi = tab_ref[:, 256:384]
    return v * cos + pltpu.roll(v, n - half, 1) * s_lo + pltpu.roll(v, half, 1) * s_hi


def _proj_kernel(x_ref, mod_ref, g1_ref, w_ref, wuq_ref, wukv_ref, gq_ref, gkv_ref, lbp_ref, taba_ref, tabc_ref,
                 qa_ref, kva_ref, bqig_ref, gates_ref, qc_ref, kc_ref, vc_ref, gl_ref):
    x = x_ref[...]
    h = _rms(x, g1_ref[...]) * (1.0 + mod_ref[1:2, :]) + mod_ref[0:1, :]
    hb = h.astype(BF16)

    def seg(off, width):
        return jnp.dot(hb, w_ref[:, off:off + width], preferred_element_type=F32)

    aq = seg(OFF_AQ, 512) * (HEAD_DIM ** -0.5)
    for j in range(4):
        qa_ref[:, 128 * j:128 * (j + 1)] = _rope(aq[:, 128 * j:128 * (j + 1)], taba_ref, 32).astype(BF16)
    kva_ref[:, 0:128] = _rope(seg(OFF_AK, 128), taba_ref, 32).astype(BF16)
    kva_ref[:, 128:256] = seg(OFF_AV, 128).astype(BF16)

    bqig_ref[:, 0:512] = seg(OFF_BQ, 512).astype(BF16)
    bqig_ref[:, 512:1024] = seg(OFF_BI, 512).astype(BF16)
    bqig_ref[:, 1024:1536] = seg(OFF_BG, 512).astype(BF16)
    for d, off in enumerate((OFF_BZF, OFF_BZB)):
        z = seg(off, 512)
        log_lb = lbp_ref[0:1, 512 * d:512 * (d + 1)]
        log1m_lb = lbp_ref[1:2, 512 * d:512 * (d + 1)]
        one_m_lb = lbp_ref[2:3, 512 * d:512 * (d + 1)]
        e = jnp.exp(-jnp.abs(z))
        log_sig = jnp.minimum(z, 0.0) - jnp.log(1.0 + e)
        b = log1m_lb + log_sig
        mx = jnp.maximum(log_lb, b)
        logf = mx + jnp.log(1.0 + jnp.exp(-jnp.abs(log_lb - b)))
        r = 1.0 / (1.0 + e)
        key = one_m_lb * jnp.where(z >= 0.0, e * r, r)
        gates_ref[:, 512 * d:512 * (d + 1)] = logf
        gates_ref[:, 1024 + 512 * d:1024 + 512 * (d + 1)] = key

    cq = _rms(seg(OFF_CQ, C_Q_LORA), gq_ref[...]).astype(BF16)
    qh = jnp.dot(cq, wuq_ref[...], preferred_element_type=F32) * ((C_NOPE + C_ROPE) ** -0.5)
    ckv = _rms(seg(OFF_CKV, C_KV_LORA), gkv_ref[...]).astype(BF16)
    kvh = jnp.dot(ckv, wukv_ref[...], preferred_element_type=F32)
    kr = _rope(seg(OFF_CKR, 128), tabc_ref, 16)
    for j in range(C_HEADS):
        sl = slice(C_HEAD_PAD * j, C_HEAD_PAD * (j + 1))
        qc_ref[:, sl] = _rope(qh[:, sl], tabc_ref, 16).astype(BF16)
        kc_ref[:, sl] = (kvh[:, sl] + kr).astype(BF16)
    vc_ref[...] = kvh[:, C_HEADS * C_HEAD_PAD:].astype(BF16)

    for j in range(6):
        gl_ref[:, 512 * j:512 * (j + 1)] = seg(OFF_GL + 512 * j, 512).astype(BF16)


def _projection(x, mod, g1, w_in_p, wuq_p, wukv_p, gq, gkv, lbp, taba, tabc, dims):
    B, S, L = dims
    n = x.shape[0]
    tm = TOKEN_BLOCK
    nlat = B * S // tm
    spb = S // tm

    def row(i):
        return (i, 0)

    def mod_row(i):
        return (jnp.minimum(i // spb, B), 0, 0)

    def tab_row(i):
        return (jnp.where(i < nlat, i % spb, spb), 0)

    const = lambda i: (0, 0)
    widths = (512, 256, 1536, 2048, 1024, 1024, 512, 3072)
    dtypes = (BF16, BF16, BF16, F32, BF16, BF16, BF16, BF16)
    return pl.pallas_call(
        _proj_kernel,
        grid=(n // tm,),
        in_specs=[
            pl.BlockSpec((tm, D_MODEL), row),
            pl.BlockSpec((None, 6, D_MODEL), mod_row),
            pl.BlockSpec((1, D_MODEL), const),
            pl.BlockSpec((D_MODEL, IN_W_PACKED), const),
            pl.BlockSpec(wuq_p.shape, const),
            pl.BlockSpec(wukv_p.shape, const),
            pl.BlockSpec((1, C_Q_LORA), const),
            pl.BlockSpec((1, C_KV_LORA), const),
            pl.BlockSpec((8, 2 * B_W), const),
            pl.BlockSpec((tm, 384), tab_row),
            pl.BlockSpec((tm, 384), tab_row),
        ],
        out_specs=[pl.BlockSpec((tm, w), row) for w in widths],
        out_shape=[jax.ShapeDtypeStruct((n, w), dt) for w, dt in zip(widths, dtypes)],
        compiler_params=_cparams(("parallel",)),
        name="in_proj",
    )(x, mod, g1, w_in_p, wuq_p, wukv_p, gq, gkv, lbp, taba, tabc)


def _gqa_kernel(q_ref, kp_ref, kc_ref, kn_ref, kx_ref, sink_ref, o_ref, *, nlat_blocks, seq):
    j = pl.program_id(1)
    is_lat = j < nlat_blocks
    q = q_ref[...]
    kv = jnp.concatenate([kp_ref[...], kc_ref[...], kn_ref[...], kx_ref[...]], axis=0)
    nk = kv.shape[0]
    rows = lax.broadcasted_iota(jnp.int32, (A_BLOCK, nk), 0)
    cols = lax.broadcasted_iota(jnp.int32, (A_BLOCK, nk), 1)
    qpos = j * A_BLOCK + rows
    kpos = (j - 1) * A_BLOCK + cols
    kend = jnp.where(is_lat, seq, 0)
    band = (kpos >= 0) & (kpos < kend) & (jnp.abs(qpos - kpos) <= A_WINDOW)
    valid = band | (cols >= 3 * A_BLOCK)
    for g in range(A_KV_HEADS):
        kg = kv[:, HEAD_DIM * g:HEAD_DIM * (g + 1)]
        vg = kv[:, 128 + HEAD_DIM * g:128 + HEAD_DIM * (g + 1)]
        for hh in range(A_GROUP):
            hd = g * A_GROUP + hh
            qh = q[:, HEAD_DIM * hd:HEAD_DIM * (hd + 1)]
            s = lax.dot_general(qh, kg, (((1,), (1,)), ((), ())), preferred_element_type=F32)
            s = jnp.where(valid, s, -jnp.inf)
            snk = sink_ref[hd:hd + 1, 0:1]
            m = jnp.maximum(jnp.max(s, axis=-1, keepdims=True), snk)
            p = jnp.exp(s - m)
            denom = jnp.sum(p, axis=-1, keepdims=True) + jnp.exp(snk - m)
            o = jnp.dot(p.astype(BF16), vg, preferred_element_type=F32) / denom
            o_ref[:, HEAD_DIM * hd:HEAD_DIM * (hd + 1)] = o.astype(BF16)


def _window_gqa(qa, kva, sink, dims, need_ctx):
    B, S, L = dims
    n = qa.shape[0]
    nb = S // A_BLOCK
    ncb = L // A_BLOCK
    nq = nb + (ncb if need_ctx else 0)

    def q_row(b, j):
        return (jnp.where(j < nb, b * nb + j, B * nb + b * ncb + (j - nb)), 0)

    def k_row(delta):
        def f(b, j):
            return (b * nb + jnp.clip(j + delta, 0, nb - 1), 0)
        return f

    def ctx_row(b, j):
        return (B * S // L + b, 0)

    return pl.pallas_call(
        functools.partial(_gqa_kernel, nlat_blocks=nb, seq=S),
        grid=(B, nq),
        in_specs=[
            pl.BlockSpec((A_BLOCK, 512), q_row),
            pl.BlockSpec((A_BLOCK, 256), k_row(-1)),
            pl.BlockSpec((A_BLOCK, 256), k_row(0)),
            pl.BlockSpec((A_BLOCK, 256), k_row(1)),
            pl.BlockSpec((L, 256), ctx_row),
            pl.BlockSpec((8, 128), lambda b, j: (0, 0)),
        ],
        out_specs=pl.BlockSpec((A_BLOCK, 512), q_row),
        out_shape=jax.ShapeDtypeStruct((n if need_ctx else B * S, 512), BF16),
        compiler_params=_cparams(("parallel", "parallel")),
        name="window_gqa",
    )(qa, kva, kva, kva, kva, sink)


def _hgrn_kernel(q_ref, v_ref, g_ref, k_ref, o_ref, st_ref):
    d = pl.program_id(1)
    c = pl.program_id(2)
    C = B_CHUNK

    @pl.when(c == 0)
    def _():
        st_ref[...] = jnp.zeros_like(st_ref)

    rows = lax.broadcasted_iota(jnp.int32, (C, C), 0)
    cols = lax.broadcasted_iota(jnp.int32, (C, C), 1)
    fwd = d == 0
    causal = (rows - cols) * jnp.where(fwd, 1, -1) >= 0
    g = g_ref[...]
    bc = jnp.dot(causal.astype(F32), g, preferred_element_type=F32, precision=HIGHEST)
    tot = jnp.sum(g, axis=0, keepdims=True)
    rho = jnp.where(fwd, bc[C // 2 - 1:C // 2, :], bc[C // 2:C // 2 + 1, :])
    q = q_ref[...].astype(F32)
    key = k_ref[...]
    v = v_ref[...]
    qe = (q * jnp.exp(bc - rho)).astype(BF16)
    ke = (key * jnp.exp(rho - bc)).astype(BF16)
    qs = (q * jnp.exp(bc)).astype(BF16)
    ks = (key * jnp.exp(tot - bc)).astype(BF16)
    dec = jnp.exp(tot)
    for hd in range(B_HEADS):
        sl = slice(B_DK * hd, B_DK * (hd + 1))
        st = st_ref[hd]
        att = lax.dot_general(qe[:, sl], ke[:, sl], (((1,), (1,)), ((), ())), preferred_element_type=F32)
        att = jnp.where(causal, att, 0.0).astype(BF16)
        o = lax.dot_general(qs[:, sl], st.astype(BF16), (((1,), (1,)), ((), ())), preferred_element_type=F32)
        o = o + jnp.dot(att, v[:, sl], preferred_element_type=F32)
        o_ref[:, sl] = o
        upd = lax.dot_general(v[:, sl], ks[:, sl], (((0,), (0,)), ((), ())), preferred_element_type=F32)
        st_ref[hd] = st * dec[:, sl] + upd


def _hgrn2_scan(bqig, gates, dims):
    B, S, L = dims
    n = bqig.shape[0]
    C = B_CHUNK
    ns, nl = S // C, L // C
    nch = ns + nl

    def rowblk(b, d, c):
        fwd_id = jnp.where(c < nl, ns + c, c - nl)
        cid = jnp.where(d == 0, fwd_id, nch - 1 - c)
        return jnp.where(cid < ns, b * ns + cid, B * ns + b * nl + (cid - ns))

    return pl.pallas_call(
        _hgrn_kernel,
        grid=(B, 2, nch),
        in_specs=[
            pl.BlockSpec((C, B_W), lambda b, d, c: (rowblk(b, d, c), 0)),
            pl.BlockSpec((C, B_W), lambda b, d, c: (rowblk(b, d, c), 1)),
            pl.BlockSpec((C, B_W), lambda b, d, c: (rowblk(b, d, c), d)),
            pl.BlockSpec((C, B_W), lambda b, d, c: (rowblk(b, d, c), 2 + d)),
        ],
        out_specs=pl.BlockSpec((None, C, B_W), lambda b, d, c: (d, rowblk(b, d, c), 0)),
        out_shape=jax.ShapeDtypeStruct((2, n, B_W), F32),
        scratch_shapes=[pltpu.VMEM((B_HEADS, B_DK, B_DK), F32)],
        compiler_params=_cparams(("parallel", "parallel", "arbitrary")),
        name="hgrn2_scan",
    )(bqig, bqig, gates, gates)


def _mla_kernel(q_ref, k_ref, v_ref, *rest, with_ctx):
    if with_ctx:
        kx_ref, vx_ref, o_ref, m_ref, l_ref, acc_ref = rest
    else:
        _, o_ref, m_ref, l_ref, acc_ref = rest
    kstep = pl.program_id(2)

    def update(k_all, v_all, first):
        for hd in range(C_HEADS):
            qh = q_ref[:, C_HEAD_PAD * hd:C_HEAD_PAD * (hd + 1)]
            s = lax.dot_general(qh, k_all[:, C_HEAD_PAD * hd:C_HEAD_PAD * (hd + 1)],
                                (((1,), (1,)), ((), ())), preferred_element_type=F32)
            vh = v_all[:, C_V * hd:C_V * (hd + 1)]
            smax = jnp.max(s, axis=-1, keepdims=True)
            if first:
                p = jnp.exp(s - smax)
                m_ref[hd] = smax
                l_ref[hd] = jnp.sum(p, axis=-1, keepdims=True)
                acc_ref[hd] = jnp.dot(p.astype(BF16), vh, preferred_element_type=F32)
            else:
                m_old = m_ref[hd]
                m_new = jnp.maximum(m_old, smax)
                alpha = jnp.exp(m_old - m_new)
                p = jnp.exp(s - m_new)
                m_ref[hd] = m_new
                l_ref[hd] = alpha * l_ref[hd] + jnp.sum(p, axis=-1, keepdims=True)
                acc_ref[hd] = alpha * acc_ref[hd] + jnp.dot(p.astype(BF16), vh, preferred_element_type=F32)

    if with_ctx:
        @pl.when(kstep == 0)
        def _():
            update(kx_ref[...], vx_ref[...], True)

        update(k_ref[...], v_ref[...], False)
    else:
        @pl.when(kstep == 0)
        def _():
            update(k_ref[...], v_ref[...], True)

        @pl.when(kstep > 0)
        def _():
            update(k_ref[...], v_ref[...], False)

    @pl.when(kstep == pl.num_programs(2) - 1)
    def _():
        for hd in range(C_HEADS):
            o_ref[:, C_V * hd:C_V * (hd + 1)] = (acc_ref[hd] / l_ref[hd]).astype(BF16)


def _mla_attention(qc, kc, vc, dims, need_ctx):
    B, S, L = dims
    n = qc.shape[0]
    tq = min(MLA_Q_BLOCK, S)
    tk = min(MLA_K_BLOCK, S)
    nq, nk = S // tq, S // tk
    hw = C_HEADS * C_HEAD_PAD
    vw = C_HEADS * C_V
    scratch = lambda t: [pltpu.VMEM((C_HEADS, t, 1), F32), pltpu.VMEM((C_HEADS, t, 1), F32),
                         pltpu.VMEM((C_HEADS, t, C_V), F32)]
    ctx_row = lambda b, i, k: (B * S // L + b, 0)
    o_lat = pl.pallas_call(
        functools.partial(_mla_kernel, with_ctx=True),
        grid=(B, nq, nk),
        in_specs=[
            pl.BlockSpec((tq, hw), lambda b, i, k: (b * nq + i, 0)),
            pl.BlockSpec((tk, hw), lambda b, i, k: (b * nk + k, 0)),
            pl.BlockSpec((tk, vw), lambda b, i, k: (b * nk + k, 0)),
            pl.BlockSpec((L, hw), ctx_row),
            pl.BlockSpec((L, vw), ctx_row),
        ],
        out_specs=pl.BlockSpec((tq, vw), lambda b, i, k: (b * nq + i, 0)),
        out_shape=jax.ShapeDtypeStruct((n if need_ctx else B * S, vw), BF16),
        scratch_shapes=scratch(tq),
        compiler_params=_cparams(("parallel", "parallel", "arbitrary")),
        name="mla_latent",
    )(qc, kc, vc, kc, vc)
    if not need_ctx:
        return o_lat
    return pl.pallas_call(
        functools.partial(_mla_kernel, with_ctx=False),
        grid=(B, 1, 1),
        in_specs=[
            pl.BlockSpec((L, hw), ctx_row),
            pl.BlockSpec((L, hw), ctx_row),
            pl.BlockSpec((L, vw), ctx_row),
            pl.BlockSpec(memory_space=pl.ANY),
        ],
        out_specs=pl.BlockSpec((L, vw), ctx_row),
        out_shape=jax.ShapeDtypeStruct((n, vw), BF16),
        scratch_shapes=scratch(L),
        input_output_aliases={3: 0},
        compiler_params=_cparams(("parallel", "arbitrary", "arbitrary")),
        name="mla_context",
    )(qc, kc, vc, o_lat)


def _group_sum(x, ones_ref):
    hi = x.astype(BF16)
    lo = (x - hi.astype(F32)).astype(BF16)
    return (jnp.dot(hi, ones_ref[...], preferred_element_type=F32)
            + jnp.dot(lo, ones_ref[...], preferred_element_type=F32))


def _merge_kernel(x_ref, oa_ref, of_ref, ob_ref, bg_ref, oc_ref, gl_ref, wbr_ref, wout_ref, gn_ref, mod_ref,
                  g2_ref, ones_ref, xo_ref, h2_ref):
    ob = of_ref[...] + ob_ref[...]
    ms = _group_sum(ob * ob, ones_ref) * (1.0 / B_DK)
    obn = ob * lax.rsqrt(ms + EPS) * gn_ref[...]
    bg = bg_ref[...].astype(F32)
    bb = (obn * (bg * jax.nn.sigmoid(bg))).astype(BF16)
    branches = (oa_ref[...], bb, oc_ref[...])
    y = None
    for nbr in range(N_BRANCH):
        gate = jax.nn.sigmoid(gl_ref[:, D_MODEL * nbr:D_MODEL * (nbr + 1)].astype(F32))
        t = gate * jnp.dot(branches[nbr], wbr_ref[nbr], preferred_element_type=F32)
        y = t if y is None else y + t
    upd = jnp.dot(y.astype(BF16), wout_ref[...], preferred_element_type=F32)
    xn = x_ref[...] + mod_ref[2:3, :] * upd
    xo_ref[...] = xn
    h2 = _rms(xn, g2_ref[...]) * (1.0 + mod_ref[4:5, :]) + mod_ref[3:4, :]
    h2_ref[...] = h2.astype(BF16)


def _merge(x, oa, ohg, bqig, oc, gl, wbr, wout, gn, mod, g2, ones, dims, need_ctx):
    B, S, L = dims
    n = x.shape[0]
    tm = TOKEN_BLOCK
    spb = S // tm
    nblk = (n if need_ctx else B * S) // tm
    row = lambda i: (i, 0)
    const2 = lambda i: (0, 0)
    return pl.pallas_call(
        _merge_kernel,
        grid=(nblk,),
        in_specs=[
            pl.BlockSpec((tm, D_MODEL), row),
            pl.BlockSpec((tm, 512), row),
            pl.BlockSpec((None, tm, B_W), lambda i: (0, i, 0)),
            pl.BlockSpec((None, tm, B_W), lambda i: (1, i, 0)),
            pl.BlockSpec((tm, B_W), lambda i: (i, 2)),
            pl.BlockSpec((tm, 512), row),
            pl.BlockSpec((tm, N_BRANCH * D_MODEL), row),
            pl.BlockSpec((N_BRANCH, BRANCH_W, D_MODEL), lambda i: (0, 0, 0)),
            pl.BlockSpec((D_MODEL, D_MODEL), const2),
            pl.BlockSpec((1, B_W), const2),
            pl.BlockSpec((None, 6, D_MODEL), lambda i: (jnp.minimum(i // spb, B), 0, 0)),
            pl.BlockSpec((1, D_MODEL), const2),
            pl.BlockSpec((B_W, B_W), const2),
        ],
        out_specs=[pl.BlockSpec((tm, D_MODEL), row), pl.BlockSpec((tm, D_MODEL), row)],
        out_shape=[jax.ShapeDtypeStruct((nblk * tm, D_MODEL), F32), jax.ShapeDtypeStruct((nblk * tm, D_MODEL), BF16)],
        compiler_params=_cparams(("parallel",)),
        name="branch_merge",
    )(x, oa, ohg, ohg, bqig, oc, gl, wbr, wout, gn, mod, g2, ones)


def _moe_kernel(h_ref, x_ref, mod_ref, wr_ref, w13_ref, w2_ref, o_ref, comb_ref, y_ref):
    e = pl.program_id(1)
    h = h_ref[...]
    lane = lax.broadcasted_iota(jnp.int32, (h.shape[0], ROUTER_W), 1)

    @pl.when(e == 0)
    def _():
        logits = jnp.dot(h, wr_ref[...], preferred_element_type=F32)
        big = jnp.int32(ROUTER_W)
        is_grp = (lane >= N_EXPERTS) & (lane < N_EXPERTS + N_GROUPS)
        gl = jnp.where(is_grp, logits, -jnp.inf)
        gmax = jnp.max(gl, axis=-1, keepdims=True)
        gsel = jnp.min(jnp.where(gl == gmax, lane, big), axis=-1, keepdims=True) - N_EXPERTS
        gw = 1.0 / jnp.sum(jnp.exp(gl - gmax), axis=-1, keepdims=True)
        in_grp = (lane >= gsel * EXPERTS_PER_GROUP) & (lane < (gsel + 1) * EXPERTS_PER_GROUP)
        el = jnp.where(in_grp, logits, -jnp.inf)
        m1 = jnp.max(el, axis=-1, keepdims=True)
        i1 = jnp.min(jnp.where(el == m1, lane, big), axis=-1, keepdims=True)
        el2 = jnp.where(lane == i1, -jnp.inf, el)
        m2 = jnp.max(el2, axis=-1, keepdims=True)
        i2 = jnp.min(jnp.where(el2 == m2, lane, big), axis=-1, keepdims=True)
        e2 = jnp.exp(m2 - m1)
        w1 = gw / (1.0 + e2)
        w2 = gw * e2 / (1.0 + e2)
        comb_ref[...] = jnp.where(lane == i1, w1, 0.0) + jnp.where(lane == i2, w2, 0.0)
        y_ref[...] = jnp.zeros_like(y_ref)

    ce = jnp.sum(jnp.where(lane == e, comb_ref[...], 0.0), axis=-1, keepdims=True)
    h13 = jnp.dot(h, w13_ref[...], preferred_element_type=F32)
    a1 = h13[:, :D_EXPERT]
    a = a1 * jax.nn.sigmoid(a1) * h13[:, D_EXPERT:]
    y_ref[...] += jnp.dot((a * ce).astype(BF16), w2_ref[...], preferred_element_type=F32)

    @pl.when(e == N_EXPERTS - 1)
    def _():
        o_ref[...] = x_ref[...] + mod_ref[5:6, :] * y_ref[...]


def _moe(h2, x, mod, wr, w13, w2, dims, need_ctx):
    B, S, L = dims
    n = x.shape[0]
    tm = min(MOE_TOKEN_BLOCK, S, B * L)
    spb = S // tm
    nblk = (n if need_ctx else B * S) // tm
    row = lambda i, e: (i, 0)
    return pl.pallas_call(
        _moe_kernel,
        grid=(nblk, N_EXPERTS),
        in_specs=[
            pl.BlockSpec((tm, D_MODEL), row),
            pl.BlockSpec((tm, D_MODEL), row),
            pl.BlockSpec((None, 6, D_MODEL), lambda i, e: (jnp.minimum(i // spb, B), 0, 0)),
            pl.BlockSpec((D_MODEL, ROUTER_W), lambda i, e: (0, 0)),
            pl.BlockSpec((None, D_MODEL, 2 * D_EXPERT), lambda i, e: (e, 0, 0)),
            pl.BlockSpec((None, D_EXPERT, D_MODEL), lambda i, e: (e, 0, 0)),
        ],
        out_specs=pl.BlockSpec((tm, D_MODEL), row),
        out_shape=jax.ShapeDtypeStruct((nblk * tm, D_MODEL), F32),
        scratch_shapes=[pltpu.VMEM((tm, ROUTER_W), F32), pltpu.VMEM((tm, D_MODEL), F32)],
        compiler_params=_cparams(("parallel", "arbitrary")),
        name="hier_moe",
    )(h2, x, mod, wr, w13, w2)


def _final_kernel(x_ref, g_ref, o_ref):
    o_ref[...] = _rms(x_ref[...], g_ref[...])


def _final_norm(x, g, dims):
    B, S, L = dims
    tm = TOKEN_BLOCK
    return pl.pallas_call(
        _final_kernel,
        grid=(B * S // tm,),
        in_specs=[pl.BlockSpec((tm, D_MODEL), lambda i: (i, 0)), pl.BlockSpec((1, D_MODEL), lambda i: (0, 0))],
        out_specs=pl.BlockSpec((tm, D_MODEL), lambda i: (i, 0)),
        out_shape=jax.ShapeDtypeStruct((B * S, D_MODEL), F32),
        compiler_params=_cparams(("parallel",)),
        name="final_norm",
    )(x, g)


def _rope_tables(S, L):
    rows = S // GRID_W
    pos_r = np.repeat(np.arange(rows, dtype=np.float32), GRID_W)
    pos_c = np.tile(np.arange(GRID_W, dtype=np.float32), rows)

    def angles(rot_dim):
        nf = rot_dim // 4
        inv = jnp.asarray(ROPE_BASE, F32) ** (-jnp.arange(nf, dtype=F32) / nf)
        ang = jnp.concatenate([pos_r[:, None] * inv, pos_c[:, None] * inv], axis=-1)
        return jnp.cos(ang), jnp.sin(ang)

    def with_ctx(cos, s_lo, s_hi):
        ident = jnp.concatenate([jnp.ones((L, 128), F32), jnp.zeros((L, 256), F32)], axis=-1)
        return jnp.concatenate([jnp.concatenate([cos, s_lo, s_hi], axis=-1), ident], axis=0)

    cos, sin = angles(HEAD_DIM)
    z = jnp.zeros_like(sin)
    taba = with_ctx(jnp.tile(cos, (1, 4)), jnp.tile(jnp.concatenate([-sin, z], -1), (1, 2)),
                    jnp.tile(jnp.concatenate([z, sin], -1), (1, 2)))
    cos, sin = angles(C_ROPE)
    z = jnp.zeros_like(sin)
    one64, zero64, zero32 = jnp.ones((S, 64), F32), jnp.zeros((S, 64), F32), jnp.zeros((S, 32), F32)
    tabc = with_ctx(jnp.concatenate([one64, cos, cos, one64[:, :32]], -1),
                    jnp.concatenate([zero64, -sin, z, zero32], -1),
                    jnp.concatenate([zero64, z, sin, zero32], -1))
    return taba, tabc


def _pack_w_in(w):
    pad = lambda k: jnp.zeros((w.shape[0], k), w.dtype)
    return jnp.concatenate([w[:, :3712], pad(64), w[:, 3712:3744], pad(32), w[:, 3744:]], axis=-1).astype(BF16)


def _pack_w_uq(w):
    w = w.reshape(C_Q_LORA, C_HEADS, C_NOPE + C_ROPE)
    w = jnp.pad(w, ((0, 0), (0, 0), (0, C_HEAD_PAD - C_NOPE - C_ROPE)))
    return w.reshape(C_Q_LORA, C_HEADS * C_HEAD_PAD).astype(BF16)


def _pack_w_ukv(w):
    w = w.reshape(C_KV_LORA, C_HEADS, C_NOPE + C_V)
    wk = jnp.pad(w[:, :, :C_NOPE], ((0, 0), (0, 0), (0, C_HEAD_PAD - C_NOPE))).reshape(C_KV_LORA, -1)
    wv = w[:, :, C_NOPE:].reshape(C_KV_LORA, -1)
    return jnp.concatenate([wk, wv], axis=-1).astype(BF16)


def kernel(x, c, ctx, c_ctx, w_mod, b_mod, g_norm1, g_norm2, w_in, a_sink, b_lb_logits, b_onorm, c_qnorm, c_kvnorm,
           w_uq, w_ukv, w_br, w_out, w_rg, w_re, w1, w3, w2, g_final):
    B, S, _ = x.shape
    L = ctx.shape[1]
    depth = w_in.shape[0]
    assert L == TOKEN_BLOCK and S % MLA_Q_BLOCK == 0 and S % GRID_W == 0
    dims = (B, S, L)

    xs = jnp.concatenate([x.reshape(B * S, D_MODEL), ctx.reshape(B * L, D_MODEL)], axis=0)
    cc = jnp.zeros((8, D_MODEL), F32).at[:B].set(c).at[B].set(c_ctx)
    mod_all = _modulation(cc, w_mod, b_mod).reshape(depth, 8, 6, D_MODEL)

    lb_all = jnp.cumsum(jax.nn.softmax(b_lb_logits.astype(F32), axis=0), axis=0)
    lb_all = (lb_all - lb_all[0:1]).reshape(depth, 1, 2 * B_W)
    lbp_all = jnp.concatenate([jnp.log(lb_all), jnp.log1p(-lb_all), 1.0 - lb_all,
                               jnp.zeros((depth, 5, 2 * B_W), F32)], axis=1)

    taba, tabc = _rope_tables(S, L)
    ones = jnp.kron(jnp.eye(B_HEADS, dtype=F32), jnp.ones((B_DK, B_DK), F32)).astype(BF16)

    for l in range(depth):
        need_ctx = l < depth - 1
        mod = mod_all[l]
        wr = jnp.concatenate([w_re[l], w_rg[l], jnp.zeros((D_MODEL, ROUTER_W - N_EXPERTS - N_GROUPS), F32)],
                             axis=-1).astype(BF16)
        w13 = jnp.concatenate([w1[l], w3[l]], axis=-1).astype(BF16)
        sink = jnp.broadcast_to(a_sink[l].astype(F32)[:, None], (A_HEADS, 128))

        qa, kva, bqig, gates, qc, kc, vc, gl = _projection(
            xs, mod, g_norm1[l][None], _pack_w_in(w_in[l]), _pack_w_uq(w_uq[l]), _pack_w_ukv(w_ukv[l]),
            c_qnorm[l][None], c_kvnorm[l][None], lbp_all[l], taba, tabc, dims)
        oa = _window_gqa(qa, kva, sink, dims, need_ctx)
        ohg = _hgrn2_scan(bqig, gates, dims)
        oc = _mla_attention(qc, kc, vc, dims, need_ctx)
        xs, h2 = _merge(xs, oa, ohg, bqig, oc, gl, w_br[l].astype(BF16), w_out[l].astype(BF16), b_onorm[l][None],
                        mod, g_norm2[l][None], ones, dims, need_ctx)
        xs = _moe(h2, xs, mod, wr, w13, w2[l].astype(BF16), dims, need_ctx)

    return _final_norm(xs, g_final[None], dims).reshape(B, S, D_MODEL)
```

```python
import functools

import jax
import jax.numpy as jnp
import numpy as np
from jax import lax
from jax.experimental import pallas as pl
from jax.experimental.pallas import tpu as pltpu

F32 = jnp.float32
BF16 = jnp.bfloat16
HIGHEST = lax.Precision.HIGHEST

D_MODEL = 1024
GRID_W = 64
HEAD_DIM = 64
ROPE_BASE = 10000.0
EPS = 1e-6
A_HEADS = 8
A_KV_HEADS = 2
A_GROUP = A_HEADS // A_KV_HEADS
A_WINDOW = 128
A_BLOCK = A_WINDOW
A_QBLOCK = 2 * A_BLOCK
B_HEADS = 8
B_DK = 64
B_W = B_HEADS * B_DK
B_CHUNK = 64
C_HEADS = 8
C_NOPE = 64
C_ROPE = 32
C_V = 64
C_Q_LORA = 256
C_KV_LORA = 128
C_HEAD_PAD = 128
N_BRANCH = 3
BRANCH_W = 512
N_GROUPS = 4
EXPERTS_PER_GROUP = 8
N_EXPERTS = N_GROUPS * EXPERTS_PER_GROUP
D_EXPERT = 256
ROUTER_W = 128

OFF_AQ, OFF_AK, OFF_AV = 0, 512, 640
OFF_BQ, OFF_BI, OFF_BZF, OFF_BZB, OFF_BG = 768, 1280, 1792, 2304, 2816
OFF_CQ, OFF_CKV, OFF_CKR, OFF_GL = 3328, 3584, 3712, 3840
IN_W_PACKED = OFF_GL + N_BRANCH * D_MODEL

TOKEN_BLOCK = 256
MOE_TOKEN_BLOCK = 1024
MLA_Q_BLOCK = 512
MLA_K_BLOCK = 1024
VMEM_LIMIT = 56 * 1024 * 1024


def _cparams(sem):
    return pltpu.CompilerParams(dimension_semantics=sem, vmem_limit_bytes=VMEM_LIMIT)


def _mod_kernel(cc_ref, w_ref, b_ref, o_ref):
    cc = cc_ref[...]
    a = cc * jax.nn.sigmoid(cc)
    o_ref[...] = jnp.dot(a, w_ref[...], preferred_element_type=F32, precision=HIGHEST) + b_ref[...]


def _modulation(cc, w_mod, b_mod):
    depth = w_mod.shape[0]
    nj = 6
    return pl.pallas_call(
        _mod_kernel,
        grid=(depth, nj),
        in_specs=[
            pl.BlockSpec((8, D_MODEL), lambda l, j: (0, 0)),
            pl.BlockSpec((None, D_MODEL, D_MODEL), lambda l, j: (l, 0, j)),
            pl.BlockSpec((None, 1, D_MODEL), lambda l, j: (l, 0, j)),
        ],
        out_specs=pl.BlockSpec((None, 8, D_MODEL), lambda l, j: (l, 0, j)),
        out_shape=jax.ShapeDtypeStruct((depth, 8, 6 * D_MODEL), F32),
        compiler_params=_cparams(("arbitrary", "arbitrary")),
        name="adaln_mod",
    )(cc, w_mod, b_mod.reshape(depth, 1, 6 * D_MODEL))


def _rms(x, g):
    return x * lax.rsqrt(jnp.mean(x * x, axis=-1, keepdims=True) + EPS) * g


def _rope(v, tab_ref, half):
    n = v.shape[-1]
    cos = tab_ref[:, 0:128]
    s_lo = tab_ref[:, 128:256]
    s_hi = tab_ref[:, 256:384]
    return v * cos + pltpu.roll(v, n - half, 1) * s_lo + pltpu.roll(v, half, 1) * s_hi


def _proj_kernel(x_ref, mod_ref, g1_ref, w_ref, wuq_ref, wukv_ref, gq_ref, gkv_ref, lbp_ref, taba_ref, tabc_ref,
                 qa_ref, ka_ref, va_ref, bqig_ref, gates_ref, qc_ref, kc_ref, vc_ref, gl_ref):
    x = x_ref[...]
    h = _rms(x, g1_ref[...]) * (1.0 + mod_ref[1:2, :]) + mod_ref[0:1, :]
    hb = h.astype(BF16)

    def seg(off, width):
        return jnp.dot(hb, w_ref[:, off:off + width], preferred_element_type=F32)

    aq = seg(OFF_AQ, 512) * (HEAD_DIM ** -0.5)
    for j in range(4):
        qa_ref[128 * j:128 * (j + 1), :] = _rope(aq[:, 128 * j:128 * (j + 1)], taba_ref, 32).T.astype(BF16)
    ka_ref[...] = _rope(seg(OFF_AK, 128), taba_ref, 32).astype(BF16)
    va_ref[...] = seg(OFF_AV, 128).T.astype(BF16)

    bqig_ref[:, 0:512] = seg(OFF_BQ, 512).astype(BF16)
    bqig_ref[:, 512:1024] = seg(OFF_BI, 512).astype(BF16)
    bqig_ref[:, 1024:1536] = seg(OFF_BG, 512).astype(BF16)
    for d, off in enumerate((OFF_BZF, OFF_BZB)):
        z = seg(off, 512)
        log_lb = lbp_ref[0:1, 512 * d:512 * (d + 1)]
        log1m_lb = lbp_ref[1:2, 512 * d:512 * (d + 1)]
        one_m_lb = lbp_ref[2:3, 512 * d:512 * (d + 1)]
        e = jnp.exp(-jnp.abs(z))
        log_sig = jnp.minimum(z, 0.0) - jnp.log(1.0 + e)
        b = log1m_lb + log_sig
        mx = jnp.maximum(log_lb, b)
        logf = mx + jnp.log(1.0 + jnp.exp(-jnp.abs(log_lb - b)))
        r = 1.0 / (1.0 + e)
        key = one_m_lb * jnp.where(z >= 0.0, e * r, r)
        gates_ref[:, 512 * d:512 * (d + 1)] = logf
        gates_ref[:, 1024 + 512 * d:1024 + 512 * (d + 1)] = key

    cq = _rms(seg(OFF_CQ, C_Q_LORA), gq_ref[...]).astype(BF16)
    qh = jnp.dot(cq, wuq_ref[...], preferred_element_type=F32) * ((C_NOPE + C_ROPE) ** -0.5)
    ckv = _rms(seg(OFF_CKV, C_KV_LORA), gkv_ref[...]).astype(BF16)
    kvh = jnp.dot(ckv, wukv_ref[...], preferred_element_type=F32)
    kr = _rope(seg(OFF_CKR, 128), tabc_ref, 16)
    for j in range(C_HEADS):
        sl = slice(C_HEAD_PAD * j, C_HEAD_PAD * (j + 1))
        qc_ref[sl, :] = _rope(qh[:, sl], tabc_ref, 16).T.astype(BF16)
        kc_ref[:, sl] = (kvh[:, sl] + kr).astype(BF16)
    vc_ref[...] = kvh[:, C_HEADS * C_HEAD_PAD:].T.astype(BF16)

    for j in range(6):
        gl_ref[:, 512 * j:512 * (j + 1)] = seg(OFF_GL + 512 * j, 512).astype(BF16)


def _projection(x, mod, g1, w_in_p, wuq_p, wukv_p, gq, gkv, lbp, taba, tabc, dims):
    B, S, L = dims
    n = x.shape[0]
    tm = TOKEN_BLOCK
    nlat = B * S // tm
    spb = S // tm

    def row(i):
        return (i, 0)

    def mod_row(i):
        return (jnp.minimum(i // spb, B), 0, 0)

    def tab_row(i):
        return (jnp.where(i < nlat, i % spb, spb), 0)

    const = lambda i: (0, 0)
    widths = (512, 128, 128, 1536, 2048, 1024, 1024, 512, 3072)
    dtypes = (BF16, BF16, BF16, BF16, F32, BF16, BF16, BF16, BF16)
    transposed = (0, 2, 5, 7)
    return pl.pallas_call(
        _proj_kernel,
        grid=(n // tm,),
        in_specs=[
            pl.BlockSpec((tm, D_MODEL), row),
            pl.BlockSpec((None, 6, D_MODEL), mod_row),
            pl.BlockSpec((1, D_MODEL), const),
            pl.BlockSpec((D_MODEL, IN_W_PACKED), const),
            pl.BlockSpec(wuq_p.shape, const),
            pl.BlockSpec(wukv_p.shape, const),
            pl.BlockSpec((1, C_Q_LORA), const),
            pl.BlockSpec((1, C_KV_LORA), const),
            pl.BlockSpec((8, 2 * B_W), const),
            pl.BlockSpec((tm, 384), tab_row),
            pl.BlockSpec((tm, 384), tab_row),
        ],
        out_specs=[pl.BlockSpec((w, tm), lambda i: (0, i)) if k in transposed else pl.BlockSpec((tm, w), row)
                   for k, w in enumerate(widths)],
        out_shape=[jax.ShapeDtypeStruct((w, n) if k in transposed else (n, w), dt)
                   for k, (w, dt) in enumerate(zip(widths, dtypes))],
        compiler_params=_cparams(("parallel",)),
        name="in_proj",
    )(x, mod, g1, w_in_p, wuq_p, wukv_p, gq, gkv, lbp, taba, tabc)


def _gqa_kernel(qt_ref, k0_ref, k1_ref, k2_ref, k3_ref, kx_ref, v0_ref, v1_ref, v2_ref, v3_ref, vx_ref, sink_ref,
                o_ref, bias_ref, s_ref, p_ref, *, nlat_blocks, seq):
    j = pl.program_id(1)
    nband = 4 * A_BLOCK
    nk = bias_ref.shape[0]
    rows = lax.broadcasted_iota(jnp.int32, (nk, A_QBLOCK), 0)
    cols = lax.broadcasted_iota(jnp.int32, (nk, A_QBLOCK), 1)
    qpos = j * A_QBLOCK + cols
    kpos = j * A_QBLOCK - A_BLOCK + rows
    kend = jnp.where(j < nlat_blocks, seq, 0)
    valid = ((kpos >= 0) & (kpos < kend) & (jnp.abs(qpos - kpos) <= A_WINDOW)) | (rows >= nband)
    bias_ref[...] = jnp.where(valid, 0.0, -jnp.inf)
    k = jnp.concatenate([k0_ref[...], k1_ref[...], k2_ref[...], k3_ref[...], kx_ref[...]], axis=0)
    vt = jnp.concatenate([v0_ref[...], v1_ref[...], v2_ref[...], v3_ref[...], vx_ref[...]], axis=1)
    for hd in range(A_HEADS):
        g = hd // A_GROUP
        hs = slice(HEAD_DIM * hd, HEAD_DIM * (hd + 1))
        gs = slice(HEAD_DIM * g, HEAD_DIM * (g + 1))
        s_ref[...] = jnp.dot(k[:, gs], qt_ref[hs, :], preferred_element_type=F32) + bias_ref[...]
        snk = sink_ref[hd:hd + 1, :]
        m = jnp.maximum(jnp.max(s_ref[...], axis=0, keepdims=True), snk)
        p = jnp.exp(s_ref[...] - m)
        denom = jnp.sum(p, axis=0, keepdims=True) + jnp.exp(snk - m)
        p_ref[...] = p.astype(BF16)
        o = jnp.dot(vt[gs, :], p_ref[...], preferred_element_type=F32) / denom
        o_ref[hs, :] = o.astype(BF16)


def _window_gqa(qat, ka, vat, sink, dims, need_ctx):
    B, S, L = dims
    n = ka.shape[0]
    assert L == A_QBLOCK
    nb = S // A_BLOCK
    nqb = S // A_QBLOCK
    nq = nqb + (1 if need_ctx else 0)
    nk = 4 * A_BLOCK + L

    def q_col(b, j):
        return (0, jnp.where(j < nqb, b * nqb + j, B * nqb + b))

    def kblk(delta):
        return lambda b, j: b * nb + jnp.clip(2 * j + delta, 0, nb - 1)

    k_specs = [pl.BlockSpec((A_BLOCK, 128), (lambda f: lambda b, j: (f(b, j), 0))(kblk(dl))) for dl in (-1, 0, 1, 2)]
    v_specs = [pl.BlockSpec((128, A_BLOCK), (lambda f: lambda b, j: (0, f(b, j)))(kblk(dl))) for dl in (-1, 0, 1, 2)]
    return pl.pallas_call(
        functools.partial(_gqa_kernel, nlat_blocks=nqb, seq=S),
        grid=(B, nq),
        in_specs=[pl.BlockSpec((512, A_QBLOCK), q_col)] + k_specs
        + [pl.BlockSpec((L, 128), lambda b, j: (B * S // L + b, 0))] + v_specs
        + [pl.BlockSpec((128, L), lambda b, j: (0, B * S // L + b)),
           pl.BlockSpec((8, A_QBLOCK), lambda b, j: (0, 0))],
        out_specs=pl.BlockSpec((512, A_QBLOCK), q_col),
        out_shape=jax.ShapeDtypeStruct((512, n if need_ctx else B * S), BF16),
        scratch_shapes=[pltpu.VMEM((nk, A_QBLOCK), F32), pltpu.VMEM((nk, A_QBLOCK), F32),
                        pltpu.VMEM((nk, A_QBLOCK), BF16)],
        compiler_params=_cparams(("parallel", "parallel")),
        name="window_gqa",
    )(qat, ka, ka, ka, ka, ka, vat, vat, vat, vat, vat, sink)


def _hgrn_kernel(q_ref, v_ref, g_ref, k_ref, o_ref, st_ref):
    d = pl.program_id(1)
    c = pl.program_id(2)
    C = B_CHUNK

    @pl.when(c == 0)
    def _():
        st_ref[...] = jnp.zeros_like(st_ref)

    rows = lax.broadcasted_iota(jnp.int32, (C, C), 0)
    cols = lax.broadcasted_iota(jnp.int32, (C, C), 1)
    fwd = d == 0
    causal = (rows - cols) * jnp.where(fwd, 1, -1) >= 0
    g = g_ref[...]
    bc = jnp.dot(causal.astype(F32), g, preferred_element_type=F32, precision=HIGHEST)
    tot = jnp.sum(g, axis=0, keepdims=True)
    rho = jnp.where(fwd, bc[C // 2 - 1:C // 2, :], bc[C // 2:C // 2 + 1, :])
    q = q_ref[...].astype(F32)
    key = k_ref[...]
    v = v_ref[...]
    qe = (q * jnp.exp(bc - rho)).astype(BF16)
    ke = (key * jnp.exp(rho - bc)).astype(BF16)
    qs = (q * jnp.exp(bc)).astype(BF16)
    ks = (key * jnp.exp(tot - bc)).astype(BF16)
    dec = jnp.exp(tot)
    for hd in range(B_HEADS):
        sl = slice(B_DK * hd, B_DK * (hd + 1))
        st = st_ref[hd]
        att = lax.dot_general(qe[:, sl], ke[:, sl], (((1,), (1,)), ((), ())), preferred_element_type=F32)
        att = jnp.where(causal, att, 0.0).astype(BF16)
        o = lax.dot_general(qs[:, sl], st.astype(BF16), (((1,), (1,)), ((), ())), preferred_element_type=F32)
        o = o + jnp.dot(att, v[:, sl], preferred_element_type=F32)
        o_ref[:, sl] = o
        upd = lax.dot_general(v[:, sl], ks[:, sl], (((0,), (0,)), ((), ())), preferred_element_type=F32)
        st_ref[hd] = st * dec[:, sl] + upd


def _hgrn2_scan(bqig, gates, dims):
    B, S, L = dims
    n = bqig.shape[0]
    C = B_CHUNK
    ns, nl = S // C, L // C
    nch = ns + nl

    def rowblk(b, d, c):
        fwd_id = jnp.where(c < nl, ns + c, c - nl)
        cid = jnp.where(d == 0, fwd_id, nch - 1 - c)
        return jnp.where(cid < ns, b * ns + cid, B * ns + b * nl + (cid - ns))

    return pl.pallas_call(
        _hgrn_kernel,
        grid=(B, 2, nch),
        in_specs=[
            pl.BlockSpec((C, B_W), lambda b, d, c: (rowblk(b, d, c), 0)),
            pl.BlockSpec((C, B_W), lambda b, d, c: (rowblk(b, d, c), 1)),
            pl.BlockSpec((C, B_W), lambda b, d, c: (rowblk(b, d, c), d)),
            pl.BlockSpec((C, B_W), lambda b, d, c: (rowblk(b, d, c), 2 + d)),
        ],
        out_specs=pl.BlockSpec((None, C, B_W), lambda b, d, c: (d, rowblk(b, d, c), 0)),
        out_shape=jax.ShapeDtypeStruct((2, n, B_W), F32),
        scratch_shapes=[pltpu.VMEM((B_HEADS, B_DK, B_DK), F32)],
        compiler_params=_cparams(("parallel", "parallel", "arbitrary")),
        name="hgrn2_scan",
    )(bqig, bqig, gates, gates)


def _mla_kernel(qt_ref, k_ref, vt_ref, *rest, with_ctx):
    if with_ctx:
        kx_ref, vxt_ref, o_ref, m_ref, l_ref, acc_ref, s_ref, p_ref, a_ref = rest
    else:
        _, o_ref, m_ref, l_ref, acc_ref, s_ref, p_ref, a_ref = rest
    kstep = pl.program_id(2)
    tq = qt_ref.shape[1]

    def kv_pass(k_ref, vt_ref, first):
        nkeys = k_ref.shape[0]

        def scores(hd):
            sl = slice(C_HEAD_PAD * hd, C_HEAD_PAD * (hd + 1))
            s_ref[hd % 2, 0:nkeys, :] = jnp.dot(k_ref[:, sl], qt_ref[sl, :], preferred_element_type=F32)

        scores(0)
        for hd in range(C_HEADS):
            if hd + 1 < C_HEADS:
                scores(hd + 1)
            buf = hd % 2
            for c in range(tq // 128):
                cols = slice(128 * c, 128 * (c + 1))
                smax = jnp.max(s_ref[buf, 0:nkeys, cols], axis=0, keepdims=True)
                if first:
                    m_new = smax
                    p = jnp.exp(s_ref[buf, 0:nkeys, cols] - m_new)
                    l_ref[hd, :, cols] = jnp.sum(p, axis=0, keepdims=True)
                else:
                    m_old = m_ref[hd, :, cols]
                    m_new = jnp.maximum(m_old, smax)
                    alpha = jnp.exp(m_old - m_new)
                    p = jnp.exp(s_ref[buf, 0:nkeys, cols] - m_new)
                    l_ref[hd, :, cols] = alpha * l_ref[hd, :, cols] + jnp.sum(p, axis=0, keepdims=True)
                    a_ref[buf, :, cols] = alpha
                m_ref[hd, :, cols] = m_new
                p_ref[buf, 0:nkeys, cols] = p.astype(BF16)
            pv = jnp.dot(vt_ref[C_V * hd:C_V * (hd + 1), :], p_ref[buf, 0:nkeys, :], preferred_element_type=F32)
            if first:
                acc_ref[hd] = pv
            else:
                acc_ref[hd] = a_ref[buf] * acc_ref[hd] + pv

    if with_ctx:
        @pl.when(kstep == 0)
        def _():
            kv_pass(kx_ref, vxt_ref, True)

        kv_pass(k_ref, vt_ref, False)
    else:
        kv_pass(k_ref, vt_ref, True)

    @pl.when(kstep == pl.num_programs(2) - 1)
    def _():
        for hd in range(C_HEADS):
            o_ref[C_V * hd:C_V * (hd + 1), :] = (acc_ref[hd] / l_ref[hd]).astype(BF16)


def _mla_attention(qct, kc, vct, dims, need_ctx):
    B, S, L = dims
    n = kc.shape[0]
    tq = min(MLA_Q_BLOCK, S)
    tk = min(MLA_K_BLOCK, S)
    nq, nk = S // tq, S // tk
    hw = C_HEADS * C_HEAD_PAD
    vw = C_HEADS * C_V
    scratch = lambda t, nkeys: [
        pltpu.VMEM((C_HEADS, 1, t), F32), pltpu.VMEM((C_HEADS, 1, t), F32), pltpu.VMEM((C_HEADS, C_V, t), F32),
        pltpu.VMEM((2, nkeys, t), F32), pltpu.VMEM((2, nkeys, t), BF16), pltpu.VMEM((2, 1, t), F32)]
    ctx_row = lambda b, i, k: (B * S // L + b, 0)
    ctx_col = lambda b, i, k: (0, B * S // L + b)
    o_lat = pl.pallas_call(
        functools.partial(_mla_kernel, with_ctx=True),
        grid=(B, nq, nk),
        in_specs=[
            pl.BlockSpec((hw, tq), lambda b, i, k: (0, b * nq + i)),
            pl.BlockSpec((tk, hw), lambda b, i, k: (b * nk + k, 0)),
            pl.BlockSpec((vw, tk), lambda b, i, k: (0, b * nk + k)),
            pl.BlockSpec((L, hw), ctx_row),
            pl.BlockSpec((vw, L), ctx_col),
        ],
        out_specs=pl.BlockSpec((vw, tq), lambda b, i, k: (0, b * nq + i)),
        out_shape=jax.ShapeDtypeStruct((vw, n if need_ctx else B * S), BF16),
        scratch_shapes=scratch(tq, tk),
        compiler_params=_cparams(("parallel", "parallel", "arbitrary")),
        name="mla_latent",
    )(qct, kc, vct, kc, vct)
    if not need_ctx:
        return o_lat
    return pl.pallas_call(
        functools.partial(_mla_kernel, with_ctx=False),
        grid=(B, 1, 1),
        in_specs=[
            pl.BlockSpec((hw, L), ctx_col),
            pl.BlockSpec((L, hw), ctx_row),
            pl.BlockSpec((vw, L), ctx_col),
            pl.BlockSpec(memory_space=pl.ANY),
        ],
        out_specs=pl.BlockSpec((vw, L), ctx_col),
        out_shape=jax.ShapeDtypeStruct((vw, n), BF16),
        scratch_shapes=scratch(L, L),
        input_output_aliases={3: 0},
        compiler_params=_cparams(("parallel", "arbitrary", "arbitrary")),
        name="mla_context",
    )(qct, kc, vct, o_lat)


def _group_sum(x, ones_ref):
    hi = x.astype(BF16)
    lo = (x - hi.astype(F32)).astype(BF16)
    return (jnp.dot(hi, ones_ref[...], preferred_element_type=F32)
            + jnp.dot(lo, ones_ref[...], preferred_element_type=F32))


def _merge_kernel(x_ref, oa_ref, of_ref, ob_ref, bg_ref, oc_ref, gl_ref, wbr_ref, wout_ref, gn_ref, mod_ref,
                  g2_ref, ones_ref, xo_ref, h2_ref):
    ob = of_ref[...] + ob_ref[...]
    ms = _group_sum(ob * ob, ones_ref) * (1.0 / B_DK)
    obn = ob * lax.rsqrt(ms + EPS) * gn_ref[...]
    bg = bg_ref[...].astype(F32)
    bb = (obn * (bg * jax.nn.sigmoid(bg))).astype(BF16)
    branches = ((oa_ref[...], 0), (bb, 1), (oc_ref[...], 0))
    y = None
    for nbr, (br, axis) in enumerate(branches):
        gate = jax.nn.sigmoid(gl_ref[:, D_MODEL * nbr:D_MODEL * (nbr + 1)].astype(F32))
        t = gate * lax.dot_general(br, wbr_ref[nbr], (((axis,), (0,)), ((), ())), preferred_element_type=F32)
        y = t if y is None else y + t
    upd = jnp.dot(y.astype(BF16), wout_ref[...], preferred_element_type=F32)
    xn = x_ref[...] + mod_ref[2:3, :] * upd
    xo_ref[...] = xn
    h2 = _rms(xn, g2_ref[...]) * (1.0 + mod_ref[4:5, :]) + mod_ref[3:4, :]
    h2_ref[...] = h2.astype(BF16)


def _merge(x, oa, ohg, bqig, oc, gl, wbr, wout, gn, mod, g2, ones, dims, need_ctx):
    B, S, L = dims
    n = x.shape[0]
    tm = TOKEN_BLOCK
    spb = S // tm
    nblk = (n if need_ctx else B * S) // tm
    row = lambda i: (i, 0)
    const2 = lambda i: (0, 0)
    return pl.pallas_call(
        _merge_kernel,
        grid=(nblk,),
        in_specs=[
            pl.BlockSpec((tm, D_MODEL), row),
            pl.BlockSpec((512, tm), lambda i: (0, i)),
            pl.BlockSpec((None, tm, B_W), lambda i: (0, i, 0)),
            pl.BlockSpec((None, tm, B_W), lambda i: (1, i, 0)),
            pl.BlockSpec((tm, B_W), lambda i: (i, 2)),
            pl.BlockSpec((512, tm), lambda i: (0, i)),
            pl.BlockSpec((tm, N_BRANCH * D_MODEL), row),
            pl.BlockSpec((N_BRANCH, BRANCH_W, D_MODEL), lambda i: (0, 0, 0)),
            pl.BlockSpec((D_MODEL, D_MODEL), const2),
            pl.BlockSpec((1, B_W), const2),
            pl.BlockSpec((None, 6, D_MODEL), lambda i: (jnp.minimum(i // spb, B), 0, 0)),
            pl.BlockSpec((1, D_MODEL), const2),
            pl.BlockSpec((B_W, B_W), const2),
        ],
        out_specs=[pl.BlockSpec((tm, D_MODEL), row), pl.BlockSpec((tm, D_MODEL), row)],
        out_shape=[jax.ShapeDtypeStruct((nblk * tm, D_MODEL), F32), jax.ShapeDtypeStruct((nblk * tm, D_MODEL), BF16)],
        compiler_params=_cparams(("parallel",)),
        name="branch_merge",
    )(x, oa, ohg, ohg, bqig, oc, gl, wbr, wout, gn, mod, g2, ones)


def _moe_kernel(h_ref, x_ref, mod_ref, wr_ref, w13_ref, w2_ref, o_ref, comb_ref, y_ref):
    e = pl.program_id(1)
    h = h_ref[...]
    lane = lax.broadcasted_iota(jnp.int32, (h.shape[0], ROUTER_W), 1)

    @pl.when(e == 0)
    def _():
        logits = jnp.dot(h, wr_ref[...], preferred_element_type=F32)
        big = jnp.int32(ROUTER_W)
        is_grp = (lane >= N_EXPERTS) & (lane < N_EXPERTS + N_GROUPS)
        gl = jnp.where(is_grp, logits, -jnp.inf)
        gmax = jnp.max(gl, axis=-1, keepdims=True)
        gsel = jnp.min(jnp.where(gl == gmax, lane, big), axis=-1, keepdims=True) - N_EXPERTS
        gw = 1.0 / jnp.sum(jnp.exp(gl - gmax), axis=-1, keepdims=True)
        in_grp = (lane >= gsel * EXPERTS_PER_GROUP) & (lane < (gsel + 1) * EXPERTS_PER_GROUP)
        el = jnp.where(in_grp, logits, -jnp.inf)
        m1 = jnp.max(el, axis=-1, keepdims=True)
        i1 = jnp.min(jnp.where(el == m1, lane, big), axis=-1, keepdims=True)
        el2 = jnp.where(lane == i1, -jnp.inf, el)
        m2 = jnp.max(el2, axis=-1, keepdims=True)
        i2 = jnp.min(jnp.where(el2 == m2, lane, big), axis=-1, keepdims=True)
        e2 = jnp.exp(m2 - m1)
        w1 = gw / (1.0 + e2)
        w2 = gw * e2 / (1.0 + e2)
        comb_ref[...] = jnp.where(lane == i1, w1, 0.0) + jnp.where(lane == i2, w2, 0.0)
        y_ref[...] = jnp.zeros_like(y_ref)

    ce = jnp.sum(jnp.where(lane == e, comb_ref[...], 0.0), axis=-1, keepdims=True)
    h13 = jnp.dot(h, w13_ref[...], preferred_element_type=F32)
    a1 = h13[:, :D_EXPERT]
    a = a1 * jax.nn.sigmoid(a1) * h13[:, D_EXPERT:]
    y_ref[...] += jnp.dot((a * ce).astype(BF16), w2_ref[...], preferred_element_type=F32)

    @pl.when(e == N_EXPERTS - 1)
    def _():
        o_ref[...] = x_ref[...] + mod_ref[5:6, :] * y_ref[...]


def _moe(h2, x, mod, wr, w13, w2, dims, need_ctx):
    B, S, L = dims
    n = x.shape[0]
    tm = min(MOE_TOKEN_BLOCK, S, B * L)
    spb = S // tm
    nblk = (n if need_ctx else B * S) // tm
    row = lambda i, e: (i, 0)
    return pl.pallas_call(
        _moe_kernel,
        grid=(nblk, N_EXPERTS),
        in_specs=[
            pl.BlockSpec((tm, D_MODEL), row),
            pl.BlockSpec((tm, D_MODEL), row),
            pl.BlockSpec((None, 6, D_MODEL), lambda i, e: (jnp.minimum(i // spb, B), 0, 0)),
            pl.BlockSpec((D_MODEL, ROUTER_W), lambda i, e: (0, 0)),
            pl.BlockSpec((None, D_MODEL, 2 * D_EXPERT), lambda i, e: (e, 0, 0)),
            pl.BlockSpec((None, D_EXPERT, D_MODEL), lambda i, e: (e, 0, 0)),
        ],
        out_specs=pl.BlockSpec((tm, D_MODEL), row),
        out_shape=jax.ShapeDtypeStruct((nblk * tm, D_MODEL), F32),
        scratch_shapes=[pltpu.VMEM((tm, ROUTER_W), F32), pltpu.VMEM((tm, D_MODEL), F32)],
        compiler_params=_cparams(("parallel", "arbitrary")),
        name="hier_moe",
    )(h2, x, mod, wr, w13, w2)


def _final_kernel(x_ref, g_ref, o_ref):
    o_ref[...] = _rms(x_ref[...], g_ref[...])


def _final_norm(x, g, dims):
    B, S, L = dims
    tm = TOKEN_BLOCK
    return pl.pallas_call(
        _final_kernel,
        grid=(B * S // tm,),
        in_specs=[pl.BlockSpec((tm, D_MODEL), lambda i: (i, 0)), pl.BlockSpec((1, D_MODEL), lambda i: (0, 0))],
        out_specs=pl.BlockSpec((tm, D_MODEL), lambda i: (i, 0)),
        out_shape=jax.ShapeDtypeStruct((B * S, D_MODEL), F32),
        compiler_params=_cparams(("parallel",)),
        name="final_norm",
    )(x, g)


def _rope_tables(S, L):
    rows = S // GRID_W
    pos_r = np.repeat(np.arange(rows, dtype=np.float32), GRID_W)
    pos_c = np.tile(np.arange(GRID_W, dtype=np.float32), rows)

    def angles(rot_dim):
        nf = rot_dim // 4
        inv = jnp.asarray(ROPE_BASE, F32) ** (-jnp.arange(nf, dtype=F32) / nf)
        ang = jnp.concatenate([pos_r[:, None] * inv, pos_c[:, None] * inv], axis=-1)
        return jnp.cos(ang), jnp.sin(ang)

    def with_ctx(cos, s_lo, s_hi):
        ident = jnp.concatenate([jnp.ones((L, 128), F32), jnp.zeros((L, 256), F32)], axis=-1)
        return jnp.concatenate([jnp.concatenate([cos, s_lo, s_hi], axis=-1), ident], axis=0)

    cos, sin = angles(HEAD_DIM)
    z = jnp.zeros_like(sin)
    taba = with_ctx(jnp.tile(cos, (1, 4)), jnp.tile(jnp.concatenate([-sin, z], -1), (1, 2)),
                    jnp.tile(jnp.concatenate([z, sin], -1), (1, 2)))
    cos, sin = angles(C_ROPE)
    z = jnp.zeros_like(sin)
    one64, zero64, zero32 = jnp.ones((S, 64), F32), jnp.zeros((S, 64), F32), jnp.zeros((S, 32), F32)
    tabc = with_ctx(jnp.concatenate([one64, cos, cos, one64[:, :32]], -1),
                    jnp.concatenate([zero64, -sin, z, zero32], -1),
                    jnp.concatenate([zero64, z, sin, zero32], -1))
    return taba, tabc


def _pack_w_in(w):
    pad = lambda k: jnp.zeros((w.shape[0], k), w.dtype)
    return jnp.concatenate([w[:, :3712], pad(64), w[:, 3712:3744], pad(32), w[:, 3744:]], axis=-1).astype(BF16)


def _pack_w_uq(w):
    w = w.reshape(C_Q_LORA, C_HEADS, C_NOPE + C_ROPE)
    w = jnp.pad(w, ((0, 0), (0, 0), (0, C_HEAD_PAD - C_NOPE - C_ROPE)))
    return w.reshape(C_Q_LORA, C_HEADS * C_HEAD_PAD).astype(BF16)


def _pack_w_ukv(w):
    w = w.reshape(C_KV_LORA, C_HEADS, C_NOPE + C_V)
    wk = jnp.pad(w[:, :, :C_NOPE], ((0, 0), (0, 0), (0, C_HEAD_PAD - C_NOPE))).reshape(C_KV_LORA, -1)
    wv = w[:, :, C_NOPE:].reshape(C_KV_LORA, -1)
    return jnp.concatenate([wk, wv], axis=-1).astype(BF16)


def kernel(x, c, ctx, c_ctx, w_mod, b_mod, g_norm1, g_norm2, w_in, a_sink, b_lb_logits, b_onorm, c_qnorm, c_kvnorm,
           w_uq, w_ukv, w_br, w_out, w_rg, w_re, w1, w3, w2, g_final):
    B, S, _ = x.shape
    L = ctx.shape[1]
    depth = w_in.shape[0]
    assert L == TOKEN_BLOCK and S % MLA_Q_BLOCK == 0 and S % GRID_W == 0
    dims = (B, S, L)

    xs = jnp.concatenate([x.reshape(B * S, D_MODEL), ctx.reshape(B * L, D_MODEL)], axis=0)
    cc = jnp.zeros((8, D_MODEL), F32).at[:B].set(c).at[B].set(c_ctx)
    mod_all = _modulation(cc, w_mod, b_mod).reshape(depth, 8, 6, D_MODEL)

    lb_all = jnp.cumsum(jax.nn.softmax(b_lb_logits.astype(F32), axis=0), axis=0)
    lb_all = (lb_all - lb_all[0:1]).reshape(depth, 1, 2 * B_W)
    lbp_all = jnp.concatenate([jnp.log(lb_all), jnp.log1p(-lb_all), 1.0 - lb_all,
                               jnp.zeros((depth, 5, 2 * B_W), F32)], axis=1)

    taba, tabc = _rope_tables(S, L)
    ones = jnp.kron(jnp.eye(B_HEADS, dtype=F32), jnp.ones((B_DK, B_DK), F32)).astype(BF16)

    for l in range(depth):
        need_ctx = l < depth - 1
        mod = mod_all[l]
        wr = jnp.concatenate([w_re[l], w_rg[l], jnp.zeros((D_MODEL, ROUTER_W - N_EXPERTS - N_GROUPS), F32)],
                             axis=-1).astype(BF16)
        w13 = jnp.concatenate([w1[l], w3[l]], axis=-1).astype(BF16)
        sink = jnp.broadcast_to(a_sink[l].astype(F32)[:, None], (A_HEADS, A_QBLOCK))

        qa, ka, va, bqig, gates, qc, kc, vc, gl = _projection(
            xs, mod, g_norm1[l][None], _pack_w_in(w_in[l]), _pack_w_uq(w_uq[l]), _pack_w_ukv(w_ukv[l]),
            c_qnorm[l][None], c_kvnorm[l][None], lbp_all[l], taba, tabc, dims)
        oa = _window_gqa(qa, ka, va, sink, dims, need_ctx)
        ohg = _hgrn2_scan(bqig, gates, dims)
        oc = _mla_attention(qc, kc, vc, dims, need_ctx)
        xs, h2 = _merge(xs, oa, ohg, bqig, oc, gl, w_br[l].astype(BF16), w_out[l].astype(BF16), b_onorm[l][None],
                        mod, g_norm2[l][None], ones, dims, need_ctx)
        xs = _moe(h2, xs, mod, wr, w13, w2[l].astype(BF16), dims, need_ctx)

    return _final_norm(xs, g_final[None], dims).reshape(B, S, D_MODEL)
```

```python
import functools

import jax
import jax.numpy as jnp
import numpy as np
from jax import lax
from jax.experimental import pallas as pl
from jax.experimental.pallas import tpu as pltpu

F32 = jnp.float32
BF16 = jnp.bfloat16
HIGHEST = lax.Precision.HIGHEST

D_MODEL = 1024
GRID_W = 64
HEAD_DIM = 64
ROPE_BASE = 10000.0
EPS = 1e-6
A_HEADS = 8
A_KV_HEADS = 2
A_GROUP = A_HEADS // A_KV_HEADS
A_WINDOW = 128
A_BLOCK = A_WINDOW
A_QBLOCK = 2 * A_BLOCK
B_HEADS = 8
B_DK = 64
B_W = B_HEADS * B_DK
B_CHUNK = 64
B_BLOCK = 256
C_HEADS = 8
C_NOPE = 64
C_ROPE = 32
C_V = 64
C_Q_LORA = 256
C_KV_LORA = 128
C_HEAD_PAD = 128
N_BRANCH = 3
BRANCH_W = 512
N_GROUPS = 4
EXPERTS_PER_GROUP = 8
N_EXPERTS = N_GROUPS * EXPERTS_PER_GROUP
D_EXPERT = 256
ROUTER_W = 128
V_SLAB = 80
LOG2E = 1.4426950408889634

OFF_AQ, OFF_AK, OFF_AV = 0, 512, 640
OFF_BQ, OFF_BI, OFF_BZF, OFF_BZB, OFF_BG = 768, 1280, 1792, 2304, 2816
OFF_CQ, OFF_CKV, OFF_CKR, OFF_GL = 3328, 3584, 3712, 3840
IN_W_PACKED = OFF_GL + N_BRANCH * D_MODEL

TOKEN_BLOCK = 256
MOE_TOKEN_BLOCK = 1024
MOE_EXPERTS_PER_STEP = 4
MLA_Q_BLOCK = 512
MLA_K_BLOCK = 1024
VMEM_LIMIT = 56 * 1024 * 1024


def _cparams(sem):
    return pltpu.CompilerParams(dimension_semantics=sem, vmem_limit_bytes=VMEM_LIMIT)


def _mod_kernel(cc_ref, w_ref, b_ref, o_ref):
    cc = cc_ref[...]
    a = cc * jax.nn.sigmoid(cc)
    o_ref[...] = jnp.dot(a, w_ref[...], preferred_element_type=F32, precision=HIGHEST) + b_ref[...]


def _modulation(cc, w_mod, b_mod):
    depth = w_mod.shape[0]
    nj = 6
    return pl.pallas_call(
        _mod_kernel,
        grid=(depth, nj),
        in_specs=[
            pl.BlockSpec((8, D_MODEL), lambda l, j: (0, 0)),
            pl.BlockSpec((None, D_MODEL, D_MODEL), lambda l, j: (l, 0, j)),
            pl.BlockSpec((None, 1, D_MODEL), lambda l, j: (l, 0, j)),
        ],
        out_specs=pl.BlockSpec((None, 8, D_MODEL), lambda l, j: (l, 0, j)),
        out_shape=jax.ShapeDtypeStruct((depth, 8, 6 * D_MODEL), F32),
        compiler_params=_cparams(("arbitrary", "arbitrary")),
        name="adaln_mod",
    )(cc, w_mod, b_mod.reshape(depth, 1, 6 * D_MODEL))


def _rms(x, g):
    return x * lax.rsqrt(jnp.mean(x * x, axis=-1, keepdims=True) + EPS) * g


def _rope(v, tab_ref, half):
    n = v.shape[-1]
    cos = tab_ref[:, 0:128]
    s_lo = tab_ref[:, 128:256]
    s_hi = tab_ref[:, 256:384]
    return v * cos + pltpu.roll(v, n - half, 1) * s_lo + pltpu.roll(v, half, 1) * s_hi


def _store_v_slabs(ref, vt, heads):
    ones = jnp.ones((V_SLAB - HEAD_DIM, vt.shape[1]), BF16)
    for hd in range(heads):
        ref[V_SLAB * hd:V_SLAB * hd + HEAD_DIM, :] = vt[HEAD_DIM * hd:HEAD_DIM * (hd + 1), :].astype(BF16)
        ref[V_SLAB * hd + HEAD_DIM:V_SLAB * (hd + 1), :] = ones


def _proj_kernel(x_ref, mod_ref, g1_ref, w_ref, wuq_ref, wukv_ref, gq_ref, gkv_ref, lbp_ref, taba_ref, tabc_ref,
                 qa_ref, ka_ref, va_ref, bqig_ref, gates_ref, qc_ref, kc_ref, vc_ref, gl_ref):
    x = x_ref[...]
    h = _rms(x, g1_ref[...]) * (1.0 + mod_ref[1:2, :]) + mod_ref[0:1, :]
    hb = h.astype(BF16)

    def seg(off, width):
        return jnp.dot(hb, w_ref[:, off:off + width], preferred_element_type=F32)

    aq = seg(OFF_AQ, 512) * (HEAD_DIM ** -0.5 * LOG2E)
    for j in range(4):
        qa_ref[128 * j:128 * (j + 1), :] = _rope(aq[:, 128 * j:128 * (j + 1)], taba_ref, 32).T.astype(BF16)
    ka_ref[...] = _rope(seg(OFF_AK, 128), taba_ref, 32).astype(BF16)
    _store_v_slabs(va_ref, seg(OFF_AV, 128).T, A_KV_HEADS)

    bqig_ref[:, 0:512] = seg(OFF_BQ, 512).astype(BF16)
    bqig_ref[:, 512:1024] = seg(OFF_BI, 512).astype(BF16)
    bqig_ref[:, 1024:1536] = seg(OFF_BG, 512).astype(BF16)
    for d, off in enumerate((OFF_BZF, OFF_BZB)):
        z = seg(off, 512)
        log_lb = lbp_ref[0:1, 512 * d:512 * (d + 1)]
        log1m_lb = lbp_ref[1:2, 512 * d:512 * (d + 1)]
        one_m_lb = lbp_ref[2:3, 512 * d:512 * (d + 1)]
        e = jnp.exp(-jnp.abs(z))
        log_sig = jnp.minimum(z, 0.0) - jnp.log(1.0 + e)
        b = log1m_lb + log_sig
        mx = jnp.maximum(log_lb, b)
        logf = mx + jnp.log(1.0 + jnp.exp(-jnp.abs(log_lb - b)))
        r = 1.0 / (1.0 + e)
        key = one_m_lb * jnp.where(z >= 0.0, e * r, r)
        gates_ref[:, 512 * d:512 * (d + 1)] = logf
        gates_ref[:, 1024 + 512 * d:1024 + 512 * (d + 1)] = key

    cq = _rms(seg(OFF_CQ, C_Q_LORA), gq_ref[...]).astype(BF16)
    qh = jnp.dot(cq, wuq_ref[...], preferred_element_type=F32) * ((C_NOPE + C_ROPE) ** -0.5 * LOG2E)
    ckv = _rms(seg(OFF_CKV, C_KV_LORA), gkv_ref[...]).astype(BF16)
    kvh = jnp.dot(ckv, wukv_ref[...], preferred_element_type=F32)
    kr = _rope(seg(OFF_CKR, 128), tabc_ref, 16)
    for j in range(C_HEADS):
        sl = slice(C_HEAD_PAD * j, C_HEAD_PAD * (j + 1))
        qc_ref[sl, :] = _rope(qh[:, sl], tabc_ref, 16).T.astype(BF16)
        kc_ref[:, sl] = (kvh[:, sl] + kr).astype(BF16)
    _store_v_slabs(vc_ref, kvh[:, C_HEADS * C_HEAD_PAD:].T, C_HEADS)

    for j in range(6):
        gl_ref[:, 512 * j:512 * (j + 1)] = seg(OFF_GL + 512 * j, 512).astype(BF16)


def _projection(x, mod, g1, w_in_p, wuq_p, wukv_p, gq, gkv, lbp, taba, tabc, dims):
    B, S, L = dims
    n = x.shape[0]
    tm = TOKEN_BLOCK
    nlat = B * S // tm
    spb = S // tm

    def row(i):
        return (i, 0)

    def mod_row(i):
        return (jnp.minimum(i // spb, B), 0, 0)

    def tab_row(i):
        return (jnp.where(i < nlat, i % spb, spb), 0)

    const = lambda i: (0, 0)
    widths = (512, 128, A_KV_HEADS * V_SLAB, 1536, 2048, 1024, 1024, C_HEADS * V_SLAB, 3072)
    dtypes = (BF16, BF16, BF16, BF16, F32, BF16, BF16, BF16, BF16)
    transposed = (0, 2, 5, 7)
    return pl.pallas_call(
        _proj_kernel,
        grid=(n // tm,),
        in_specs=[
            pl.BlockSpec((tm, D_MODEL), row),
            pl.BlockSpec((None, 6, D_MODEL), mod_row),
            pl.BlockSpec((1, D_MODEL), const),
            pl.BlockSpec((D_MODEL, IN_W_PACKED), const),
            pl.BlockSpec(wuq_p.shape, const),
            pl.BlockSpec(wukv_p.shape, const),
            pl.BlockSpec((1, C_Q_LORA), const),
            pl.BlockSpec((1, C_KV_LORA), const),
            pl.BlockSpec((8, 2 * B_W), const),
            pl.BlockSpec((tm, 384), tab_row),
            pl.BlockSpec((tm, 384), tab_row),
        ],
        out_specs=[pl.BlockSpec((w, tm), lambda i: (0, i)) if k in transposed else pl.BlockSpec((tm, w), row)
                   for k, w in enumerate(widths)],
        out_shape=[jax.ShapeDtypeStruct((w, n) if k in transposed else (n, w), dt)
                   for k, (w, dt) in enumerate(zip(widths, dtypes))],
        compiler_params=_cparams(("parallel",)),
        name="in_proj",
    )(x, mod, g1, w_in_p, wuq_p, wukv_p, gq, gkv, lbp, taba, tabc)


def _gqa_kernel(qt_ref, k0_ref, k1_ref, k2_ref, k3_ref, kx_ref, v0_ref, v1_ref, v2_ref, v3_ref, vx_ref, sink_ref,
                o_ref, bias_ref, s_ref, p_ref, *, nlat_blocks, seq):
    j = pl.program_id(1)
    nband = 4 * A_BLOCK
    nk = bias_ref.shape[0]
    rows = lax.broadcasted_iota(jnp.int32, (nk, A_QBLOCK), 0)
    cols = lax.broadcasted_iota(jnp.int32, (nk, A_QBLOCK), 1)
    qpos = j * A_QBLOCK + cols
    kpos = j * A_QBLOCK - A_BLOCK + rows
    kend = jnp.where(j < nlat_blocks, seq, 0)
    valid = ((kpos >= 0) & (kpos < kend) & (jnp.abs(qpos - kpos) <= A_WINDOW)) | (rows >= nband)
    bias_ref[...] = jnp.where(valid, 0.0, -jnp.inf)
    k = jnp.concatenate([k0_ref[...], k1_ref[...], k2_ref[...], k3_ref[...], kx_ref[...]], axis=0)
    vt = jnp.concatenate([v0_ref[...], v1_ref[...], v2_ref[...], v3_ref[...], vx_ref[...]], axis=1)
    for hd in range(A_HEADS):
        g = hd // A_GROUP
        hs = slice(HEAD_DIM * hd, HEAD_DIM * (hd + 1))
        s_ref[...] = (jnp.dot(k[:, HEAD_DIM * g:HEAD_DIM * (g + 1)], qt_ref[hs, :], preferred_element_type=F32)
                      + bias_ref[...])
        snk = sink_ref[hd:hd + 1, :]
        m = jnp.maximum(jnp.max(s_ref[...], axis=0, keepdims=True), snk)
        p_ref[...] = jnp.exp2(s_ref[...] - m).astype(BF16)
        pv = jnp.dot(vt[V_SLAB * g:V_SLAB * (g + 1), :], p_ref[...], preferred_element_type=F32)
        denom = pv[HEAD_DIM:HEAD_DIM + 1, :] + jnp.exp2(snk - m)
        o_ref[hs, :] = (pv[0:HEAD_DIM, :] / denom).astype(BF16)


def _window_gqa(qat, ka, vat, sink, dims, need_ctx):
    B, S, L = dims
    n = ka.shape[0]
    assert L == A_QBLOCK
    nb = S // A_BLOCK
    nqb = S // A_QBLOCK
    nq = nqb + (1 if need_ctx else 0)
    nk = 4 * A_BLOCK + L

    def q_col(b, j):
        return (0, jnp.where(j < nqb, b * nqb + j, B * nqb + b))

    def kblk(delta):
        return lambda b, j: b * nb + jnp.clip(2 * j + delta, 0, nb - 1)

    k_specs = [pl.BlockSpec((A_BLOCK, 128), (lambda f: lambda b, j: (f(b, j), 0))(kblk(dl))) for dl in (-1, 0, 1, 2)]
    vs = A_KV_HEADS * V_SLAB
    v_specs = [pl.BlockSpec((vs, A_BLOCK), (lambda f: lambda b, j: (0, f(b, j)))(kblk(dl))) for dl in (-1, 0, 1, 2)]
    return pl.pallas_call(
        functools.partial(_gqa_kernel, nlat_blocks=nqb, seq=S),
        grid=(B, nq),
        in_specs=[pl.BlockSpec((512, A_QBLOCK), q_col)] + k_specs
        + [pl.BlockSpec((L, 128), lambda b, j: (B * S // L + b, 0))] + v_specs
        + [pl.BlockSpec((vs, L), lambda b, j: (0, B * S // L + b)),
           pl.BlockSpec((8, A_QBLOCK), lambda b, j: (0, 0))],
        out_specs=pl.BlockSpec((512, A_QBLOCK), q_col),
        out_shape=jax.ShapeDtypeStruct((512, n if need_ctx else B * S), BF16),
        scratch_shapes=[pltpu.VMEM((nk, A_QBLOCK), F32), pltpu.VMEM((nk, A_QBLOCK), F32),
                        pltpu.VMEM((nk, A_QBLOCK), BF16)],
        compiler_params=_cparams(("parallel", "parallel")),
        name="window_gqa",
    )(qat, ka, ka, ka, ka, ka, vat, vat, vat, vat, vat, sink)


def _hgrn_chunk(q_ref, v_ref, g_ref, k_ref, o_ref, st_ref, r0, reverse):
    C = B_CHUNK
    rs = slice(r0, r0 + C)
    rows = lax.broadcasted_iota(jnp.int32, (C, C), 0)
    cols = lax.broadcasted_iota(jnp.int32, (C, C), 1)
    causal = (rows <= cols) if reverse else (rows >= cols)
    g = g_ref[rs, :]
    bc = jnp.dot(causal.astype(F32), g, preferred_element_type=F32, precision=HIGHEST)
    tot = jnp.sum(g, axis=0, keepdims=True)
    mid = C // 2 if reverse else C // 2 - 1
    rho = bc[mid:mid + 1, :]
    q = q_ref[rs, :].astype(F32)
    key = k_ref[rs, :]
    v = v_ref[rs, :]
    qe = (q * jnp.exp(bc - rho)).astype(BF16)
    ke = (key * jnp.exp(rho - bc)).astype(BF16)
    qs = (q * jnp.exp(bc)).astype(BF16)
    ks = (key * jnp.exp(tot - bc)).astype(BF16)
    dec = jnp.exp(tot)
    for hd in range(B_HEADS):
        sl = slice(B_DK * hd, B_DK * (hd + 1))
        st = st_ref[hd]
        att = lax.dot_general(qe[:, sl], ke[:, sl], (((1,), (1,)), ((), ())), preferred_element_type=F32)
        att = jnp.where(causal, att, 0.0).astype(BF16)
        o = lax.dot_general(qs[:, sl], st.astype(BF16), (((1,), (1,)), ((), ())), preferred_element_type=F32)
        o = o + jnp.dot(att, v[:, sl], preferred_element_type=F32)
        o_ref[rs, sl] = o
        upd = lax.dot_general(v[:, sl], ks[:, sl], (((0,), (0,)), ((), ())), preferred_element_type=F32)
        st_ref[hd] = st * dec[:, sl] + upd


def _hgrn_kernel(qf_ref, vf_ref, gf_ref, kf_ref, qb_ref, vb_ref, gb_ref, kb_ref, of_ref, ob_ref, stf_ref, stb_ref):
    @pl.when(pl.program_id(1) == 0)
    def _():
        stf_ref[...] = jnp.zeros_like(stf_ref)
        stb_ref[...] = jnp.zeros_like(stb_ref)

    nchunk = qf_ref.shape[0] // B_CHUNK
    for i in range(nchunk):
        _hgrn_chunk(qf_ref, vf_ref, gf_ref, kf_ref, of_ref, stf_ref, B_CHUNK * i, False)
        _hgrn_chunk(qb_ref, vb_ref, gb_ref, kb_ref, ob_ref, stb_ref, B_CHUNK * (nchunk - 1 - i), True)


def _hgrn2_scan(bqig, gates, dims):
    B, S, L = dims
    n = bqig.shape[0]
    T = B_BLOCK
    assert L == T
    ns = S // T

    def fwd_blk(b, c):
        return jnp.where(c == 0, B * ns + b, b * ns + c - 1)

    def bwd_blk(b, c):
        return jnp.where(c == 0, B * ns + b, b * ns + ns - c)

    def specs(blk, d):
        return [pl.BlockSpec((T, B_W), lambda b, c: (blk(b, c), 0)),
                pl.BlockSpec((T, B_W), lambda b, c: (blk(b, c), 1)),
                pl.BlockSpec((T, B_W), lambda b, c: (blk(b, c), d)),
                pl.BlockSpec((T, B_W), lambda b, c: (blk(b, c), 2 + d))]

    return pl.pallas_call(
        _hgrn_kernel,
        grid=(B, ns + 1),
        in_specs=specs(fwd_blk, 0) + specs(bwd_blk, 1),
        out_specs=[pl.BlockSpec((T, B_W), lambda b, c: (fwd_blk(b, c), 0)),
                   pl.BlockSpec((T, B_W), lambda b, c: (bwd_blk(b, c), 0))],
        out_shape=[jax.ShapeDtypeStruct((n, B_W), F32), jax.ShapeDtypeStruct((n, B_W), F32)],
        scratch_shapes=[pltpu.VMEM((B_HEADS, B_DK, B_DK), F32), pltpu.VMEM((B_HEADS, B_DK, B_DK), F32)],
        compiler_params=_cparams(("parallel", "arbitrary")),
        name="hgrn2_scan",
    )(bqig, bqig, gates, gates, bqig, bqig, gates, gates)


def _mla_kernel(qt_ref, k_ref, vt_ref, *rest, with_ctx):
    if with_ctx:
        kx_ref, vxt_ref, o_ref, m_ref, acc_ref, *bufs = rest
    else:
        _, o_ref, m_ref, acc_ref, *bufs = rest
    s_refs, p_refs, a_refs = bufs[0:2], bufs[2:4], bufs[4:6]
    kstep = pl.program_id(2)
    tq = qt_ref.shape[1]

    def kv_pass(k_ref, vt_ref, first):
        nkeys = k_ref.shape[0]

        def scores(hd):
            sl = slice(C_HEAD_PAD * hd, C_HEAD_PAD * (hd + 1))
            s_refs[hd % 2][0:nkeys, :] = jnp.dot(k_ref[:, sl], qt_ref[sl, :], preferred_element_type=F32)

        scores(0)
        for hd in range(C_HEADS):
            if hd + 1 < C_HEADS:
                scores(hd + 1)
            s_ref, p_ref, a_ref = s_refs[hd % 2], p_refs[hd % 2], a_refs[hd % 2]
            for c in range(tq // 128):
                cols = slice(128 * c, 128 * (c + 1))
                smax = jnp.max(s_ref[0:nkeys, cols], axis=0, keepdims=True)
                if first:
                    m_new = smax
                else:
                    m_old = m_ref[hd, :, cols]
                    m_new = jnp.maximum(m_old, smax)
                    a_ref[:, cols] = jnp.exp2(m_old - m_new)
                m_ref[hd, :, cols] = m_new
                p_ref[0:nkeys, cols] = jnp.exp2(s_ref[0:nkeys, cols] - m_new).astype(BF16)
            pv = jnp.dot(vt_ref[V_SLAB * hd:V_SLAB * (hd + 1), :], p_ref[0:nkeys, :], preferred_element_type=F32)
            if first:
                acc_ref[hd] = pv
            else:
                acc_ref[hd] = a_ref[...] * acc_ref[hd] + pv

    if with_ctx:
        @pl.when(kstep == 0)
        def _():
            kv_pass(kx_ref, vxt_ref, True)

        kv_pass(k_ref, vt_ref, False)
    else:
        kv_pass(k_ref, vt_ref, True)

    @pl.when(kstep == pl.num_programs(2) - 1)
    def _():
        for hd in range(C_HEADS):
            o_ref[C_V * hd:C_V * (hd + 1), :] = (acc_ref[hd, 0:C_V, :] / acc_ref[hd, C_V:C_V + 1, :]).astype(BF16)


def _mla_attention(qct, kc, vct, dims, need_ctx):
    B, S, L = dims
    n = kc.shape[0]
    tq = min(MLA_Q_BLOCK, S)
    tk = min(MLA_K_BLOCK, S)
    nq, nk = S // tq, S // tk
    hw = C_HEADS * C_HEAD_PAD
    vw = C_HEADS * C_V
    vs = C_HEADS * V_SLAB
    scratch = lambda t, nkeys: [
        pltpu.VMEM((C_HEADS, 1, t), F32), pltpu.VMEM((C_HEADS, V_SLAB, t), F32),
        pltpu.VMEM((nkeys, t), F32), pltpu.VMEM((nkeys, t), F32),
        pltpu.VMEM((nkeys, t), BF16), pltpu.VMEM((nkeys, t), BF16),
        pltpu.VMEM((1, t), F32), pltpu.VMEM((1, t), F32)]
    ctx_row = lambda b, i, k: (B * S // L + b, 0)
    ctx_col = lambda b, i, k: (0, B * S // L + b)
    o_lat = pl.pallas_call(
        functools.partial(_mla_kernel, with_ctx=True),
        grid=(B, nq, nk),
        in_specs=[
            pl.BlockSpec((hw, tq), lambda b, i, k: (0, b * nq + i)),
            pl.BlockSpec((tk, hw), lambda b, i, k: (b * nk + k, 0)),
            pl.BlockSpec((vs, tk), lambda b, i, k: (0, b * nk + k)),
            pl.BlockSpec((L, hw), ctx_row),
            pl.BlockSpec((vs, L), ctx_col),
        ],
        out_specs=pl.BlockSpec((vw, tq), lambda b, i, k: (0, b * nq + i)),
        out_shape=jax.ShapeDtypeStruct((vw, n if need_ctx else B * S), BF16),
        scratch_shapes=scratch(tq, tk),
        compiler_params=_cparams(("parallel", "parallel", "arbitrary")),
        name="mla_latent",
    )(qct, kc, vct, kc, vct)
    if not need_ctx:
        return o_lat
    return pl.pallas_call(
        functools.partial(_mla_kernel, with_ctx=False),
        grid=(B, 1, 1),
        in_specs=[
            pl.BlockSpec((hw, L), ctx_col),
            pl.BlockSpec((L, hw), ctx_row),
            pl.BlockSpec((vs, L), ctx_col),
            pl.BlockSpec(memory_space=pl.ANY),
        ],
        out_specs=pl.BlockSpec((vw, L), ctx_col),
        out_shape=jax.ShapeDtypeStruct((vw, n), BF16),
        scratch_shapes=scratch(L, L),
        input_output_aliases={3: 0},
        compiler_params=_cparams(("parallel", "arbitrary", "arbitrary")),
        name="mla_context",
    )(qct, kc, vct, o_lat)


def _group_sum(x, ones_ref):
    hi = x.astype(BF16)
    lo = (x - hi.astype(F32)).astype(BF16)
    return (jnp.dot(hi, ones_ref[...], preferred_element_type=F32)
            + jnp.dot(lo, ones_ref[...], preferred_element_type=F32))


def _merge_kernel(x_ref, oa_ref, of_ref, ob_ref, bg_ref, oc_ref, gl_ref, wbr_ref, wout_ref, gn_ref, mod_ref,
                  g2_ref, ones_ref, xo_ref, h2_ref):
    ob = of_ref[...] + ob_ref[...]
    ms = _group_sum(ob * ob, ones_ref) * (1.0 / B_DK)
    obn = ob * lax.rsqrt(ms + EPS) * gn_ref[...]
    bg = bg_ref[...].astype(F32)
    bb = (obn * (bg * jax.nn.sigmoid(bg))).astype(BF16)
    branches = ((oa_ref[...], 0), (bb, 1), (oc_ref[...], 0))
    y = None
    for nbr, (br, axis) in enumerate(branches):
        gate = jax.nn.sigmoid(gl_ref[:, D_MODEL * nbr:D_MODEL * (nbr + 1)].astype(F32))
        t = gate * lax.dot_general(br, wbr_ref[nbr], (((axis,), (0,)), ((), ())), preferred_element_type=F32)
        y = t if y is None else y + t
    upd = jnp.dot(y.astype(BF16), wout_ref[...], preferred_element_type=F32)
    xn = x_ref[...] + mod_ref[2:3, :] * upd
    xo_ref[...] = xn
    h2 = _rms(xn, g2_ref[...]) * (1.0 + mod_ref[4:5, :]) + mod_ref[3:4, :]
    h2_ref[...] = h2.astype(BF16)


def _merge(x, oa, ohg, bqig, oc, gl, wbr, wout, gn, mod, g2, ones, dims, need_ctx):
    B, S, L = dims
    n = x.shape[0]
    tm = TOKEN_BLOCK
    spb = S // tm
    nblk = (n if need_ctx else B * S) // tm
    row = lambda i: (i, 0)
    const2 = lambda i: (0, 0)
    return pl.pallas_call(
        _merge_kernel,
        grid=(nblk,),
        in_specs=[
            pl.BlockSpec((tm, D_MODEL), row),
            pl.BlockSpec((512, tm), lambda i: (0, i)),
            pl.BlockSpec((tm, B_W), row),
            pl.BlockSpec((tm, B_W), row),
            pl.BlockSpec((tm, B_W), lambda i: (i, 2)),
            pl.BlockSpec((512, tm), lambda i: (0, i)),
            pl.BlockSpec((tm, N_BRANCH * D_MODEL), row),
            pl.BlockSpec((N_BRANCH, BRANCH_W, D_MODEL), lambda i: (0, 0, 0)),
            pl.BlockSpec((D_MODEL, D_MODEL), const2),
            pl.BlockSpec((1, B_W), const2),
            pl.BlockSpec((None, 6, D_MODEL), lambda i: (jnp.minimum(i // spb, B), 0, 0)),
            pl.BlockSpec((1, D_MODEL), const2),
            pl.BlockSpec((B_W, B_W), const2),
        ],
        out_specs=[pl.BlockSpec((tm, D_MODEL), row), pl.BlockSpec((tm, D_MODEL), row)],
        out_shape=[jax.ShapeDtypeStruct((nblk * tm, D_MODEL), F32), jax.ShapeDtypeStruct((nblk * tm, D_MODEL), BF16)],
        compiler_params=_cparams(("parallel",)),
        name="branch_merge",
    )(x, oa, ohg[0], ohg[1], bqig, oc, gl, wbr, wout, gn, mod, g2, ones)


def _moe_kernel(h_ref, x_ref, mod_ref, wr_ref, w13_ref, w2_ref, o_ref, comb_ref, y_ref):
    e = pl.program_id(1)
    h = h_ref[...]
    lane = lax.broadcasted_iota(jnp.int32, (h.shape[0], ROUTER_W), 1)

    @pl.when(e == 0)
    def _():
        logits = jnp.dot(h, wr_ref[...], preferred_element_type=F32)
        big = jnp.int32(ROUTER_W)
        is_grp = (lane >= N_EXPERTS) & (lane < N_EXPERTS + N_GROUPS)
        gl = jnp.where(is_grp, logits, -jnp.inf)
        gmax = jnp.max(gl, axis=-1, keepdims=True)
        gsel = jnp.min(jnp.where(gl == gmax, lane, big), axis=-1, keepdims=True) - N_EXPERTS
        gw = 1.0 / jnp.sum(jnp.exp(gl - gmax), axis=-1, keepdims=True)
        in_grp = (lane >= gsel * EXPERTS_PER_GROUP) & (lane < (gsel + 1) * EXPERTS_PER_GROUP)
        el = jnp.where(in_grp, logits, -jnp.inf)
        m1 = jnp.max(el, axis=-1, keepdims=True)
        i1 = jnp.min(jnp.where(el == m1, lane, big), axis=-1, keepdims=True)
        el2 = jnp.where(lane == i1, -jnp.inf, el)
        m2 = jnp.max(el2, axis=-1, keepdims=True)
        i2 = jnp.min(jnp.where(el2 == m2, lane, big), axis=-1, keepdims=True)
        e2 = jnp.exp(m2 - m1)
        w1 = gw / (1.0 + e2)
        w2 = gw * e2 / (1.0 + e2)
        comb_ref[...] = jnp.where(lane == i1, w1, 0.0) + jnp.where(lane == i2, w2, 0.0)
        y_ref[...] = jnp.zeros_like(y_ref)

    acts = []
    for k in range(MOE_EXPERTS_PER_STEP):
        ce = jnp.sum(jnp.where(lane == e * MOE_EXPERTS_PER_STEP + k, comb_ref[...], 0.0), axis=-1, keepdims=True)
        h13 = jnp.dot(h, w13_ref[k], preferred_element_type=F32)
        a1 = h13[:, :D_EXPERT]
        acts.append((a1 * jax.nn.sigmoid(a1) * h13[:, D_EXPERT:] * ce).astype(BF16))
    y_ref[...] += jnp.dot(jnp.concatenate(acts, axis=-1), w2_ref[...], preferred_element_type=F32)

    @pl.when(e == pl.num_programs(1) - 1)
    def _():
        o_ref[...] = x_ref[...] + mod_ref[5:6, :] * y_ref[...]


def _moe(h2, x, mod, wr, w13, w2, dims, need_ctx):
    B, S, L = dims
    n = x.shape[0]
    tm = min(MOE_TOKEN_BLOCK, S, B * L)
    spb = S // tm
    nblk = (n if need_ctx else B * S) // tm
    row = lambda i, e: (i, 0)
    eps = MOE_EXPERTS_PER_STEP
    return pl.pallas_call(
        _moe_kernel,
        grid=(nblk, N_EXPERTS // eps),
        in_specs=[
            pl.BlockSpec((tm, D_MODEL), row),
            pl.BlockSpec((tm, D_MODEL), row),
            pl.BlockSpec((None, 6, D_MODEL), lambda i, e: (jnp.minimum(i // spb, B), 0, 0)),
            pl.BlockSpec((D_MODEL, ROUTER_W), lambda i, e: (0, 0)),
            pl.BlockSpec((eps, D_MODEL, 2 * D_EXPERT), lambda i, e: (e, 0, 0)),
            pl.BlockSpec((eps * D_EXPERT, D_MODEL), lambda i, e: (e, 0)),
        ],
        out_specs=pl.BlockSpec((tm, D_MODEL), row),
        out_shape=jax.ShapeDtypeStruct((nblk * tm, D_MODEL), F32),
        scratch_shapes=[pltpu.VMEM((tm, ROUTER_W), F32), pltpu.VMEM((tm, D_MODEL), F32)],
        compiler_params=_cparams(("parallel", "arbitrary")),
        name="hier_moe",
    )(h2, x, mod, wr, w13, w2.reshape(N_EXPERTS * D_EXPERT, D_MODEL))


def _final_kernel(x_ref, g_ref, o_ref):
    o_ref[...] = _rms(x_ref[...], g_ref[...])


def _final_norm(x, g, dims):
    B, S, L = dims
    tm = TOKEN_BLOCK
    return pl.pallas_call(
        _final_kernel,
        grid=(B * S // tm,),
        in_specs=[pl.BlockSpec((tm, D_MODEL), lambda i: (i, 0)), pl.BlockSpec((1, D_MODEL), lambda i: (0, 0))],
        out_specs=pl.BlockSpec((tm, D_MODEL), lambda i: (i, 0)),
        out_shape=jax.ShapeDtypeStruct((B * S, D_MODEL), F32),
        compiler_params=_cparams(("parallel",)),
        name="final_norm",
    )(x, g)


def _rope_tables(S, L):
    rows = S // GRID_W
    pos_r = np.repeat(np.arange(rows, dtype=np.float32), GRID_W)
    pos_c = np.tile(np.arange(GRID_W, dtype=np.float32), rows)

    def angles(rot_dim):
        nf = rot_dim // 4
        inv = jnp.asarray(ROPE_BASE, F32) ** (-jnp.arange(nf, dtype=F32) / nf)
        ang = jnp.concatenate([pos_r[:, None] * inv, pos_c[:, None] * inv], axis=-1)
        return jnp.cos(ang), jnp.sin(ang)

    def with_ctx(cos, s_lo, s_hi):
        ident = jnp.concatenate([jnp.ones((L, 128), F32), jnp.zeros((L, 256), F32)], axis=-1)
        return jnp.concatenate([jnp.concatenate([cos, s_lo, s_hi], axis=-1), ident], axis=0)

    cos, sin = angles(HEAD_DIM)
    z = jnp.zeros_like(sin)
    taba = with_ctx(jnp.tile(cos, (1, 4)), jnp.tile(jnp.concatenate([-sin, z], -1), (1, 2)),
                    jnp.tile(jnp.concatenate([z, sin], -1), (1, 2)))
    cos, sin = angles(C_ROPE)
    z = jnp.zeros_like(sin)
    one64, zero64, zero32 = jnp.ones((S, 64), F32), jnp.zeros((S, 64), F32), jnp.zeros((S, 32), F32)
    tabc = with_ctx(jnp.concatenate([one64, cos, cos, one64[:, :32]], -1),
                    jnp.concatenate([zero64, -sin, z, zero32], -1),
                    jnp.concatenate([zero64, z, sin, zero32], -1))
    return taba, tabc


def _pack_w_in(w):
    pad = lambda k: jnp.zeros((w.shape[0], k), w.dtype)
    return jnp.concatenate([w[:, :3712], pad(64), w[:, 3712:3744], pad(32), w[:, 3744:]], axis=-1).astype(BF16)


def _pack_w_uq(w):
    w = w.reshape(C_Q_LORA, C_HEADS, C_NOPE + C_ROPE)
    w = jnp.pad(w, ((0, 0), (0, 0), (0, C_HEAD_PAD - C_NOPE - C_ROPE)))
    return w.reshape(C_Q_LORA, C_HEADS * C_HEAD_PAD).astype(BF16)


def _pack_w_ukv(w):
    w = w.reshape(C_KV_LORA, C_HEADS, C_NOPE + C_V)
    wk = jnp.pad(w[:, :, :C_NOPE], ((0, 0), (0, 0), (0, C_HEAD_PAD - C_NOPE))).reshape(C_KV_LORA, -1)
    wv = w[:, :, C_NOPE:].reshape(C_KV_LORA, -1)
    return jnp.concatenate([wk, wv], axis=-1).astype(BF16)


def kernel(x, c, ctx, c_ctx, w_mod, b_mod, g_norm1, g_norm2, w_in, a_sink, b_lb_logits, b_onorm, c_qnorm, c_kvnorm,
           w_uq, w_ukv, w_br, w_out, w_rg, w_re, w1, w3, w2, g_final):
    B, S, _ = x.shape
    L = ctx.shape[1]
    depth = w_in.shape[0]
    assert L == TOKEN_BLOCK and S % MLA_Q_BLOCK == 0 and S % GRID_W == 0
    dims = (B, S, L)

    xs = jnp.concatenate([x.reshape(B * S, D_MODEL), ctx.reshape(B * L, D_MODEL)], axis=0)
    cc = jnp.zeros((8, D_MODEL), F32).at[:B].set(c).at[B].set(c_ctx)
    mod_all = _modulation(cc, w_mod, b_mod).reshape(depth, 8, 6, D_MODEL)

    lb_all = jnp.cumsum(jax.nn.softmax(b_lb_logits.astype(F32), axis=0), axis=0)
    lb_all = (lb_all - lb_all[0:1]).reshape(depth, 1, 2 * B_W)
    lbp_all = jnp.concatenate([jnp.log(lb_all), jnp.log1p(-lb_all), 1.0 - lb_all,
                               jnp.zeros((depth, 5, 2 * B_W), F32)], axis=1)

    taba, tabc = _rope_tables(S, L)
    ones = jnp.kron(jnp.eye(B_HEADS, dtype=F32), jnp.ones((B_DK, B_DK), F32)).astype(BF16)

    for l in range(depth):
        need_ctx = l < depth - 1
        mod = mod_all[l]
        wr = jnp.concatenate([w_re[l], w_rg[l], jnp.zeros((D_MODEL, ROUTER_W - N_EXPERTS - N_GROUPS), F32)],
                             axis=-1).astype(BF16)
        w13 = jnp.concatenate([w1[l], w3[l]], axis=-1).astype(BF16)
        sink = jnp.broadcast_to(a_sink[l].astype(F32)[:, None] * LOG2E, (A_HEADS, A_QBLOCK))

        qa, ka, va, bqig, gates, qc, kc, vc, gl = _projection(
            xs, mod, g_norm1[l][None], _pack_w_in(w_in[l]), _pack_w_uq(w_uq[l]), _pack_w_ukv(w_ukv[l]),
            c_qnorm[l][None], c_kvnorm[l][None], lbp_all[l], taba, tabc, dims)
        oa = _window_gqa(qa, ka, va, sink, dims, need_ctx)
        ohg = _hgrn2_scan(bqig, gates, dims)
        oc = _mla_attention(qc, kc, vc, dims, need_ctx)
        xs, h2 = _merge(xs, oa, ohg, bqig, oc, gl, w_br[l].astype(BF16), w_out[l].astype(BF16), b_onorm[l][None],
                        mod, g_norm2[l][None], ones, dims, need_ctx)
        xs = _moe(h2, xs, mod, wr, w13, w2[l].astype(BF16), dims, need_ctx)

    return _final_norm(xs, g_final[None], dims).reshape(B, S, D_MODEL)
```

```python
import functools

import jax
import jax.numpy as jnp
import numpy as np
from jax import lax
from jax.experimental import pallas as pl
from jax.experimental.pallas import tpu as pltpu

F32 = jnp.float32
BF16 = jnp.bfloat16
HIGHEST = lax.Precision.HIGHEST

D_MODEL = 1024
GRID_W = 64
HEAD_DIM = 64
ROPE_BASE = 10000.0
EPS = 1e-6
A_HEADS = 8
A_KV_HEADS = 2
A_GROUP = A_HEADS // A_KV_HEADS
A_WINDOW = 128
A_BLOCK = A_WINDOW
A_QBLOCK = 2 * A_BLOCK
B_HEADS = 8
B_DK = 64
B_W = B_HEADS * B_DK
B_CHUNK = 64
B_BLOCK = 256
C_HEADS = 8
C_NOPE = 64
C_ROPE = 32
C_V = 64
C_Q_LORA = 256
C_KV_LORA = 128
C_HEAD_PAD = 128
N_BRANCH = 3
BRANCH_W = 512
N_GROUPS = 4
EXPERTS_PER_GROUP = 8
N_EXPERTS = N_GROUPS * EXPERTS_PER_GROUP
D_EXPERT = 256
ROUTER_W = 128
V_SLAB = 80
LOG2E = 1.4426950408889634

OFF_AQ, OFF_AK, OFF_AV = 0, 512, 640
OFF_BQ, OFF_BI, OFF_BZF, OFF_BZB, OFF_BG = 768, 1280, 1792, 2304, 2816
OFF_CQ, OFF_CKV, OFF_CKR, OFF_GL = 3328, 3584, 3712, 3840
IN_W_PACKED = OFF_GL + N_BRANCH * D_MODEL

TOKEN_BLOCK = 256
PROJ_BLOCK = 512
MOE_TOKEN_BLOCK = 1024
MOE_EXPERTS_PER_STEP = 4
MLA_Q_BLOCK = 512
MLA_K_BLOCK = 1024
MLA_KEY_CHUNK = 256
VMEM_LIMIT = 56 * 1024 * 1024
PROJ_VMEM_LIMIT = 61 * 1024 * 1024


def _cparams(sem, vmem_limit=VMEM_LIMIT):
    return pltpu.CompilerParams(dimension_semantics=sem, vmem_limit_bytes=vmem_limit)


def _mod_kernel(cc_ref, w_ref, b_ref, o_ref):
    cc = cc_ref[...]
    a = cc * jax.nn.sigmoid(cc)
    o_ref[...] = jnp.dot(a, w_ref[...], preferred_element_type=F32, precision=HIGHEST) + b_ref[...]


def _modulation(cc, w_mod, b_mod):
    depth = w_mod.shape[0]
    nj = 6
    return pl.pallas_call(
        _mod_kernel,
        grid=(depth, nj),
        in_specs=[
            pl.BlockSpec((8, D_MODEL), lambda l, j: (0, 0)),
            pl.BlockSpec((None, D_MODEL, D_MODEL), lambda l, j: (l, 0, j)),
            pl.BlockSpec((None, 1, D_MODEL), lambda l, j: (l, 0, j)),
        ],
        out_specs=pl.BlockSpec((None, 8, D_MODEL), lambda l, j: (l, 0, j)),
        out_shape=jax.ShapeDtypeStruct((depth, 8, 6 * D_MODEL), F32),
        compiler_params=_cparams(("arbitrary", "arbitrary")),
        name="adaln_mod",
    )(cc, w_mod, b_mod.reshape(depth, 1, 6 * D_MODEL))


def _rms(x, g):
    return x * lax.rsqrt(jnp.mean(x * x, axis=-1, keepdims=True) + EPS) * g


def _rope(v, tab_ref, half):
    n = v.shape[-1]
    cos = tab_ref[:, 0:128]
    s_lo = tab_ref[:, 128:256]
    s_hi = tab_ref[:, 256:384]
    return v * cos + pltpu.roll(v, n - half, 1) * s_lo + pltpu.roll(v, half, 1) * s_hi


def _store_v_slabs(ref, vt, heads):
    ones = jnp.ones((V_SLAB - HEAD_DIM, vt.shape[1]), BF16)
    for hd in range(heads):
        ref[V_SLAB * hd:V_SLAB * hd + HEAD_DIM, :] = vt[HEAD_DIM * hd:HEAD_DIM * (hd + 1), :].astype(BF16)
        ref[V_SLAB * hd + HEAD_DIM:V_SLAB * (hd + 1), :] = ones


def _proj_kernel(x_ref, mod_ref, g1_ref, w_ref, wuq_ref, wukv_ref, gq_ref, gkv_ref, lbp_ref, taba_ref, tabc_ref,
                 qa_ref, ka_ref, va_ref, bqig_ref, gates_ref, qc_ref, kc_ref, vc_ref, gl_ref):
    x = x_ref[...]
    h = _rms(x, g1_ref[...]) * (1.0 + mod_ref[1:2, :]) + mod_ref[0:1, :]
    hb = h.astype(BF16)

    def seg(off, width):
        return jnp.dot(hb, w_ref[:, off:off + width], preferred_element_type=F32)

    aq = seg(OFF_AQ, 512) * (HEAD_DIM ** -0.5 * LOG2E)
    for j in range(4):
        qa_ref[128 * j:128 * (j + 1), :] = _rope(aq[:, 128 * j:128 * (j + 1)], taba_ref, 32).T.astype(BF16)
    ka_ref[...] = _rope(seg(OFF_AK, 128), taba_ref, 32).astype(BF16)
    _store_v_slabs(va_ref, seg(OFF_AV, 128).T, A_KV_HEADS)

    bqig_ref[:, 0:512] = seg(OFF_BQ, 512).astype(BF16)
    bqig_ref[:, 512:1024] = seg(OFF_BI, 512).astype(BF16)
    bqig_ref[:, 1024:1536] = seg(OFF_BG, 512).astype(BF16)
    for d, off in enumerate((OFF_BZF, OFF_BZB)):
        z = seg(off, 512)
        log_lb = lbp_ref[0:1, 512 * d:512 * (d + 1)]
        log1m_lb = lbp_ref[1:2, 512 * d:512 * (d + 1)]
        one_m_lb = lbp_ref[2:3, 512 * d:512 * (d + 1)]
        e = jnp.exp(-jnp.abs(z))
        log_sig = jnp.minimum(z, 0.0) - jnp.log(1.0 + e)
        b = log1m_lb + log_sig
        mx = jnp.maximum(log_lb, b)
        logf = mx + jnp.log(1.0 + jnp.exp(-jnp.abs(log_lb - b)))
        r = 1.0 / (1.0 + e)
        key = one_m_lb * jnp.where(z >= 0.0, e * r, r)
        gates_ref[:, 512 * d:512 * (d + 1)] = logf
        gates_ref[:, 1024 + 512 * d:1024 + 512 * (d + 1)] = key

    cq = _rms(seg(OFF_CQ, C_Q_LORA), gq_ref[...]).astype(BF16)
    qh = jnp.dot(cq, wuq_ref[...], preferred_element_type=F32) * ((C_NOPE + C_ROPE) ** -0.5 * LOG2E)
    ckv = _rms(seg(OFF_CKV, C_KV_LORA), gkv_ref[...]).astype(BF16)
    kvh = jnp.dot(ckv, wukv_ref[...], preferred_element_type=F32)
    kr = _rope(seg(OFF_CKR, 128), tabc_ref, 16)
    for j in range(C_HEADS):
        sl = slice(C_HEAD_PAD * j, C_HEAD_PAD * (j + 1))
        qc_ref[sl, :] = _rope(qh[:, sl], tabc_ref, 16).T.astype(BF16)
        kc_ref[:, sl] = (kvh[:, sl] + kr).astype(BF16)
    _store_v_slabs(vc_ref, kvh[:, C_HEADS * C_HEAD_PAD:].T, C_HEADS)

    for j in range(6):
        gl_ref[:, 512 * j:512 * (j + 1)] = seg(OFF_GL + 512 * j, 512).astype(BF16)


def _projection(x, mod, g1, w_in_p, wuq_p, wukv_p, gq, gkv, lbp, taba, tabc, dims):
    B, S, L = dims
    n = x.shape[0]
    tm = PROJ_BLOCK
    nlat = B * S // tm
    spb = S // tm

    def row(i):
        return (i, 0)

    def mod_row(i):
        return (jnp.minimum(i // spb, B), 0, 0)

    def tab_row(i):
        return (jnp.where(i < nlat, i % spb, spb), 0)

    const = lambda i: (0, 0)
    resident = functools.partial(pl.BlockSpec, index_map=const, pipeline_mode=pl.Buffered(1))
    widths = (512, 128, A_KV_HEADS * V_SLAB, 1536, 2048, 1024, 1024, C_HEADS * V_SLAB, 3072)
    dtypes = (BF16, BF16, BF16, BF16, F32, BF16, BF16, BF16, BF16)
    transposed = (0, 2, 5, 7)
    return pl.pallas_call(
        _proj_kernel,
        grid=(n // tm,),
        in_specs=[
            pl.BlockSpec((tm, D_MODEL), row),
            pl.BlockSpec((None, 6, D_MODEL), mod_row),
            pl.BlockSpec((1, D_MODEL), const),
            resident((D_MODEL, IN_W_PACKED)),
            resident(wuq_p.shape),
            resident(wukv_p.shape),
            pl.BlockSpec((1, C_Q_LORA), const),
            pl.BlockSpec((1, C_KV_LORA), const),
            pl.BlockSpec((8, 2 * B_W), const),
            pl.BlockSpec((tm, 384), tab_row),
            pl.BlockSpec((tm, 384), tab_row),
        ],
        out_specs=[pl.BlockSpec((w, tm), lambda i: (0, i)) if k in transposed else pl.BlockSpec((tm, w), row)
                   for k, w in enumerate(widths)],
        out_shape=[jax.ShapeDtypeStruct((w, n) if k in transposed else (n, w), dt)
                   for k, (w, dt) in enumerate(zip(widths, dtypes))],
        compiler_params=_cparams(("parallel",), PROJ_VMEM_LIMIT),
        name="in_proj",
    )(x, mod, g1, w_in_p, wuq_p, wukv_p, gq, gkv, lbp, taba, tabc)


def _gqa_kernel(qt_ref, k0_ref, k1_ref, k2_ref, k3_ref, kx_ref, v0_ref, v1_ref, v2_ref, v3_ref, vx_ref, sink_ref,
                o_ref, bias_ref, s0_ref, s1_ref, p0_ref, p1_ref, *, nlat_blocks, seq):
    j = pl.program_id(1)
    nband = 4 * A_BLOCK
    nk = bias_ref.shape[0]
    width = A_GROUP * A_QBLOCK
    rows = lax.broadcasted_iota(jnp.int32, (nk, A_QBLOCK), 0)
    cols = lax.broadcasted_iota(jnp.int32, (nk, A_QBLOCK), 1)
    qpos = j * A_QBLOCK + cols
    kpos = j * A_QBLOCK - A_BLOCK + rows
    kend = jnp.where(j < nlat_blocks, seq, 0)
    valid = ((kpos >= 0) & (kpos < kend) & (jnp.abs(qpos - kpos) <= A_WINDOW)) | (rows >= nband)
    bias = jnp.where(valid, 0.0, -jnp.inf)
    for hh in range(A_GROUP):
        bias_ref[:, A_QBLOCK * hh:A_QBLOCK * (hh + 1)] = bias
    k = jnp.concatenate([k0_ref[...], k1_ref[...], k2_ref[...], k3_ref[...], kx_ref[...]], axis=0)
    vt = jnp.concatenate([v0_ref[...], v1_ref[...], v2_ref[...], v3_ref[...], vx_ref[...]], axis=1)
    s_refs, p_refs = (s0_ref, s1_ref), (p0_ref, p1_ref)
    nchunk = nk // A_QBLOCK
    krows = lambda r: slice(A_QBLOCK * r, A_QBLOCK * (r + 1))
    smax, pv = [None] * A_KV_HEADS, [None] * A_KV_HEADS

    def stage_scores(g, r):
        qg = jnp.concatenate([qt_ref[HEAD_DIM * hd:HEAD_DIM * (hd + 1), :]
                              for hd in range(A_GROUP * g, A_GROUP * (g + 1))], axis=1)
        sc = (jnp.dot(k[krows(r), HEAD_DIM * g:HEAD_DIM * (g + 1)], qg, preferred_element_type=F32)
              + bias_ref[krows(r), :])
        s_refs[g][krows(r), :] = sc
        cm = jnp.max(sc, axis=0, keepdims=True)
        smax[g] = cm if smax[g] is None else jnp.maximum(smax[g], cm)

    def stage_exp(g, r, m):
        p_refs[g][krows(r), :] = jnp.exp2(s_refs[g][krows(r), :] - m).astype(BF16)

    def stage_pv(g, r, m):
        t = jnp.dot(vt[V_SLAB * g:V_SLAB * (g + 1), krows(r)], p_refs[g][krows(r), :], preferred_element_type=F32)
        pv[g] = t if pv[g] is None else pv[g] + t
        if r == nchunk - 1:
            denom = pv[g][HEAD_DIM:HEAD_DIM + 1, :] + jnp.exp2(sink_ref[g] - m)
            o = pv[g][0:HEAD_DIM, :] / denom
            for hh in range(A_GROUP):
                hd = A_GROUP * g + hh
                o_ref[HEAD_DIM * hd:HEAD_DIM * (hd + 1), :] = o[:, A_QBLOCK * hh:A_QBLOCK * (hh + 1)].astype(BF16)

    m = [None] * A_KV_HEADS
    for r in range(nchunk):
        stage_scores(0, r)
    for g in range(A_KV_HEADS + 1):
        if g < A_KV_HEADS:
            m[g] = jnp.maximum(smax[g], sink_ref[g])
        for r in range(nchunk):
            if g + 1 < A_KV_HEADS:
                stage_scores(g + 1, r)
            if g < A_KV_HEADS:
                stage_exp(g, r, m[g])
            if g >= 1:
                stage_pv(g - 1, r, m[g - 1])


def _window_gqa(qat, ka, vat, sink, dims, need_ctx):
    B, S, L = dims
    n = ka.shape[0]
    assert L == A_QBLOCK
    nb = S // A_BLOCK
    nqb = S // A_QBLOCK
    nq = nqb + (1 if need_ctx else 0)
    nk = 4 * A_BLOCK + L
    width = A_GROUP * A_QBLOCK

    def q_col(b, j):
        return (0, jnp.where(j < nqb, b * nqb + j, B * nqb + b))

    def kblk(delta):
        return lambda b, j: b * nb + jnp.clip(2 * j + delta, 0, nb - 1)

    k_specs = [pl.BlockSpec((A_BLOCK, 128), (lambda f: lambda b, j: (f(b, j), 0))(kblk(dl))) for dl in (-1, 0, 1, 2)]
    vs = A_KV_HEADS * V_SLAB
    v_specs = [pl.BlockSpec((vs, A_BLOCK), (lambda f: lambda b, j: (0, f(b, j)))(kblk(dl))) for dl in (-1, 0, 1, 2)]
    return pl.pallas_call(
        functools.partial(_gqa_kernel, nlat_blocks=nqb, seq=S),
        grid=(B, nq),
        in_specs=[pl.BlockSpec((512, A_QBLOCK), q_col)] + k_specs
        + [pl.BlockSpec((L, 128), lambda b, j: (B * S // L + b, 0))] + v_specs
        + [pl.BlockSpec((vs, L), lambda b, j: (0, B * S // L + b)),
           pl.BlockSpec((A_KV_HEADS, 1, width), lambda b, j: (0, 0, 0))],
        out_specs=pl.BlockSpec((512, A_QBLOCK), q_col),
        out_shape=jax.ShapeDtypeStruct((512, n if need_ctx else B * S), BF16),
        scratch_shapes=[pltpu.VMEM((nk, width), F32), pltpu.VMEM((nk, width), F32), pltpu.VMEM((nk, width), F32),
                        pltpu.VMEM((nk, width), BF16), pltpu.VMEM((nk, width), BF16)],
        compiler_params=_cparams(("parallel", "parallel")),
        name="window_gqa",
    )(qat, ka, ka, ka, ka, ka, vat, vat, vat, vat, vat, sink)


def _hgrn_chunk(q_ref, v_ref, g_ref, k_ref, o_ref, st_ref, r0, reverse):
    C = B_CHUNK
    rs = slice(r0, r0 + C)
    rows = lax.broadcasted_iota(jnp.int32, (C, C), 0)
    cols = lax.broadcasted_iota(jnp.int32, (C, C), 1)
    causal = (rows <= cols) if reverse else (rows >= cols)
    g = g_ref[rs, :]
    bc = jnp.dot(causal.astype(F32), g, preferred_element_type=F32, precision=HIGHEST)
    tot = jnp.sum(g, axis=0, keepdims=True)
    mid = C // 2 if reverse else C // 2 - 1
    rho = bc[mid:mid + 1, :]
    q = q_ref[rs, :].astype(F32)
    key = k_ref[rs, :]
    v = v_ref[rs, :]
    qe = (q * jnp.exp(bc - rho)).astype(BF16)
    ke = (key * jnp.exp(rho - bc)).astype(BF16)
    qs = (q * jnp.exp(bc)).astype(BF16)
    ks = (key * jnp.exp(tot - bc)).astype(BF16)
    dec = jnp.exp(tot)
    for hd in range(B_HEADS):
        sl = slice(B_DK * hd, B_DK * (hd + 1))
        st = st_ref[hd]
        att = lax.dot_general(qe[:, sl], ke[:, sl], (((1,), (1,)), ((), ())), preferred_element_type=F32)
        att = jnp.where(causal, att, 0.0).astype(BF16)
        o = lax.dot_general(qs[:, sl], st.astype(BF16), (((1,), (1,)), ((), ())), preferred_element_type=F32)
        o = o + jnp.dot(att, v[:, sl], preferred_element_type=F32)
        o_ref[rs, sl] = o
        upd = lax.dot_general(v[:, sl], ks[:, sl], (((0,), (0,)), ((), ())), preferred_element_type=F32)
        st_ref[hd] = st * dec[:, sl] + upd


def _hgrn_kernel(qf_ref, vf_ref, gf_ref, kf_ref, qb_ref, vb_ref, gb_ref, kb_ref, of_ref, ob_ref, stf_ref, stb_ref):
    @pl.when(pl.program_id(1) == 0)
    def _():
        stf_ref[...] = jnp.zeros_like(stf_ref)
        stb_ref[...] = jnp.zeros_like(stb_ref)

    nchunk = qf_ref.shape[0] // B_CHUNK
    for i in range(nchunk):
        _hgrn_chunk(qf_ref, vf_ref, gf_ref, kf_ref, of_ref, stf_ref, B_CHUNK * i, False)
        _hgrn_chunk(qb_ref, vb_ref, gb_ref, kb_ref, ob_ref, stb_ref, B_CHUNK * (nchunk - 1 - i), True)


def _hgrn2_scan(bqig, gates, dims):
    B, S, L = dims
    n = bqig.shape[0]
    T = B_BLOCK
    assert L == T
    ns = S // T

    def fwd_blk(b, c):
        return jnp.where(c == 0, B * ns + b, b * ns + c - 1)

    def bwd_blk(b, c):
        return jnp.where(c == 0, B * ns + b, b * ns + ns - c)

    def specs(blk, d):
        return [pl.BlockSpec((T, B_W), lambda b, c: (blk(b, c), 0)),
                pl.BlockSpec((T, B_W), lambda b, c: (blk(b, c), 1)),
                pl.BlockSpec((T, B_W), lambda b, c: (blk(b, c), d)),
                pl.BlockSpec((T, B_W), lambda b, c: (blk(b, c), 2 + d))]

    return pl.pallas_call(
        _hgrn_kernel,
        grid=(B, ns + 1),
        in_specs=specs(fwd_blk, 0) + specs(bwd_blk, 1),
        out_specs=[pl.BlockSpec((T, B_W), lambda b, c: (fwd_blk(b, c), 0)),
                   pl.BlockSpec((T, B_W), lambda b, c: (bwd_blk(b, c), 0))],
        out_shape=[jax.ShapeDtypeStruct((n, B_W), F32), jax.ShapeDtypeStruct((n, B_W), F32)],
        scratch_shapes=[pltpu.VMEM((B_HEADS, B_DK, B_DK), F32), pltpu.VMEM((B_HEADS, B_DK, B_DK), F32)],
        compiler_params=_cparams(("parallel", "arbitrary")),
        name="hgrn2_scan",
    )(bqig, bqig, gates, gates, bqig, bqig, gates, gates)


def _mla_kernel(qt_ref, k_ref, vt_ref, *rest, with_ctx):
    if with_ctx:
        kx_ref, vxt_ref, o_ref, m_ref, acc_ref, *bufs = rest
    else:
        _, o_ref, m_ref, acc_ref, *bufs = rest
    s_refs, p_refs = bufs[0:2], bufs[2:4]
    kstep = pl.program_id(2)
    tq = qt_ref.shape[1]

    def kv_pass(k_ref, vt_ref, first):
        nkeys = k_ref.shape[0]
        nchunk = nkeys // MLA_KEY_CHUNK
        krows = lambda r: slice(MLA_KEY_CHUNK * r, MLA_KEY_CHUNK * (r + 1))

        smax = [None] * C_HEADS
        m_new = [None] * C_HEADS
        alpha = [None] * C_HEADS
        pv = [None] * C_HEADS

        def stage_scores(hd, r):
            sl = slice(C_HEAD_PAD * hd, C_HEAD_PAD * (hd + 1))
            sc = jnp.dot(k_ref[krows(r), sl], qt_ref[sl, :], preferred_element_type=F32)
            s_refs[hd % 2][krows(r), :] = sc
            cm = jnp.max(sc, axis=0, keepdims=True)
            smax[hd] = cm if smax[hd] is None else jnp.maximum(smax[hd], cm)

        def stage_stats(hd):
            if first:
                m_new[hd] = smax[hd]
            else:
                m_old = m_ref[hd]
                m_new[hd] = jnp.maximum(m_old, smax[hd])
                alpha[hd] = jnp.exp2(m_old - m_new[hd])
            m_ref[hd] = m_new[hd]

        def stage_exp(hd, r):
            p_refs[hd % 2][krows(r), :] = jnp.exp2(s_refs[hd % 2][krows(r), :] - m_new[hd]).astype(BF16)

        def stage_pv(hd, r):
            t = jnp.dot(vt_ref[V_SLAB * hd:V_SLAB * (hd + 1), krows(r)], p_refs[hd % 2][krows(r), :],
                        preferred_element_type=F32)
            pv[hd] = t if pv[hd] is None else pv[hd] + t
            if r == nchunk - 1:
                acc_ref[hd] = pv[hd] if first else alpha[hd] * acc_ref[hd] + pv[hd]

        for r in range(nchunk):
            stage_scores(0, r)
        for hd in range(C_HEADS + 1):
            if hd < C_HEADS:
                stage_stats(hd)
            for r in range(nchunk):
                if hd + 1 < C_HEADS:
                    stage_scores(hd + 1, r)
                if hd < C_HEADS:
                    stage_exp(hd, r)
                if hd >= 1:
                    stage_pv(hd - 1, r)

    if with_ctx:
        @pl.when(kstep == 0)
        def _():
            kv_pass(kx_ref, vxt_ref, True)

        kv_pass(k_ref, vt_ref, False)
    else:
        kv_pass(k_ref, vt_ref, True)

    @pl.when(kstep == pl.num_programs(2) - 1)
    def _():
        for hd in range(C_HEADS):
            o_ref[C_V * hd:C_V * (hd + 1), :] = (acc_ref[hd, 0:C_V, :] / acc_ref[hd, C_V:C_V + 1, :]).astype(BF16)


def _mla_attention(qct, kc, vct, dims, need_ctx):
    B, S, L = dims
    n = kc.shape[0]
    tq = min(MLA_Q_BLOCK, S)
    tk = min(MLA_K_BLOCK, S)
    nq, nk = S // tq, S // tk
    hw = C_HEADS * C_HEAD_PAD
    vw = C_HEADS * C_V
    vs = C_HEADS * V_SLAB
    scratch = lambda t, nkeys: [
        pltpu.VMEM((C_HEADS, 1, t), F32), pltpu.VMEM((C_HEADS, V_SLAB, t), F32),
        pltpu.VMEM((nkeys, t), F32), pltpu.VMEM((nkeys, t), F32),
        pltpu.VMEM((nkeys, t), BF16), pltpu.VMEM((nkeys, t), BF16)]
    ctx_row = lambda b, i, k: (B * S // L + b, 0)
    ctx_col = lambda b, i, k: (0, B * S // L + b)
    o_lat = pl.pallas_call(
        functools.partial(_mla_kernel, with_ctx=True),
        grid=(B, nq, nk),
        in_specs=[
            pl.BlockSpec((hw, tq), lambda b, i, k: (0, b * nq + i)),
            pl.BlockSpec((tk, hw), lambda b, i, k: (b * nk + k, 0)),
            pl.BlockSpec((vs, tk), lambda b, i, k: (0, b * nk + k)),
            pl.BlockSpec((L, hw), ctx_row),
            pl.BlockSpec((vs, L), ctx_col),
        ],
        out_specs=pl.BlockSpec((vw, tq), lambda b, i, k: (0, b * nq + i)),
        out_shape=jax.ShapeDtypeStruct((vw, n if need_ctx else B * S), BF16),
        scratch_shapes=scratch(tq, tk),
        compiler_params=_cparams(("parallel", "parallel", "arbitrary")),
        name="mla_latent",
    )(qct, kc, vct, kc, vct)
    if not need_ctx:
        return o_lat
    return pl.pallas_call(
        functools.partial(_mla_kernel, with_ctx=False),
        grid=(B, 1, 1),
        in_specs=[
            pl.BlockSpec((hw, L), ctx_col),
            pl.BlockSpec((L, hw), ctx_row),
            pl.BlockSpec((vs, L), ctx_col),
            pl.BlockSpec(memory_space=pl.ANY),
        ],
        out_specs=pl.BlockSpec((vw, L), ctx_col),
        out_shape=jax.ShapeDtypeStruct((vw, n), BF16),
        scratch_shapes=scratch(L, L),
        input_output_aliases={3: 0},
        compiler_params=_cparams(("parallel", "arbitrary", "arbitrary")),
        name="mla_context",
    )(qct, kc, vct, o_lat)


def _group_sum(x, ones_ref):
    hi = x.astype(BF16)
    lo = (x - hi.astype(F32)).astype(BF16)
    return (jnp.dot(hi, ones_ref[...], preferred_element_type=F32)
            + jnp.dot(lo, ones_ref[...], preferred_element_type=F32))


def _merge_kernel(x_ref, oa_ref, of_ref, ob_ref, bg_ref, oc_ref, gl_ref, wbr_ref, wout_ref, gn_ref, mod_ref,
                  g2_ref, ones_ref, xo_ref, h2_ref):
    ob = of_ref[...] + ob_ref[...]
    ms = _group_sum(ob * ob, ones_ref) * (1.0 / B_DK)
    obn = ob * lax.rsqrt(ms + EPS) * gn_ref[...]
    bg = bg_ref[...].astype(F32)
    bb = (obn * (bg * jax.nn.sigmoid(bg))).astype(BF16)
    branches = ((oa_ref[...], 0), (bb, 1), (oc_ref[...], 0))
    y = None
    for nbr, (br, axis) in enumerate(branches):
        gate = jax.nn.sigmoid(gl_ref[:, D_MODEL * nbr:D_MODEL * (nbr + 1)].astype(F32))
        t = gate * lax.dot_general(br, wbr_ref[nbr], (((axis,), (0,)), ((), ())), preferred_element_type=F32)
        y = t if y is None else y + t
    upd = jnp.dot(y.astype(BF16), wout_ref[...], preferred_element_type=F32)
    xn = x_ref[...] + mod_ref[2:3, :] * upd
    xo_ref[...] = xn
    h2 = _rms(xn, g2_ref[...]) * (1.0 + mod_ref[4:5, :]) + mod_ref[3:4, :]
    h2_ref[...] = h2.astype(BF16)


def _merge(x, oa, ohg, bqig, oc, gl, wbr, wout, gn, mod, g2, ones, dims, need_ctx):
    B, S, L = dims
    n = x.shape[0]
    tm = PROJ_BLOCK
    spb = S // tm
    nblk = (n if need_ctx else B * S) // tm
    row = lambda i: (i, 0)
    const2 = lambda i: (0, 0)
    return pl.pallas_call(
        _merge_kernel,
        grid=(nblk,),
        in_specs=[
            pl.BlockSpec((tm, D_MODEL), row),
            pl.BlockSpec((512, tm), lambda i: (0, i)),
            pl.BlockSpec((tm, B_W), row),
            pl.BlockSpec((tm, B_W), row),
            pl.BlockSpec((tm, B_W), lambda i: (i, 2)),
            pl.BlockSpec((512, tm), lambda i: (0, i)),
            pl.BlockSpec((tm, N_BRANCH * D_MODEL), row),
            pl.BlockSpec((N_BRANCH, BRANCH_W, D_MODEL), lambda i: (0, 0, 0)),
            pl.BlockSpec((D_MODEL, D_MODEL), const2),
            pl.BlockSpec((1, B_W), const2),
            pl.BlockSpec((None, 6, D_MODEL), lambda i: (jnp.minimum(i // spb, B), 0, 0)),
            pl.BlockSpec((1, D_MODEL), const2),
            pl.BlockSpec((B_W, B_W), const2),
        ],
        out_specs=[pl.BlockSpec((tm, D_MODEL), row), pl.BlockSpec((tm, D_MODEL), row)],
        out_shape=[jax.ShapeDtypeStruct((nblk * tm, D_MODEL), F32), jax.ShapeDtypeStruct((nblk * tm, D_MODEL), BF16)],
        compiler_params=_cparams(("parallel",)),
        name="branch_merge",
    )(x, oa, ohg[0], ohg[1], bqig, oc, gl, wbr, wout, gn, mod, g2, ones)


def _moe_kernel(h_ref, x_ref, mod_ref, wr_ref, w13_ref, w2_ref, gf_ref, o_ref, comb_ref, y_ref, *, final_norm):
    e = pl.program_id(1)
    h = h_ref[...]
    lane = lax.broadcasted_iota(jnp.int32, (h.shape[0], ROUTER_W), 1)

    @pl.when(e == 0)
    def _():
        logits = jnp.dot(h, wr_ref[...], preferred_element_type=F32)
        big = jnp.int32(ROUTER_W)
        is_grp = (lane >= N_EXPERTS) & (lane < N_EXPERTS + N_GROUPS)
        gl = jnp.where(is_grp, logits, -jnp.inf)
        gmax = jnp.max(gl, axis=-1, keepdims=True)
        gsel = jnp.min(jnp.where(gl == gmax, lane, big), axis=-1, keepdims=True) - N_EXPERTS
        gw = 1.0 / jnp.sum(jnp.exp(gl - gmax), axis=-1, keepdims=True)
        in_grp = (lane >= gsel * EXPERTS_PER_GROUP) & (lane < (gsel + 1) * EXPERTS_PER_GROUP)
        el = jnp.where(in_grp, logits, -jnp.inf)
        m1 = jnp.max(el, axis=-1, keepdims=True)
        i1 = jnp.min(jnp.where(el == m1, lane, big), axis=-1, keepdims=True)
        el2 = jnp.where(lane == i1, -jnp.inf, el)
        m2 = jnp.max(el2, axis=-1, keepdims=True)
        i2 = jnp.min(jnp.where(el2 == m2, lane, big), axis=-1, keepdims=True)
        e2 = jnp.exp(m2 - m1)
        w1 = gw / (1.0 + e2)
        w2 = gw * e2 / (1.0 + e2)
        comb_ref[...] = jnp.where(lane == i1, w1, 0.0) + jnp.where(lane == i2, w2, 0.0)
        y_ref[...] = jnp.zeros_like(y_ref)

    acts = []
    for k in range(MOE_EXPERTS_PER_STEP):
        ce = jnp.sum(jnp.where(lane == e * MOE_EXPERTS_PER_STEP + k, comb_ref[...], 0.0), axis=-1, keepdims=True)
        h13 = jnp.dot(h, w13_ref[k], preferred_element_type=F32)
        a1 = h13[:, :D_EXPERT]
        acts.append((a1 * jax.nn.sigmoid(a1) * h13[:, D_EXPERT:] * ce).astype(BF16))
    y_ref[...] += jnp.dot(jnp.concatenate(acts, axis=-1), w2_ref[...], preferred_element_type=F32)

    @pl.when(e == pl.num_programs(1) - 1)
    def _():
        xn = x_ref[...] + mod_ref[5:6, :] * y_ref[...]
        o_ref[...] = _rms(xn, gf_ref[...]) if final_norm else xn


def _moe(h2, x, mod, wr, w13, w2, g_final, dims, need_ctx, final_norm):
    B, S, L = dims
    n = x.shape[0]
    tm = min(MOE_TOKEN_BLOCK, S, B * L)
    spb = S // tm
    nblk = (n if need_ctx else B * S) // tm
    row = lambda i, e: (i, 0)
    eps = MOE_EXPERTS_PER_STEP
    return pl.pallas_call(
        functools.partial(_moe_kernel, final_norm=final_norm),
        grid=(nblk, N_EXPERTS // eps),
        in_specs=[
            pl.BlockSpec((tm, D_MODEL), row),
            pl.BlockSpec((tm, D_MODEL), row),
            pl.BlockSpec((None, 6, D_MODEL), lambda i, e: (jnp.minimum(i // spb, B), 0, 0)),
            pl.BlockSpec((D_MODEL, ROUTER_W), lambda i, e: (0, 0)),
            pl.BlockSpec((eps, D_MODEL, 2 * D_EXPERT), lambda i, e: (e, 0, 0)),
            pl.BlockSpec((eps * D_EXPERT, D_MODEL), lambda i, e: (e, 0)),
            pl.BlockSpec((1, D_MODEL), lambda i, e: (0, 0)),
        ],
        out_specs=pl.BlockSpec((tm, D_MODEL), row),
        out_shape=jax.ShapeDtypeStruct((nblk * tm, D_MODEL), F32),
        scratch_shapes=[pltpu.VMEM((tm, ROUTER_W), F32), pltpu.VMEM((tm, D_MODEL), F32)],
        compiler_params=_cparams(("parallel", "arbitrary")),
        name="hier_moe",
    )(h2, x, mod, wr, w13, w2.reshape(N_EXPERTS * D_EXPERT, D_MODEL), g_final)


def _rope_tables(S, L):
    rows = S // GRID_W
    pos_r = np.repeat(np.arange(rows, dtype=np.float32), GRID_W)
    pos_c = np.tile(np.arange(GRID_W, dtype=np.float32), rows)

    def angles(rot_dim):
        nf = rot_dim // 4
        inv = jnp.asarray(ROPE_BASE, F32) ** (-jnp.arange(nf, dtype=F32) / nf)
        ang = jnp.concatenate([pos_r[:, None] * inv, pos_c[:, None] * inv], axis=-1)
        return jnp.cos(ang), jnp.sin(ang)

    def with_ctx(cos, s_lo, s_hi):
        ident = jnp.concatenate([jnp.ones((PROJ_BLOCK, 128), F32), jnp.zeros((PROJ_BLOCK, 256), F32)], axis=-1)
        return jnp.concatenate([jnp.concatenate([cos, s_lo, s_hi], axis=-1), ident], axis=0)

    cos, sin = angles(HEAD_DIM)
    z = jnp.zeros_like(sin)
    taba = with_ctx(jnp.tile(cos, (1, 4)), jnp.tile(jnp.concatenate([-sin, z], -1), (1, 2)),
                    jnp.tile(jnp.concatenate([z, sin], -1), (1, 2)))
    cos, sin = angles(C_ROPE)
    z = jnp.zeros_like(sin)
    one64, zero64, zero32 = jnp.ones((S, 64), F32), jnp.zeros((S, 64), F32), jnp.zeros((S, 32), F32)
    tabc = with_ctx(jnp.concatenate([one64, cos, cos, one64[:, :32]], -1),
                    jnp.concatenate([zero64, -sin, z, zero32], -1),
                    jnp.concatenate([zero64, z, sin, zero32], -1))
    return taba, tabc


def _pack_w_in(w):
    pad = lambda k: jnp.zeros((w.shape[0], k), w.dtype)
    return jnp.concatenate([w[:, :3712], pad(64), w[:, 3712:3744], pad(32), w[:, 3744:]], axis=-1).astype(BF16)


def _pack_w_uq(w):
    w = w.reshape(C_Q_LORA, C_HEADS, C_NOPE + C_ROPE)
    w = jnp.pad(w, ((0, 0), (0, 0), (0, C_HEAD_PAD - C_NOPE - C_ROPE)))
    return w.reshape(C_Q_LORA, C_HEADS * C_HEAD_PAD).astype(BF16)


def _pack_w_ukv(w):
    w = w.reshape(C_KV_LORA, C_HEADS, C_NOPE + C_V)
    wk = jnp.pad(w[:, :, :C_NOPE], ((0, 0), (0, 0), (0, C_HEAD_PAD - C_NOPE))).reshape(C_KV_LORA, -1)
    wv = w[:, :, C_NOPE:].reshape(C_KV_LORA, -1)
    return jnp.concatenate([wk, wv], axis=-1).astype(BF16)


def kernel(x, c, ctx, c_ctx, w_mod, b_mod, g_norm1, g_norm2, w_in, a_sink, b_lb_logits, b_onorm, c_qnorm, c_kvnorm,
           w_uq, w_ukv, w_br, w_out, w_rg, w_re, w1, w3, w2, g_final):
    B, S, _ = x.shape
    L = ctx.shape[1]
    depth = w_in.shape[0]
    assert L == TOKEN_BLOCK and S % MLA_Q_BLOCK == 0 and S % GRID_W == 0
    assert S % PROJ_BLOCK == 0 and (B * L) % PROJ_BLOCK == 0
    dims = (B, S, L)

    xs = jnp.concatenate([x.reshape(B * S, D_MODEL), ctx.reshape(B * L, D_MODEL)], axis=0)
    cc = jnp.zeros((8, D_MODEL), F32).at[:B].set(c).at[B].set(c_ctx)
    mod_all = _modulation(cc, w_mod, b_mod).reshape(depth, 8, 6, D_MODEL)

    lb_all = jnp.cumsum(jax.nn.softmax(b_lb_logits.astype(F32), axis=0), axis=0)
    lb_all = (lb_all - lb_all[0:1]).reshape(depth, 1, 2 * B_W)
    lbp_all = jnp.concatenate([jnp.log(lb_all), jnp.log1p(-lb_all), 1.0 - lb_all,
                               jnp.zeros((depth, 5, 2 * B_W), F32)], axis=1)

    taba, tabc = _rope_tables(S, L)
    ones = jnp.kron(jnp.eye(B_HEADS, dtype=F32), jnp.ones((B_DK, B_DK), F32)).astype(BF16)

    for l in range(depth):
        need_ctx = l < depth - 1
        mod = mod_all[l]
        wr = jnp.concatenate([w_re[l], w_rg[l], jnp.zeros((D_MODEL, ROUTER_W - N_EXPERTS - N_GROUPS), F32)],
                             axis=-1).astype(BF16)
        w13 = jnp.concatenate([w1[l], w3[l]], axis=-1).astype(BF16)
        sink = jnp.repeat(a_sink[l].astype(F32).reshape(A_KV_HEADS, 1, A_GROUP) * LOG2E, A_QBLOCK, axis=-1)

        qa, ka, va, bqig, gates, qc, kc, vc, gl = _projection(
            xs, mod, g_norm1[l][None], _pack_w_in(w_in[l]), _pack_w_uq(w_uq[l]), _pack_w_ukv(w_ukv[l]),
            c_qnorm[l][None], c_kvnorm[l][None], lbp_all[l], taba, tabc, dims)
        oa = _window_gqa(qa, ka, va, sink, dims, need_ctx)
        ohg = _hgrn2_scan(bqig, gates, dims)
        oc = _mla_attention(qc, kc, vc, dims, need_ctx)
        xs, h2 = _merge(xs, oa, ohg, bqig, oc, gl, w_br[l].astype(BF16), w_out[l].astype(BF16), b_onorm[l][None],
                        mod, g_norm2[l][None], ones, dims, need_ctx)
        xs = _moe(h2, xs, mod, wr, w13, w2[l].astype(BF16), g_final[None], dims, need_ctx, final_norm=not need_ctx)

    return xs.reshape(B, S, D_MODEL)
```

```python
import functools

import jax
import jax.numpy as jnp
import numpy as np
from jax import lax
from jax.experimental import pallas as pl
from jax.experimental.pallas import tpu as pltpu

F32 = jnp.float32
BF16 = jnp.bfloat16
HIGHEST = lax.Precision.HIGHEST

D_MODEL = 1024
GRID_W = 64
HEAD_DIM = 64
ROPE_BASE = 10000.0
EPS = 1e-6
A_HEADS = 8
A_KV_HEADS = 2
A_GROUP = A_HEADS // A_KV_HEADS
A_WINDOW = 128
A_BLOCK = A_WINDOW
A_QBLOCK = 2 * A_BLOCK
B_HEADS = 8
B_DK = 64
B_W = B_HEADS * B_DK
B_CHUNK = 64
B_BLOCK = 256
C_HEADS = 8
C_NOPE = 64
C_ROPE = 32
C_V = 64
C_Q_LORA = 256
C_KV_LORA = 128
C_HEAD_PAD = 128
N_BRANCH = 3
BRANCH_W = 512
N_GROUPS = 4
EXPERTS_PER_GROUP = 8
N_EXPERTS = N_GROUPS * EXPERTS_PER_GROUP
D_EXPERT = 256
ROUTER_W = 128
V_SLAB = 80
LOG2E = 1.4426950408889634

OFF_AQ, OFF_AK, OFF_AV = 0, 512, 640
OFF_BQ, OFF_BI, OFF_BZF, OFF_BZB, OFF_BG = 768, 1280, 1792, 2304, 2816
OFF_CQ, OFF_CKV, OFF_CKR, OFF_GL = 3328, 3584, 3712, 3840
IN_W_PACKED = OFF_GL + N_BRANCH * D_MODEL

TOKEN_BLOCK = 256
PROJ_BLOCK = 512
MOE_TOKEN_BLOCK = 1024
MOE_EXPERTS_PER_STEP = 2
MOE_TILE = 320
MOE_ALIGN = 16
MOE_SORT_PAD = 384
MLA_Q_BLOCK = 512
MLA_K_BLOCK = 1024
MLA_KEY_CHUNK = 256
VMEM_LIMIT = 56 * 1024 * 1024
PROJ_VMEM_LIMIT = 61 * 1024 * 1024


def _cparams(sem, vmem_limit=VMEM_LIMIT):
    return pltpu.CompilerParams(dimension_semantics=sem, vmem_limit_bytes=vmem_limit)


def _mod_kernel(cc_ref, w_ref, b_ref, o_ref):
    cc = cc_ref[...]
    a = cc * jax.nn.sigmoid(cc)
    o_ref[...] = jnp.dot(a, w_ref[...], preferred_element_type=F32, precision=HIGHEST) + b_ref[...]


def _modulation(cc, w_mod, b_mod):
    depth = w_mod.shape[0]
    nj = 6
    return pl.pallas_call(
        _mod_kernel,
        grid=(depth, nj),
        in_specs=[
            pl.BlockSpec((8, D_MODEL), lambda l, j: (0, 0)),
            pl.BlockSpec((None, D_MODEL, D_MODEL), lambda l, j: (l, 0, j)),
            pl.BlockSpec((None, 1, D_MODEL), lambda l, j: (l, 0, j)),
        ],
        out_specs=pl.BlockSpec((None, 8, D_MODEL), lambda l, j: (l, 0, j)),
        out_shape=jax.ShapeDtypeStruct((depth, 8, 6 * D_MODEL), F32),
        compiler_params=_cparams(("arbitrary", "arbitrary")),
        name="adaln_mod",
    )(cc, w_mod, b_mod.reshape(depth, 1, 6 * D_MODEL))


def _rms(x, g):
    return x * lax.rsqrt(jnp.mean(x * x, axis=-1, keepdims=True) + EPS) * g


def _rope(v, tab_ref, half):
    n = v.shape[-1]
    cos = tab_ref[:, 0:128]
    s_lo = tab_ref[:, 128:256]
    s_hi = tab_ref[:, 256:384]
    return v * cos + pltpu.roll(v, n - half, 1) * s_lo + pltpu.roll(v, half, 1) * s_hi


def _store_v_slabs(ref, vt, heads):
    ones = jnp.ones((V_SLAB - HEAD_DIM, vt.shape[1]), BF16)
    for hd in range(heads):
        ref[V_SLAB * hd:V_SLAB * hd + HEAD_DIM, :] = vt[HEAD_DIM * hd:HEAD_DIM * (hd + 1), :].astype(BF16)
        ref[V_SLAB * hd + HEAD_DIM:V_SLAB * (hd + 1), :] = ones


def _proj_kernel(x_ref, mod_ref, g1_ref, w_ref, wuq_ref, wukv_ref, gq_ref, gkv_ref, lbp_ref, taba_ref, tabc_ref,
                 qa_ref, ka_ref, va_ref, bqig_ref, gates_ref, qc_ref, kc_ref, vc_ref, gl_ref):
    x = x_ref[...]
    h = _rms(x, g1_ref[...]) * (1.0 + mod_ref[1:2, :]) + mod_ref[0:1, :]
    hb = h.astype(BF16)

    def seg(off, width):
        return jnp.dot(hb, w_ref[:, off:off + width], preferred_element_type=F32)

    aq = seg(OFF_AQ, 512) * (HEAD_DIM ** -0.5 * LOG2E)
    for j in range(4):
        qa_ref[128 * j:128 * (j + 1), :] = _rope(aq[:, 128 * j:128 * (j + 1)], taba_ref, 32).T.astype(BF16)
    ka_ref[...] = _rope(seg(OFF_AK, 128), taba_ref, 32).astype(BF16)
    _store_v_slabs(va_ref, seg(OFF_AV, 128).T, A_KV_HEADS)

    bqig_ref[:, 0:512] = seg(OFF_BQ, 512).astype(BF16)
    bqig_ref[:, 512:1024] = seg(OFF_BI, 512).astype(BF16)
    bqig_ref[:, 1024:1536] = seg(OFF_BG, 512).astype(BF16)
    for d, off in enumerate((OFF_BZF, OFF_BZB)):
        z = seg(off, 512)
        log_lb = lbp_ref[0:1, 512 * d:512 * (d + 1)]
        log1m_lb = lbp_ref[1:2, 512 * d:512 * (d + 1)]
        one_m_lb = lbp_ref[2:3, 512 * d:512 * (d + 1)]
        e = jnp.exp(-jnp.abs(z))
        log_sig = jnp.minimum(z, 0.0) - jnp.log(1.0 + e)
        b = log1m_lb + log_sig
        mx = jnp.maximum(log_lb, b)
        logf = mx + jnp.log(1.0 + jnp.exp(-jnp.abs(log_lb - b)))
        r = 1.0 / (1.0 + e)
        key = one_m_lb * jnp.where(z >= 0.0, e * r, r)
        gates_ref[:, 512 * d:512 * (d + 1)] = logf
        gates_ref[:, 1024 + 512 * d:1024 + 512 * (d + 1)] = key

    cq = _rms(seg(OFF_CQ, C_Q_LORA), gq_ref[...]).astype(BF16)
    qh = jnp.dot(cq, wuq_ref[...], preferred_element_type=F32) * ((C_NOPE + C_ROPE) ** -0.5 * LOG2E)
    ckv = _rms(seg(OFF_CKV, C_KV_LORA), gkv_ref[...]).astype(BF16)
    kvh = jnp.dot(ckv, wukv_ref[...], preferred_element_type=F32)
    kr = _rope(seg(OFF_CKR, 128), tabc_ref, 16)
    for j in range(C_HEADS):
        sl = slice(C_HEAD_PAD * j, C_HEAD_PAD * (j + 1))
        qc_ref[sl, :] = _rope(qh[:, sl], tabc_ref, 16).T.astype(BF16)
        kc_ref[:, sl] = (kvh[:, sl] + kr).astype(BF16)
    _store_v_slabs(vc_ref, kvh[:, C_HEADS * C_HEAD_PAD:].T, C_HEADS)

    for j in range(6):
        gl_ref[:, 512 * j:512 * (j + 1)] = seg(OFF_GL + 512 * j, 512).astype(BF16)


def _projection(x, mod, g1, w_in_p, wuq_p, wukv_p, gq, gkv, lbp, taba, tabc, dims):
    B, S, L = dims
    n = x.shape[0]
    tm = PROJ_BLOCK
    nlat = B * S // tm
    spb = S // tm

    def row(i):
        return (i, 0)

    def mod_row(i):
        return (jnp.minimum(i // spb, B), 0, 0)

    def tab_row(i):
        return (jnp.where(i < nlat, i % spb, spb), 0)

    const = lambda i: (0, 0)
    resident = functools.partial(pl.BlockSpec, index_map=const, pipeline_mode=pl.Buffered(1))
    widths = (512, 128, A_KV_HEADS * V_SLAB, 1536, 2048, 1024, 1024, C_HEADS * V_SLAB, 3072)
    dtypes = (BF16, BF16, BF16, BF16, F32, BF16, BF16, BF16, BF16)
    transposed = (0, 2, 5, 7)
    return pl.pallas_call(
        _proj_kernel,
        grid=(n // tm,),
        in_specs=[
            pl.BlockSpec((tm, D_MODEL), row),
            pl.BlockSpec((None, 6, D_MODEL), mod_row),
            pl.BlockSpec((1, D_MODEL), const),
            resident((D_MODEL, IN_W_PACKED)),
            resident(wuq_p.shape),
            resident(wukv_p.shape),
            pl.BlockSpec((1, C_Q_LORA), const),
            pl.BlockSpec((1, C_KV_LORA), const),
            pl.BlockSpec((8, 2 * B_W), const),
            pl.BlockSpec((tm, 384), tab_row),
            pl.BlockSpec((tm, 384), tab_row),
        ],
        out_specs=[pl.BlockSpec((w, tm), lambda i: (0, i)) if k in transposed else pl.BlockSpec((tm, w), row)
                   for k, w in enumerate(widths)],
        out_shape=[jax.ShapeDtypeStruct((w, n) if k in transposed else (n, w), dt)
                   for k, (w, dt) in enumerate(zip(widths, dtypes))],
        compiler_params=_cparams(("parallel",), PROJ_VMEM_LIMIT),
        name="in_proj",
    )(x, mod, g1, w_in_p, wuq_p, wukv_p, gq, gkv, lbp, taba, tabc)


def _gqa_kernel(qt_ref, k0_ref, k1_ref, k2_ref, k3_ref, kx_ref, v0_ref, v1_ref, v2_ref, v3_ref, vx_ref, sink_ref,
                o_ref, bias_ref, s0_ref, s1_ref, p0_ref, p1_ref, *, nlat_blocks, seq):
    j = pl.program_id(1)
    nband = 4 * A_BLOCK
    nk = bias_ref.shape[0]
    width = A_GROUP * A_QBLOCK
    rows = lax.broadcasted_iota(jnp.int32, (nk, A_QBLOCK), 0)
    cols = lax.broadcasted_iota(jnp.int32, (nk, A_QBLOCK), 1)
    qpos = j * A_QBLOCK + cols
    kpos = j * A_QBLOCK - A_BLOCK + rows
    kend = jnp.where(j < nlat_blocks, seq, 0)
    valid = ((kpos >= 0) & (kpos < kend) & (jnp.abs(qpos - kpos) <= A_WINDOW)) | (rows >= nband)
    bias = jnp.where(valid, 0.0, -jnp.inf)
    for hh in range(A_GROUP):
        bias_ref[:, A_QBLOCK * hh:A_QBLOCK * (hh + 1)] = bias
    k = jnp.concatenate([k0_ref[...], k1_ref[...], k2_ref[...], k3_ref[...], kx_ref[...]], axis=0)
    vt = jnp.concatenate([v0_ref[...], v1_ref[...], v2_ref[...], v3_ref[...], vx_ref[...]], axis=1)
    s_refs, p_refs = (s0_ref, s1_ref), (p0_ref, p1_ref)
    nchunk = nk // A_QBLOCK
    krows = lambda r: slice(A_QBLOCK * r, A_QBLOCK * (r + 1))
    smax, pv = [None] * A_KV_HEADS, [None] * A_KV_HEADS

    def stage_scores(g, r):
        qg = jnp.concatenate([qt_ref[HEAD_DIM * hd:HEAD_DIM * (hd + 1), :]
                              for hd in range(A_GROUP * g, A_GROUP * (g + 1))], axis=1)
        sc = (jnp.dot(k[krows(r), HEAD_DIM * g:HEAD_DIM * (g + 1)], qg, preferred_element_type=F32)
              + bias_ref[krows(r), :])
        s_refs[g][krows(r), :] = sc
        cm = jnp.max(sc, axis=0, keepdims=True)
        smax[g] = cm if smax[g] is None else jnp.maximum(smax[g], cm)

    def stage_exp(g, r, m):
        p_refs[g][krows(r), :] = jnp.exp2(s_refs[g][krows(r), :] - m).astype(BF16)

    def stage_pv(g, r, m):
        t = jnp.dot(vt[V_SLAB * g:V_SLAB * (g + 1), krows(r)], p_refs[g][krows(r), :], preferred_element_type=F32)
        pv[g] = t if pv[g] is None else pv[g] + t
        if r == nchunk - 1:
            denom = pv[g][HEAD_DIM:HEAD_DIM + 1, :] + jnp.exp2(sink_ref[g] - m)
            o = pv[g][0:HEAD_DIM, :] / denom
            for hh in range(A_GROUP):
                hd = A_GROUP * g + hh
                o_ref[HEAD_DIM * hd:HEAD_DIM * (hd + 1), :] = o[:, A_QBLOCK * hh:A_QBLOCK * (hh + 1)].astype(BF16)

    m = [None] * A_KV_HEADS
    for r in range(nchunk):
        stage_scores(0, r)
    for g in range(A_KV_HEADS + 1):
        if g < A_KV_HEADS:
            m[g] = jnp.maximum(smax[g], sink_ref[g])
        for r in range(nchunk):
            if g + 1 < A_KV_HEADS:
                stage_scores(g + 1, r)
            if g < A_KV_HEADS:
                stage_exp(g, r, m[g])
            if g >= 1:
                stage_pv(g - 1, r, m[g - 1])


def _window_gqa(qat, ka, vat, sink, dims, need_ctx):
    B, S, L = dims
    n = ka.shape[0]
    assert L == A_QBLOCK
    nb = S // A_BLOCK
    nqb = S // A_QBLOCK
    nq = nqb + (1 if need_ctx else 0)
    nk = 4 * A_BLOCK + L
    width = A_GROUP * A_QBLOCK

    def q_col(b, j):
        return (0, jnp.where(j < nqb, b * nqb + j, B * nqb + b))

    def kblk(delta):
        return lambda b, j: b * nb + jnp.clip(2 * j + delta, 0, nb - 1)

    k_specs = [pl.BlockSpec((A_BLOCK, 128), (lambda f: lambda b, j: (f(b, j), 0))(kblk(dl))) for dl in (-1, 0, 1, 2)]
    vs = A_KV_HEADS * V_SLAB
    v_specs = [pl.BlockSpec((vs, A_BLOCK), (lambda f: lambda b, j: (0, f(b, j)))(kblk(dl))) for dl in (-1, 0, 1, 2)]
    return pl.pallas_call(
        functools.partial(_gqa_kernel, nlat_blocks=nqb, seq=S),
        grid=(B, nq),
        in_specs=[pl.BlockSpec((512, A_QBLOCK), q_col)] + k_specs
        + [pl.BlockSpec((L, 128), lambda b, j: (B * S // L + b, 0))] + v_specs
        + [pl.BlockSpec((vs, L), lambda b, j: (0, B * S // L + b)),
           pl.BlockSpec((A_KV_HEADS, 1, width), lambda b, j: (0, 0, 0))],
        out_specs=pl.BlockSpec((512, A_QBLOCK), q_col),
        out_shape=jax.ShapeDtypeStruct((512, n if need_ctx else B * S), BF16),
        scratch_shapes=[pltpu.VMEM((nk, width), F32), pltpu.VMEM((nk, width), F32), pltpu.VMEM((nk, width), F32),
                        pltpu.VMEM((nk, width), BF16), pltpu.VMEM((nk, width), BF16)],
        compiler_params=_cparams(("parallel", "parallel")),
        name="window_gqa",
    )(qat, ka, ka, ka, ka, ka, vat, vat, vat, vat, vat, sink)


def _hgrn_chunk(q_ref, v_ref, g_ref, k_ref, o_ref, st_ref, r0, reverse):
    C = B_CHUNK
    rs = slice(r0, r0 + C)
    rows = lax.broadcasted_iota(jnp.int32, (C, C), 0)
    cols = lax.broadcasted_iota(jnp.int32, (C, C), 1)
    causal = (rows <= cols) if reverse else (rows >= cols)
    g = g_ref[rs, :]
    bc = jnp.dot(causal.astype(F32), g, preferred_element_type=F32, precision=HIGHEST)
    tot = jnp.sum(g, axis=0, keepdims=True)
    mid = C // 2 if reverse else C // 2 - 1
    rho = bc[mid:mid + 1, :]
    q = q_ref[rs, :].astype(F32)
    key = k_ref[rs, :]
    v = v_ref[rs, :]
    qe = (q * jnp.exp(bc - rho)).astype(BF16)
    ke = (key * jnp.exp(rho - bc)).astype(BF16)
    qs = (q * jnp.exp(bc)).astype(BF16)
    ks = (key * jnp.exp(tot - bc)).astype(BF16)
    dec = jnp.exp(tot)
    for hd in range(B_HEADS):
        sl = slice(B_DK * hd, B_DK * (hd + 1))
        st = st_ref[hd]
        att = lax.dot_general(qe[:, sl], ke[:, sl], (((1,), (1,)), ((), ())), preferred_element_type=F32)
        att = jnp.where(causal, att, 0.0).astype(BF16)
        o = lax.dot_general(qs[:, sl], st.astype(BF16), (((1,), (1,)), ((), ())), preferred_element_type=F32)
        o = o + jnp.dot(att, v[:, sl], preferred_element_type=F32)
        o_ref[rs, sl] = o
        upd = lax.dot_general(v[:, sl], ks[:, sl], (((0,), (0,)), ((), ())), preferred_element_type=F32)
        st_ref[hd] = st * dec[:, sl] + upd


def _hgrn_kernel(qf_ref, vf_ref, gf_ref, kf_ref, qb_ref, vb_ref, gb_ref, kb_ref, of_ref, ob_ref, stf_ref, stb_ref):
    @pl.when(pl.program_id(1) == 0)
    def _():
        stf_ref[...] = jnp.zeros_like(stf_ref)
        stb_ref[...] = jnp.zeros_like(stb_ref)

    nchunk = qf_ref.shape[0] // B_CHUNK
    for i in range(nchunk):
        _hgrn_chunk(qf_ref, vf_ref, gf_ref, kf_ref, of_ref, stf_ref, B_CHUNK * i, False)
        _hgrn_chunk(qb_ref, vb_ref, gb_ref, kb_ref, ob_ref, stb_ref, B_CHUNK * (nchunk - 1 - i), True)


def _hgrn2_scan(bqig, gates, dims):
    B, S, L = dims
    n = bqig.shape[0]
    T = B_BLOCK
    assert L == T
    ns = S // T

    def fwd_blk(b, c):
        return jnp.where(c == 0, B * ns + b, b * ns + c - 1)

    def bwd_blk(b, c):
        return jnp.where(c == 0, B * ns + b, b * ns + ns - c)

    def specs(blk, d):
        return [pl.BlockSpec((T, B_W), lambda b, c: (blk(b, c), 0)),
                pl.BlockSpec((T, B_W), lambda b, c: (blk(b, c), 1)),
                pl.BlockSpec((T, B_W), lambda b, c: (blk(b, c), d)),
                pl.BlockSpec((T, B_W), lambda b, c: (blk(b, c), 2 + d))]

    return pl.pallas_call(
        _hgrn_kernel,
        grid=(B, ns + 1),
        in_specs=specs(fwd_blk, 0) + specs(bwd_blk, 1),
        out_specs=[pl.BlockSpec((T, B_W), lambda b, c: (fwd_blk(b, c), 0)),
                   pl.BlockSpec((T, B_W), lambda b, c: (bwd_blk(b, c), 0))],
        out_shape=[jax.ShapeDtypeStruct((n, B_W), F32), jax.ShapeDtypeStruct((n, B_W), F32)],
        scratch_shapes=[pltpu.VMEM((B_HEADS, B_DK, B_DK), F32), pltpu.VMEM((B_HEADS, B_DK, B_DK), F32)],
        compiler_params=_cparams(("parallel", "arbitrary")),
        name="hgrn2_scan",
    )(bqig, bqig, gates, gates, bqig, bqig, gates, gates)


def _mla_kernel(qt_ref, k_ref, vt_ref, *rest, with_ctx):
    if with_ctx:
        kx_ref, vxt_ref, o_ref, m_ref, acc_ref, *bufs = rest
    else:
        _, o_ref, m_ref, acc_ref, *bufs = rest
    s_refs, p_refs = bufs[0:2], bufs[2:4]
    kstep = pl.program_id(2)
    tq = qt_ref.shape[1]

    def kv_pass(k_ref, vt_ref, first):
        nkeys = k_ref.shape[0]
        nchunk = nkeys // MLA_KEY_CHUNK
        krows = lambda r: slice(MLA_KEY_CHUNK * r, MLA_KEY_CHUNK * (r + 1))

        smax = [None] * C_HEADS
        m_new = [None] * C_HEADS
        alpha = [None] * C_HEADS
        pv = [None] * C_HEADS

        def stage_scores(hd, r):
            sl = slice(C_HEAD_PAD * hd, C_HEAD_PAD * (hd + 1))
            sc = jnp.dot(k_ref[krows(r), sl], qt_ref[sl, :], preferred_element_type=F32)
            s_refs[hd % 2][krows(r), :] = sc
            cm = jnp.max(sc, axis=0, keepdims=True)
            smax[hd] = cm if smax[hd] is None else jnp.maximum(smax[hd], cm)

        def stage_stats(hd):
            if first:
                m_new[hd] = smax[hd]
            else:
                m_old = m_ref[hd]
                m_new[hd] = jnp.maximum(m_old, smax[hd])
                alpha[hd] = jnp.exp2(m_old - m_new[hd])
            m_ref[hd] = m_new[hd]

        def stage_exp(hd, r):
            p_refs[hd % 2][krows(r), :] = jnp.exp2(s_refs[hd % 2][krows(r), :] - m_new[hd]).astype(BF16)

        def stage_pv(hd, r):
            t = jnp.dot(vt_ref[V_SLAB * hd:V_SLAB * (hd + 1), krows(r)], p_refs[hd % 2][krows(r), :],
                        preferred_element_type=F32)
            pv[hd] = t if pv[hd] is None else pv[hd] + t
            if r == nchunk - 1:
                acc_ref[hd] = pv[hd] if first else alpha[hd] * acc_ref[hd] + pv[hd]

        for r in range(nchunk):
            stage_scores(0, r)
        for hd in range(C_HEADS + 1):
            if hd < C_HEADS:
                stage_stats(hd)
            for r in range(nchunk):
                if hd + 1 < C_HEADS:
                    stage_scores(hd + 1, r)
                if hd < C_HEADS:
                    stage_exp(hd, r)
                if hd >= 1:
                    stage_pv(hd - 1, r)

    if with_ctx:
        @pl.when(kstep == 0)
        def _():
            kv_pass(kx_ref, vxt_ref, True)

        kv_pass(k_ref, vt_ref, False)
    else:
        kv_pass(k_ref, vt_ref, True)

    @pl.when(kstep == pl.num_programs(2) - 1)
    def _():
        for hd in range(C_HEADS):
            o_ref[C_V * hd:C_V * (hd + 1), :] = (acc_ref[hd, 0:C_V, :] / acc_ref[hd, C_V:C_V + 1, :]).astype(BF16)


def _mla_attention(qct, kc, vct, dims, need_ctx):
    B, S, L = dims
    n = kc.shape[0]
    tq = min(MLA_Q_BLOCK, S)
    tk = min(MLA_K_BLOCK, S)
    nq, nk = S // tq, S // tk
    hw = C_HEADS * C_HEAD_PAD
    vw = C_HEADS * C_V
    vs = C_HEADS * V_SLAB
    scratch = lambda t, nkeys: [
        pltpu.VMEM((C_HEADS, 1, t), F32), pltpu.VMEM((C_HEADS, V_SLAB, t), F32),
        pltpu.VMEM((nkeys, t), F32), pltpu.VMEM((nkeys, t), F32),
        pltpu.VMEM((nkeys, t), BF16), pltpu.VMEM((nkeys, t), BF16)]
    ctx_row = lambda b, i, k: (B * S // L + b, 0)
    ctx_col = lambda b, i, k: (0, B * S // L + b)
    o_lat = pl.pallas_call(
        functools.partial(_mla_kernel, with_ctx=True),
        grid=(B, nq, nk),
        in_specs=[
            pl.BlockSpec((hw, tq), lambda b, i, k: (0, b * nq + i)),
            pl.BlockSpec((tk, hw), lambda b, i, k: (b * nk + k, 0)),
            pl.BlockSpec((vs, tk), lambda b, i, k: (0, b * nk + k)),
            pl.BlockSpec((L, hw), ctx_row),
            pl.BlockSpec((vs, L), ctx_col),
        ],
        out_specs=pl.BlockSpec((vw, tq), lambda b, i, k: (0, b * nq + i)),
        out_shape=jax.ShapeDtypeStruct((vw, n if need_ctx else B * S), BF16),
        scratch_shapes=scratch(tq, tk),
        compiler_params=_cparams(("parallel", "parallel", "arbitrary")),
        name="mla_latent",
    )(qct, kc, vct, kc, vct)
    if not need_ctx:
        return o_lat
    return pl.pallas_call(
        functools.partial(_mla_kernel, with_ctx=False),
        grid=(B, 1, 1),
        in_specs=[
            pl.BlockSpec((hw, L), ctx_col),
            pl.BlockSpec((L, hw), ctx_row),
            pl.BlockSpec((vs, L), ctx_col),
            pl.BlockSpec(memory_space=pl.ANY),
        ],
        out_specs=pl.BlockSpec((vw, L), ctx_col),
        out_shape=jax.ShapeDtypeStruct((vw, n), BF16),
        scratch_shapes=scratch(L, L),
        input_output_aliases={3: 0},
        compiler_params=_cparams(("parallel", "arbitrary", "arbitrary")),
        name="mla_context",
    )(qct, kc, vct, o_lat)


def _group_sum(x, ones_ref):
    hi = x.astype(BF16)
    lo = (x - hi.astype(F32)).astype(BF16)
    return (jnp.dot(hi, ones_ref[...], preferred_element_type=F32)
            + jnp.dot(lo, ones_ref[...], preferred_element_type=F32))


def _merge_kernel(x_ref, oa_ref, of_ref, ob_ref, bg_ref, oc_ref, gl_ref, wbr_ref, wout_ref, gn_ref, mod_ref,
                  g2_ref, ones_ref, xo_ref, h2_ref):
    ob = of_ref[...] + ob_ref[...]
    ms = _group_sum(ob * ob, ones_ref) * (1.0 / B_DK)
    obn = ob * lax.rsqrt(ms + EPS) * gn_ref[...]
    bg = bg_ref[...].astype(F32)
    bb = (obn * (bg * jax.nn.sigmoid(bg))).astype(BF16)
    branches = ((oa_ref[...], 0), (bb, 1), (oc_ref[...], 0))
    y = None
    for nbr, (br, axis) in enumerate(branches):
        gate = jax.nn.sigmoid(gl_ref[:, D_MODEL * nbr:D_MODEL * (nbr + 1)].astype(F32))
        t = gate * lax.dot_general(br, wbr_ref[nbr], (((axis,), (0,)), ((), ())), preferred_element_type=F32)
        y = t if y is None else y + t
    upd = jnp.dot(y.astype(BF16), wout_ref[...], preferred_element_type=F32)
    xn = x_ref[...] + mod_ref[2:3, :] * upd
    xo_ref[...] = xn
    h2 = _rms(xn, g2_ref[...]) * (1.0 + mod_ref[4:5, :]) + mod_ref[3:4, :]
    h2_ref[...] = h2.astype(BF16)


def _merge(x, oa, ohg, bqig, oc, gl, wbr, wout, gn, mod, g2, ones, dims, need_ctx):
    B, S, L = dims
    n = x.shape[0]
    tm = PROJ_BLOCK
    spb = S // tm
    nblk = (n if need_ctx else B * S) // tm
    row = lambda i: (i, 0)
    const2 = lambda i: (0, 0)
    return pl.pallas_call(
        _merge_kernel,
        grid=(nblk,),
        in_specs=[
            pl.BlockSpec((tm, D_MODEL), row),
            pl.BlockSpec((512, tm), lambda i: (0, i)),
            pl.BlockSpec((tm, B_W), row),
            pl.BlockSpec((tm, B_W), row),
            pl.BlockSpec((tm, B_W), lambda i: (i, 2)),
            pl.BlockSpec((512, tm), lambda i: (0, i)),
            pl.BlockSpec((tm, N_BRANCH * D_MODEL), row),
            pl.BlockSpec((N_BRANCH, BRANCH_W, D_MODEL), lambda i: (0, 0, 0)),
            pl.BlockSpec((D_MODEL, D_MODEL), const2),
            pl.BlockSpec((1, B_W), const2),
            pl.BlockSpec((None, 6, D_MODEL), lambda i: (jnp.minimum(i // spb, B), 0, 0)),
            pl.BlockSpec((1, D_MODEL), const2),
            pl.BlockSpec((B_W, B_W), const2),
        ],
        out_specs=[pl.BlockSpec((tm, D_MODEL), row), pl.BlockSpec((tm, D_MODEL), row)],
        out_shape=[jax.ShapeDtypeStruct((nblk * tm, D_MODEL), F32), jax.ShapeDtypeStruct((nblk * tm, D_MODEL), BF16)],
        compiler_params=_cparams(("parallel",)),
        name="branch_merge",
    )(x, oa, ohg[0], ohg[1], bqig, oc, gl, wbr, wout, gn, mod, g2, ones)


def _route_kernel(h_ref, wr_ref, comb_ref, pos_ref, cnt_ref):
    h = h_ref[...]
    T = h.shape[0]
    lane = lax.broadcasted_iota(jnp.int32, (T, ROUTER_W), 1)
    logits = jnp.dot(h, wr_ref[...], preferred_element_type=F32)
    big = jnp.int32(ROUTER_W)
    is_grp = (lane >= N_EXPERTS) & (lane < N_EXPERTS + N_GROUPS)
    gl = jnp.where(is_grp, logits, -jnp.inf)
    gmax = jnp.max(gl, axis=-1, keepdims=True)
    gsel = jnp.min(jnp.where(gl == gmax, lane, big), axis=-1, keepdims=True) - N_EXPERTS
    gw = 1.0 / jnp.sum(jnp.exp(gl - gmax), axis=-1, keepdims=True)
    in_grp = (lane >= gsel * EXPERTS_PER_GROUP) & (lane < (gsel + 1) * EXPERTS_PER_GROUP)
    el = jnp.where(in_grp, logits, -jnp.inf)
    m1 = jnp.max(el, axis=-1, keepdims=True)
    i1 = jnp.min(jnp.where(el == m1, lane, big), axis=-1, keepdims=True)
    el2 = jnp.where(lane == i1, -jnp.inf, el)
    m2 = jnp.max(el2, axis=-1, keepdims=True)
    i2 = jnp.min(jnp.where(el2 == m2, lane, big), axis=-1, keepdims=True)
    e2 = jnp.exp(m2 - m1)
    w1 = gw / (1.0 + e2)
    w2 = gw * e2 / (1.0 + e2)
    comb_ref[...] = jnp.where(lane == i1, w1, 0.0) + jnp.where(lane == i2, w2, 0.0)

    onehot = lane == gsel
    ones = jnp.where(onehot, 1.0, 0.0)
    rows = lax.broadcasted_iota(jnp.int32, (T, T), 0)
    cols = lax.broadcasted_iota(jnp.int32, (T, T), 1)
    earlier = jnp.where(rows > cols, 1.0, 0.0).astype(BF16)
    before = jnp.dot(earlier, ones.astype(BF16), preferred_element_type=F32)
    rank = jnp.sum(jnp.where(onehot, before, 0.0), axis=-1, keepdims=True)
    cnt = jnp.sum(ones, axis=0, keepdims=True)
    padded = jnp.floor((cnt + (MOE_ALIGN - 1)) * (1.0 / MOE_ALIGN)) * MOE_ALIGN
    seg = [jnp.sum(jnp.where(lane[0:1] == g, padded, 0.0), axis=-1, keepdims=True) for g in range(N_GROUPS - 1)]
    start = jnp.where(gsel == 0, 0.0, jnp.where(gsel == 1, seg[0], jnp.where(gsel == 2, seg[0] + seg[1],
                                                                             seg[0] + seg[1] + seg[2])))
    pos_ref[...] = jnp.broadcast_to(start + rank, (T, ROUTER_W))
    cnt_ref[...] = jnp.broadcast_to(cnt, (8, ROUTER_W)).astype(jnp.int32)


def _moe_kernel(cnt_ref, h_ref, x_ref, mod_ref, pos_ref, comb_ref, w13_ref, w2_ref, gf_ref, o_ref,
                pt_ref, xs_ref, cs_ref, ys_ref, *, final_norm):
    i = pl.program_id(0)
    e = pl.program_id(1)
    T = h_ref.shape[0]
    R = xs_ref.shape[0]
    gather = lambda a: lax.dot_general(pt_ref[...], a, (((0,), (0,)), ((), ())), preferred_element_type=F32)

    @pl.when(e == 0)
    def _():
        slot = lax.broadcasted_iota(jnp.int32, (T, R), 1).astype(F32)
        pt_ref[...] = jnp.where(pos_ref[:, 0:1] == slot, 1.0, 0.0).astype(BF16)
        xs_ref[...] = gather(h_ref[...]).astype(BF16)
        comb = comb_ref[...]
        hi = comb.astype(BF16)
        r1 = comb - hi.astype(F32)
        mid = r1.astype(BF16)
        lo = (r1 - mid.astype(F32)).astype(BF16)
        cs_ref[...] = gather(hi) + gather(mid) + gather(lo)
        ys_ref[...] = jnp.zeros_like(ys_ref)

    g = (e * MOE_EXPERTS_PER_STEP) // EXPERTS_PER_GROUP
    cnt = [cnt_ref[i * N_GROUPS + gg] for gg in range(N_GROUPS)]
    seg = [(c + (MOE_ALIGN - 1)) // MOE_ALIGN * MOE_ALIGN for c in cnt]
    start = (jnp.where(g > 0, seg[0], 0) + jnp.where(g > 1, seg[1], 0) + jnp.where(g > 2, seg[2], 0))
    cnt_g = jnp.where(g == 0, cnt[0], jnp.where(g == 1, cnt[1], jnp.where(g == 2, cnt[2], cnt[3])))
    lane = lax.broadcasted_iota(jnp.int32, (MOE_TILE, ROUTER_W), 1)

    def tile(t, carry):
        r0 = pl.multiple_of(start + t * MOE_TILE, MOE_ALIGN)
        xt = xs_ref[pl.ds(r0, MOE_TILE), :]
        ct = cs_ref[pl.ds(r0, MOE_TILE), :]
        acts = []
        for k in range(MOE_EXPERTS_PER_STEP):
            ce = jnp.sum(jnp.where(lane == e * MOE_EXPERTS_PER_STEP + k, ct, 0.0), axis=-1, keepdims=True)
            h13 = jnp.dot(xt, w13_ref[k], preferred_element_type=F32)
            a1 = h13[:, :D_EXPERT]
            acts.append((a1 * jax.nn.sigmoid(a1) * h13[:, D_EXPERT:] * ce).astype(BF16))
        ys_ref[pl.ds(r0, MOE_TILE), :] += jnp.dot(jnp.concatenate(acts, axis=-1), w2_ref[...],
                                                   preferred_element_type=F32)
        return carry

    lax.fori_loop(0, (cnt_g + MOE_TILE - 1) // MOE_TILE, tile, 0)

    @pl.when(e == pl.num_programs(1) - 1)
    def _():
        y = jnp.dot(pt_ref[...], ys_ref[...].astype(BF16), preferred_element_type=F32)
        xn = x_ref[...] + mod_ref[5:6, :] * y
        o_ref[...] = _rms(xn, gf_ref[...]) if final_norm else xn


def _moe(h2, x, mod, wr, w13, w2, g_final, dims, need_ctx, final_norm):
    B, S, L = dims
    n = x.shape[0]
    tm = min(MOE_TOKEN_BLOCK, S, B * L)
    spb = S // tm
    nblk = (n if need_ctx else B * S) // tm
    comb, pos, cnt = pl.pallas_call(
        _route_kernel,
        grid=(nblk,),
        in_specs=[pl.BlockSpec((tm, D_MODEL), lambda i: (i, 0)), pl.BlockSpec((D_MODEL, ROUTER_W), lambda i: (0, 0))],
        out_specs=[pl.BlockSpec((tm, ROUTER_W), lambda i: (i, 0)), pl.BlockSpec((tm, ROUTER_W), lambda i: (i, 0)),
                   pl.BlockSpec((8, ROUTER_W), lambda i: (i, 0))],
        out_shape=[jax.ShapeDtypeStruct((nblk * tm, ROUTER_W), F32), jax.ShapeDtypeStruct((nblk * tm, ROUTER_W), F32),
                   jax.ShapeDtypeStruct((nblk * 8, ROUTER_W), jnp.int32)],
        compiler_params=_cparams(("parallel",)),
        name="moe_route",
    )(h2, wr)
    cnt = cnt.reshape(nblk, 8, ROUTER_W)[:, 0, :N_GROUPS].reshape(nblk * N_GROUPS)

    row = lambda i, e, c: (i, 0)
    eps = MOE_EXPERTS_PER_STEP
    assert MOE_SORT_PAD >= (N_GROUPS - 1) * (MOE_ALIGN - 1) + MOE_TILE - 1 and EXPERTS_PER_GROUP % eps == 0
    slots = tm + MOE_SORT_PAD
    return pl.pallas_call(
        functools.partial(_moe_kernel, final_norm=final_norm),
        grid_spec=pltpu.PrefetchScalarGridSpec(
            num_scalar_prefetch=1,
            grid=(nblk, N_EXPERTS // eps),
            in_specs=[
                pl.BlockSpec((tm, D_MODEL), row),
                pl.BlockSpec((tm, D_MODEL), row),
                pl.BlockSpec((None, 6, D_MODEL), lambda i, e, c: (jnp.minimum(i // spb, B), 0, 0)),
                pl.BlockSpec((tm, ROUTER_W), row),
                pl.BlockSpec((tm, ROUTER_W), row),
                pl.BlockSpec((eps, D_MODEL, 2 * D_EXPERT), lambda i, e, c: (e, 0, 0)),
                pl.BlockSpec((eps * D_EXPERT, D_MODEL), lambda i, e, c: (e, 0)),
                pl.BlockSpec((1, D_MODEL), lambda i, e, c: (0, 0)),
            ],
            out_specs=pl.BlockSpec((tm, D_MODEL), row),
            scratch_shapes=[pltpu.VMEM((tm, slots), BF16), pltpu.VMEM((slots, D_MODEL), BF16),
                            pltpu.VMEM((slots, ROUTER_W), F32), pltpu.VMEM((slots, D_MODEL), F32)],
        ),
        out_shape=jax.ShapeDtypeStruct((nblk * tm, D_MODEL), F32),
        compiler_params=_cparams(("parallel", "arbitrary")),
        name="hier_moe",
    )(cnt, h2, x, mod, pos, comb, w13, w2.reshape(N_EXPERTS * D_EXPERT, D_MODEL), g_final)


def _rope_tables(S, L):
    rows = S // GRID_W
    pos_r = np.repeat(np.arange(rows, dtype=np.float32), GRID_W)
    pos_c = np.tile(np.arange(GRID_W, dtype=np.float32), rows)

    def angles(rot_dim):
        nf = rot_dim // 4
        inv = jnp.asarray(ROPE_BASE, F32) ** (-jnp.arange(nf, dtype=F32) / nf)
        ang = jnp.concatenate([pos_r[:, None] * inv, pos_c[:, None] * inv], axis=-1)
        return jnp.cos(ang), jnp.sin(ang)

    def with_ctx(cos, s_lo, s_hi):
        ident = jnp.concatenate([jnp.ones((PROJ_BLOCK, 128), F32), jnp.zeros((PROJ_BLOCK, 256), F32)], axis=-1)
        return jnp.concatenate([jnp.concatenate([cos, s_lo, s_hi], axis=-1), ident], axis=0)

    cos, sin = angles(HEAD_DIM)
    z = jnp.zeros_like(sin)
    taba = with_ctx(jnp.tile(cos, (1, 4)), jnp.tile(jnp.concatenate([-sin, z], -1), (1, 2)),
                    jnp.tile(jnp.concatenate([z, sin], -1), (1, 2)))
    cos, sin = angles(C_ROPE)
    z = jnp.zeros_like(sin)
    one64, zero64, zero32 = jnp.ones((S, 64), F32), jnp.zeros((S, 64), F32), jnp.zeros((S, 32), F32)
    tabc = with_ctx(jnp.concatenate([one64, cos, cos, one64[:, :32]], -1),
                    jnp.concatenate([zero64, -sin, z, zero32], -1),
                    jnp.concatenate([zero64, z, sin, zero32], -1))
    return taba, tabc


def _pack_w_in(w):
    pad = lambda k: jnp.zeros((w.shape[0], k), w.dtype)
    return jnp.concatenate([w[:, :3712], pad(64), w[:, 3712:3744], pad(32), w[:, 3744:]], axis=-1).astype(BF16)


def _pack_w_uq(w):
    w = w.reshape(C_Q_LORA, C_HEADS, C_NOPE + C_ROPE)
    w = jnp.pad(w, ((0, 0), (0, 0), (0, C_HEAD_PAD - C_NOPE - C_ROPE)))
    return w.reshape(C_Q_LORA, C_HEADS * C_HEAD_PAD).astype(BF16)


def _pack_w_ukv(w):
    w = w.reshape(C_KV_LORA, C_HEADS, C_NOPE + C_V)
    wk = jnp.pad(w[:, :, :C_NOPE], ((0, 0), (0, 0), (0, C_HEAD_PAD - C_NOPE))).reshape(C_KV_LORA, -1)
    wv = w[:, :, C_NOPE:].reshape(C_KV_LORA, -1)
    return jnp.concatenate([wk, wv], axis=-1).astype(BF16)


def kernel(x, c, ctx, c_ctx, w_mod, b_mod, g_norm1, g_norm2, w_in, a_sink, b_lb_logits, b_onorm, c_qnorm, c_kvnorm,
           w_uq, w_ukv, w_br, w_out, w_rg, w_re, w1, w3, w2, g_final):
    B, S, _ = x.shape
    L = ctx.shape[1]
    depth = w_in.shape[0]
    assert L == TOKEN_BLOCK and S % MLA_Q_BLOCK == 0 and S % GRID_W == 0
    assert S % PROJ_BLOCK == 0 and (B * L) % PROJ_BLOCK == 0
    dims = (B, S, L)

    xs = jnp.concatenate([x.reshape(B * S, D_MODEL), ctx.reshape(B * L, D_MODEL)], axis=0)
    cc = jnp.zeros((8, D_MODEL), F32).at[:B].set(c).at[B].set(c_ctx)
    mod_all = _modulation(cc, w_mod, b_mod).reshape(depth, 8, 6, D_MODEL)

    lb_all = jnp.cumsum(jax.nn.softmax(b_lb_logits.astype(F32), axis=0), axis=0)
    lb_all = (lb_all - lb_all[0:1]).reshape(depth, 1, 2 * B_W)
    lbp_all = jnp.concatenate([jnp.log(lb_all), jnp.log1p(-lb_all), 1.0 - lb_all,
                               jnp.zeros((depth, 5, 2 * B_W), F32)], axis=1)

    taba, tabc = _rope_tables(S, L)
    ones = jnp.kron(jnp.eye(B_HEADS, dtype=F32), jnp.ones((B_DK, B_DK), F32)).astype(BF16)

    for l in range(depth):
        need_ctx = l < depth - 1
        mod = mod_all[l]
        wr = jnp.concatenate([w_re[l], w_rg[l], jnp.zeros((D_MODEL, ROUTER_W - N_EXPERTS - N_GROUPS), F32)],
                             axis=-1).astype(BF16)
        w13 = jnp.concatenate([w1[l], w3[l]], axis=-1).astype(BF16)
        sink = jnp.repeat(a_sink[l].astype(F32).reshape(A_KV_HEADS, 1, A_GROUP) * LOG2E, A_QBLOCK, axis=-1)

        qa, ka, va, bqig, gates, qc, kc, vc, gl = _projection(
            xs, mod, g_norm1[l][None], _pack_w_in(w_in[l]), _pack_w_uq(w_uq[l]), _pack_w_ukv(w_ukv[l]),
            c_qnorm[l][None], c_kvnorm[l][None], lbp_all[l], taba, tabc, dims)
        oa = _window_gqa(qa, ka, va, sink, dims, need_ctx)
        ohg = _hgrn2_scan(bqig, gates, dims)
        oc = _mla_attention(qc, kc, vc, dims, need_ctx)
        xs, h2 = _merge(xs, oa, ohg, bqig, oc, gl, w_br[l].astype(BF16), w_out[l].astype(BF16), b_onorm[l][None],
                        mod, g_norm2[l][None], ones, dims, need_ctx)
        xs = _moe(h2, xs, mod, wr, w13, w2[l].astype(BF16), g_final[None], dims, need_ctx, final_norm=not need_ctx)

    return xs.reshape(B, S, D_MODEL)
```

```python
import functools

import jax
import jax.numpy as jnp
import numpy as np
from jax import lax
from jax.experimental import pallas as pl
from jax.experimental.pallas import tpu as pltpu

F32 = jnp.float32
BF16 = jnp.bfloat16
HIGHEST = lax.Precision.HIGHEST

D_MODEL = 1024
GRID_W = 64
HEAD_DIM = 64
ROPE_BASE = 10000.0
EPS = 1e-6
A_HEADS = 8
A_KV_HEADS = 2
A_GROUP = A_HEADS // A_KV_HEADS
A_WINDOW = 128
A_BLOCK = A_WINDOW
A_QBLOCK = 2 * A_BLOCK
B_HEADS = 8
B_DK = 64
B_W = B_HEADS * B_DK
B_CHUNK = 64
B_BLOCK = 256
C_HEADS = 8
C_NOPE = 64
C_ROPE = 32
C_V = 64
C_Q_LORA = 256
C_KV_LORA = 128
C_HEAD_PAD = 128
N_BRANCH = 3
BRANCH_W = 512
N_GROUPS = 4
EXPERTS_PER_GROUP = 8
N_EXPERTS = N_GROUPS * EXPERTS_PER_GROUP
D_EXPERT = 256
ROUTER_W = 128
V_SLAB = 80
LOG2E = 1.4426950408889634

OFF_AQ, OFF_AK, OFF_AV = 0, 512, 640
OFF_BQ, OFF_BI, OFF_BZF, OFF_BZB, OFF_BG = 768, 1280, 1792, 2304, 2816
OFF_CQ, OFF_CKV, OFF_CKR, OFF_GL = 3328, 3584, 3712, 3840
IN_W_PACKED = OFF_GL + N_BRANCH * D_MODEL

TOKEN_BLOCK = 256
PROJ_BLOCK = 512
MOE_TOKEN_BLOCK = 1024
MOE_EXPERTS_PER_STEP = 4
MOE_TILE = 320
MOE_ALIGN = 16
MOE_SORT_PAD = 384
MLA_Q_BLOCK = 512
MLA_K_BLOCK = 1024
MLA_KEY_CHUNK = 256
VMEM_LIMIT = 56 * 1024 * 1024
PROJ_VMEM_LIMIT = 61 * 1024 * 1024


def _cparams(sem, vmem_limit=VMEM_LIMIT, **kw):
    return pltpu.CompilerParams(dimension_semantics=sem, vmem_limit_bytes=vmem_limit, **kw)


def _mod_kernel(cc_ref, w_ref, b_ref, o_ref):
    cc = cc_ref[...]
    a = cc * jax.nn.sigmoid(cc)
    o_ref[...] = jnp.dot(a, w_ref[...], preferred_element_type=F32, precision=HIGHEST) + b_ref[...]


def _modulation(cc, w_mod, b_mod):
    depth = w_mod.shape[0]
    nj = 6
    return pl.pallas_call(
        _mod_kernel,
        grid=(depth, nj),
        in_specs=[
            pl.BlockSpec((8, D_MODEL), lambda l, j: (0, 0)),
            pl.BlockSpec((None, D_MODEL, D_MODEL), lambda l, j: (l, 0, j)),
            pl.BlockSpec((None, 1, D_MODEL), lambda l, j: (l, 0, j)),
        ],
        out_specs=pl.BlockSpec((None, 8, D_MODEL), lambda l, j: (l, 0, j)),
        out_shape=jax.ShapeDtypeStruct((depth, 8, 6 * D_MODEL), F32),
        compiler_params=_cparams(("arbitrary", "arbitrary")),
        name="adaln_mod",
    )(cc, w_mod, b_mod.reshape(depth, 1, 6 * D_MODEL))


def _rms(x, g):
    return x * lax.rsqrt(jnp.mean(x * x, axis=-1, keepdims=True) + EPS) * g


def _rope(v, tab_ref, half):
    n = v.shape[-1]
    cos = tab_ref[:, 0:128]
    s_lo = tab_ref[:, 128:256]
    s_hi = tab_ref[:, 256:384]
    return v * cos + pltpu.roll(v, n - half, 1) * s_lo + pltpu.roll(v, half, 1) * s_hi


def _store_v_slabs(ref, vt, heads):
    ones = jnp.ones((V_SLAB - HEAD_DIM, vt.shape[1]), BF16)
    for hd in range(heads):
        ref[V_SLAB * hd:V_SLAB * hd + HEAD_DIM, :] = vt[HEAD_DIM * hd:HEAD_DIM * (hd + 1), :].astype(BF16)
        ref[V_SLAB * hd + HEAD_DIM:V_SLAB * (hd + 1), :] = ones


def _proj_kernel(x_ref, mod_ref, g1_ref, w_ref, wuq_ref, wukv_ref, gq_ref, gkv_ref, lbp_ref, taba_ref, tabc_ref,
                 qa_ref, ka_ref, va_ref, bqig_ref, gates_ref, qc_ref, kc_ref, vc_ref, gl_ref):
    x = x_ref[...]
    h = _rms(x, g1_ref[...]) * (1.0 + mod_ref[1:2, :]) + mod_ref[0:1, :]
    hb = h.astype(BF16)

    def seg(off, width):
        return jnp.dot(hb, w_ref[:, off:off + width], preferred_element_type=F32)

    aq = seg(OFF_AQ, 512) * (HEAD_DIM ** -0.5 * LOG2E)
    for j in range(4):
        qa_ref[128 * j:128 * (j + 1), :] = _rope(aq[:, 128 * j:128 * (j + 1)], taba_ref, 32).T.astype(BF16)
    ka_ref[...] = _rope(seg(OFF_AK, 128), taba_ref, 32).astype(BF16)
    _store_v_slabs(va_ref, seg(OFF_AV, 128).T, A_KV_HEADS)

    bqig_ref[:, 0:512] = seg(OFF_BQ, 512).astype(BF16)
    bqig_ref[:, 512:1024] = seg(OFF_BI, 512).astype(BF16)
    bqig_ref[:, 1024:1536] = seg(OFF_BG, 512).astype(BF16)
    for d, off in enumerate((OFF_BZF, OFF_BZB)):
        z = seg(off, 512)
        log_lb = lbp_ref[0:1, 512 * d:512 * (d + 1)]
        log1m_lb = lbp_ref[1:2, 512 * d:512 * (d + 1)]
        one_m_lb = lbp_ref[2:3, 512 * d:512 * (d + 1)]
        e = jnp.exp(-jnp.abs(z))
        log_sig = jnp.minimum(z, 0.0) - jnp.log(1.0 + e)
        b = log1m_lb + log_sig
        mx = jnp.maximum(log_lb, b)
        logf = mx + jnp.log(1.0 + jnp.exp(-jnp.abs(log_lb - b)))
        r = 1.0 / (1.0 + e)
        key = one_m_lb * jnp.where(z >= 0.0, e * r, r)
        gates_ref[:, 512 * d:512 * (d + 1)] = logf
        gates_ref[:, 1024 + 512 * d:1024 + 512 * (d + 1)] = key

    cq = _rms(seg(OFF_CQ, C_Q_LORA), gq_ref[...]).astype(BF16)
    qh = jnp.dot(cq, wuq_ref[...], preferred_element_type=F32) * ((C_NOPE + C_ROPE) ** -0.5 * LOG2E)
    ckv = _rms(seg(OFF_CKV, C_KV_LORA), gkv_ref[...]).astype(BF16)
    kvh = jnp.dot(ckv, wukv_ref[...], preferred_element_type=F32)
    kr = _rope(seg(OFF_CKR, 128), tabc_ref, 16)
    for j in range(C_HEADS):
        sl = slice(C_HEAD_PAD * j, C_HEAD_PAD * (j + 1))
        qc_ref[sl, :] = _rope(qh[:, sl], tabc_ref, 16).T.astype(BF16)
        kc_ref[:, sl] = (kvh[:, sl] + kr).astype(BF16)
    _store_v_slabs(vc_ref, kvh[:, C_HEADS * C_HEAD_PAD:].T, C_HEADS)

    for j in range(6):
        gl_ref[:, 512 * j:512 * (j + 1)] = seg(OFF_GL + 512 * j, 512).astype(BF16)


def _projection(x, mod, g1, w_in_p, wuq_p, wukv_p, gq, gkv, lbp, taba, tabc, dims):
    B, S, L = dims
    n = x.shape[0]
    tm = PROJ_BLOCK
    nlat = B * S // tm
    spb = S // tm

    def row(i):
        return (i, 0)

    def mod_row(i):
        return (jnp.minimum(i // spb, B), 0, 0)

    def tab_row(i):
        return (jnp.where(i < nlat, i % spb, spb), 0)

    const = lambda i: (0, 0)
    resident = functools.partial(pl.BlockSpec, index_map=const, pipeline_mode=pl.Buffered(1))
    widths = (512, 128, A_KV_HEADS * V_SLAB, 1536, 2048, 1024, 1024, C_HEADS * V_SLAB, 3072)
    dtypes = (BF16, BF16, BF16, BF16, F32, BF16, BF16, BF16, BF16)
    transposed = (0, 2, 5, 7)
    return pl.pallas_call(
        _proj_kernel,
        grid=(n // tm,),
        in_specs=[
            pl.BlockSpec((tm, D_MODEL), row),
            pl.BlockSpec((None, 6, D_MODEL), mod_row),
            pl.BlockSpec((1, D_MODEL), const),
            resident((D_MODEL, IN_W_PACKED)),
            resident(wuq_p.shape),
            resident(wukv_p.shape),
            pl.BlockSpec((1, C_Q_LORA), const),
            pl.BlockSpec((1, C_KV_LORA), const),
            pl.BlockSpec((8, 2 * B_W), const),
            pl.BlockSpec((tm, 384), tab_row),
            pl.BlockSpec((tm, 384), tab_row),
        ],
        out_specs=[pl.BlockSpec((w, tm), lambda i: (0, i)) if k in transposed else pl.BlockSpec((tm, w), row)
                   for k, w in enumerate(widths)],
        out_shape=[jax.ShapeDtypeStruct((w, n) if k in transposed else (n, w), dt)
                   for k, (w, dt) in enumerate(zip(widths, dtypes))],
        compiler_params=_cparams(("parallel",), PROJ_VMEM_LIMIT),
        name="in_proj",
    )(x, mod, g1, w_in_p, wuq_p, wukv_p, gq, gkv, lbp, taba, tabc)


def _gqa_kernel(qt_ref, k0_ref, k1_ref, k2_ref, k3_ref, kx_ref, v0_ref, v1_ref, v2_ref, v3_ref, vx_ref, sink_ref,
                o_ref, bias_ref, s0_ref, s1_ref, p0_ref, p1_ref, *, nlat_blocks, seq):
    j = pl.program_id(1)
    nband = 4 * A_BLOCK
    nk = bias_ref.shape[0]
    width = A_GROUP * A_QBLOCK
    rows = lax.broadcasted_iota(jnp.int32, (nk, A_QBLOCK), 0)
    cols = lax.broadcasted_iota(jnp.int32, (nk, A_QBLOCK), 1)
    qpos = j * A_QBLOCK + cols
    kpos = j * A_QBLOCK - A_BLOCK + rows
    kend = jnp.where(j < nlat_blocks, seq, 0)
    valid = ((kpos >= 0) & (kpos < kend) & (jnp.abs(qpos - kpos) <= A_WINDOW)) | (rows >= nband)
    bias = jnp.where(valid, 0.0, -jnp.inf)
    for hh in range(A_GROUP):
        bias_ref[:, A_QBLOCK * hh:A_QBLOCK * (hh + 1)] = bias
    k = jnp.concatenate([k0_ref[...], k1_ref[...], k2_ref[...], k3_ref[...], kx_ref[...]], axis=0)
    vt = jnp.concatenate([v0_ref[...], v1_ref[...], v2_ref[...], v3_ref[...], vx_ref[...]], axis=1)
    s_refs, p_refs = (s0_ref, s1_ref), (p0_ref, p1_ref)
    nchunk = nk // A_QBLOCK
    krows = lambda r: slice(A_QBLOCK * r, A_QBLOCK * (r + 1))
    smax, pv = [None] * A_KV_HEADS, [None] * A_KV_HEADS

    def stage_scores(g, r):
        qg = jnp.concatenate([qt_ref[HEAD_DIM * hd:HEAD_DIM * (hd + 1), :]
                              for hd in range(A_GROUP * g, A_GROUP * (g + 1))], axis=1)
        sc = (jnp.dot(k[krows(r), HEAD_DIM * g:HEAD_DIM * (g + 1)], qg, preferred_element_type=F32)
              + bias_ref[krows(r), :])
        s_refs[g][krows(r), :] = sc
        cm = jnp.max(sc, axis=0, keepdims=True)
        smax[g] = cm if smax[g] is None else jnp.maximum(smax[g], cm)

    def stage_exp(g, r, m):
        p_refs[g][krows(r), :] = jnp.exp2(s_refs[g][krows(r), :] - m).astype(BF16)

    def stage_pv(g, r, m):
        t = jnp.dot(vt[V_SLAB * g:V_SLAB * (g + 1), krows(r)], p_refs[g][krows(r), :], preferred_element_type=F32)
        pv[g] = t if pv[g] is None else pv[g] + t
        if r == nchunk - 1:
            denom = pv[g][HEAD_DIM:HEAD_DIM + 1, :] + jnp.exp2(sink_ref[g] - m)
            o = pv[g][0:HEAD_DIM, :] / denom
            for hh in range(A_GROUP):
                hd = A_GROUP * g + hh
                o_ref[HEAD_DIM * hd:HEAD_DIM * (hd + 1), :] = o[:, A_QBLOCK * hh:A_QBLOCK * (hh + 1)].astype(BF16)

    m = [None] * A_KV_HEADS
    for r in range(nchunk):
        stage_scores(0, r)
    for g in range(A_KV_HEADS + 1):
        if g < A_KV_HEADS:
            m[g] = jnp.maximum(smax[g], sink_ref[g])
        for r in range(nchunk):
            if g + 1 < A_KV_HEADS:
                stage_scores(g + 1, r)
            if g < A_KV_HEADS:
                stage_exp(g, r, m[g])
            if g >= 1:
                stage_pv(g - 1, r, m[g - 1])


def _window_gqa(qat, ka, vat, sink, dims, need_ctx):
    B, S, L = dims
    n = ka.shape[0]
    assert L == A_QBLOCK
    nb = S // A_BLOCK
    nqb = S // A_QBLOCK
    nq = nqb + (1 if need_ctx else 0)
    nk = 4 * A_BLOCK + L
    width = A_GROUP * A_QBLOCK

    def q_col(b, j):
        return (0, jnp.where(j < nqb, b * nqb + j, B * nqb + b))

    def kblk(delta):
        return lambda b, j: b * nb + jnp.clip(2 * j + delta, 0, nb - 1)

    k_specs = [pl.BlockSpec((A_BLOCK, 128), (lambda f: lambda b, j: (f(b, j), 0))(kblk(dl))) for dl in (-1, 0, 1, 2)]
    vs = A_KV_HEADS * V_SLAB
    v_specs = [pl.BlockSpec((vs, A_BLOCK), (lambda f: lambda b, j: (0, f(b, j)))(kblk(dl))) for dl in (-1, 0, 1, 2)]
    return pl.pallas_call(
        functools.partial(_gqa_kernel, nlat_blocks=nqb, seq=S),
        grid=(B, nq),
        in_specs=[pl.BlockSpec((512, A_QBLOCK), q_col)] + k_specs
        + [pl.BlockSpec((L, 128), lambda b, j: (B * S // L + b, 0))] + v_specs
        + [pl.BlockSpec((vs, L), lambda b, j: (0, B * S // L + b)),
           pl.BlockSpec((A_KV_HEADS, 1, width), lambda b, j: (0, 0, 0))],
        out_specs=pl.BlockSpec((512, A_QBLOCK), q_col),
        out_shape=jax.ShapeDtypeStruct((512, n if need_ctx else B * S), BF16),
        scratch_shapes=[pltpu.VMEM((nk, width), F32), pltpu.VMEM((nk, width), F32), pltpu.VMEM((nk, width), F32),
                        pltpu.VMEM((nk, width), BF16), pltpu.VMEM((nk, width), BF16)],
        compiler_params=_cparams(("parallel", "parallel")),
        name="window_gqa",
    )(qat, ka, ka, ka, ka, ka, vat, vat, vat, vat, vat, sink)


def _hgrn_prep(q_ref, v_ref, g_ref, k_ref, r0, reverse):
    C = B_CHUNK
    rs = slice(r0, r0 + C)
    rows = lax.broadcasted_iota(jnp.int32, (C, C), 0)
    cols = lax.broadcasted_iota(jnp.int32, (C, C), 1)
    causal = (rows <= cols) if reverse else (rows >= cols)
    g = g_ref[rs, :]
    bc = jnp.dot(causal.astype(F32), g, preferred_element_type=F32, precision=HIGHEST)
    tot = jnp.sum(g, axis=0, keepdims=True)
    mid = C // 2 if reverse else C // 2 - 1
    rho = bc[mid:mid + 1, :]
    q = q_ref[rs, :].astype(F32)
    key = k_ref[rs, :]
    v = v_ref[rs, :]
    qe = (q * jnp.exp(bc - rho)).astype(BF16)
    ke = (key * jnp.exp(rho - bc)).astype(BF16)
    qs = (q * jnp.exp(bc)).astype(BF16)
    ks = (key * jnp.exp(tot - bc)).astype(BF16)
    dec = jnp.exp(tot)
    return rs, causal, v, qe, ke, qs, ks, dec


def _hgrn_heads(prep, o_ref, st_ref):
    rs, causal, v, qe, ke, qs, ks, dec = prep
    for hd in range(B_HEADS):
        sl = slice(B_DK * hd, B_DK * (hd + 1))
        st = st_ref[hd]
        att = lax.dot_general(qe[:, sl], ke[:, sl], (((1,), (1,)), ((), ())), preferred_element_type=F32)
        att = jnp.where(causal, att, 0.0).astype(BF16)
        o = lax.dot_general(qs[:, sl], st.astype(BF16), (((1,), (1,)), ((), ())), preferred_element_type=F32)
        o = o + jnp.dot(att, v[:, sl], preferred_element_type=F32)
        o_ref[rs, sl] = o
        upd = lax.dot_general(v[:, sl], ks[:, sl], (((0,), (0,)), ((), ())), preferred_element_type=F32)
        st_ref[hd] = st * dec[:, sl] + upd


def _hgrn_kernel(qf_ref, vf_ref, gf_ref, kf_ref, qb_ref, vb_ref, gb_ref, kb_ref, of_ref, ob_ref, stf_ref, stb_ref):
    @pl.when(pl.program_id(1) == 0)
    def _():
        stf_ref[...] = jnp.zeros_like(stf_ref)
        stb_ref[...] = jnp.zeros_like(stb_ref)

    nchunk = qf_ref.shape[0] // B_CHUNK
    preps = []
    for i in range(nchunk):
        preps.append((_hgrn_prep(qf_ref, vf_ref, gf_ref, kf_ref, B_CHUNK * i, False), of_ref, stf_ref))
        preps.append((_hgrn_prep(qb_ref, vb_ref, gb_ref, kb_ref, B_CHUNK * (nchunk - 1 - i), True), ob_ref, stb_ref))
    for prep, o_ref, st_ref in preps:
        _hgrn_heads(prep, o_ref, st_ref)


def _hgrn2_scan(bqig, gates, dims):
    B, S, L = dims
    n = bqig.shape[0]
    T = B_BLOCK
    assert L == T
    ns = S // T

    def fwd_blk(b, c):
        return jnp.where(c == 0, B * ns + b, b * ns + c - 1)

    def bwd_blk(b, c):
        return jnp.where(c == 0, B * ns + b, b * ns + ns - c)

    def specs(blk, d):
        return [pl.BlockSpec((T, B_W), lambda b, c: (blk(b, c), 0)),
                pl.BlockSpec((T, B_W), lambda b, c: (blk(b, c), 1)),
                pl.BlockSpec((T, B_W), lambda b, c: (blk(b, c), d)),
                pl.BlockSpec((T, B_W), lambda b, c: (blk(b, c), 2 + d))]

    return pl.pallas_call(
        _hgrn_kernel,
        grid=(B, ns + 1),
        in_specs=specs(fwd_blk, 0) + specs(bwd_blk, 1),
        out_specs=[pl.BlockSpec((T, B_W), lambda b, c: (fwd_blk(b, c), 0)),
                   pl.BlockSpec((T, B_W), lambda b, c: (bwd_blk(b, c), 0))],
        out_shape=[jax.ShapeDtypeStruct((n, B_W), F32), jax.ShapeDtypeStruct((n, B_W), F32)],
        scratch_shapes=[pltpu.VMEM((B_HEADS, B_DK, B_DK), F32), pltpu.VMEM((B_HEADS, B_DK, B_DK), F32)],
        compiler_params=_cparams(("parallel", "arbitrary")),
        name="hgrn2_scan",
    )(bqig, bqig, gates, gates, bqig, bqig, gates, gates)


def _mla_kernel(qt_ref, k_ref, vt_ref, *rest, with_ctx):
    if with_ctx:
        kx_ref, vxt_ref, o_ref, m_ref, acc_ref, *bufs = rest
    else:
        _, o_ref, m_ref, acc_ref, *bufs = rest
    s_refs, p_refs = bufs[0:2], bufs[2:4]
    kstep = pl.program_id(2)
    tq = qt_ref.shape[1]

    def kv_pass(k_ref, vt_ref, first):
        nkeys = k_ref.shape[0]
        nchunk = nkeys // MLA_KEY_CHUNK
        krows = lambda r: slice(MLA_KEY_CHUNK * r, MLA_KEY_CHUNK * (r + 1))

        smax = [None] * C_HEADS
        m_new = [None] * C_HEADS
        alpha = [None] * C_HEADS
        pv = [None] * C_HEADS

        def stage_scores(hd, r):
            sl = slice(C_HEAD_PAD * hd, C_HEAD_PAD * (hd + 1))
            sc = jnp.dot(k_ref[krows(r), sl], qt_ref[sl, :], preferred_element_type=F32)
            s_refs[hd % 2][krows(r), :] = sc
            cm = jnp.max(sc, axis=0, keepdims=True)
            smax[hd] = cm if smax[hd] is None else jnp.maximum(smax[hd], cm)

        def stage_stats(hd):
            if first:
                m_new[hd] = smax[hd]
            else:
                m_old = m_ref[hd]
                m_new[hd] = jnp.maximum(m_old, smax[hd])
                alpha[hd] = jnp.exp2(m_old - m_new[hd])
            m_ref[hd] = m_new[hd]

        def stage_exp(hd, r):
            p_refs[hd % 2][krows(r), :] = jnp.exp2(s_refs[hd % 2][krows(r), :] - m_new[hd]).astype(BF16)

        def stage_pv(hd, r):
            t = jnp.dot(vt_ref[V_SLAB * hd:V_SLAB * (hd + 1), krows(r)], p_refs[hd % 2][krows(r), :],
                        preferred_element_type=F32)
            pv[hd] = t if pv[hd] is None else pv[hd] + t
            if r == nchunk - 1:
                acc_ref[hd] = pv[hd] if first else alpha[hd] * acc_ref[hd] + pv[hd]

        for r in range(nchunk):
            stage_scores(0, r)
        for hd in range(C_HEADS + 1):
            if hd < C_HEADS:
                stage_stats(hd)
            for r in range(nchunk):
                if hd + 1 < C_HEADS:
                    stage_scores(hd + 1, r)
                if hd < C_HEADS:
                    stage_exp(hd, r)
                if hd >= 1:
                    stage_pv(hd - 1, r)

    if with_ctx:
        @pl.when(kstep == 0)
        def _():
            kv_pass(kx_ref, vxt_ref, True)

        kv_pass(k_ref, vt_ref, False)
    else:
        kv_pass(k_ref, vt_ref, True)

    @pl.when(kstep == pl.num_programs(2) - 1)
    def _():
        for hd in range(C_HEADS):
            o_ref[C_V * hd:C_V * (hd + 1), :] = (acc_ref[hd, 0:C_V, :] / acc_ref[hd, C_V:C_V + 1, :]).astype(BF16)


def _mla_attention(qct, kc, vct, dims, need_ctx):
    B, S, L = dims
    n = kc.shape[0]
    tq = min(MLA_Q_BLOCK, S)
    tk = min(MLA_K_BLOCK, S)
    nq, nk = S // tq, S // tk
    hw = C_HEADS * C_HEAD_PAD
    vw = C_HEADS * C_V
    vs = C_HEADS * V_SLAB
    scratch = lambda t, nkeys: [
        pltpu.VMEM((C_HEADS, 1, t), F32), pltpu.VMEM((C_HEADS, V_SLAB, t), F32),
        pltpu.VMEM((nkeys, t), F32), pltpu.VMEM((nkeys, t), F32),
        pltpu.VMEM((nkeys, t), BF16), pltpu.VMEM((nkeys, t), BF16)]
    ctx_row = lambda b, i, k: (B * S // L + b, 0)
    ctx_col = lambda b, i, k: (0, B * S // L + b)
    o_lat = pl.pallas_call(
        functools.partial(_mla_kernel, with_ctx=True),
        grid=(B, nq, nk),
        in_specs=[
            pl.BlockSpec((hw, tq), lambda b, i, k: (0, b * nq + i)),
            pl.BlockSpec((tk, hw), lambda b, i, k: (b * nk + k, 0)),
            pl.BlockSpec((vs, tk), lambda b, i, k: (0, b * nk + k)),
            pl.BlockSpec((L, hw), ctx_row),
            pl.BlockSpec((vs, L), ctx_col),
        ],
        out_specs=pl.BlockSpec((vw, tq), lambda b, i, k: (0, b * nq + i)),
        out_shape=jax.ShapeDtypeStruct((vw, n if need_ctx else B * S), BF16),
        scratch_shapes=scratch(tq, tk),
        compiler_params=_cparams(("parallel", "parallel", "arbitrary")),
        name="mla_latent",
    )(qct, kc, vct, kc, vct)
    if not need_ctx:
        return o_lat
    return pl.pallas_call(
        functools.partial(_mla_kernel, with_ctx=False),
        grid=(B, 1, 1),
        in_specs=[
            pl.BlockSpec((hw, L), ctx_col),
            pl.BlockSpec((L, hw), ctx_row),
            pl.BlockSpec((vs, L), ctx_col),
            pl.BlockSpec(memory_space=pl.ANY),
        ],
        out_specs=pl.BlockSpec((vw, L), ctx_col),
        out_shape=jax.ShapeDtypeStruct((vw, n), BF16),
        scratch_shapes=scratch(L, L),
        input_output_aliases={3: 0},
        compiler_params=_cparams(("parallel", "arbitrary", "arbitrary")),
        name="mla_context",
    )(qct, kc, vct, o_lat)


def _group_sum(x, ones_ref):
    hi = x.astype(BF16)
    lo = (x - hi.astype(F32)).astype(BF16)
    return (jnp.dot(hi, ones_ref[...], preferred_element_type=F32)
            + jnp.dot(lo, ones_ref[...], preferred_element_type=F32))


def _merge_kernel(x_ref, oa_ref, of_ref, ob_ref, bg_ref, oc_ref, gl_ref, wbr_ref, wout_ref, gn_ref, mod_ref,
                  g2_ref, ones_ref, xo_ref, h2_ref):
    ob = of_ref[...] + ob_ref[...]
    ms = _group_sum(ob * ob, ones_ref) * (1.0 / B_DK)
    obn = ob * lax.rsqrt(ms + EPS) * gn_ref[...]
    bg = bg_ref[...].astype(F32)
    bb = (obn * (bg * jax.nn.sigmoid(bg))).astype(BF16)
    branches = ((oa_ref[...], 0), (bb, 1), (oc_ref[...], 0))
    y = None
    for nbr, (br, axis) in enumerate(branches):
        gate = jax.nn.sigmoid(gl_ref[:, D_MODEL * nbr:D_MODEL * (nbr + 1)].astype(F32))
        t = gate * lax.dot_general(br, wbr_ref[nbr], (((axis,), (0,)), ((), ())), preferred_element_type=F32)
        y = t if y is None else y + t
    upd = jnp.dot(y.astype(BF16), wout_ref[...], preferred_element_type=F32)
    xn = x_ref[...] + mod_ref[2:3, :] * upd
    xo_ref[...] = xn
    h2 = _rms(xn, g2_ref[...]) * (1.0 + mod_ref[4:5, :]) + mod_ref[3:4, :]
    h2_ref[...] = h2.astype(BF16)


def _merge(x, oa, ohg, bqig, oc, gl, wbr, wout, gn, mod, g2, ones, dims, need_ctx):
    B, S, L = dims
    n = x.shape[0]
    tm = PROJ_BLOCK
    spb = S // tm
    nblk = (n if need_ctx else B * S) // tm
    row = lambda i: (i, 0)
    const2 = lambda i: (0, 0)
    return pl.pallas_call(
        _merge_kernel,
        grid=(nblk,),
        in_specs=[
            pl.BlockSpec((tm, D_MODEL), row),
            pl.BlockSpec((512, tm), lambda i: (0, i)),
            pl.BlockSpec((tm, B_W), row),
            pl.BlockSpec((tm, B_W), row),
            pl.BlockSpec((tm, B_W), lambda i: (i, 2)),
            pl.BlockSpec((512, tm), lambda i: (0, i)),
            pl.BlockSpec((tm, N_BRANCH * D_MODEL), row),
            pl.BlockSpec((N_BRANCH, BRANCH_W, D_MODEL), lambda i: (0, 0, 0)),
            pl.BlockSpec((D_MODEL, D_MODEL), const2),
            pl.BlockSpec((1, B_W), const2),
            pl.BlockSpec((None, 6, D_MODEL), lambda i: (jnp.minimum(i // spb, B), 0, 0)),
            pl.BlockSpec((1, D_MODEL), const2),
            pl.BlockSpec((B_W, B_W), const2),
        ],
        out_specs=[pl.BlockSpec((tm, D_MODEL), row), pl.BlockSpec((tm, D_MODEL), row)],
        out_shape=[jax.ShapeDtypeStruct((nblk * tm, D_MODEL), F32), jax.ShapeDtypeStruct((nblk * tm, D_MODEL), BF16)],
        compiler_params=_cparams(("parallel",)),
        name="branch_merge",
    )(x, oa, ohg[0], ohg[1], bqig, oc, gl, wbr, wout, gn, mod, g2, ones)


def _route_kernel(h_ref, wr_ref, comb_ref, pos_ref, cnt_ref):
    h = h_ref[...]
    T = h.shape[0]
    lane = lax.broadcasted_iota(jnp.int32, (T, ROUTER_W), 1)
    logits = jnp.dot(h, wr_ref[...], preferred_element_type=F32)
    big = jnp.int32(ROUTER_W)
    is_grp = (lane >= N_EXPERTS) & (lane < N_EXPERTS + N_GROUPS)
    gl = jnp.where(is_grp, logits, -jnp.inf)
    gmax = jnp.max(gl, axis=-1, keepdims=True)
    gsel = jnp.min(jnp.where(gl == gmax, lane, big), axis=-1, keepdims=True) - N_EXPERTS
    gw = 1.0 / jnp.sum(jnp.exp(gl - gmax), axis=-1, keepdims=True)
    in_grp = (lane >= gsel * EXPERTS_PER_GROUP) & (lane < (gsel + 1) * EXPERTS_PER_GROUP)
    el = jnp.where(in_grp, logits, -jnp.inf)
    m1 = jnp.max(el, axis=-1, keepdims=True)
    i1 = jnp.min(jnp.where(el == m1, lane, big), axis=-1, keepdims=True)
    el2 = jnp.where(lane == i1, -jnp.inf, el)
    m2 = jnp.max(el2, axis=-1, keepdims=True)
    i2 = jnp.min(jnp.where(el2 == m2, lane, big), axis=-1, keepdims=True)
    e2 = jnp.exp(m2 - m1)
    w1 = gw / (1.0 + e2)
    w2 = gw * e2 / (1.0 + e2)
    comb_ref[...] = jnp.where(lane == i1, w1, 0.0) + jnp.where(lane == i2, w2, 0.0)

    onehot = lane == gsel
    ones = jnp.where(onehot, 1.0, 0.0)
    rows = lax.broadcasted_iota(jnp.int32, (T, T), 0)
    cols = lax.broadcasted_iota(jnp.int32, (T, T), 1)
    earlier = jnp.where(rows > cols, 1.0, 0.0).astype(BF16)
    before = jnp.dot(earlier, ones.astype(BF16), preferred_element_type=F32)
    rank = jnp.sum(jnp.where(onehot, before, 0.0), axis=-1, keepdims=True)
    cnt = jnp.sum(ones, axis=0, keepdims=True)
    padded = jnp.floor((cnt + (MOE_ALIGN - 1)) * (1.0 / MOE_ALIGN)) * MOE_ALIGN
    seg = [jnp.sum(jnp.where(lane[0:1] == g, padded, 0.0), axis=-1, keepdims=True) for g in range(N_GROUPS - 1)]
    start = jnp.where(gsel == 0, 0.0, jnp.where(gsel == 1, seg[0], jnp.where(gsel == 2, seg[0] + seg[1],
                                                                             seg[0] + seg[1] + seg[2])))
    pos_ref[...] = jnp.broadcast_to(start + rank, (T, ROUTER_W))
    cnt_ref[...] = jnp.broadcast_to(cnt, (8, ROUTER_W)).astype(jnp.int32)


def _moe_kernel(cnt_ref, h_ref, x_ref, mod_ref, pos_ref, comb_ref, w13_ref, w2_ref, gf_ref, o_ref,
                pt_ref, xs_ref, cs_ref, ys_ref, *, final_norm):
    i = pl.program_id(0)
    e = pl.program_id(1)
    T = h_ref.shape[0]
    R = xs_ref.shape[0]
    gather = lambda a: lax.dot_general(pt_ref[...], a, (((0,), (0,)), ((), ())), preferred_element_type=F32)

    @pl.when(e == 0)
    def _():
        slot = lax.broadcasted_iota(jnp.int32, (T, R), 1).astype(F32)
        pt_ref[...] = jnp.where(pos_ref[:, 0:1] == slot, 1.0, 0.0).astype(BF16)
        xs_ref[...] = gather(h_ref[...]).astype(BF16)
        comb = comb_ref[...]
        hi = comb.astype(BF16)
        r1 = comb - hi.astype(F32)
        mid = r1.astype(BF16)
        lo = (r1 - mid.astype(F32)).astype(BF16)
        cs_ref[...] = gather(hi) + gather(mid) + gather(lo)
        ys_ref[...] = jnp.zeros_like(ys_ref)

    g = (e * MOE_EXPERTS_PER_STEP) // EXPERTS_PER_GROUP
    cnt = [cnt_ref[i * N_GROUPS + gg] for gg in range(N_GROUPS)]
    seg = [(c + (MOE_ALIGN - 1)) // MOE_ALIGN * MOE_ALIGN for c in cnt]
    start = (jnp.where(g > 0, seg[0], 0) + jnp.where(g > 1, seg[1], 0) + jnp.where(g > 2, seg[2], 0))
    cnt_g = jnp.where(g == 0, cnt[0], jnp.where(g == 1, cnt[1], jnp.where(g == 2, cnt[2], cnt[3])))
    lane = lax.broadcasted_iota(jnp.int32, (MOE_TILE, ROUTER_W), 1)

    def tile(t, carry):
        r0 = pl.multiple_of(start + t * MOE_TILE, MOE_ALIGN)
        xt = xs_ref[pl.ds(r0, MOE_TILE), :]
        ct = cs_ref[pl.ds(r0, MOE_TILE), :]
        acts = []
        for k in range(MOE_EXPERTS_PER_STEP):
            ce = jnp.sum(jnp.where(lane == e * MOE_EXPERTS_PER_STEP + k, ct, 0.0), axis=-1, keepdims=True)
            h13 = jnp.dot(xt, w13_ref[k], preferred_element_type=F32)
            a1 = h13[:, :D_EXPERT]
            acts.append((a1 * jax.nn.sigmoid(a1) * h13[:, D_EXPERT:] * ce).astype(BF16))
        ys_ref[pl.ds(r0, MOE_TILE), :] += jnp.dot(jnp.concatenate(acts, axis=-1), w2_ref[...],
                                                   preferred_element_type=F32)
        return carry

    lax.fori_loop(0, (cnt_g + MOE_TILE - 1) // MOE_TILE, tile, 0)

    @pl.when(e == pl.num_programs(1) - 1)
    def _():
        y = jnp.dot(pt_ref[...], ys_ref[...].astype(BF16), preferred_element_type=F32)
        xn = x_ref[...] + mod_ref[5:6, :] * y
        o_ref[...] = _rms(xn, gf_ref[...]) if final_norm else xn


def _moe(h2, x, mod, wr, w13, w2, g_final, dims, need_ctx, final_norm):
    B, S, L = dims
    n = x.shape[0]
    tm = min(MOE_TOKEN_BLOCK, S, B * L)
    spb = S // tm
    nblk = (n if need_ctx else B * S) // tm
    comb, pos, cnt = pl.pallas_call(
        _route_kernel,
        grid=(nblk,),
        in_specs=[pl.BlockSpec((tm, D_MODEL), lambda i: (i, 0)), pl.BlockSpec((D_MODEL, ROUTER_W), lambda i: (0, 0))],
        out_specs=[pl.BlockSpec((tm, ROUTER_W), lambda i: (i, 0)), pl.BlockSpec((tm, ROUTER_W), lambda i: (i, 0)),
                   pl.BlockSpec((8, ROUTER_W), lambda i: (i, 0))],
        out_shape=[jax.ShapeDtypeStruct((nblk * tm, ROUTER_W), F32), jax.ShapeDtypeStruct((nblk * tm, ROUTER_W), F32),
                   jax.ShapeDtypeStruct((nblk * 8, ROUTER_W), jnp.int32)],
        compiler_params=_cparams(("parallel",)),
        name="moe_route",
    )(h2, wr)
    cnt = cnt.reshape(nblk, 8, ROUTER_W)[:, 0, :N_GROUPS].reshape(nblk * N_GROUPS)

    row = lambda i, e, c: (i, 0)
    eps = MOE_EXPERTS_PER_STEP
    assert MOE_SORT_PAD >= (N_GROUPS - 1) * (MOE_ALIGN - 1) + MOE_TILE - 1 and EXPERTS_PER_GROUP % eps == 0
    slots = tm + MOE_SORT_PAD
    return pl.pallas_call(
        functools.partial(_moe_kernel, final_norm=final_norm),
        grid_spec=pltpu.PrefetchScalarGridSpec(
            num_scalar_prefetch=1,
            grid=(nblk, N_EXPERTS // eps),
            in_specs=[
                pl.BlockSpec((tm, D_MODEL), row),
                pl.BlockSpec((tm, D_MODEL), row),
                pl.BlockSpec((None, 6, D_MODEL), lambda i, e, c: (jnp.minimum(i // spb, B), 0, 0)),
                pl.BlockSpec((tm, ROUTER_W), row),
                pl.BlockSpec((tm, ROUTER_W), row),
                pl.BlockSpec((eps, D_MODEL, 2 * D_EXPERT), lambda i, e, c: (e, 0, 0)),
                pl.BlockSpec((eps * D_EXPERT, D_MODEL), lambda i, e, c: (e, 0)),
                pl.BlockSpec((1, D_MODEL), lambda i, e, c: (0, 0)),
            ],
            out_specs=pl.BlockSpec((tm, D_MODEL), row),
            scratch_shapes=[pltpu.VMEM((tm, slots), BF16), pltpu.VMEM((slots, D_MODEL), BF16),
                            pltpu.VMEM((slots, ROUTER_W), F32), pltpu.VMEM((slots, D_MODEL), F32)],
        ),
        out_shape=jax.ShapeDtypeStruct((nblk * tm, D_MODEL), F32),
        compiler_params=_cparams(("parallel", "arbitrary")),
        name="hier_moe",
    )(cnt, h2, x, mod, pos, comb, w13, w2.reshape(N_EXPERTS * D_EXPERT, D_MODEL), g_final)


def _rope_tables(S, L):
    rows = S // GRID_W
    pos_r = np.repeat(np.arange(rows, dtype=np.float32), GRID_W)
    pos_c = np.tile(np.arange(GRID_W, dtype=np.float32), rows)

    def angles(rot_dim):
        nf = rot_dim // 4
        inv = jnp.asarray(ROPE_BASE, F32) ** (-jnp.arange(nf, dtype=F32) / nf)
        ang = jnp.concatenate([pos_r[:, None] * inv, pos_c[:, None] * inv], axis=-1)
        return jnp.cos(ang), jnp.sin(ang)

    def with_ctx(cos, s_lo, s_hi):
        ident = jnp.concatenate([jnp.ones((PROJ_BLOCK, 128), F32), jnp.zeros((PROJ_BLOCK, 256), F32)], axis=-1)
        return jnp.concatenate([jnp.concatenate([cos, s_lo, s_hi], axis=-1), ident], axis=0)

    cos, sin = angles(HEAD_DIM)
    z = jnp.zeros_like(sin)
    taba = with_ctx(jnp.tile(cos, (1, 4)), jnp.tile(jnp.concatenate([-sin, z], -1), (1, 2)),
                    jnp.tile(jnp.concatenate([z, sin], -1), (1, 2)))
    cos, sin = angles(C_ROPE)
    z = jnp.zeros_like(sin)
    one64, zero64, zero32 = jnp.ones((S, 64), F32), jnp.zeros((S, 64), F32), jnp.zeros((S, 32), F32)
    tabc = with_ctx(jnp.concatenate([one64, cos, cos, one64[:, :32]], -1),
                    jnp.concatenate([zero64, -sin, z, zero32], -1),
                    jnp.concatenate([zero64, z, sin, zero32], -1))
    return taba, tabc


def _pack_w_in(w):
    pad = lambda k: jnp.zeros((w.shape[0], k), w.dtype)
    return jnp.concatenate([w[:, :3712], pad(64), w[:, 3712:3744], pad(32), w[:, 3744:]], axis=-1).astype(BF16)


def _pack_w_uq(w):
    w = w.reshape(C_Q_LORA, C_HEADS, C_NOPE + C_ROPE)
    w = jnp.pad(w, ((0, 0), (0, 0), (0, C_HEAD_PAD - C_NOPE - C_ROPE)))
    return w.reshape(C_Q_LORA, C_HEADS * C_HEAD_PAD).astype(BF16)


def _pack_w_ukv(w):
    w = w.reshape(C_KV_LORA, C_HEADS, C_NOPE + C_V)
    wk = jnp.pad(w[:, :, :C_NOPE], ((0, 0), (0, 0), (0, C_HEAD_PAD - C_NOPE))).reshape(C_KV_LORA, -1)
    wv = w[:, :, C_NOPE:].reshape(C_KV_LORA, -1)
    return jnp.concatenate([wk, wv], axis=-1).astype(BF16)


def kernel(x, c, ctx, c_ctx, w_mod, b_mod, g_norm1, g_norm2, w_in, a_sink, b_lb_logits, b_onorm, c_qnorm, c_kvnorm,
           w_uq, w_ukv, w_br, w_out, w_rg, w_re, w1, w3, w2, g_final):
    B, S, _ = x.shape
    L = ctx.shape[1]
    depth = w_in.shape[0]
    assert L == TOKEN_BLOCK and S % MLA_Q_BLOCK == 0 and S % GRID_W == 0
    assert S % PROJ_BLOCK == 0 and (B * L) % PROJ_BLOCK == 0
    dims = (B, S, L)

    xs = jnp.concatenate([x.reshape(B * S, D_MODEL), ctx.reshape(B * L, D_MODEL)], axis=0)
    cc = jnp.zeros((8, D_MODEL), F32).at[:B].set(c).at[B].set(c_ctx)
    mod_all = _modulation(cc, w_mod, b_mod).reshape(depth, 8, 6, D_MODEL)

    lb_all = jnp.cumsum(jax.nn.softmax(b_lb_logits.astype(F32), axis=0), axis=0)
    lb_all = (lb_all - lb_all[0:1]).reshape(depth, 1, 2 * B_W)
    lbp_all = jnp.concatenate([jnp.log(lb_all), jnp.log1p(-lb_all), 1.0 - lb_all,
                               jnp.zeros((depth, 5, 2 * B_W), F32)], axis=1)

    taba, tabc = _rope_tables(S, L)
    ones = jnp.kron(jnp.eye(B_HEADS, dtype=F32), jnp.ones((B_DK, B_DK), F32)).astype(BF16)

    for l in range(depth):
        need_ctx = l < depth - 1
        mod = mod_all[l]
        wr = jnp.concatenate([w_re[l], w_rg[l], jnp.zeros((D_MODEL, ROUTER_W - N_EXPERTS - N_GROUPS), F32)],
                             axis=-1).astype(BF16)
        w13 = jnp.concatenate([w1[l], w3[l]], axis=-1).astype(BF16)
        sink = jnp.repeat(a_sink[l].astype(F32).reshape(A_KV_HEADS, 1, A_GROUP) * LOG2E, A_QBLOCK, axis=-1)

        qa, ka, va, bqig, gates, qc, kc, vc, gl = _projection(
            xs, mod, g_norm1[l][None], _pack_w_in(w_in[l]), _pack_w_uq(w_uq[l]), _pack_w_ukv(w_ukv[l]),
            c_qnorm[l][None], c_kvnorm[l][None], lbp_all[l], taba, tabc, dims)
        oa = _window_gqa(qa, ka, va, sink, dims, need_ctx)
        ohg = _hgrn2_scan(bqig, gates, dims)
        oc = _mla_attention(qc, kc, vc, dims, need_ctx)
        xs, h2 = _merge(xs, oa, ohg, bqig, oc, gl, w_br[l].astype(BF16), w_out[l].astype(BF16), b_onorm[l][None],
                        mod, g_norm2[l][None], ones, dims, need_ctx)
        xs = _moe(h2, xs, mod, wr, w13, w2[l].astype(BF16), g_final[None], dims, need_ctx, final_norm=not need_ctx)

    return xs.reshape(B, S, D_MODEL)
```

```python
import functools

import jax
import jax.numpy as jnp
import numpy as np
from jax import lax
from jax.experimental import pallas as pl
from jax.experimental.pallas import tpu as pltpu

F32 = jnp.float32
BF16 = jnp.bfloat16
HIGHEST = lax.Precision.HIGHEST

D_MODEL = 1024
GRID_W = 64
HEAD_DIM = 64
ROPE_BASE = 10000.0
EPS = 1e-6
A_HEADS = 8
A_KV_HEADS = 2
A_GROUP = A_HEADS // A_KV_HEADS
A_WINDOW = 128
A_BLOCK = A_WINDOW
A_QBLOCK = 2 * A_BLOCK
B_HEADS = 8
B_DK = 64
B_W = B_HEADS * B_DK
B_CHUNK = 64
B_BLOCK = 256
C_HEADS = 8
C_NOPE = 64
C_ROPE = 32
C_V = 64
C_Q_LORA = 256
C_KV_LORA = 128
C_HEAD_PAD = 128
N_BRANCH = 3
BRANCH_W = 512
N_GROUPS = 4
EXPERTS_PER_GROUP = 8
N_EXPERTS = N_GROUPS * EXPERTS_PER_GROUP
D_EXPERT = 256
ROUTER_W = 128
V_SLAB = 80
LOG2E = 1.4426950408889634

OFF_AQ, OFF_AK, OFF_AV = 0, 512, 640
OFF_BQ, OFF_BI, OFF_BZF, OFF_BZB, OFF_BG = 768, 1280, 1792, 2304, 2816
OFF_CQ, OFF_CKV, OFF_CKR, OFF_GL = 3328, 3584, 3712, 3840
IN_W_PACKED = OFF_GL + N_BRANCH * D_MODEL

TOKEN_BLOCK = 256
PROJ_BLOCK = 512
MOE_TOKEN_BLOCK = 1024
MOE_EXPERTS_PER_STEP = 4
MOE_TILE = 320
MOE_ALIGN = 16
MOE_SORT_PAD = 384
MLA_Q_BLOCK = 2048
MLA_K_BLOCK = 1024
MLA_COL_GROUP = 512
MLA_KEY_CHUNK = 256
VMEM_LIMIT = 56 * 1024 * 1024
PROJ_VMEM_LIMIT = 61 * 1024 * 1024


def _cparams(sem, vmem_limit=VMEM_LIMIT, **kw):
    return pltpu.CompilerParams(dimension_semantics=sem, vmem_limit_bytes=vmem_limit, **kw)


def _mod_kernel(cc_ref, w_ref, b_ref, o_ref):
    cc = cc_ref[...]
    a = cc * jax.nn.sigmoid(cc)
    o_ref[...] = jnp.dot(a, w_ref[...], preferred_element_type=F32, precision=HIGHEST) + b_ref[...]


def _modulation(cc, w_mod, b_mod):
    depth = w_mod.shape[0]
    nj = 6
    return pl.pallas_call(
        _mod_kernel,
        grid=(depth, nj),
        in_specs=[
            pl.BlockSpec((8, D_MODEL), lambda l, j: (0, 0)),
            pl.BlockSpec((None, D_MODEL, D_MODEL), lambda l, j: (l, 0, j)),
            pl.BlockSpec((None, 1, D_MODEL), lambda l, j: (l, 0, j)),
        ],
        out_specs=pl.BlockSpec((None, 8, D_MODEL), lambda l, j: (l, 0, j)),
        out_shape=jax.ShapeDtypeStruct((depth, 8, 6 * D_MODEL), F32),
        compiler_params=_cparams(("arbitrary", "arbitrary")),
        name="adaln_mod",
    )(cc, w_mod, b_mod.reshape(depth, 1, 6 * D_MODEL))


def _rms(x, g):
    return x * lax.rsqrt(jnp.mean(x * x, axis=-1, keepdims=True) + EPS) * g


def _rope(v, tab_ref, half):
    n = v.shape[-1]
    cos = tab_ref[:, 0:128]
    s_lo = tab_ref[:, 128:256]
    s_hi = tab_ref[:, 256:384]
    return v * cos + pltpu.roll(v, n - half, 1) * s_lo + pltpu.roll(v, half, 1) * s_hi


def _store_v_slabs(ref, vt, heads):
    ones = jnp.ones((V_SLAB - HEAD_DIM, vt.shape[1]), BF16)
    for hd in range(heads):
        ref[V_SLAB * hd:V_SLAB * hd + HEAD_DIM, :] = vt[HEAD_DIM * hd:HEAD_DIM * (hd + 1), :].astype(BF16)
        ref[V_SLAB * hd + HEAD_DIM:V_SLAB * (hd + 1), :] = ones


def _proj_kernel(x_ref, mod_ref, g1_ref, w_ref, wuq_ref, wukv_ref, gq_ref, gkv_ref, lbp_ref, taba_ref, tabc_ref,
                 qa_ref, ka_ref, va_ref, bqig_ref, gates_ref, qc_ref, kc_ref, vc_ref, gl_ref):
    x = x_ref[...]
    h = _rms(x, g1_ref[...]) * (1.0 + mod_ref[1:2, :]) + mod_ref[0:1, :]
    hb = h.astype(BF16)

    def seg(off, width):
        return jnp.dot(hb, w_ref[:, off:off + width], preferred_element_type=F32)

    aq = seg(OFF_AQ, 512) * (HEAD_DIM ** -0.5 * LOG2E)
    for j in range(4):
        qa_ref[128 * j:128 * (j + 1), :] = _rope(aq[:, 128 * j:128 * (j + 1)], taba_ref, 32).T.astype(BF16)
    ka_ref[...] = _rope(seg(OFF_AK, 128), taba_ref, 32).astype(BF16)
    _store_v_slabs(va_ref, seg(OFF_AV, 128).T, A_KV_HEADS)

    bqig_ref[:, 0:512] = seg(OFF_BQ, 512).astype(BF16)
    bqig_ref[:, 512:1024] = seg(OFF_BI, 512).astype(BF16)
    bqig_ref[:, 1024:1536] = seg(OFF_BG, 512).astype(BF16)
    for d, off in enumerate((OFF_BZF, OFF_BZB)):
        z = seg(off, 512)
        log_lb = lbp_ref[0:1, 512 * d:512 * (d + 1)]
        log1m_lb = lbp_ref[1:2, 512 * d:512 * (d + 1)]
        one_m_lb = lbp_ref[2:3, 512 * d:512 * (d + 1)]
        e = jnp.exp(-jnp.abs(z))
        log_sig = jnp.minimum(z, 0.0) - jnp.log(1.0 + e)
        b = log1m_lb + log_sig
        mx = jnp.maximum(log_lb, b)
        logf = mx + jnp.log(1.0 + jnp.exp(-jnp.abs(log_lb - b)))
        r = 1.0 / (1.0 + e)
        key = one_m_lb * jnp.where(z >= 0.0, e * r, r)
        gates_ref[:, 512 * d:512 * (d + 1)] = logf
        gates_ref[:, 1024 + 512 * d:1024 + 512 * (d + 1)] = key

    cq = _rms(seg(OFF_CQ, C_Q_LORA), gq_ref[...]).astype(BF16)
    qh = jnp.dot(cq, wuq_ref[...], preferred_element_type=F32) * ((C_NOPE + C_ROPE) ** -0.5 * LOG2E)
    ckv = _rms(seg(OFF_CKV, C_KV_LORA), gkv_ref[...]).astype(BF16)
    kvh = jnp.dot(ckv, wukv_ref[...], preferred_element_type=F32)
    kr = _rope(seg(OFF_CKR, 128), tabc_ref, 16)
    for j in range(C_HEADS):
        sl = slice(C_HEAD_PAD * j, C_HEAD_PAD * (j + 1))
        qc_ref[sl, :] = _rope(qh[:, sl], tabc_ref, 16).T.astype(BF16)
        kc_ref[:, sl] = (kvh[:, sl] + kr).astype(BF16)
    _store_v_slabs(vc_ref, kvh[:, C_HEADS * C_HEAD_PAD:].T, C_HEADS)

    for j in range(6):
        gl_ref[:, 512 * j:512 * (j + 1)] = seg(OFF_GL + 512 * j, 512).astype(BF16)


def _projection(x, mod, g1, w_in_p, wuq_p, wukv_p, gq, gkv, lbp, taba, tabc, dims):
    B, S, L = dims
    n = x.shape[0]
    tm = PROJ_BLOCK
    nlat = B * S // tm
    spb = S // tm

    def row(i):
        return (i, 0)

    def mod_row(i):
        return (jnp.minimum(i // spb, B), 0, 0)

    def tab_row(i):
        return (jnp.where(i < nlat, i % spb, spb), 0)

    const = lambda i: (0, 0)
    resident = functools.partial(pl.BlockSpec, index_map=const, pipeline_mode=pl.Buffered(1))
    widths = (512, 128, A_KV_HEADS * V_SLAB, 1536, 2048, 1024, 1024, C_HEADS * V_SLAB, 3072)
    dtypes = (BF16, BF16, BF16, BF16, F32, BF16, BF16, BF16, BF16)
    transposed = (0, 2, 5, 7)
    return pl.pallas_call(
        _proj_kernel,
        grid=(n // tm,),
        in_specs=[
            pl.BlockSpec((tm, D_MODEL), row),
            pl.BlockSpec((None, 6, D_MODEL), mod_row),
            pl.BlockSpec((1, D_MODEL), const),
            resident((D_MODEL, IN_W_PACKED)),
            resident(wuq_p.shape),
            resident(wukv_p.shape),
            pl.BlockSpec((1, C_Q_LORA), const),
            pl.BlockSpec((1, C_KV_LORA), const),
            pl.BlockSpec((8, 2 * B_W), const),
            pl.BlockSpec((tm, 384), tab_row),
            pl.BlockSpec((tm, 384), tab_row),
        ],
        out_specs=[pl.BlockSpec((w, tm), lambda i: (0, i)) if k in transposed else pl.BlockSpec((tm, w), row)
                   for k, w in enumerate(widths)],
        out_shape=[jax.ShapeDtypeStruct((w, n) if k in transposed else (n, w), dt)
                   for k, (w, dt) in enumerate(zip(widths, dtypes))],
        compiler_params=_cparams(("parallel",), PROJ_VMEM_LIMIT),
        name="in_proj",
    )(x, mod, g1, w_in_p, wuq_p, wukv_p, gq, gkv, lbp, taba, tabc)


def _gqa_kernel(qt_ref, k0_ref, k1_ref, k2_ref, k3_ref, kx_ref, v0_ref, v1_ref, v2_ref, v3_ref, vx_ref, sink_ref,
                o_ref, bias_ref, s0_ref, s1_ref, p0_ref, p1_ref, *, nlat_blocks, seq):
    j = pl.program_id(1)
    nband = 4 * A_BLOCK
    nk = bias_ref.shape[0]
    width = A_GROUP * A_QBLOCK
    rows = lax.broadcasted_iota(jnp.int32, (nk, A_QBLOCK), 0)
    cols = lax.broadcasted_iota(jnp.int32, (nk, A_QBLOCK), 1)
    qpos = j * A_QBLOCK + cols
    kpos = j * A_QBLOCK - A_BLOCK + rows
    kend = jnp.where(j < nlat_blocks, seq, 0)
    valid = ((kpos >= 0) & (kpos < kend) & (jnp.abs(qpos - kpos) <= A_WINDOW)) | (rows >= nband)
    bias = jnp.where(valid, 0.0, -jnp.inf)
    for hh in range(A_GROUP):
        bias_ref[:, A_QBLOCK * hh:A_QBLOCK * (hh + 1)] = bias
    k = jnp.concatenate([k0_ref[...], k1_ref[...], k2_ref[...], k3_ref[...], kx_ref[...]], axis=0)
    vt = jnp.concatenate([v0_ref[...], v1_ref[...], v2_ref[...], v3_ref[...], vx_ref[...]], axis=1)
    s_refs, p_refs = (s0_ref, s1_ref), (p0_ref, p1_ref)
    nchunk = nk // A_QBLOCK
    krows = lambda r: slice(A_QBLOCK * r, A_QBLOCK * (r + 1))
    smax, pv = [None] * A_KV_HEADS, [None] * A_KV_HEADS

    def stage_scores(g, r):
        qg = jnp.concatenate([qt_ref[HEAD_DIM * hd:HEAD_DIM * (hd + 1), :]
                              for hd in range(A_GROUP * g, A_GROUP * (g + 1))], axis=1)
        sc = (jnp.dot(k[krows(r), HEAD_DIM * g:HEAD_DIM * (g + 1)], qg, preferred_element_type=F32)
              + bias_ref[krows(r), :])
        s_refs[g][krows(r), :] = sc
        cm = jnp.max(sc, axis=0, keepdims=True)
        smax[g] = cm if smax[g] is None else jnp.maximum(smax[g], cm)

    def stage_exp(g, r, m):
        p_refs[g][krows(r), :] = jnp.exp2(s_refs[g][krows(r), :] - m).astype(BF16)

    def stage_pv(g, r, m):
        t = jnp.dot(vt[V_SLAB * g:V_SLAB * (g + 1), krows(r)], p_refs[g][krows(r), :], preferred_element_type=F32)
        pv[g] = t if pv[g] is None else pv[g] + t
        if r == nchunk - 1:
            denom = pv[g][HEAD_DIM:HEAD_DIM + 1, :] + jnp.exp2(sink_ref[g] - m)
            o = pv[g][0:HEAD_DIM, :] / denom
            for hh in range(A_GROUP):
                hd = A_GROUP * g + hh
                o_ref[HEAD_DIM * hd:HEAD_DIM * (hd + 1), :] = o[:, A_QBLOCK * hh:A_QBLOCK * (hh + 1)].astype(BF16)

    m = [None] * A_KV_HEADS
    for r in range(nchunk):
        stage_scores(0, r)
    for g in range(A_KV_HEADS + 1):
        if g < A_KV_HEADS:
            m[g] = jnp.maximum(smax[g], sink_ref[g])
        for r in range(nchunk):
            if g + 1 < A_KV_HEADS:
                stage_scores(g + 1, r)
            if g < A_KV_HEADS:
                stage_exp(g, r, m[g])
            if g >= 1:
                stage_pv(g - 1, r, m[g - 1])


def _window_gqa(qat, ka, vat, sink, dims, need_ctx):
    B, S, L = dims
    n = ka.shape[0]
    assert L == A_QBLOCK
    nb = S // A_BLOCK
    nqb = S // A_QBLOCK
    nq = nqb + (1 if need_ctx else 0)
    nk = 4 * A_BLOCK + L
    width = A_GROUP * A_QBLOCK

    def q_col(b, j):
        return (0, jnp.where(j < nqb, b * nqb + j, B * nqb + b))

    def kblk(delta):
        return lambda b, j: b * nb + jnp.clip(2 * j + delta, 0, nb - 1)

    k_specs = [pl.BlockSpec((A_BLOCK, 128), (lambda f: lambda b, j: (f(b, j), 0))(kblk(dl))) for dl in (-1, 0, 1, 2)]
    vs = A_KV_HEADS * V_SLAB
    v_specs = [pl.BlockSpec((vs, A_BLOCK), (lambda f: lambda b, j: (0, f(b, j)))(kblk(dl))) for dl in (-1, 0, 1, 2)]
    return pl.pallas_call(
        functools.partial(_gqa_kernel, nlat_blocks=nqb, seq=S),
        grid=(B, nq),
        in_specs=[pl.BlockSpec((512, A_QBLOCK), q_col)] + k_specs
        + [pl.BlockSpec((L, 128), lambda b, j: (B * S // L + b, 0))] + v_specs
        + [pl.BlockSpec((vs, L), lambda b, j: (0, B * S // L + b)),
           pl.BlockSpec((A_KV_HEADS, 1, width), lambda b, j: (0, 0, 0))],
        out_specs=pl.BlockSpec((512, A_QBLOCK), q_col),
        out_shape=jax.ShapeDtypeStruct((512, n if need_ctx else B * S), BF16),
        scratch_shapes=[pltpu.VMEM((nk, width), F32), pltpu.VMEM((nk, width), F32), pltpu.VMEM((nk, width), F32),
                        pltpu.VMEM((nk, width), BF16), pltpu.VMEM((nk, width), BF16)],
        compiler_params=_cparams(("parallel", "parallel")),
        name="window_gqa",
    )(qat, ka, ka, ka, ka, ka, vat, vat, vat, vat, vat, sink)


def _hgrn_prep(q_ref, v_ref, g_ref, k_ref, r0, reverse):
    C = B_CHUNK
    rs = slice(r0, r0 + C)
    rows = lax.broadcasted_iota(jnp.int32, (C, C), 0)
    cols = lax.broadcasted_iota(jnp.int32, (C, C), 1)
    causal = (rows <= cols) if reverse else (rows >= cols)
    g = g_ref[rs, :]
    bc = jnp.dot(causal.astype(F32), g, preferred_element_type=F32, precision=HIGHEST)
    tot = jnp.sum(g, axis=0, keepdims=True)
    mid = C // 2 if reverse else C // 2 - 1
    rho = bc[mid:mid + 1, :]
    q = q_ref[rs, :].astype(F32)
    key = k_ref[rs, :]
    v = v_ref[rs, :]
    qe = (q * jnp.exp(bc - rho)).astype(BF16)
    ke = (key * jnp.exp(rho - bc)).astype(BF16)
    qs = (q * jnp.exp(bc)).astype(BF16)
    ks = (key * jnp.exp(tot - bc)).astype(BF16)
    dec = jnp.exp(tot)
    return rs, causal, v, qe, ke, qs, ks, dec


def _hgrn_heads(prep, o_ref, st_ref):
    rs, causal, v, qe, ke, qs, ks, dec = prep
    for hd in range(B_HEADS):
        sl = slice(B_DK * hd, B_DK * (hd + 1))
        st = st_ref[hd]
        att = lax.dot_general(qe[:, sl], ke[:, sl], (((1,), (1,)), ((), ())), preferred_element_type=F32)
        att = jnp.where(causal, att, 0.0).astype(BF16)
        o = lax.dot_general(qs[:, sl], st.astype(BF16), (((1,), (1,)), ((), ())), preferred_element_type=F32)
        o = o + jnp.dot(att, v[:, sl], preferred_element_type=F32)
        o_ref[rs, sl] = o
        upd = lax.dot_general(v[:, sl], ks[:, sl], (((0,), (0,)), ((), ())), preferred_element_type=F32)
        st_ref[hd] = st * dec[:, sl] + upd


def _hgrn_kernel(qf_ref, vf_ref, gf_ref, kf_ref, qb_ref, vb_ref, gb_ref, kb_ref, of_ref, ob_ref, stf_ref, stb_ref):
    @pl.when(pl.program_id(1) == 0)
    def _():
        stf_ref[...] = jnp.zeros_like(stf_ref)
        stb_ref[...] = jnp.zeros_like(stb_ref)

    nchunk = qf_ref.shape[0] // B_CHUNK
    preps = []
    for i in range(nchunk):
        preps.append((_hgrn_prep(qf_ref, vf_ref, gf_ref, kf_ref, B_CHUNK * i, False), of_ref, stf_ref))
        preps.append((_hgrn_prep(qb_ref, vb_ref, gb_ref, kb_ref, B_CHUNK * (nchunk - 1 - i), True), ob_ref, stb_ref))
    for prep, o_ref, st_ref in preps:
        _hgrn_heads(prep, o_ref, st_ref)


def _hgrn2_scan(bqig, gates, dims):
    B, S, L = dims
    n = bqig.shape[0]
    T = B_BLOCK
    assert L == T
    ns = S // T

    def fwd_blk(b, c):
        return jnp.where(c == 0, B * ns + b, b * ns + c - 1)

    def bwd_blk(b, c):
        return jnp.where(c == 0, B * ns + b, b * ns + ns - c)

    def specs(blk, d):
        return [pl.BlockSpec((T, B_W), lambda b, c: (blk(b, c), 0)),
                pl.BlockSpec((T, B_W), lambda b, c: (blk(b, c), 1)),
                pl.BlockSpec((T, B_W), lambda b, c: (blk(b, c), d)),
                pl.BlockSpec((T, B_W), lambda b, c: (blk(b, c), 2 + d))]

    return pl.pallas_call(
        _hgrn_kernel,
        grid=(B, ns + 1),
        in_specs=specs(fwd_blk, 0) + specs(bwd_blk, 1),
        out_specs=[pl.BlockSpec((T, B_W), lambda b, c: (fwd_blk(b, c), 0)),
                   pl.BlockSpec((T, B_W), lambda b, c: (bwd_blk(b, c), 0))],
        out_shape=[jax.ShapeDtypeStruct((n, B_W), F32), jax.ShapeDtypeStruct((n, B_W), F32)],
        scratch_shapes=[pltpu.VMEM((B_HEADS, B_DK, B_DK), F32), pltpu.VMEM((B_HEADS, B_DK, B_DK), F32)],
        compiler_params=_cparams(("parallel", "arbitrary")),
        name="hgrn2_scan",
    )(bqig, bqig, gates, gates, bqig, bqig, gates, gates)


def _mla_kernel(qt_ref, k_ref, vt_ref, *rest, with_ctx):
    if with_ctx:
        kx_ref, vxt_ref, o_ref, m_ref, acc_ref, *bufs = rest
    else:
        _, o_ref, m_ref, acc_ref, *bufs = rest
    s_refs, p_refs = bufs[0:2], bufs[2:4]
    kstep = pl.program_id(2)
    tq = qt_ref.shape[1]

    def kv_pass(k_ref, vt_ref, first):
        nkeys = k_ref.shape[0]
        nchunk = nkeys // MLA_KEY_CHUNK
        krows = lambda r: slice(MLA_KEY_CHUNK * r, MLA_KEY_CHUNK * (r + 1))

        group = s_refs[0].shape[1]
        items = [(c, hd) for c in range(tq // group) for hd in range(C_HEADS)]
        n = len(items)
        smax = [None] * n
        m_new = [None] * n
        alpha = [None] * n
        pv = [None] * n
        cols = lambda i: slice(group * items[i][0], group * (items[i][0] + 1))

        def stage_scores(i, r):
            hd = items[i][1]
            sl = slice(C_HEAD_PAD * hd, C_HEAD_PAD * (hd + 1))
            sc = jnp.dot(k_ref[krows(r), sl], qt_ref[sl, cols(i)], preferred_element_type=F32)
            s_refs[i % 2][krows(r), :] = sc
            cm = jnp.max(sc, axis=0, keepdims=True)
            smax[i] = cm if smax[i] is None else jnp.maximum(smax[i], cm)

        def stage_stats(i):
            hd = items[i][1]
            if first:
                m_new[i] = smax[i]
            else:
                m_old = m_ref[hd, :, cols(i)]
                m_new[i] = jnp.maximum(m_old, smax[i])
                alpha[i] = jnp.exp2(m_old - m_new[i])
            m_ref[hd, :, cols(i)] = m_new[i]

        def stage_exp(i, r):
            p_refs[i % 2][krows(r), :] = jnp.exp2(s_refs[i % 2][krows(r), :] - m_new[i]).astype(BF16)

        def stage_pv(i, r):
            hd = items[i][1]
            t = jnp.dot(vt_ref[V_SLAB * hd:V_SLAB * (hd + 1), krows(r)], p_refs[i % 2][krows(r), :],
                        preferred_element_type=F32)
            pv[i] = t if pv[i] is None else pv[i] + t
            if r == nchunk - 1:
                acc_ref[hd, :, cols(i)] = pv[i] if first else alpha[i] * acc_ref[hd, :, cols(i)] + pv[i]

        for r in range(nchunk):
            stage_scores(0, r)
        for i in range(n + 1):
            if i < n:
                stage_stats(i)
            for r in range(nchunk):
                if i + 1 < n:
                    stage_scores(i + 1, r)
                if i < n:
                    stage_exp(i, r)
                if i >= 1:
                    stage_pv(i - 1, r)

    if with_ctx:
        @pl.when(kstep == 0)
        def _():
            kv_pass(kx_ref, vxt_ref, True)

        kv_pass(k_ref, vt_ref, False)
    else:
        kv_pass(k_ref, vt_ref, True)

    @pl.when(kstep == pl.num_programs(2) - 1)
    def _():
        for hd in range(C_HEADS):
            o_ref[C_V * hd:C_V * (hd + 1), :] = (acc_ref[hd, 0:C_V, :] / acc_ref[hd, C_V:C_V + 1, :]).astype(BF16)


def _mla_attention(qct, kc, vct, dims, need_ctx):
    B, S, L = dims
    n = kc.shape[0]
    tq = min(MLA_Q_BLOCK, S)
    tk = min(MLA_K_BLOCK, S)
    nq, nk = S // tq, S // tk
    hw = C_HEADS * C_HEAD_PAD
    vw = C_HEADS * C_V
    vs = C_HEADS * V_SLAB
    scratch = lambda t, nkeys: [
        pltpu.VMEM((C_HEADS, 1, t), F32), pltpu.VMEM((C_HEADS, V_SLAB, t), F32),
        pltpu.VMEM((nkeys, min(t, MLA_COL_GROUP)), F32), pltpu.VMEM((nkeys, min(t, MLA_COL_GROUP)), F32),
        pltpu.VMEM((nkeys, min(t, MLA_COL_GROUP)), BF16), pltpu.VMEM((nkeys, min(t, MLA_COL_GROUP)), BF16)]
    ctx_row = lambda b, i, k: (B * S // L + b, 0)
    ctx_col = lambda b, i, k: (0, B * S // L + b)
    o_lat = pl.pallas_call(
        functools.partial(_mla_kernel, with_ctx=True),
        grid=(B, nq, nk),
        in_specs=[
            pl.BlockSpec((hw, tq), lambda b, i, k: (0, b * nq + i)),
            pl.BlockSpec((tk, hw), lambda b, i, k: (b * nk + k, 0)),
            pl.BlockSpec((vs, tk), lambda b, i, k: (0, b * nk + k)),
            pl.BlockSpec((L, hw), ctx_row),
            pl.BlockSpec((vs, L), ctx_col),
        ],
        out_specs=pl.BlockSpec((vw, tq), lambda b, i, k: (0, b * nq + i)),
        out_shape=jax.ShapeDtypeStruct((vw, n if need_ctx else B * S), BF16),
        scratch_shapes=scratch(tq, tk),
        compiler_params=_cparams(("parallel", "parallel", "arbitrary")),
        name="mla_latent",
    )(qct, kc, vct, kc, vct)
    if not need_ctx:
        return o_lat
    return pl.pallas_call(
        functools.partial(_mla_kernel, with_ctx=False),
        grid=(B, 1, 1),
        in_specs=[
            pl.BlockSpec((hw, L), ctx_col),
            pl.BlockSpec((L, hw), ctx_row),
            pl.BlockSpec((vs, L), ctx_col),
            pl.BlockSpec(memory_space=pl.ANY),
        ],
        out_specs=pl.BlockSpec((vw, L), ctx_col),
        out_shape=jax.ShapeDtypeStruct((vw, n), BF16),
        scratch_shapes=scratch(L, L),
        input_output_aliases={3: 0},
        compiler_params=_cparams(("parallel", "arbitrary", "arbitrary")),
        name="mla_context",
    )(qct, kc, vct, o_lat)


def _group_sum(x, ones_ref):
    hi = x.astype(BF16)
    lo = (x - hi.astype(F32)).astype(BF16)
    return (jnp.dot(hi, ones_ref[...], preferred_element_type=F32)
            + jnp.dot(lo, ones_ref[...], preferred_element_type=F32))


def _merge_kernel(x_ref, oa_ref, of_ref, ob_ref, bg_ref, oc_ref, gl_ref, wbr_ref, wout_ref, gn_ref, mod_ref,
                  g2_ref, ones_ref, xo_ref, h2_ref):
    ob = of_ref[...] + ob_ref[...]
    ms = _group_sum(ob * ob, ones_ref) * (1.0 / B_DK)
    obn = ob * lax.rsqrt(ms + EPS) * gn_ref[...]
    bg = bg_ref[...].astype(F32)
    bb = (obn * (bg * jax.nn.sigmoid(bg))).astype(BF16)
    branches = ((oa_ref[...], 0), (bb, 1), (oc_ref[...], 0))
    y = None
    for nbr, (br, axis) in enumerate(branches):
        gate = jax.nn.sigmoid(gl_ref[:, D_MODEL * nbr:D_MODEL * (nbr + 1)].astype(F32))
        t = gate * lax.dot_general(br, wbr_ref[nbr], (((axis,), (0,)), ((), ())), preferred_element_type=F32)
        y = t if y is None else y + t
    upd = jnp.dot(y.astype(BF16), wout_ref[...], preferred_element_type=F32)
    xn = x_ref[...] + mod_ref[2:3, :] * upd
    xo_ref[...] = xn
    h2 = _rms(xn, g2_ref[...]) * (1.0 + mod_ref[4:5, :]) + mod_ref[3:4, :]
    h2_ref[...] = h2.astype(BF16)


def _merge(x, oa, ohg, bqig, oc, gl, wbr, wout, gn, mod, g2, ones, dims, need_ctx):
    B, S, L = dims
    n = x.shape[0]
    tm = PROJ_BLOCK
    spb = S // tm
    nblk = (n if need_ctx else B * S) // tm
    row = lambda i: (i, 0)
    const2 = lambda i: (0, 0)
    return pl.pallas_call(
        _merge_kernel,
        grid=(nblk,),
        in_specs=[
            pl.BlockSpec((tm, D_MODEL), row),
            pl.BlockSpec((512, tm), lambda i: (0, i)),
            pl.BlockSpec((tm, B_W), row),
            pl.BlockSpec((tm, B_W), row),
            pl.BlockSpec((tm, B_W), lambda i: (i, 2)),
            pl.BlockSpec((512, tm), lambda i: (0, i)),
            pl.BlockSpec((tm, N_BRANCH * D_MODEL), row),
            pl.BlockSpec((N_BRANCH, BRANCH_W, D_MODEL), lambda i: (0, 0, 0)),
            pl.BlockSpec((D_MODEL, D_MODEL), const2),
            pl.BlockSpec((1, B_W), const2),
            pl.BlockSpec((None, 6, D_MODEL), lambda i: (jnp.minimum(i // spb, B), 0, 0)),
            pl.BlockSpec((1, D_MODEL), const2),
            pl.BlockSpec((B_W, B_W), const2),
        ],
        out_specs=[pl.BlockSpec((tm, D_MODEL), row), pl.BlockSpec((tm, D_MODEL), row)],
        out_shape=[jax.ShapeDtypeStruct((nblk * tm, D_MODEL), F32), jax.ShapeDtypeStruct((nblk * tm, D_MODEL), BF16)],
        compiler_params=_cparams(("parallel",)),
        name="branch_merge",
    )(x, oa, ohg[0], ohg[1], bqig, oc, gl, wbr, wout, gn, mod, g2, ones)


def _route_kernel(h_ref, wr_ref, comb_ref, pos_ref, cnt_ref):
    h = h_ref[...]
    T = h.shape[0]
    lane = lax.broadcasted_iota(jnp.int32, (T, ROUTER_W), 1)
    logits = jnp.dot(h, wr_ref[...], preferred_element_type=F32)
    big = jnp.int32(ROUTER_W)
    is_grp = (lane >= N_EXPERTS) & (lane < N_EXPERTS + N_GROUPS)
    gl = jnp.where(is_grp, logits, -jnp.inf)
    gmax = jnp.max(gl, axis=-1, keepdims=True)
    gsel = jnp.min(jnp.where(gl == gmax, lane, big), axis=-1, keepdims=True) - N_EXPERTS
    gw = 1.0 / jnp.sum(jnp.exp(gl - gmax), axis=-1, keepdims=True)
    in_grp = (lane >= gsel * EXPERTS_PER_GROUP) & (lane < (gsel + 1) * EXPERTS_PER_GROUP)
    el = jnp.where(in_grp, logits, -jnp.inf)
    m1 = jnp.max(el, axis=-1, keepdims=True)
    i1 = jnp.min(jnp.where(el == m1, lane, big), axis=-1, keepdims=True)
    el2 = jnp.where(lane == i1, -jnp.inf, el)
    m2 = jnp.max(el2, axis=-1, keepdims=True)
    i2 = jnp.min(jnp.where(el2 == m2, lane, big), axis=-1, keepdims=True)
    e2 = jnp.exp(m2 - m1)
    w1 = gw / (1.0 + e2)
    w2 = gw * e2 / (1.0 + e2)
    comb_ref[...] = jnp.where(lane == i1, w1, 0.0) + jnp.where(lane == i2, w2, 0.0)

    onehot = lane == gsel
    ones = jnp.where(onehot, 1.0, 0.0)
    rows = lax.broadcasted_iota(jnp.int32, (T, T), 0)
    cols = lax.broadcasted_iota(jnp.int32, (T, T), 1)
    earlier = jnp.where(rows > cols, 1.0, 0.0).astype(BF16)
    before = jnp.dot(earlier, ones.astype(BF16), preferred_element_type=F32)
    rank = jnp.sum(jnp.where(onehot, before, 0.0), axis=-1, keepdims=True)
    cnt = jnp.sum(ones, axis=0, keepdims=True)
    padded = jnp.floor((cnt + (MOE_ALIGN - 1)) * (1.0 / MOE_ALIGN)) * MOE_ALIGN
    seg = [jnp.sum(jnp.where(lane[0:1] == g, padded, 0.0), axis=-1, keepdims=True) for g in range(N_GROUPS - 1)]
    start = jnp.where(gsel == 0, 0.0, jnp.where(gsel == 1, seg[0], jnp.where(gsel == 2, seg[0] + seg[1],
                                                                             seg[0] + seg[1] + seg[2])))
    pos_ref[...] = jnp.broadcast_to(start + rank, (T, ROUTER_W))
    cnt_ref[...] = jnp.broadcast_to(cnt, (8, ROUTER_W)).astype(jnp.int32)


def _moe_kernel(cnt_ref, h_ref, x_ref, mod_ref, pos_ref, comb_ref, w13_ref, w2_ref, gf_ref, o_ref,
                pt_ref, xs_ref, cs_ref, ys_ref, *, final_norm):
    i = pl.program_id(0)
    e = pl.program_id(1)
    T = h_ref.shape[0]
    R = xs_ref.shape[0]
    gather = lambda a: lax.dot_general(pt_ref[...], a, (((0,), (0,)), ((), ())), preferred_element_type=F32)

    @pl.when(e == 0)
    def _():
        slot = lax.broadcasted_iota(jnp.int32, (T, R), 1).astype(F32)
        pt_ref[...] = jnp.where(pos_ref[:, 0:1] == slot, 1.0, 0.0).astype(BF16)
        xs_ref[...] = gather(h_ref[...]).astype(BF16)
        comb = comb_ref[...]
        hi = comb.astype(BF16)
        r1 = comb - hi.astype(F32)
        mid = r1.astype(BF16)
        lo = (r1 - mid.astype(F32)).astype(BF16)
        cs_ref[...] = gather(hi) + gather(mid) + gather(lo)
        ys_ref[...] = jnp.zeros_like(ys_ref)

    g = (e * MOE_EXPERTS_PER_STEP) // EXPERTS_PER_GROUP
    cnt = [cnt_ref[i * N_GROUPS + gg] for gg in range(N_GROUPS)]
    seg = [(c + (MOE_ALIGN - 1)) // MOE_ALIGN * MOE_ALIGN for c in cnt]
    start = (jnp.where(g > 0, seg[0], 0) + jnp.where(g > 1, seg[1], 0) + jnp.where(g > 2, seg[2], 0))
    cnt_g = jnp.where(g == 0, cnt[0], jnp.where(g == 1, cnt[1], jnp.where(g == 2, cnt[2], cnt[3])))
    lane = lax.broadcasted_iota(jnp.int32, (MOE_TILE, ROUTER_W), 1)

    def tile(t, carry):
        r0 = pl.multiple_of(start + t * MOE_TILE, MOE_ALIGN)
        xt = xs_ref[pl.ds(r0, MOE_TILE), :]
        ct = cs_ref[pl.ds(r0, MOE_TILE), :]
        acts = []
        for k in range(MOE_EXPERTS_PER_STEP):
            ce = jnp.sum(jnp.where(lane == e * MOE_EXPERTS_PER_STEP + k, ct, 0.0), axis=-1, keepdims=True)
            h13 = jnp.dot(xt, w13_ref[k], preferred_element_type=F32)
            a1 = h13[:, :D_EXPERT]
            acts.append((a1 * jax.nn.sigmoid(a1) * h13[:, D_EXPERT:] * ce).astype(BF16))
        ys_ref[pl.ds(r0, MOE_TILE), :] += jnp.dot(jnp.concatenate(acts, axis=-1), w2_ref[...],
                                                   preferred_element_type=F32)
        return carry

    lax.fori_loop(0, (cnt_g + MOE_TILE - 1) // MOE_TILE, tile, 0)

    @pl.when(e == pl.num_programs(1) - 1)
    def _():
        y = jnp.dot(pt_ref[...], ys_ref[...].astype(BF16), preferred_element_type=F32)
        xn = x_ref[...] + mod_ref[5:6, :] * y
        o_ref[...] = _rms(xn, gf_ref[...]) if final_norm else xn


def _moe(h2, x, mod, wr, w13, w2, g_final, dims, need_ctx, final_norm):
    B, S, L = dims
    n = x.shape[0]
    tm = min(MOE_TOKEN_BLOCK, S, B * L)
    spb = S // tm
    nblk = (n if need_ctx else B * S) // tm
    comb, pos, cnt = pl.pallas_call(
        _route_kernel,
        grid=(nblk,),
        in_specs=[pl.BlockSpec((tm, D_MODEL), lambda i: (i, 0)), pl.BlockSpec((D_MODEL, ROUTER_W), lambda i: (0, 0))],
        out_specs=[pl.BlockSpec((tm, ROUTER_W), lambda i: (i, 0)), pl.BlockSpec((tm, ROUTER_W), lambda i: (i, 0)),
                   pl.BlockSpec((8, ROUTER_W), lambda i: (i, 0))],
        out_shape=[jax.ShapeDtypeStruct((nblk * tm, ROUTER_W), F32), jax.ShapeDtypeStruct((nblk * tm, ROUTER_W), F32),
                   jax.ShapeDtypeStruct((nblk * 8, ROUTER_W), jnp.int32)],
        compiler_params=_cparams(("parallel",)),
        name="moe_route",
    )(h2, wr)
    cnt = cnt.reshape(nblk, 8, ROUTER_W)[:, 0, :N_GROUPS].reshape(nblk * N_GROUPS)

    row = lambda i, e, c: (i, 0)
    eps = MOE_EXPERTS_PER_STEP
    assert MOE_SORT_PAD >= (N_GROUPS - 1) * (MOE_ALIGN - 1) + MOE_TILE - 1 and EXPERTS_PER_GROUP % eps == 0
    slots = tm + MOE_SORT_PAD
    return pl.pallas_call(
        functools.partial(_moe_kernel, final_norm=final_norm),
        grid_spec=pltpu.PrefetchScalarGridSpec(
            num_scalar_prefetch=1,
            grid=(nblk, N_EXPERTS // eps),
            in_specs=[
                pl.BlockSpec((tm, D_MODEL), row),
                pl.BlockSpec((tm, D_MODEL), row),
                pl.BlockSpec((None, 6, D_MODEL), lambda i, e, c: (jnp.minimum(i // spb, B), 0, 0)),
                pl.BlockSpec((tm, ROUTER_W), row),
                pl.BlockSpec((tm, ROUTER_W), row),
                pl.BlockSpec((eps, D_MODEL, 2 * D_EXPERT), lambda i, e, c: (e, 0, 0)),
                pl.BlockSpec((eps * D_EXPERT, D_MODEL), lambda i, e, c: (e, 0)),
                pl.BlockSpec((1, D_MODEL), lambda i, e, c: (0, 0)),
            ],
            out_specs=pl.BlockSpec((tm, D_MODEL), row),
            scratch_shapes=[pltpu.VMEM((tm, slots), BF16), pltpu.VMEM((slots, D_MODEL), BF16),
                            pltpu.VMEM((slots, ROUTER_W), F32), pltpu.VMEM((slots, D_MODEL), F32)],
        ),
        out_shape=jax.ShapeDtypeStruct((nblk * tm, D_MODEL), F32),
        compiler_params=_cparams(("parallel", "arbitrary")),
        name="hier_moe",
    )(cnt, h2, x, mod, pos, comb, w13, w2.reshape(N_EXPERTS * D_EXPERT, D_MODEL), g_final)


def _rope_tables(S, L):
    rows = S // GRID_W
    pos_r = np.repeat(np.arange(rows, dtype=np.float32), GRID_W)
    pos_c = np.tile(np.arange(GRID_W, dtype=np.float32), rows)

    def angles(rot_dim):
        nf = rot_dim // 4
        inv = jnp.asarray(ROPE_BASE, F32) ** (-jnp.arange(nf, dtype=F32) / nf)
        ang = jnp.concatenate([pos_r[:, None] * inv, pos_c[:, None] * inv], axis=-1)
        return jnp.cos(ang), jnp.sin(ang)

    def with_ctx(cos, s_lo, s_hi):
        ident = jnp.concatenate([jnp.ones((PROJ_BLOCK, 128), F32), jnp.zeros((PROJ_BLOCK, 256), F32)], axis=-1)
        return jnp.concatenate([jnp.concatenate([cos, s_lo, s_hi], axis=-1), ident], axis=0)

    cos, sin = angles(HEAD_DIM)
    z = jnp.zeros_like(sin)
    taba = with_ctx(jnp.tile(cos, (1, 4)), jnp.tile(jnp.concatenate([-sin, z], -1), (1, 2)),
                    jnp.tile(jnp.concatenate([z, sin], -1), (1, 2)))
    cos, sin = angles(C_ROPE)
    z = jnp.zeros_like(sin)
    one64, zero64, zero32 = jnp.ones((S, 64), F32), jnp.zeros((S, 64), F32), jnp.zeros((S, 32), F32)
    tabc = with_ctx(jnp.concatenate([one64, cos, cos, one64[:, :32]], -1),
                    jnp.concatenate([zero64, -sin, z, zero32], -1),
                    jnp.concatenate([zero64, z, sin, zero32], -1))
    return taba, tabc


def _pack_w_in(w):
    pad = lambda k: jnp.zeros((w.shape[0], k), w.dtype)
    return jnp.concatenate([w[:, :3712], pad(64), w[:, 3712:3744], pad(32), w[:, 3744:]], axis=-1).astype(BF16)


def _pack_w_uq(w):
    w = w.reshape(C_Q_LORA, C_HEADS, C_NOPE + C_ROPE)
    w = jnp.pad(w, ((0, 0), (0, 0), (0, C_HEAD_PAD - C_NOPE - C_ROPE)))
    return w.reshape(C_Q_LORA, C_HEADS * C_HEAD_PAD).astype(BF16)


def _pack_w_ukv(w):
    w = w.reshape(C_KV_LORA, C_HEADS, C_NOPE + C_V)
    wk = jnp.pad(w[:, :, :C_NOPE], ((0, 0), (0, 0), (0, C_HEAD_PAD - C_NOPE))).reshape(C_KV_LORA, -1)
    wv = w[:, :, C_NOPE:].reshape(C_KV_LORA, -1)
    return jnp.concatenate([wk, wv], axis=-1).astype(BF16)


def kernel(x, c, ctx, c_ctx, w_mod, b_mod, g_norm1, g_norm2, w_in, a_sink, b_lb_logits, b_onorm, c_qnorm, c_kvnorm,
           w_uq, w_ukv, w_br, w_out, w_rg, w_re, w1, w3, w2, g_final):
    B, S, _ = x.shape
    L = ctx.shape[1]
    depth = w_in.shape[0]
    assert L == TOKEN_BLOCK and S % MLA_Q_BLOCK == 0 and S % GRID_W == 0
    assert S % PROJ_BLOCK == 0 and (B * L) % PROJ_BLOCK == 0
    dims = (B, S, L)

    xs = jnp.concatenate([x.reshape(B * S, D_MODEL), ctx.reshape(B * L, D_MODEL)], axis=0)
    cc = jnp.zeros((8, D_MODEL), F32).at[:B].set(c).at[B].set(c_ctx)
    mod_all = _modulation(cc, w_mod, b_mod).reshape(depth, 8, 6, D_MODEL)

    lb_all = jnp.cumsum(jax.nn.softmax(b_lb_logits.astype(F32), axis=0), axis=0)
    lb_all = (lb_all - lb_all[0:1]).reshape(depth, 1, 2 * B_W)
    lbp_all = jnp.concatenate([jnp.log(lb_all), jnp.log1p(-lb_all), 1.0 - lb_all,
                               jnp.zeros((depth, 5, 2 * B_W), F32)], axis=1)

    taba, tabc = _rope_tables(S, L)
    ones = jnp.kron(jnp.eye(B_HEADS, dtype=F32), jnp.ones((B_DK, B_DK), F32)).astype(BF16)

    for l in range(depth):
        need_ctx = l < depth - 1
        mod = mod_all[l]
        wr = jnp.concatenate([w_re[l], w_rg[l], jnp.zeros((D_MODEL, ROUTER_W - N_EXPERTS - N_GROUPS), F32)],
                             axis=-1).astype(BF16)
        w13 = jnp.concatenate([w1[l], w3[l]], axis=-1).astype(BF16)
        sink = jnp.repeat(a_sink[l].astype(F32).reshape(A_KV_HEADS, 1, A_GROUP) * LOG2E, A_QBLOCK, axis=-1)

        qa, ka, va, bqig, gates, qc, kc, vc, gl = _projection(
            xs, mod, g_norm1[l][None], _pack_w_in(w_in[l]), _pack_w_uq(w_uq[l]), _pack_w_ukv(w_ukv[l]),
            c_qnorm[l][None], c_kvnorm[l][None], lbp_all[l], taba, tabc, dims)
        oa = _window_gqa(qa, ka, va, sink, dims, need_ctx)
        ohg = _hgrn2_scan(bqig, gates, dims)
        oc = _mla_attention(qc, kc, vc, dims, need_ctx)
        xs, h2 = _merge(xs, oa, ohg, bqig, oc, gl, w_br[l].astype(BF16), w_out[l].astype(BF16), b_onorm[l][None],
                        mod, g_norm2[l][None], ones, dims, need_ctx)
        xs = _moe(h2, xs, mod, wr, w13, w2[l].astype(BF16), g_final[None], dims, need_ctx, final_norm=not need_ctx)

    return xs.reshape(B, S, D_MODEL)
```

```python
import functools

import jax
import jax.numpy as jnp
import numpy as np
from jax import lax
from jax.experimental import pallas as pl
from jax.experimental.pallas import tpu as pltpu

F32 = jnp.float32
BF16 = jnp.bfloat16
HIGHEST = lax.Precision.HIGHEST

D_MODEL = 1024
GRID_W = 64
HEAD_DIM = 64
ROPE_BASE = 10000.0
EPS = 1e-6
A_HEADS = 8
A_KV_HEADS = 2
A_GROUP = A_HEADS // A_KV_HEADS
A_WINDOW = 128
A_BLOCK = A_WINDOW
A_QBLOCK = 2 * A_BLOCK
A_STEP_BLOCKS = 4
B_HEADS = 8
B_DK = 64
B_W = B_HEADS * B_DK
B_CHUNK = 64
B_BLOCK = 256
C_HEADS = 8
C_NOPE = 64
C_ROPE = 32
C_V = 64
C_Q_LORA = 256
C_KV_LORA = 128
C_HEAD_PAD = 128
N_BRANCH = 3
BRANCH_W = 512
N_GROUPS = 4
EXPERTS_PER_GROUP = 8
N_EXPERTS = N_GROUPS * EXPERTS_PER_GROUP
D_EXPERT = 256
ROUTER_W = 128
V_SLAB = 80
LOG2E = 1.4426950408889634

OFF_AQ, OFF_AK, OFF_AV = 0, 512, 640
OFF_BQ, OFF_BI, OFF_BZF, OFF_BZB, OFF_BG = 768, 1280, 1792, 2304, 2816
OFF_CQ, OFF_CKV, OFF_CKR, OFF_GL = 3328, 3584, 3712, 3840
IN_W_PACKED = OFF_GL + N_BRANCH * D_MODEL

TOKEN_BLOCK = 256
PROJ_BLOCK = 512
MOE_TOKEN_BLOCK = 1024
MOE_EXPERTS_PER_STEP = 4
MOE_TILE = 288
MOE_ALIGN = 16
MOE_SORT_PAD = 384
MLA_Q_BLOCK = 2048
MLA_K_BLOCK = 1024
MLA_COL_GROUP = 512
MLA_KEY_CHUNK = 256
VMEM_LIMIT = 56 * 1024 * 1024
PROJ_VMEM_LIMIT = 61 * 1024 * 1024


def _cparams(sem, vmem_limit=VMEM_LIMIT, **kw):
    return pltpu.CompilerParams(dimension_semantics=sem, vmem_limit_bytes=vmem_limit, **kw)


def _mod_kernel(cc_ref, w_ref, b_ref, o_ref):
    cc = cc_ref[...]
    a = cc * jax.nn.sigmoid(cc)
    o_ref[...] = jnp.dot(a, w_ref[...], preferred_element_type=F32, precision=HIGHEST) + b_ref[...]


def _modulation(cc, w_mod, b_mod):
    depth = w_mod.shape[0]
    nj = 6
    return pl.pallas_call(
        _mod_kernel,
        grid=(depth, nj),
        in_specs=[
            pl.BlockSpec((8, D_MODEL), lambda l, j: (0, 0)),
            pl.BlockSpec((None, D_MODEL, D_MODEL), lambda l, j: (l, 0, j)),
            pl.BlockSpec((None, 1, D_MODEL), lambda l, j: (l, 0, j)),
        ],
        out_specs=pl.BlockSpec((None, 8, D_MODEL), lambda l, j: (l, 0, j)),
        out_shape=jax.ShapeDtypeStruct((depth, 8, 6 * D_MODEL), F32),
        compiler_params=_cparams(("arbitrary", "arbitrary")),
        name="adaln_mod",
    )(cc, w_mod, b_mod.reshape(depth, 1, 6 * D_MODEL))


def _rms(x, g):
    return x * lax.rsqrt(jnp.mean(x * x, axis=-1, keepdims=True) + EPS) * g


def _rope(v, tab_ref, half):
    n = v.shape[-1]
    cos = tab_ref[:, 0:128]
    s_lo = tab_ref[:, 128:256]
    s_hi = tab_ref[:, 256:384]
    return v * cos + pltpu.roll(v, n - half, 1) * s_lo + pltpu.roll(v, half, 1) * s_hi


def _store_v_slabs(ref, vt, heads):
    ones = jnp.ones((V_SLAB - HEAD_DIM, vt.shape[1]), BF16)
    for hd in range(heads):
        ref[V_SLAB * hd:V_SLAB * hd + HEAD_DIM, :] = vt[HEAD_DIM * hd:HEAD_DIM * (hd + 1), :].astype(BF16)
        ref[V_SLAB * hd + HEAD_DIM:V_SLAB * (hd + 1), :] = ones


def _proj_kernel(x_ref, mod_ref, g1_ref, w_ref, wuq_ref, wukv_ref, gq_ref, gkv_ref, lbp_ref, taba_ref, tabc_ref,
                 qa_ref, ka_ref, va_ref, bqig_ref, gates_ref, qc_ref, kc_ref, vc_ref, gl_ref):
    x = x_ref[...]
    h = _rms(x, g1_ref[...]) * (1.0 + mod_ref[1:2, :]) + mod_ref[0:1, :]
    hb = h.astype(BF16)

    def seg(off, width):
        return jnp.dot(hb, w_ref[:, off:off + width], preferred_element_type=F32)

    aq = seg(OFF_AQ, 512) * (HEAD_DIM ** -0.5 * LOG2E)
    for j in range(4):
        qa_ref[128 * j:128 * (j + 1), :] = _rope(aq[:, 128 * j:128 * (j + 1)], taba_ref, 32).T.astype(BF16)
    ka_ref[...] = _rope(seg(OFF_AK, 128), taba_ref, 32).astype(BF16)
    _store_v_slabs(va_ref, seg(OFF_AV, 128).T, A_KV_HEADS)

    bqig_ref[:, 0:512] = seg(OFF_BQ, 512).astype(BF16)
    bqig_ref[:, 512:1024] = seg(OFF_BI, 512).astype(BF16)
    bqig_ref[:, 1024:1536] = seg(OFF_BG, 512).astype(BF16)
    for d, off in enumerate((OFF_BZF, OFF_BZB)):
        z = seg(off, 512)
        log_lb = lbp_ref[0:1, 512 * d:512 * (d + 1)]
        log1m_lb = lbp_ref[1:2, 512 * d:512 * (d + 1)]
        one_m_lb = lbp_ref[2:3, 512 * d:512 * (d + 1)]
        e = jnp.exp(-jnp.abs(z))
        log_sig = jnp.minimum(z, 0.0) - jnp.log(1.0 + e)
        b = log1m_lb + log_sig
        mx = jnp.maximum(log_lb, b)
        logf = mx + jnp.log(1.0 + jnp.exp(-jnp.abs(log_lb - b)))
        r = 1.0 / (1.0 + e)
        key = one_m_lb * jnp.where(z >= 0.0, e * r, r)
        gates_ref[:, 512 * d:512 * (d + 1)] = logf
        gates_ref[:, 1024 + 512 * d:1024 + 512 * (d + 1)] = key

    cq = _rms(seg(OFF_CQ, C_Q_LORA), gq_ref[...]).astype(BF16)
    qh = jnp.dot(cq, wuq_ref[...], preferred_element_type=F32) * ((C_NOPE + C_ROPE) ** -0.5 * LOG2E)
    ckv = _rms(seg(OFF_CKV, C_KV_LORA), gkv_ref[...]).astype(BF16)
    kvh = jnp.dot(ckv, wukv_ref[...], preferred_element_type=F32)
    kr = _rope(seg(OFF_CKR, 128), tabc_ref, 16)
    for j in range(C_HEADS):
        sl = slice(C_HEAD_PAD * j, C_HEAD_PAD * (j + 1))
        qc_ref[sl, :] = _rope(qh[:, sl], tabc_ref, 16).T.astype(BF16)
        kc_ref[:, sl] = (kvh[:, sl] + kr).astype(BF16)
    _store_v_slabs(vc_ref, kvh[:, C_HEADS * C_HEAD_PAD:].T, C_HEADS)

    for j in range(6):
        gl_ref[:, 512 * j:512 * (j + 1)] = seg(OFF_GL + 512 * j, 512).astype(BF16)


def _projection(x, mod, g1, w_in_p, wuq_p, wukv_p, gq, gkv, lbp, taba, tabc, dims):
    B, S, L = dims
    n = x.shape[0]
    tm = PROJ_BLOCK
    nlat = B * S // tm
    spb = S // tm

    def row(i):
        return (i, 0)

    def mod_row(i):
        return (jnp.minimum(i // spb, B), 0, 0)

    def tab_row(i):
        return (jnp.where(i < nlat, i % spb, spb), 0)

    const = lambda i: (0, 0)
    resident = functools.partial(pl.BlockSpec, index_map=const, pipeline_mode=pl.Buffered(1))
    widths = (512, 128, A_KV_HEADS * V_SLAB, 1536, 2048, 1024, 1024, C_HEADS * V_SLAB, 3072)
    dtypes = (BF16, BF16, BF16, BF16, F32, BF16, BF16, BF16, BF16)
    transposed = (0, 2, 5, 7)
    return pl.pallas_call(
        _proj_kernel,
        grid=(n // tm,),
        in_specs=[
            pl.BlockSpec((tm, D_MODEL), row),
            pl.BlockSpec((None, 6, D_MODEL), mod_row),
            pl.BlockSpec((1, D_MODEL), const),
            resident((D_MODEL, IN_W_PACKED)),
            resident(wuq_p.shape),
            resident(wukv_p.shape),
            pl.BlockSpec((1, C_Q_LORA), const),
            pl.BlockSpec((1, C_KV_LORA), const),
            pl.BlockSpec((8, 2 * B_W), const),
            pl.BlockSpec((tm, 384), tab_row),
            pl.BlockSpec((tm, 384), tab_row),
        ],
        out_specs=[pl.BlockSpec((w, tm), lambda i: (0, i)) if k in transposed else pl.BlockSpec((tm, w), row)
                   for k, w in enumerate(widths)],
        out_shape=[jax.ShapeDtypeStruct((w, n) if k in transposed else (n, w), dt)
                   for k, (w, dt) in enumerate(zip(widths, dtypes))],
        compiler_params=_cparams(("parallel",), PROJ_VMEM_LIMIT),
        name="in_proj",
    )(x, mod, g1, w_in_p, wuq_p, wukv_p, gq, gkv, lbp, taba, tabc)


def _gqa_ctx_kernel(qt_ref, kx_ref, vx_ref, sink_ref, _, o_ref):
    for g in range(A_KV_HEADS):
        qg = jnp.concatenate([qt_ref[HEAD_DIM * hd:HEAD_DIM * (hd + 1), :]
                              for hd in range(A_GROUP * g, A_GROUP * (g + 1))], axis=1)
        s = jnp.dot(kx_ref[:, HEAD_DIM * g:HEAD_DIM * (g + 1)], qg, preferred_element_type=F32)
        m = jnp.maximum(jnp.max(s, axis=0, keepdims=True), sink_ref[g])
        p = jnp.exp2(s - m).astype(BF16)
        pv = jnp.dot(vx_ref[V_SLAB * g:V_SLAB * (g + 1), :], p, preferred_element_type=F32)
        o = pv[0:HEAD_DIM, :] / (pv[HEAD_DIM:HEAD_DIM + 1, :] + jnp.exp2(sink_ref[g] - m))
        for hh in range(A_GROUP):
            hd = A_GROUP * g + hh
            o_ref[HEAD_DIM * hd:HEAD_DIM * (hd + 1), :] = o[:, A_QBLOCK * hh:A_QBLOCK * (hh + 1)].astype(BF16)


def _gqa_lat_kernel(qt_ref, kp_ref, kc_ref, kn_ref, kx_ref, vp_ref, vc_ref, vn_ref, vx_ref, sink_ref, o_ref,
                    bias_ref, s0_ref, s1_ref, p0_ref, p1_ref, *, seq):
    j = pl.program_id(1)
    nband = 4 * A_BLOCK
    nsb, nk = bias_ref.shape[0], bias_ref.shape[1]
    step_q = nsb * A_QBLOCK
    rows = lax.broadcasted_iota(jnp.int32, (nk, A_QBLOCK), 0)
    cols = lax.broadcasted_iota(jnp.int32, (nk, A_QBLOCK), 1)
    for qb in range(nsb):
        qpos = j * step_q + qb * A_QBLOCK + cols
        kpos = j * step_q + qb * A_QBLOCK - A_BLOCK + rows
        valid = ((kpos >= 0) & (kpos < seq) & (jnp.abs(qpos - kpos) <= A_WINDOW)) | (rows >= nband)
        bias_ref[qb] = jnp.where(valid, 0.0, -jnp.inf)
    k = jnp.concatenate([kp_ref[...], kc_ref[...], kn_ref[...]], axis=0)
    vt = jnp.concatenate([vp_ref[...], vc_ref[...], vn_ref[...]], axis=1)
    s_refs, p_refs = (s0_ref, s1_ref), (p0_ref, p1_ref)
    nchunk = nk // A_QBLOCK
    krows = lambda r: slice(A_QBLOCK * r, A_QBLOCK * (r + 1))
    items = [(qb, g) for qb in range(nsb) for g in range(A_KV_HEADS)]
    n = len(items)
    smax, m, pv = [None] * n, [None] * n, [None] * n

    def keys(i, r):
        qb, g = items[i]
        if A_QBLOCK * (r + 1) <= nband:
            sl = slice(A_QBLOCK * (qb + r), A_QBLOCK * (qb + r + 1))
            return k[sl, HEAD_DIM * g:HEAD_DIM * (g + 1)], vt[V_SLAB * g:V_SLAB * (g + 1), sl]
        return kx_ref[:, HEAD_DIM * g:HEAD_DIM * (g + 1)], vx_ref[V_SLAB * g:V_SLAB * (g + 1), :]

    def stage_scores(i, r):
        qb, g = items[i]
        qcols = slice(A_QBLOCK * qb, A_QBLOCK * (qb + 1))
        qg = jnp.concatenate([qt_ref[HEAD_DIM * hd:HEAD_DIM * (hd + 1), qcols]
                              for hd in range(A_GROUP * g, A_GROUP * (g + 1))], axis=1)
        bias = bias_ref[qb, krows(r), :]
        sc = (jnp.dot(keys(i, r)[0], qg, preferred_element_type=F32) + jnp.concatenate([bias] * A_GROUP, axis=1))
        s_refs[i % 2][krows(r), :] = sc
        cm = jnp.max(sc, axis=0, keepdims=True)
        smax[i] = cm if smax[i] is None else jnp.maximum(smax[i], cm)

    def stage_exp(i, r):
        p_refs[i % 2][krows(r), :] = jnp.exp2(s_refs[i % 2][krows(r), :] - m[i]).astype(BF16)

    def stage_pv(i, r):
        qb, g = items[i]
        t = jnp.dot(keys(i, r)[1], p_refs[i % 2][krows(r), :], preferred_element_type=F32)
        pv[i] = t if pv[i] is None else pv[i] + t
        if r == nchunk - 1:
            denom = pv[i][HEAD_DIM:HEAD_DIM + 1, :] + jnp.exp2(sink_ref[g] - m[i])
            o = pv[i][0:HEAD_DIM, :] / denom
            for hh in range(A_GROUP):
                hd = A_GROUP * g + hh
                o_ref[HEAD_DIM * hd:HEAD_DIM * (hd + 1), A_QBLOCK * qb:A_QBLOCK * (qb + 1)] = (
                    o[:, A_QBLOCK * hh:A_QBLOCK * (hh + 1)].astype(BF16))

    for r in range(nchunk):
        stage_scores(0, r)
    for i in range(n + 1):
        if i < n:
            m[i] = jnp.maximum(smax[i], sink_ref[items[i][1]])
        for r in range(nchunk):
            if i + 1 < n:
                stage_scores(i + 1, r)
            if i < n:
                stage_exp(i, r)
            if i >= 1:
                stage_pv(i - 1, r)


def _window_gqa(qat, ka, vat, sink, dims, need_ctx):
    B, S, L = dims
    n = ka.shape[0]
    assert L == A_QBLOCK
    nb = S // A_BLOCK
    nqb = S // A_QBLOCK
    nk = 4 * A_BLOCK + L
    width = A_GROUP * A_QBLOCK
    vs = A_KV_HEADS * V_SLAB
    step_q = min(A_STEP_BLOCKS * A_QBLOCK, S)
    nstep = S // step_q
    bps = step_q // A_BLOCK
    ctx_row = lambda b, j: (B * S // L + b, 0)
    ctx_col = lambda b, j: (0, B * S // L + b)
    sink_spec = pl.BlockSpec((A_KV_HEADS, 1, width), lambda b, j: (0, 0, 0))
    stage = [pltpu.VMEM((nk, width), F32), pltpu.VMEM((nk, width), F32),
             pltpu.VMEM((nk, width), BF16), pltpu.VMEM((nk, width), BF16)]
    prev_blk = lambda b, j: b * nb + jnp.maximum(bps * j - 1, 0)
    next_blk = lambda b, j: b * nb + jnp.minimum(bps * (j + 1), nb - 1)
    o_lat = pl.pallas_call(
        functools.partial(_gqa_lat_kernel, seq=S),
        grid=(B, nstep),
        in_specs=[
            pl.BlockSpec((512, step_q), lambda b, j: (0, b * nstep + j)),
            pl.BlockSpec((A_BLOCK, 128), lambda b, j: (prev_blk(b, j), 0)),
            pl.BlockSpec((step_q, 128), lambda b, j: (b * nstep + j, 0)),
            pl.BlockSpec((A_BLOCK, 128), lambda b, j: (next_blk(b, j), 0)),
            pl.BlockSpec((L, 128), ctx_row),
            pl.BlockSpec((vs, A_BLOCK), lambda b, j: (0, prev_blk(b, j))),
            pl.BlockSpec((vs, step_q), lambda b, j: (0, b * nstep + j)),
            pl.BlockSpec((vs, A_BLOCK), lambda b, j: (0, next_blk(b, j))),
            pl.BlockSpec((vs, L), ctx_col),
            sink_spec,
        ],
        out_specs=pl.BlockSpec((512, step_q), lambda b, j: (0, b * nstep + j)),
        out_shape=jax.ShapeDtypeStruct((512, n if need_ctx else B * S), BF16),
        scratch_shapes=[pltpu.VMEM((step_q // A_QBLOCK, nk, A_QBLOCK), F32)] + stage,
        compiler_params=_cparams(("parallel", "parallel")),
        name="window_gqa",
    )(qat, ka, ka, ka, ka, vat, vat, vat, vat, sink)
    if not need_ctx:
        return o_lat
    return pl.pallas_call(
        _gqa_ctx_kernel,
        grid=(B,),
        in_specs=[pl.BlockSpec((512, L), lambda b: (0, B * S // L + b)),
                  pl.BlockSpec((L, 128), lambda b: (B * S // L + b, 0)),
                  pl.BlockSpec((vs, L), lambda b: (0, B * S // L + b)),
                  pl.BlockSpec((A_KV_HEADS, 1, width), lambda b: (0, 0, 0)),
                  pl.BlockSpec(memory_space=pl.ANY)],
        out_specs=pl.BlockSpec((512, L), lambda b: (0, B * S // L + b)),
        out_shape=jax.ShapeDtypeStruct((512, n), BF16),
        input_output_aliases={4: 0},
        compiler_params=_cparams(("parallel",)),
        name="window_gqa_context",
    )(qat, ka, vat, sink, o_lat)


def _hgrn_prep(q_ref, v_ref, g_ref, k_ref, r0, reverse):
    C = B_CHUNK
    rs = slice(r0, r0 + C)
    rows = lax.broadcasted_iota(jnp.int32, (C, C), 0)
    cols = lax.broadcasted_iota(jnp.int32, (C, C), 1)
    causal = (rows <= cols) if reverse else (rows >= cols)
    g = g_ref[rs, :]
    bc = jnp.dot(causal.astype(F32), g, preferred_element_type=F32, precision=HIGHEST)
    tot = jnp.sum(g, axis=0, keepdims=True)
    mid = C // 2 if reverse else C // 2 - 1
    rho = bc[mid:mid + 1, :]
    q = q_ref[rs, :].astype(F32)
    key = k_ref[rs, :]
    v = v_ref[rs, :]
    qe = (q * jnp.exp(bc - rho)).astype(BF16)
    ke = (key * jnp.exp(rho - bc)).astype(BF16)
    qs = (q * jnp.exp(bc)).astype(BF16)
    ks = (key * jnp.exp(tot - bc)).astype(BF16)
    dec = jnp.exp(tot)
    return rs, causal, v, qe, ke, qs, ks, dec


def _hgrn_heads(prep, o_ref, st_ref):
    rs, causal, v, qe, ke, qs, ks, dec = prep
    for hd in range(B_HEADS):
        sl = slice(B_DK * hd, B_DK * (hd + 1))
        st = st_ref[hd]
        att = lax.dot_general(qe[:, sl], ke[:, sl], (((1,), (1,)), ((), ())), preferred_element_type=F32)
        att = jnp.where(causal, att, 0.0).astype(BF16)
        o = lax.dot_general(qs[:, sl], st.astype(BF16), (((1,), (1,)), ((), ())), preferred_element_type=F32)
        o = o + jnp.dot(att, v[:, sl], preferred_element_type=F32)
        o_ref[rs, sl] = o
        upd = lax.dot_general(v[:, sl], ks[:, sl], (((0,), (0,)), ((), ())), preferred_element_type=F32)
        st_ref[hd] = st * dec[:, sl] + upd


def _hgrn_kernel(qf_ref, vf_ref, gf_ref, kf_ref, qb_ref, vb_ref, gb_ref, kb_ref, of_ref, ob_ref, stf_ref, stb_ref):
    @pl.when(pl.program_id(1) == 0)
    def _():
        stf_ref[...] = jnp.zeros_like(stf_ref)
        stb_ref[...] = jnp.zeros_like(stb_ref)

    nchunk = qf_ref.shape[0] // B_CHUNK
    preps = []
    for i in range(nchunk):
        preps.append((_hgrn_prep(qf_ref, vf_ref, gf_ref, kf_ref, B_CHUNK * i, False), of_ref, stf_ref))
        preps.append((_hgrn_prep(qb_ref, vb_ref, gb_ref, kb_ref, B_CHUNK * (nchunk - 1 - i), True), ob_ref, stb_ref))
    for prep, o_ref, st_ref in preps:
        _hgrn_heads(prep, o_ref, st_ref)


def _hgrn2_scan(bqig, gates, dims):
    B, S, L = dims
    n = bqig.shape[0]
    T = B_BLOCK
    assert L == T
    ns = S // T

    def fwd_blk(b, c):
        return jnp.where(c == 0, B * ns + b, b * ns + c - 1)

    def bwd_blk(b, c):
        return jnp.where(c == 0, B * ns + b, b * ns + ns - c)

    def specs(blk, d):
        return [pl.BlockSpec((T, B_W), lambda b, c: (blk(b, c), 0)),
                pl.BlockSpec((T, B_W), lambda b, c: (blk(b, c), 1)),
                pl.BlockSpec((T, B_W), lambda b, c: (blk(b, c), d)),
                pl.BlockSpec((T, B_W), lambda b, c: (blk(b, c), 2 + d))]

    return pl.pallas_call(
        _hgrn_kernel,
        grid=(B, ns + 1),
        in_specs=specs(fwd_blk, 0) + specs(bwd_blk, 1),
        out_specs=[pl.BlockSpec((T, B_W), lambda b, c: (fwd_blk(b, c), 0)),
                   pl.BlockSpec((T, B_W), lambda b, c: (bwd_blk(b, c), 0))],
        out_shape=[jax.ShapeDtypeStruct((n, B_W), F32), jax.ShapeDtypeStruct((n, B_W), F32)],
        scratch_shapes=[pltpu.VMEM((B_HEADS, B_DK, B_DK), F32), pltpu.VMEM((B_HEADS, B_DK, B_DK), F32)],
        compiler_params=_cparams(("parallel", "arbitrary")),
        name="hgrn2_scan",
    )(bqig, bqig, gates, gates, bqig, bqig, gates, gates)


def _mla_kernel(qt_ref, k_ref, vt_ref, *rest, with_ctx):
    if with_ctx:
        kx_ref, vxt_ref, o_ref, m_ref, acc_ref, *bufs = rest
    else:
        _, o_ref, m_ref, acc_ref, *bufs = rest
    s_refs, p_refs = bufs[0:2], bufs[2:4]
    kstep = pl.program_id(2)
    tq = qt_ref.shape[1]

    def kv_pass(k_ref, vt_ref, first):
        nkeys = k_ref.shape[0]
        nchunk = nkeys // MLA_KEY_CHUNK
        krows = lambda r: slice(MLA_KEY_CHUNK * r, MLA_KEY_CHUNK * (r + 1))

        group = s_refs[0].shape[1]
        items = [(c, hd) for c in range(tq // group) for hd in range(C_HEADS)]
        n = len(items)
        smax = [None] * n
        m_new = [None] * n
        alpha = [None] * n
        pv = [None] * n
        cols = lambda i: slice(group * items[i][0], group * (items[i][0] + 1))

        def stage_scores(i, r):
            hd = items[i][1]
            sl = slice(C_HEAD_PAD * hd, C_HEAD_PAD * (hd + 1))
            sc = jnp.dot(k_ref[krows(r), sl], qt_ref[sl, cols(i)], preferred_element_type=F32)
            s_refs[i % 2][krows(r), :] = sc
            cm = jnp.max(sc, axis=0, keepdims=True)
            smax[i] = cm if smax[i] is None else jnp.maximum(smax[i], cm)

        def stage_stats(i):
            hd = items[i][1]
            if first:
                m_new[i] = smax[i]
            else:
                m_old = m_ref[hd, :, cols(i)]
                m_new[i] = jnp.maximum(m_old, smax[i])
                alpha[i] = jnp.exp2(m_old - m_new[i])
            m_ref[hd, :, cols(i)] = m_new[i]

        def stage_exp(i, r):
            p_refs[i % 2][krows(r), :] = jnp.exp2(s_refs[i % 2][krows(r), :] - m_new[i]).astype(BF16)

        def stage_pv(i, r):
            hd = items[i][1]
            t = jnp.dot(vt_ref[V_SLAB * hd:V_SLAB * (hd + 1), krows(r)], p_refs[i % 2][krows(r), :],
                        preferred_element_type=F32)
            pv[i] = t if pv[i] is None else pv[i] + t
            if r == nchunk - 1:
                acc_ref[hd, :, cols(i)] = pv[i] if first else alpha[i] * acc_ref[hd, :, cols(i)] + pv[i]

        for r in range(nchunk):
            stage_scores(0, r)
        for i in range(n + 1):
            if i < n:
                stage_stats(i)
            for r in range(nchunk):
                if i + 1 < n:
                    stage_scores(i + 1, r)
                if i < n:
                    stage_exp(i, r)
                if i >= 1:
                    stage_pv(i - 1, r)

    if with_ctx:
        @pl.when(kstep == 0)
        def _():
            kv_pass(kx_ref, vxt_ref, True)

        kv_pass(k_ref, vt_ref, False)
    else:
        kv_pass(k_ref, vt_ref, True)

    @pl.when(kstep == pl.num_programs(2) - 1)
    def _():
        for hd in range(C_HEADS):
            o_ref[C_V * hd:C_V * (hd + 1), :] = (acc_ref[hd, 0:C_V, :] / acc_ref[hd, C_V:C_V + 1, :]).astype(BF16)


def _mla_attention(qct, kc, vct, dims, need_ctx):
    B, S, L = dims
    n = kc.shape[0]
    tq = min(MLA_Q_BLOCK, S)
    tk = min(MLA_K_BLOCK, S)
    nq, nk = S // tq, S // tk
    hw = C_HEADS * C_HEAD_PAD
    vw = C_HEADS * C_V
    vs = C_HEADS * V_SLAB
    scratch = lambda t, nkeys: [
        pltpu.VMEM((C_HEADS, 1, t), F32), pltpu.VMEM((C_HEADS, V_SLAB, t), F32),
        pltpu.VMEM((nkeys, min(t, MLA_COL_GROUP)), F32), pltpu.VMEM((nkeys, min(t, MLA_COL_GROUP)), F32),
        pltpu.VMEM((nkeys, min(t, MLA_COL_GROUP)), BF16), pltpu.VMEM((nkeys, min(t, MLA_COL_GROUP)), BF16)]
    ctx_row = lambda b, i, k: (B * S // L + b, 0)
    ctx_col = lambda b, i, k: (0, B * S // L + b)
    o_lat = pl.pallas_call(
        functools.partial(_mla_kernel, with_ctx=True),
        grid=(B, nq, nk),
        in_specs=[
            pl.BlockSpec((hw, tq), lambda b, i, k: (0, b * nq + i)),
            pl.BlockSpec((tk, hw), lambda b, i, k: (b * nk + k, 0)),
            pl.BlockSpec((vs, tk), lambda b, i, k: (0, b * nk + k)),
            pl.BlockSpec((L, hw), ctx_row),
            pl.BlockSpec((vs, L), ctx_col),
        ],
        out_specs=pl.BlockSpec((vw, tq), lambda b, i, k: (0, b * nq + i)),
        out_shape=jax.ShapeDtypeStruct((vw, n if need_ctx else B * S), BF16),
        scratch_shapes=scratch(tq, tk),
        compiler_params=_cparams(("parallel", "parallel", "arbitrary")),
        name="mla_latent",
    )(qct, kc, vct, kc, vct)
    if not need_ctx:
        return o_lat
    return pl.pallas_call(
        functools.partial(_mla_kernel, with_ctx=False),
        grid=(B, 1, 1),
        in_specs=[
            pl.BlockSpec((hw, L), ctx_col),
            pl.BlockSpec((L, hw), ctx_row),
            pl.BlockSpec((vs, L), ctx_col),
            pl.BlockSpec(memory_space=pl.ANY),
        ],
        out_specs=pl.BlockSpec((vw, L), ctx_col),
        out_shape=jax.ShapeDtypeStruct((vw, n), BF16),
        scratch_shapes=scratch(L, L),
        input_output_aliases={3: 0},
        compiler_params=_cparams(("parallel", "arbitrary", "arbitrary")),
        name="mla_context",
    )(qct, kc, vct, o_lat)


def _group_sum(x, ones_ref):
    hi = x.astype(BF16)
    lo = (x - hi.astype(F32)).astype(BF16)
    return (jnp.dot(hi, ones_ref[...], preferred_element_type=F32)
            + jnp.dot(lo, ones_ref[...], preferred_element_type=F32))


def _merge_kernel(x_ref, oa_ref, of_ref, ob_ref, bg_ref, oc_ref, gl_ref, wbr_ref, wout_ref, gn_ref, mod_ref,
                  g2_ref, ones_ref, xo_ref, h2_ref):
    ob = of_ref[...] + ob_ref[...]
    ms = _group_sum(ob * ob, ones_ref) * (1.0 / B_DK)
    obn = ob * lax.rsqrt(ms + EPS) * gn_ref[...]
    bg = bg_ref[...].astype(F32)
    bb = (obn * (bg * jax.nn.sigmoid(bg))).astype(BF16)
    branches = ((oa_ref[...], 0), (bb, 1), (oc_ref[...], 0))
    y = None
    for nbr, (br, axis) in enumerate(branches):
        gate = jax.nn.sigmoid(gl_ref[:, D_MODEL * nbr:D_MODEL * (nbr + 1)].astype(F32))
        t = gate * lax.dot_general(br, wbr_ref[nbr], (((axis,), (0,)), ((), ())), preferred_element_type=F32)
        y = t if y is None else y + t
    upd = jnp.dot(y.astype(BF16), wout_ref[...], preferred_element_type=F32)
    xn = x_ref[...] + mod_ref[2:3, :] * upd
    xo_ref[...] = xn
    h2 = _rms(xn, g2_ref[...]) * (1.0 + mod_ref[4:5, :]) + mod_ref[3:4, :]
    h2_ref[...] = h2.astype(BF16)


def _merge(x, oa, ohg, bqig, oc, gl, wbr, wout, gn, mod, g2, ones, dims, need_ctx):
    B, S, L = dims
    n = x.shape[0]
    tm = PROJ_BLOCK
    spb = S // tm
    nblk = (n if need_ctx else B * S) // tm
    row = lambda i: (i, 0)
    const2 = lambda i: (0, 0)
    return pl.pallas_call(
        _merge_kernel,
        grid=(nblk,),
        in_specs=[
            pl.BlockSpec((tm, D_MODEL), row),
            pl.BlockSpec((512, tm), lambda i: (0, i)),
            pl.BlockSpec((tm, B_W), row),
            pl.BlockSpec((tm, B_W), row),
            pl.BlockSpec((tm, B_W), lambda i: (i, 2)),
            pl.BlockSpec((512, tm), lambda i: (0, i)),
            pl.BlockSpec((tm, N_BRANCH * D_MODEL), row),
            pl.BlockSpec((N_BRANCH, BRANCH_W, D_MODEL), lambda i: (0, 0, 0)),
            pl.BlockSpec((D_MODEL, D_MODEL), const2),
            pl.BlockSpec((1, B_W), const2),
            pl.BlockSpec((None, 6, D_MODEL), lambda i: (jnp.minimum(i // spb, B), 0, 0)),
            pl.BlockSpec((1, D_MODEL), const2),
            pl.BlockSpec((B_W, B_W), const2),
        ],
        out_specs=[pl.BlockSpec((tm, D_MODEL), row), pl.BlockSpec((tm, D_MODEL), row)],
        out_shape=[jax.ShapeDtypeStruct((nblk * tm, D_MODEL), F32), jax.ShapeDtypeStruct((nblk * tm, D_MODEL), BF16)],
        compiler_params=_cparams(("parallel",)),
        name="branch_merge",
    )(x, oa, ohg[0], ohg[1], bqig, oc, gl, wbr, wout, gn, mod, g2, ones)


def _route_kernel(h_ref, wr_ref, comb_ref, pos_ref, cnt_ref):
    h = h_ref[...]
    T = h.shape[0]
    lane = lax.broadcasted_iota(jnp.int32, (T, ROUTER_W), 1)
    logits = jnp.dot(h, wr_ref[...], preferred_element_type=F32)
    big = jnp.int32(ROUTER_W)
    is_grp = (lane >= N_EXPERTS) & (lane < N_EXPERTS + N_GROUPS)
    gl = jnp.where(is_grp, logits, -jnp.inf)
    gmax = jnp.max(gl, axis=-1, keepdims=True)
    gsel = jnp.min(jnp.where(gl == gmax, lane, big), axis=-1, keepdims=True) - N_EXPERTS
    gw = 1.0 / jnp.sum(jnp.exp(gl - gmax), axis=-1, keepdims=True)
    in_grp = (lane >= gsel * EXPERTS_PER_GROUP) & (lane < (gsel + 1) * EXPERTS_PER_GROUP)
    el = jnp.where(in_grp, logits, -jnp.inf)
    m1 = jnp.max(el, axis=-1, keepdims=True)
    i1 = jnp.min(jnp.where(el == m1, lane, big), axis=-1, keepdims=True)
    el2 = jnp.where(lane == i1, -jnp.inf, el)
    m2 = jnp.max(el2, axis=-1, keepdims=True)
    i2 = jnp.min(jnp.where(el2 == m2, lane, big), axis=-1, keepdims=True)
    e2 = jnp.exp(m2 - m1)
    w1 = gw / (1.0 + e2)
    w2 = gw * e2 / (1.0 + e2)
    comb_ref[...] = jnp.where(lane == i1, w1, 0.0) + jnp.where(lane == i2, w2, 0.0)

    onehot = lane == gsel
    ones = jnp.where(onehot, 1.0, 0.0)
    rows = lax.broadcasted_iota(jnp.int32, (T, T), 0)
    cols = lax.broadcasted_iota(jnp.int32, (T, T), 1)
    earlier = jnp.where(rows > cols, 1.0, 0.0).astype(BF16)
    before = jnp.dot(earlier, ones.astype(BF16), preferred_element_type=F32)
    rank = jnp.sum(jnp.where(onehot, before, 0.0), axis=-1, keepdims=True)
    cnt = jnp.sum(ones, axis=0, keepdims=True)
    padded = jnp.floor((cnt + (MOE_ALIGN - 1)) * (1.0 / MOE_ALIGN)) * MOE_ALIGN
    seg = [jnp.sum(jnp.where(lane[0:1] == g, padded, 0.0), axis=-1, keepdims=True) for g in range(N_GROUPS - 1)]
    start = jnp.where(gsel == 0, 0.0, jnp.where(gsel == 1, seg[0], jnp.where(gsel == 2, seg[0] + seg[1],
                                                                             seg[0] + seg[1] + seg[2])))
    pos_ref[...] = jnp.broadcast_to(start + rank, (T, ROUTER_W))
    cnt_ref[...] = jnp.broadcast_to(cnt, (8, ROUTER_W)).astype(jnp.int32)


def _moe_kernel(cnt_ref, h_ref, x_ref, mod_ref, pos_ref, comb_ref, w13_ref, w2_ref, gf_ref, o_ref,
                pt_ref, xs_ref, cs_ref, ys_ref, *, final_norm):
    i = pl.program_id(0)
    e = pl.program_id(1)
    T = h_ref.shape[0]
    R = xs_ref.shape[0]
    gather = lambda a: lax.dot_general(pt_ref[...], a, (((0,), (0,)), ((), ())), preferred_element_type=F32)

    @pl.when(e == 0)
    def _():
        slot = lax.broadcasted_iota(jnp.int32, (T, R), 1).astype(F32)
        pt_ref[...] = jnp.where(pos_ref[:, 0:1] == slot, 1.0, 0.0).astype(BF16)
        xs_ref[...] = gather(h_ref[...]).astype(BF16)
        comb = comb_ref[...]
        hi = comb.astype(BF16)
        r1 = comb - hi.astype(F32)
        mid = r1.astype(BF16)
        lo = (r1 - mid.astype(F32)).astype(BF16)
        cs_ref[...] = gather(hi) + gather(mid) + gather(lo)
        ys_ref[...] = jnp.zeros_like(ys_ref)

    g = (e * MOE_EXPERTS_PER_STEP) // EXPERTS_PER_GROUP
    cnt = [cnt_ref[i * N_GROUPS + gg] for gg in range(N_GROUPS)]
    seg = [(c + (MOE_ALIGN - 1)) // MOE_ALIGN * MOE_ALIGN for c in cnt]
    start = (jnp.where(g > 0, seg[0], 0) + jnp.where(g > 1, seg[1], 0) + jnp.where(g > 2, seg[2], 0))
    cnt_g = jnp.where(g == 0, cnt[0], jnp.where(g == 1, cnt[1], jnp.where(g == 2, cnt[2], cnt[3])))
    lane = lax.broadcasted_iota(jnp.int32, (MOE_TILE, ROUTER_W), 1)

    def tile(t, carry):
        r0 = pl.multiple_of(start + t * MOE_TILE, MOE_ALIGN)
        xt = xs_ref[pl.ds(r0, MOE_TILE), :]
        ct = cs_ref[pl.ds(r0, MOE_TILE), :]
        acts = []
        for k in range(MOE_EXPERTS_PER_STEP):
            ce = jnp.sum(jnp.where(lane == e * MOE_EXPERTS_PER_STEP + k, ct, 0.0), axis=-1, keepdims=True)
            h13 = jnp.dot(xt, w13_ref[k], preferred_element_type=F32)
            a1 = h13[:, :D_EXPERT]
            acts.append((a1 * jax.nn.sigmoid(a1) * h13[:, D_EXPERT:] * ce).astype(BF16))
        ys_ref[pl.ds(r0, MOE_TILE), :] += jnp.dot(jnp.concatenate(acts, axis=-1), w2_ref[...],
                                                   preferred_element_type=F32)
        return carry

    lax.fori_loop(0, (cnt_g + MOE_TILE - 1) // MOE_TILE, tile, 0)

    @pl.when(e == pl.num_programs(1) - 1)
    def _():
        y = jnp.dot(pt_ref[...], ys_ref[...].astype(BF16), preferred_element_type=F32)
        xn = x_ref[...] + mod_ref[5:6, :] * y
        o_ref[...] = _rms(xn, gf_ref[...]) if final_norm else xn


def _moe(h2, x, mod, wr, w13, w2, g_final, dims, need_ctx, final_norm):
    B, S, L = dims
    n = x.shape[0]
    tm = min(MOE_TOKEN_BLOCK, S, B * L)
    spb = S // tm
    nblk = (n if need_ctx else B * S) // tm
    comb, pos, cnt = pl.pallas_call(
        _route_kernel,
        grid=(nblk,),
        in_specs=[pl.BlockSpec((tm, D_MODEL), lambda i: (i, 0)), pl.BlockSpec((D_MODEL, ROUTER_W), lambda i: (0, 0))],
        out_specs=[pl.BlockSpec((tm, ROUTER_W), lambda i: (i, 0)), pl.BlockSpec((tm, ROUTER_W), lambda i: (i, 0)),
                   pl.BlockSpec((8, ROUTER_W), lambda i: (i, 0))],
        out_shape=[jax.ShapeDtypeStruct((nblk * tm, ROUTER_W), F32), jax.ShapeDtypeStruct((nblk * tm, ROUTER_W), F32),
                   jax.ShapeDtypeStruct((nblk * 8, ROUTER_W), jnp.int32)],
        compiler_params=_cparams(("parallel",)),
        name="moe_route",
    )(h2, wr)
    cnt = cnt.reshape(nblk, 8, ROUTER_W)[:, 0, :N_GROUPS].reshape(nblk * N_GROUPS)

    row = lambda i, e, c: (i, 0)
    eps = MOE_EXPERTS_PER_STEP
    assert MOE_SORT_PAD >= (N_GROUPS - 1) * (MOE_ALIGN - 1) + MOE_TILE - 1 and EXPERTS_PER_GROUP % eps == 0
    slots = tm + MOE_SORT_PAD
    return pl.pallas_call(
        functools.partial(_moe_kernel, final_norm=final_norm),
        grid_spec=pltpu.PrefetchScalarGridSpec(
            num_scalar_prefetch=1,
            grid=(nblk, N_EXPERTS // eps),
            in_specs=[
                pl.BlockSpec((tm, D_MODEL), row),
                pl.BlockSpec((tm, D_MODEL), row),
                pl.BlockSpec((None, 6, D_MODEL), lambda i, e, c: (jnp.minimum(i // spb, B), 0, 0)),
                pl.BlockSpec((tm, ROUTER_W), row),
                pl.BlockSpec((tm, ROUTER_W), row),
                pl.BlockSpec((eps, D_MODEL, 2 * D_EXPERT), lambda i, e, c: (e, 0, 0)),
                pl.BlockSpec((eps * D_EXPERT, D_MODEL), lambda i, e, c: (e, 0)),
                pl.BlockSpec((1, D_MODEL), lambda i, e, c: (0, 0)),
            ],
            out_specs=pl.BlockSpec((tm, D_MODEL), row),
            scratch_shapes=[pltpu.VMEM((tm, slots), BF16), pltpu.VMEM((slots, D_MODEL), BF16),
                            pltpu.VMEM((slots, ROUTER_W), F32), pltpu.VMEM((slots, D_MODEL), F32)],
        ),
        out_shape=jax.ShapeDtypeStruct((nblk * tm, D_MODEL), F32),
        compiler_params=_cparams(("parallel", "arbitrary")),
        name="hier_moe",
    )(cnt, h2, x, mod, pos, comb, w13, w2.reshape(N_EXPERTS * D_EXPERT, D_MODEL), g_final)


def _rope_tables(S, L):
    rows = S // GRID_W
    pos_r = np.repeat(np.arange(rows, dtype=np.float32), GRID_W)
    pos_c = np.tile(np.arange(GRID_W, dtype=np.float32), rows)

    def angles(rot_dim):
        nf = rot_dim // 4
        inv = jnp.asarray(ROPE_BASE, F32) ** (-jnp.arange(nf, dtype=F32) / nf)
        ang = jnp.concatenate([pos_r[:, None] * inv, pos_c[:, None] * inv], axis=-1)
        return jnp.cos(ang), jnp.sin(ang)

    def with_ctx(cos, s_lo, s_hi):
        ident = jnp.concatenate([jnp.ones((PROJ_BLOCK, 128), F32), jnp.zeros((PROJ_BLOCK, 256), F32)], axis=-1)
        return jnp.concatenate([jnp.concatenate([cos, s_lo, s_hi], axis=-1), ident], axis=0)

    cos, sin = angles(HEAD_DIM)
    z = jnp.zeros_like(sin)
    taba = with_ctx(jnp.tile(cos, (1, 4)), jnp.tile(jnp.concatenate([-sin, z], -1), (1, 2)),
                    jnp.tile(jnp.concatenate([z, sin], -1), (1, 2)))
    cos, sin = angles(C_ROPE)
    z = jnp.zeros_like(sin)
    one64, zero64, zero32 = jnp.ones((S, 64), F32), jnp.zeros((S, 64), F32), jnp.zeros((S, 32), F32)
    tabc = with_ctx(jnp.concatenate([one64, cos, cos, one64[:, :32]], -1),
                    jnp.concatenate([zero64, -sin, z, zero32], -1),
                    jnp.concatenate([zero64, z, sin, zero32], -1))
    return taba, tabc


def _pack_w_in(w):
    pad = lambda k: jnp.zeros((w.shape[0], k), w.dtype)
    return jnp.concatenate([w[:, :3712], pad(64), w[:, 3712:3744], pad(32), w[:, 3744:]], axis=-1).astype(BF16)


def _pack_w_uq(w):
    w = w.reshape(C_Q_LORA, C_HEADS, C_NOPE + C_ROPE)
    w = jnp.pad(w, ((0, 0), (0, 0), (0, C_HEAD_PAD - C_NOPE - C_ROPE)))
    return w.reshape(C_Q_LORA, C_HEADS * C_HEAD_PAD).astype(BF16)


def _pack_w_ukv(w):
    w = w.reshape(C_KV_LORA, C_HEADS, C_NOPE + C_V)
    wk = jnp.pad(w[:, :, :C_NOPE], ((0, 0), (0, 0), (0, C_HEAD_PAD - C_NOPE))).reshape(C_KV_LORA, -1)
    wv = w[:, :, C_NOPE:].reshape(C_KV_LORA, -1)
    return jnp.concatenate([wk, wv], axis=-1).astype(BF16)


def kernel(x, c, ctx, c_ctx, w_mod, b_mod, g_norm1, g_norm2, w_in, a_sink, b_lb_logits, b_onorm, c_qnorm, c_kvnorm,
           w_uq, w_ukv, w_br, w_out, w_rg, w_re, w1, w3, w2, g_final):
    B, S, _ = x.shape
    L = ctx.shape[1]
    depth = w_in.shape[0]
    assert L == TOKEN_BLOCK and S % min(MLA_Q_BLOCK, S) == 0 and S % min(MLA_K_BLOCK, S) == 0 and S % GRID_W == 0
    assert S % PROJ_BLOCK == 0 and (B * L) % PROJ_BLOCK == 0
    dims = (B, S, L)

    xs = jnp.concatenate([x.reshape(B * S, D_MODEL), ctx.reshape(B * L, D_MODEL)], axis=0)
    cc = jnp.zeros((8, D_MODEL), F32).at[:B].set(c).at[B].set(c_ctx)
    mod_all = _modulation(cc, w_mod, b_mod).reshape(depth, 8, 6, D_MODEL)

    lb_all = jnp.cumsum(jax.nn.softmax(b_lb_logits.astype(F32), axis=0), axis=0)
    lb_all = (lb_all - lb_all[0:1]).reshape(depth, 1, 2 * B_W)
    lbp_all = jnp.concatenate([jnp.log(lb_all), jnp.log1p(-lb_all), 1.0 - lb_all,
                               jnp.zeros((depth, 5, 2 * B_W), F32)], axis=1)

    taba, tabc = _rope_tables(S, L)
    ones = jnp.kron(jnp.eye(B_HEADS, dtype=F32), jnp.ones((B_DK, B_DK), F32)).astype(BF16)

    for l in range(depth):
        need_ctx = l < depth - 1
        mod = mod_all[l]
        wr = jnp.concatenate([w_re[l], w_rg[l], jnp.zeros((D_MODEL, ROUTER_W - N_EXPERTS - N_GROUPS), F32)],
                             axis=-1).astype(BF16)
        w13 = jnp.concatenate([w1[l], w3[l]], axis=-1).astype(BF16)
        sink = jnp.repeat(a_sink[l].astype(F32).reshape(A_KV_HEADS, 1, A_GROUP) * LOG2E, A_QBLOCK, axis=-1)

        qa, ka, va, bqig, gates, qc, kc, vc, gl = _projection(
            xs, mod, g_norm1[l][None], _pack_w_in(w_in[l]), _pack_w_uq(w_uq[l]), _pack_w_ukv(w_ukv[l]),
            c_qnorm[l][None], c_kvnorm[l][None], lbp_all[l], taba, tabc, dims)
        oa = _window_gqa(qa, ka, va, sink, dims, need_ctx)
        ohg = _hgrn2_scan(bqig, gates, dims)
        oc = _mla_attention(qc, kc, vc, dims, need_ctx)
        xs, h2 = _merge(xs, oa, ohg, bqig, oc, gl, w_br[l].astype(BF16), w_out[l].astype(BF16), b_onorm[l][None],
                        mod, g_norm2[l][None], ones, dims, need_ctx)
        xs = _moe(h2, xs, mod, wr, w13, w2[l].astype(BF16), g_final[None], dims, need_ctx, final_norm=not need_ctx)

    return xs.reshape(B, S, D_MODEL)
```

```python
import functools

import jax
import jax.numpy as jnp
import numpy as np
from jax import lax
from jax.experimental import pallas as pl
from jax.experimental.pallas import tpu as pltpu

F32 = jnp.float32
BF16 = jnp.bfloat16
HIGHEST = lax.Precision.HIGHEST

D_MODEL = 1024
GRID_W = 64
HEAD_DIM = 64
ROPE_BASE = 10000.0
EPS = 1e-6
A_HEADS = 8
A_KV_HEADS = 2
A_GROUP = A_HEADS // A_KV_HEADS
A_WINDOW = 128
A_BLOCK = A_WINDOW
A_QBLOCK = 2 * A_BLOCK
A_STEP_BLOCKS = 4
B_HEADS = 8
B_DK = 64
B_W = B_HEADS * B_DK
B_CHUNK = 64
B_BLOCK = 256
C_HEADS = 8
C_NOPE = 64
C_ROPE = 32
C_V = 64
C_Q_LORA = 256
C_KV_LORA = 128
C_HEAD_PAD = 128
N_BRANCH = 3
BRANCH_W = 512
N_GROUPS = 4
EXPERTS_PER_GROUP = 8
N_EXPERTS = N_GROUPS * EXPERTS_PER_GROUP
D_EXPERT = 256
ROUTER_W = 128
V_SLAB = 80
LOG2E = 1.4426950408889634

OFF_AQ, OFF_AK, OFF_AV = 0, 512, 640
OFF_BQ, OFF_BI, OFF_BZF, OFF_BZB, OFF_BG = 768, 1280, 1792, 2304, 2816
OFF_CQ, OFF_CKV, OFF_CKR, OFF_GL = 3328, 3584, 3712, 3840
IN_W_PACKED = OFF_GL + N_BRANCH * D_MODEL

TOKEN_BLOCK = 256
PROJ_BLOCK = 512
MOE_TOKEN_BLOCK = 1024
MOE_EXPERTS_PER_STEP = 4
MOE_TILE = 288
MOE_ALIGN = 16
MOE_SORT_PAD = 384
MLA_Q_BLOCK = 2048
MLA_K_BLOCK = 1024
MLA_COL_GROUP = 512
MLA_KEY_CHUNK = 256
MLA_LAG_LIMIT = 64.0
VMEM_LIMIT = 56 * 1024 * 1024
PROJ_VMEM_LIMIT = 61 * 1024 * 1024


def _cparams(sem, vmem_limit=VMEM_LIMIT, **kw):
    return pltpu.CompilerParams(dimension_semantics=sem, vmem_limit_bytes=vmem_limit, **kw)


def _mod_kernel(cc_ref, w_ref, b_ref, o_ref):
    cc = cc_ref[...]
    a = cc * jax.nn.sigmoid(cc)
    o_ref[...] = jnp.dot(a, w_ref[...], preferred_element_type=F32, precision=HIGHEST) + b_ref[...]


def _modulation(cc, w_mod, b_mod):
    depth = w_mod.shape[0]
    nj = 6
    return pl.pallas_call(
        _mod_kernel,
        grid=(depth, nj),
        in_specs=[
            pl.BlockSpec((8, D_MODEL), lambda l, j: (0, 0)),
            pl.BlockSpec((None, D_MODEL, D_MODEL), lambda l, j: (l, 0, j)),
            pl.BlockSpec((None, 1, D_MODEL), lambda l, j: (l, 0, j)),
        ],
        out_specs=pl.BlockSpec((None, 8, D_MODEL), lambda l, j: (l, 0, j)),
        out_shape=jax.ShapeDtypeStruct((depth, 8, 6 * D_MODEL), F32),
        compiler_params=_cparams(("arbitrary", "arbitrary")),
        name="adaln_mod",
    )(cc, w_mod, b_mod.reshape(depth, 1, 6 * D_MODEL))


def _rms(x, g):
    return x * lax.rsqrt(jnp.mean(x * x, axis=-1, keepdims=True) + EPS) * g


def _rope(v, tab_ref, half):
    n = v.shape[-1]
    cos = tab_ref[:, 0:128]
    s_lo = tab_ref[:, 128:256]
    s_hi = tab_ref[:, 256:384]
    return v * cos + pltpu.roll(v, n - half, 1) * s_lo + pltpu.roll(v, half, 1) * s_hi


def _store_v_slabs(ref, vt, heads):
    ones = jnp.ones((V_SLAB - HEAD_DIM, vt.shape[1]), BF16)
    for hd in range(heads):
        ref[V_SLAB * hd:V_SLAB * hd + HEAD_DIM, :] = vt[HEAD_DIM * hd:HEAD_DIM * (hd + 1), :].astype(BF16)
        ref[V_SLAB * hd + HEAD_DIM:V_SLAB * (hd + 1), :] = ones


def _proj_kernel(x_ref, mod_ref, g1_ref, w_ref, wuq_ref, wukv_ref, gq_ref, gkv_ref, lbp_ref, taba_ref, tabc_ref,
                 qa_ref, ka_ref, va_ref, bqig_ref, gates_ref, qc_ref, kc_ref, vc_ref, gl_ref):
    x = x_ref[...]
    h = _rms(x, g1_ref[...]) * (1.0 + mod_ref[1:2, :]) + mod_ref[0:1, :]
    hb = h.astype(BF16)

    def seg(off, width):
        return jnp.dot(hb, w_ref[:, off:off + width], preferred_element_type=F32)

    aq = seg(OFF_AQ, 512) * (HEAD_DIM ** -0.5 * LOG2E)
    for j in range(4):
        qa_ref[128 * j:128 * (j + 1), :] = _rope(aq[:, 128 * j:128 * (j + 1)], taba_ref, 32).T.astype(BF16)
    ka_ref[...] = _rope(seg(OFF_AK, 128), taba_ref, 32).astype(BF16)
    _store_v_slabs(va_ref, seg(OFF_AV, 128).T, A_KV_HEADS)

    bqig_ref[:, 0:512] = seg(OFF_BQ, 512).astype(BF16)
    bqig_ref[:, 512:1024] = seg(OFF_BI, 512).astype(BF16)
    bqig_ref[:, 1024:1536] = seg(OFF_BG, 512).astype(BF16)
    for d, off in enumerate((OFF_BZF, OFF_BZB)):
        z = seg(off, 512)
        log_lb = lbp_ref[0:1, 512 * d:512 * (d + 1)]
        log1m_lb = lbp_ref[1:2, 512 * d:512 * (d + 1)]
        one_m_lb = lbp_ref[2:3, 512 * d:512 * (d + 1)]
        e = jnp.exp(-jnp.abs(z))
        log_sig = jnp.minimum(z, 0.0) - jnp.log(1.0 + e)
        b = log1m_lb + log_sig
        mx = jnp.maximum(log_lb, b)
        logf = mx + jnp.log(1.0 + jnp.exp(-jnp.abs(log_lb - b)))
        r = 1.0 / (1.0 + e)
        key = one_m_lb * jnp.where(z >= 0.0, e * r, r)
        gates_ref[:, 512 * d:512 * (d + 1)] = logf
        gates_ref[:, 1024 + 512 * d:1024 + 512 * (d + 1)] = key

    cq = _rms(seg(OFF_CQ, C_Q_LORA), gq_ref[...]).astype(BF16)
    qh = jnp.dot(cq, wuq_ref[...], preferred_element_type=F32) * ((C_NOPE + C_ROPE) ** -0.5 * LOG2E)
    ckv = _rms(seg(OFF_CKV, C_KV_LORA), gkv_ref[...]).astype(BF16)
    kvh = jnp.dot(ckv, wukv_ref[...], preferred_element_type=F32)
    kr = _rope(seg(OFF_CKR, 128), tabc_ref, 16)
    for j in range(C_HEADS):
        sl = slice(C_HEAD_PAD * j, C_HEAD_PAD * (j + 1))
        qc_ref[sl, :] = _rope(qh[:, sl], tabc_ref, 16).T.astype(BF16)
        kc_ref[:, sl] = (kvh[:, sl] + kr).astype(BF16)
    _store_v_slabs(vc_ref, kvh[:, C_HEADS * C_HEAD_PAD:].T, C_HEADS)

    for j in range(6):
        gl_ref[:, 512 * j:512 * (j + 1)] = seg(OFF_GL + 512 * j, 512).astype(BF16)


def _projection(x, mod, g1, w_in_p, wuq_p, wukv_p, gq, gkv, lbp, taba, tabc, dims):
    B, S, L = dims
    n = x.shape[0]
    tm = PROJ_BLOCK
    nlat = B * S // tm
    spb = S // tm

    def row(i):
        return (i, 0)

    def mod_row(i):
        return (jnp.minimum(i // spb, B), 0, 0)

    def tab_row(i):
        return (jnp.where(i < nlat, i % spb, spb), 0)

    const = lambda i: (0, 0)
    resident = functools.partial(pl.BlockSpec, index_map=const, pipeline_mode=pl.Buffered(1))
    widths = (512, 128, A_KV_HEADS * V_SLAB, 1536, 2048, 1024, 1024, C_HEADS * V_SLAB, 3072)
    dtypes = (BF16, BF16, BF16, BF16, F32, BF16, BF16, BF16, BF16)
    transposed = (0, 2, 5, 7)
    return pl.pallas_call(
        _proj_kernel,
        grid=(n // tm,),
        in_specs=[
            pl.BlockSpec((tm, D_MODEL), row),
            pl.BlockSpec((None, 6, D_MODEL), mod_row),
            pl.BlockSpec((1, D_MODEL), const),
            resident((D_MODEL, IN_W_PACKED)),
            resident(wuq_p.shape),
            resident(wukv_p.shape),
            pl.BlockSpec((1, C_Q_LORA), const),
            pl.BlockSpec((1, C_KV_LORA), const),
            pl.BlockSpec((8, 2 * B_W), const),
            pl.BlockSpec((tm, 384), tab_row),
            pl.BlockSpec((tm, 384), tab_row),
        ],
        out_specs=[pl.BlockSpec((w, tm), lambda i: (0, i)) if k in transposed else pl.BlockSpec((tm, w), row)
                   for k, w in enumerate(widths)],
        out_shape=[jax.ShapeDtypeStruct((w, n) if k in transposed else (n, w), dt)
                   for k, (w, dt) in enumerate(zip(widths, dtypes))],
        compiler_params=_cparams(("parallel",), PROJ_VMEM_LIMIT),
        name="in_proj",
    )(x, mod, g1, w_in_p, wuq_p, wukv_p, gq, gkv, lbp, taba, tabc)


def _gqa_ctx_kernel(qt_ref, kx_ref, vx_ref, sink_ref, _, o_ref):
    for g in range(A_KV_HEADS):
        qg = jnp.concatenate([qt_ref[HEAD_DIM * hd:HEAD_DIM * (hd + 1), :]
                              for hd in range(A_GROUP * g, A_GROUP * (g + 1))], axis=1)
        s = jnp.dot(kx_ref[:, HEAD_DIM * g:HEAD_DIM * (g + 1)], qg, preferred_element_type=F32)
        m = jnp.maximum(jnp.max(s, axis=0, keepdims=True), sink_ref[g])
        p = jnp.exp2(s - m).astype(BF16)
        pv = jnp.dot(vx_ref[V_SLAB * g:V_SLAB * (g + 1), :], p, preferred_element_type=F32)
        o = pv[0:HEAD_DIM, :] / (pv[HEAD_DIM:HEAD_DIM + 1, :] + jnp.exp2(sink_ref[g] - m))
        for hh in range(A_GROUP):
            hd = A_GROUP * g + hh
            o_ref[HEAD_DIM * hd:HEAD_DIM * (hd + 1), :] = o[:, A_QBLOCK * hh:A_QBLOCK * (hh + 1)].astype(BF16)


def _gqa_lat_kernel(qt_ref, kp_ref, kc_ref, kn_ref, kx_ref, vp_ref, vc_ref, vn_ref, vx_ref, sink_ref, o_ref,
                    bias_ref, s0_ref, s1_ref, p0_ref, p1_ref, *, seq):
    j = pl.program_id(1)
    nband = 4 * A_BLOCK
    nsb, nk = bias_ref.shape[0], bias_ref.shape[1]
    step_q = nsb * A_QBLOCK
    rows = lax.broadcasted_iota(jnp.int32, (nk, A_QBLOCK), 0)
    cols = lax.broadcasted_iota(jnp.int32, (nk, A_QBLOCK), 1)
    for qb in range(nsb):
        qpos = j * step_q + qb * A_QBLOCK + cols
        kpos = j * step_q + qb * A_QBLOCK - A_BLOCK + rows
        valid = ((kpos >= 0) & (kpos < seq) & (jnp.abs(qpos - kpos) <= A_WINDOW)) | (rows >= nband)
        bias_ref[qb] = jnp.where(valid, 0.0, -jnp.inf)
    k = jnp.concatenate([kp_ref[...], kc_ref[...], kn_ref[...]], axis=0)
    vt = jnp.concatenate([vp_ref[...], vc_ref[...], vn_ref[...]], axis=1)
    s_refs, p_refs = (s0_ref, s1_ref), (p0_ref, p1_ref)
    nchunk = nk // A_QBLOCK
    krows = lambda r: slice(A_QBLOCK * r, A_QBLOCK * (r + 1))
    items = [(qb, g) for qb in range(nsb) for g in range(A_KV_HEADS)]
    n = len(items)
    smax, m, pv = [None] * n, [None] * n, [None] * n

    def keys(i, r):
        qb, g = items[i]
        if A_QBLOCK * (r + 1) <= nband:
            sl = slice(A_QBLOCK * (qb + r), A_QBLOCK * (qb + r + 1))
            return k[sl, HEAD_DIM * g:HEAD_DIM * (g + 1)], vt[V_SLAB * g:V_SLAB * (g + 1), sl]
        return kx_ref[:, HEAD_DIM * g:HEAD_DIM * (g + 1)], vx_ref[V_SLAB * g:V_SLAB * (g + 1), :]

    def stage_scores(i, r):
        qb, g = items[i]
        qcols = slice(A_QBLOCK * qb, A_QBLOCK * (qb + 1))
        qg = jnp.concatenate([qt_ref[HEAD_DIM * hd:HEAD_DIM * (hd + 1), qcols]
                              for hd in range(A_GROUP * g, A_GROUP * (g + 1))], axis=1)
        bias = bias_ref[qb, krows(r), :]
        sc = (jnp.dot(keys(i, r)[0], qg, preferred_element_type=F32) + jnp.concatenate([bias] * A_GROUP, axis=1))
        s_refs[i % 2][krows(r), :] = sc
        cm = jnp.max(sc, axis=0, keepdims=True)
        smax[i] = cm if smax[i] is None else jnp.maximum(smax[i], cm)

    def stage_exp(i, r):
        p_refs[i % 2][krows(r), :] = jnp.exp2(s_refs[i % 2][krows(r), :] - m[i]).astype(BF16)

    def stage_pv(i, r):
        qb, g = items[i]
        t = jnp.dot(keys(i, r)[1], p_refs[i % 2][krows(r), :], preferred_element_type=F32)
        pv[i] = t if pv[i] is None else pv[i] + t
        if r == nchunk - 1:
            denom = pv[i][HEAD_DIM:HEAD_DIM + 1, :] + jnp.exp2(sink_ref[g] - m[i])
            o = pv[i][0:HEAD_DIM, :] / denom
            for hh in range(A_GROUP):
                hd = A_GROUP * g + hh
                o_ref[HEAD_DIM * hd:HEAD_DIM * (hd + 1), A_QBLOCK * qb:A_QBLOCK * (qb + 1)] = (
                    o[:, A_QBLOCK * hh:A_QBLOCK * (hh + 1)].astype(BF16))

    for r in range(nchunk):
        stage_scores(0, r)
    for i in range(n + 1):
        if i < n:
            m[i] = jnp.maximum(smax[i], sink_ref[items[i][1]])
        for r in range(nchunk):
            if i + 1 < n:
                stage_scores(i + 1, r)
            if i < n:
                stage_exp(i, r)
            if i >= 1:
                stage_pv(i - 1, r)


def _window_gqa(qat, ka, vat, sink, dims, need_ctx):
    B, S, L = dims
    n = ka.shape[0]
    assert L == A_QBLOCK
    nb = S // A_BLOCK
    nqb = S // A_QBLOCK
    nk = 4 * A_BLOCK + L
    width = A_GROUP * A_QBLOCK
    vs = A_KV_HEADS * V_SLAB
    step_q = min(A_STEP_BLOCKS * A_QBLOCK, S)
    nstep = S // step_q
    bps = step_q // A_BLOCK
    ctx_row = lambda b, j: (B * S // L + b, 0)
    ctx_col = lambda b, j: (0, B * S // L + b)
    sink_spec = pl.BlockSpec((A_KV_HEADS, 1, width), lambda b, j: (0, 0, 0))
    stage = [pltpu.VMEM((nk, width), F32), pltpu.VMEM((nk, width), F32),
             pltpu.VMEM((nk, width), BF16), pltpu.VMEM((nk, width), BF16)]
    prev_blk = lambda b, j: b * nb + jnp.maximum(bps * j - 1, 0)
    next_blk = lambda b, j: b * nb + jnp.minimum(bps * (j + 1), nb - 1)
    o_lat = pl.pallas_call(
        functools.partial(_gqa_lat_kernel, seq=S),
        grid=(B, nstep),
        in_specs=[
            pl.BlockSpec((512, step_q), lambda b, j: (0, b * nstep + j)),
            pl.BlockSpec((A_BLOCK, 128), lambda b, j: (prev_blk(b, j), 0)),
            pl.BlockSpec((step_q, 128), lambda b, j: (b * nstep + j, 0)),
            pl.BlockSpec((A_BLOCK, 128), lambda b, j: (next_blk(b, j), 0)),
            pl.BlockSpec((L, 128), ctx_row),
            pl.BlockSpec((vs, A_BLOCK), lambda b, j: (0, prev_blk(b, j))),
            pl.BlockSpec((vs, step_q), lambda b, j: (0, b * nstep + j)),
            pl.BlockSpec((vs, A_BLOCK), lambda b, j: (0, next_blk(b, j))),
            pl.BlockSpec((vs, L), ctx_col),
            sink_spec,
        ],
        out_specs=pl.BlockSpec((512, step_q), lambda b, j: (0, b * nstep + j)),
        out_shape=jax.ShapeDtypeStruct((512, n if need_ctx else B * S), BF16),
        scratch_shapes=[pltpu.VMEM((step_q // A_QBLOCK, nk, A_QBLOCK), F32)] + stage,
        compiler_params=_cparams(("parallel", "parallel")),
        name="window_gqa",
    )(qat, ka, ka, ka, ka, vat, vat, vat, vat, sink)
    if not need_ctx:
        return o_lat
    return pl.pallas_call(
        _gqa_ctx_kernel,
        grid=(B,),
        in_specs=[pl.BlockSpec((512, L), lambda b: (0, B * S // L + b)),
                  pl.BlockSpec((L, 128), lambda b: (B * S // L + b, 0)),
                  pl.BlockSpec((vs, L), lambda b: (0, B * S // L + b)),
                  pl.BlockSpec((A_KV_HEADS, 1, width), lambda b: (0, 0, 0)),
                  pl.BlockSpec(memory_space=pl.ANY)],
        out_specs=pl.BlockSpec((512, L), lambda b: (0, B * S // L + b)),
        out_shape=jax.ShapeDtypeStruct((512, n), BF16),
        input_output_aliases={4: 0},
        compiler_params=_cparams(("parallel",)),
        name="window_gqa_context",
    )(qat, ka, vat, sink, o_lat)


def _hgrn_prep(q_ref, v_ref, g_ref, k_ref, r0, reverse):
    C = B_CHUNK
    rs = slice(r0, r0 + C)
    rows = lax.broadcasted_iota(jnp.int32, (C, C), 0)
    cols = lax.broadcasted_iota(jnp.int32, (C, C), 1)
    causal = (rows <= cols) if reverse else (rows >= cols)
    g = g_ref[rs, :]
    bc = jnp.dot(causal.astype(F32), g, preferred_element_type=F32, precision=HIGHEST)
    tot = jnp.sum(g, axis=0, keepdims=True)
    mid = C // 2 if reverse else C // 2 - 1
    rho = bc[mid:mid + 1, :]
    q = q_ref[rs, :].astype(F32)
    key = k_ref[rs, :]
    v = v_ref[rs, :]
    qe = (q * jnp.exp(bc - rho)).astype(BF16)
    ke = (key * jnp.exp(rho - bc)).astype(BF16)
    qs = (q * jnp.exp(bc)).astype(BF16)
    ks = (key * jnp.exp(tot - bc)).astype(BF16)
    dec = jnp.exp(tot)
    return rs, causal, v, qe, ke, qs, ks, dec


def _hgrn_heads(prep, o_ref, st_ref):
    rs, causal, v, qe, ke, qs, ks, dec = prep
    for hd in range(B_HEADS):
        sl = slice(B_DK * hd, B_DK * (hd + 1))
        st = st_ref[hd]
        att = lax.dot_general(qe[:, sl], ke[:, sl], (((1,), (1,)), ((), ())), preferred_element_type=F32)
        att = jnp.where(causal, att, 0.0).astype(BF16)
        o = lax.dot_general(qs[:, sl], st.astype(BF16), (((1,), (1,)), ((), ())), preferred_element_type=F32)
        o = o + jnp.dot(att, v[:, sl], preferred_element_type=F32)
        o_ref[rs, sl] = o
        upd = lax.dot_general(v[:, sl], ks[:, sl], (((0,), (0,)), ((), ())), preferred_element_type=F32)
        st_ref[hd] = st * dec[:, sl] + upd


def _hgrn_kernel(qf_ref, vf_ref, gf_ref, kf_ref, qb_ref, vb_ref, gb_ref, kb_ref, of_ref, ob_ref, stf_ref, stb_ref):
    @pl.when(pl.program_id(1) == 0)
    def _():
        stf_ref[...] = jnp.zeros_like(stf_ref)
        stb_ref[...] = jnp.zeros_like(stb_ref)

    nchunk = qf_ref.shape[0] // B_CHUNK
    preps = []
    for i in range(nchunk):
        preps.append((_hgrn_prep(qf_ref, vf_ref, gf_ref, kf_ref, B_CHUNK * i, False), of_ref, stf_ref))
        preps.append((_hgrn_prep(qb_ref, vb_ref, gb_ref, kb_ref, B_CHUNK * (nchunk - 1 - i), True), ob_ref, stb_ref))
    for prep, o_ref, st_ref in preps:
        _hgrn_heads(prep, o_ref, st_ref)


def _hgrn2_scan(bqig, gates, dims):
    B, S, L = dims
    n = bqig.shape[0]
    T = B_BLOCK
    assert L == T
    ns = S // T

    def fwd_blk(b, c):
        return jnp.where(c == 0, B * ns + b, b * ns + c - 1)

    def bwd_blk(b, c):
        return jnp.where(c == 0, B * ns + b, b * ns + ns - c)

    def specs(blk, d):
        return [pl.BlockSpec((T, B_W), lambda b, c: (blk(b, c), 0)),
                pl.BlockSpec((T, B_W), lambda b, c: (blk(b, c), 1)),
                pl.BlockSpec((T, B_W), lambda b, c: (blk(b, c), d)),
                pl.BlockSpec((T, B_W), lambda b, c: (blk(b, c), 2 + d))]

    return pl.pallas_call(
        _hgrn_kernel,
        grid=(B, ns + 1),
        in_specs=specs(fwd_blk, 0) + specs(bwd_blk, 1),
        out_specs=[pl.BlockSpec((T, B_W), lambda b, c: (fwd_blk(b, c), 0)),
                   pl.BlockSpec((T, B_W), lambda b, c: (bwd_blk(b, c), 0))],
        out_shape=[jax.ShapeDtypeStruct((n, B_W), F32), jax.ShapeDtypeStruct((n, B_W), F32)],
        scratch_shapes=[pltpu.VMEM((B_HEADS, B_DK, B_DK), F32), pltpu.VMEM((B_HEADS, B_DK, B_DK), F32)],
        compiler_params=_cparams(("parallel", "arbitrary")),
        name="hgrn2_scan",
    )(bqig, bqig, gates, gates, bqig, bqig, gates, gates)


def _mla_kernel(qt_ref, k_ref, vt_ref, *rest, with_ctx):
    if with_ctx:
        kx_ref, vxt_ref, o_ref, m_ref, acc_ref, *bufs = rest
    else:
        _, o_ref, m_ref, acc_ref, *bufs = rest
    s_refs, p_refs = bufs[0:2], bufs[2:4]
    kstep = pl.program_id(2)
    tq = qt_ref.shape[1]

    def kv_pass(k_ref, vt_ref, first):
        nkeys = k_ref.shape[0]
        nchunk = nkeys // MLA_KEY_CHUNK
        krows = lambda r: slice(MLA_KEY_CHUNK * r, MLA_KEY_CHUNK * (r + 1))

        group = s_refs[0].shape[1]
        items = [(c, hd) for c in range(tq // group) for hd in range(C_HEADS)]
        n = len(items)
        smax = [None] * n
        m_new = [None] * n
        alpha = [None] * n
        pv = [None] * n
        cols = lambda i: slice(group * items[i][0], group * (items[i][0] + 1))

        def stage_scores(i, r):
            hd = items[i][1]
            sl = slice(C_HEAD_PAD * hd, C_HEAD_PAD * (hd + 1))
            sc = jnp.dot(k_ref[krows(r), sl], qt_ref[sl, cols(i)], preferred_element_type=F32)
            s_refs[i % 2][krows(r), :] = sc
            cm = jnp.max(sc, axis=0, keepdims=True)
            smax[i] = cm if smax[i] is None else jnp.maximum(smax[i], cm)

        def stage_stats(i):
            hd = items[i][1]
            if first:
                m_new[i] = smax[i]
            else:
                m_old = m_ref[hd, :, cols(i)]
                m_new[i] = jnp.maximum(m_old, smax[i])
                alpha[i] = jnp.exp2(m_old - m_new[i])
            m_ref[hd, :, cols(i)] = m_new[i]

        def stage_exp(i, r):
            p_refs[i % 2][krows(r), :] = jnp.exp2(s_refs[i % 2][krows(r), :] - m_new[i]).astype(BF16)

        def stage_pv(i, r):
            hd = items[i][1]
            t = jnp.dot(vt_ref[V_SLAB * hd:V_SLAB * (hd + 1), krows(r)], p_refs[i % 2][krows(r), :],
                        preferred_element_type=F32)
            pv[i] = t if pv[i] is None else pv[i] + t
            if r == nchunk - 1:
                acc_ref[hd, :, cols(i)] = pv[i] if first else alpha[i] * acc_ref[hd, :, cols(i)] + pv[i]

        for r in range(nchunk):
            stage_scores(0, r)
        for i in range(n + 1):
            if i < n:
                stage_stats(i)
            for r in range(nchunk):
                if i + 1 < n:
                    stage_scores(i + 1, r)
                if i < n:
                    stage_exp(i, r)
                if i >= 1:
                    stage_pv(i - 1, r)

    def kv_pass_lagged(k_ref, vt_ref, tmp_ref):
        nkeys = k_ref.shape[0]
        nchunk = nkeys // MLA_KEY_CHUNK
        krows = lambda r: slice(MLA_KEY_CHUNK * r, MLA_KEY_CHUNK * (r + 1))
        group = p_refs[0].shape[1]
        items = [(c, hd) for c in range(tq // group) for hd in range(C_HEADS)]
        n = len(items)
        cols = lambda i: slice(group * items[i][0], group * (items[i][0] + 1))
        base, smax, pv = [None] * n, [None] * n, [None] * n

        def stage_scores(i, r):
            hd = items[i][1]
            sl = slice(C_HEAD_PAD * hd, C_HEAD_PAD * (hd + 1))
            if base[i] is None:
                base[i] = m_ref[hd, :, cols(i)]
            sc = jnp.dot(k_ref[krows(r), sl], qt_ref[sl, cols(i)], preferred_element_type=F32)
            p_refs[i % 2][krows(r), :] = jnp.exp2(sc - base[i]).astype(BF16)
            cm = jnp.max(sc, axis=0, keepdims=True)
            smax[i] = cm if smax[i] is None else jnp.maximum(smax[i], cm)

        def stage_pv(i, r):
            hd = items[i][1]
            t = jnp.dot(vt_ref[V_SLAB * hd:V_SLAB * (hd + 1), krows(r)], p_refs[i % 2][krows(r), :],
                        preferred_element_type=F32)
            pv[i] = t if pv[i] is None else pv[i] + t
            if r == nchunk - 1:
                tmp_ref[hd, :, cols(i)] = pv[i]

        for r in range(nchunk):
            stage_scores(0, r)
        for i in range(n):
            for r in range(nchunk):
                if i + 1 < n:
                    stage_scores(i + 1, r)
                stage_pv(i, r)
        excess = smax[0] - base[0]
        for i in range(1, n):
            excess = jnp.maximum(excess, smax[i] - base[i])
        return jnp.max(excess)

    if with_ctx:
        @pl.when(kstep == 0)
        def _():
            kv_pass(kx_ref, vxt_ref, True)

        tmp_ref = bufs[4]
        safe = kv_pass_lagged(k_ref, vt_ref, tmp_ref) <= MLA_LAG_LIMIT

        @pl.when(safe)
        def _():
            for hd in range(C_HEADS):
                acc_ref[hd] = acc_ref[hd] + tmp_ref[hd]

        @pl.when(jnp.logical_not(safe))
        def _():
            kv_pass(k_ref, vt_ref, False)
    else:
        kv_pass(k_ref, vt_ref, True)

    @pl.when(kstep == pl.num_programs(2) - 1)
    def _():
        for hd in range(C_HEADS):
            o_ref[C_V * hd:C_V * (hd + 1), :] = (acc_ref[hd, 0:C_V, :] / acc_ref[hd, C_V:C_V + 1, :]).astype(BF16)


def _mla_attention(qct, kc, vct, dims, need_ctx):
    B, S, L = dims
    n = kc.shape[0]
    tq = min(MLA_Q_BLOCK, S)
    tk = min(MLA_K_BLOCK, S)
    nq, nk = S // tq, S // tk
    hw = C_HEADS * C_HEAD_PAD
    vw = C_HEADS * C_V
    vs = C_HEADS * V_SLAB
    scratch = lambda t, nkeys: [
        pltpu.VMEM((C_HEADS, 1, t), F32), pltpu.VMEM((C_HEADS, V_SLAB, t), F32),
        pltpu.VMEM((nkeys, min(t, MLA_COL_GROUP)), F32), pltpu.VMEM((nkeys, min(t, MLA_COL_GROUP)), F32),
        pltpu.VMEM((nkeys, min(t, MLA_COL_GROUP)), BF16), pltpu.VMEM((nkeys, min(t, MLA_COL_GROUP)), BF16)]
    ctx_row = lambda b, i, k: (B * S // L + b, 0)
    ctx_col = lambda b, i, k: (0, B * S // L + b)
    o_lat = pl.pallas_call(
        functools.partial(_mla_kernel, with_ctx=True),
        grid=(B, nq, nk),
        in_specs=[
            pl.BlockSpec((hw, tq), lambda b, i, k: (0, b * nq + i)),
            pl.BlockSpec((tk, hw), lambda b, i, k: (b * nk + k, 0)),
            pl.BlockSpec((vs, tk), lambda b, i, k: (0, b * nk + k)),
            pl.BlockSpec((L, hw), ctx_row),
            pl.BlockSpec((vs, L), ctx_col),
        ],
        out_specs=pl.BlockSpec((vw, tq), lambda b, i, k: (0, b * nq + i)),
        out_shape=jax.ShapeDtypeStruct((vw, n if need_ctx else B * S), BF16),
        scratch_shapes=scratch(tq, tk) + [pltpu.VMEM((C_HEADS, V_SLAB, tq), F32)],
        compiler_params=_cparams(("parallel", "parallel", "arbitrary")),
        name="mla_latent",
    )(qct, kc, vct, kc, vct)
    if not need_ctx:
        return o_lat
    return pl.pallas_call(
        functools.partial(_mla_kernel, with_ctx=False),
        grid=(B, 1, 1),
        in_specs=[
            pl.BlockSpec((hw, L), ctx_col),
            pl.BlockSpec((L, hw), ctx_row),
            pl.BlockSpec((vs, L), ctx_col),
            pl.BlockSpec(memory_space=pl.ANY),
        ],
        out_specs=pl.BlockSpec((vw, L), ctx_col),
        out_shape=jax.ShapeDtypeStruct((vw, n), BF16),
        scratch_shapes=scratch(L, L),
        input_output_aliases={3: 0},
        compiler_params=_cparams(("parallel", "arbitrary", "arbitrary")),
        name="mla_context",
    )(qct, kc, vct, o_lat)


def _group_sum(x, ones_ref):
    hi = x.astype(BF16)
    lo = (x - hi.astype(F32)).astype(BF16)
    return (jnp.dot(hi, ones_ref[...], preferred_element_type=F32)
            + jnp.dot(lo, ones_ref[...], preferred_element_type=F32))


def _merge_kernel(x_ref, oa_ref, of_ref, ob_ref, bg_ref, oc_ref, gl_ref, wbr_ref, wout_ref, gn_ref, mod_ref,
                  g2_ref, ones_ref, xo_ref, h2_ref):
    ob = of_ref[...] + ob_ref[...]
    ms = _group_sum(ob * ob, ones_ref) * (1.0 / B_DK)
    obn = ob * lax.rsqrt(ms + EPS) * gn_ref[...]
    bg = bg_ref[...].astype(F32)
    bb = (obn * (bg * jax.nn.sigmoid(bg))).astype(BF16)
    branches = ((oa_ref[...], 0), (bb, 1), (oc_ref[...], 0))
    y = None
    for nbr, (br, axis) in enumerate(branches):
        gate = jax.nn.sigmoid(gl_ref[:, D_MODEL * nbr:D_MODEL * (nbr + 1)].astype(F32))
        t = gate * lax.dot_general(br, wbr_ref[nbr], (((axis,), (0,)), ((), ())), preferred_element_type=F32)
        y = t if y is None else y + t
    upd = jnp.dot(y.astype(BF16), wout_ref[...], preferred_element_type=F32)
    xn = x_ref[...] + mod_ref[2:3, :] * upd
    xo_ref[...] = xn
    h2 = _rms(xn, g2_ref[...]) * (1.0 + mod_ref[4:5, :]) + mod_ref[3:4, :]
    h2_ref[...] = h2.astype(BF16)


def _merge(x, oa, ohg, bqig, oc, gl, wbr, wout, gn, mod, g2, ones, dims, need_ctx):
    B, S, L = dims
    n = x.shape[0]
    tm = PROJ_BLOCK
    spb = S // tm
    nblk = (n if need_ctx else B * S) // tm
    row = lambda i: (i, 0)
    const2 = lambda i: (0, 0)
    return pl.pallas_call(
        _merge_kernel,
        grid=(nblk,),
        in_specs=[
            pl.BlockSpec((tm, D_MODEL), row),
            pl.BlockSpec((512, tm), lambda i: (0, i)),
            pl.BlockSpec((tm, B_W), row),
            pl.BlockSpec((tm, B_W), row),
            pl.BlockSpec((tm, B_W), lambda i: (i, 2)),
            pl.BlockSpec((512, tm), lambda i: (0, i)),
            pl.BlockSpec((tm, N_BRANCH * D_MODEL), row),
            pl.BlockSpec((N_BRANCH, BRANCH_W, D_MODEL), lambda i: (0, 0, 0)),
            pl.BlockSpec((D_MODEL, D_MODEL), const2),
            pl.BlockSpec((1, B_W), const2),
            pl.BlockSpec((None, 6, D_MODEL), lambda i: (jnp.minimum(i // spb, B), 0, 0)),
            pl.BlockSpec((1, D_MODEL), const2),
            pl.BlockSpec((B_W, B_W), const2),
        ],
        out_specs=[pl.BlockSpec((tm, D_MODEL), row), pl.BlockSpec((tm, D_MODEL), row)],
        out_shape=[jax.ShapeDtypeStruct((nblk * tm, D_MODEL), F32), jax.ShapeDtypeStruct((nblk * tm, D_MODEL), BF16)],
        compiler_params=_cparams(("parallel",)),
        name="branch_merge",
    )(x, oa, ohg[0], ohg[1], bqig, oc, gl, wbr, wout, gn, mod, g2, ones)


def _route_kernel(h_ref, wr_ref, comb_ref, pos_ref, cnt_ref):
    h = h_ref[...]
    T = h.shape[0]
    lane = lax.broadcasted_iota(jnp.int32, (T, ROUTER_W), 1)
    logits = jnp.dot(h, wr_ref[...], preferred_element_type=F32)
    big = jnp.int32(ROUTER_W)
    is_grp = (lane >= N_EXPERTS) & (lane < N_EXPERTS + N_GROUPS)
    gl = jnp.where(is_grp, logits, -jnp.inf)
    gmax = jnp.max(gl, axis=-1, keepdims=True)
    gsel = jnp.min(jnp.where(gl == gmax, lane, big), axis=-1, keepdims=True) - N_EXPERTS
    gw = 1.0 / jnp.sum(jnp.exp(gl - gmax), axis=-1, keepdims=True)
    in_grp = (lane >= gsel * EXPERTS_PER_GROUP) & (lane < (gsel + 1) * EXPERTS_PER_GROUP)
    el = jnp.where(in_grp, logits, -jnp.inf)
    m1 = jnp.max(el, axis=-1, keepdims=True)
    i1 = jnp.min(jnp.where(el == m1, lane, big), axis=-1, keepdims=True)
    el2 = jnp.where(lane == i1, -jnp.inf, el)
    m2 = jnp.max(el2, axis=-1, keepdims=True)
    i2 = jnp.min(jnp.where(el2 == m2, lane, big), axis=-1, keepdims=True)
    e2 = jnp.exp(m2 - m1)
    w1 = gw / (1.0 + e2)
    w2 = gw * e2 / (1.0 + e2)
    comb_ref[...] = jnp.where(lane == i1, w1, 0.0) + jnp.where(lane == i2, w2, 0.0)

    onehot = lane == gsel
    ones = jnp.where(onehot, 1.0, 0.0)
    rows = lax.broadcasted_iota(jnp.int32, (T, T), 0)
    cols = lax.broadcasted_iota(jnp.int32, (T, T), 1)
    earlier = jnp.where(rows > cols, 1.0, 0.0).astype(BF16)
    before = jnp.dot(earlier, ones.astype(BF16), preferred_element_type=F32)
    rank = jnp.sum(jnp.where(onehot, before, 0.0), axis=-1, keepdims=True)
    cnt = jnp.sum(ones, axis=0, keepdims=True)
    padded = jnp.floor((cnt + (MOE_ALIGN - 1)) * (1.0 / MOE_ALIGN)) * MOE_ALIGN
    seg = [jnp.sum(jnp.where(lane[0:1] == g, padded, 0.0), axis=-1, keepdims=True) for g in range(N_GROUPS - 1)]
    start = jnp.where(gsel == 0, 0.0, jnp.where(gsel == 1, seg[0], jnp.where(gsel == 2, seg[0] + seg[1],
                                                                             seg[0] + seg[1] + seg[2])))
    pos_ref[...] = jnp.broadcast_to(start + rank, (T, ROUTER_W))
    cnt_ref[...] = jnp.broadcast_to(cnt, (8, ROUTER_W)).astype(jnp.int32)


def _moe_kernel(cnt_ref, h_ref, x_ref, mod_ref, pos_ref, comb_ref, w13_ref, w2_ref, gf_ref, o_ref,
                pt_ref, xs_ref, cs_ref, ys_ref, *, final_norm):
    i = pl.program_id(0)
    e = pl.program_id(1)
    T = h_ref.shape[0]
    R = xs_ref.shape[0]
    gather = lambda a: lax.dot_general(pt_ref[...], a, (((0,), (0,)), ((), ())), preferred_element_type=F32)

    @pl.when(e == 0)
    def _():
        slot = lax.broadcasted_iota(jnp.int32, (T, R), 1).astype(F32)
        pt_ref[...] = jnp.where(pos_ref[:, 0:1] == slot, 1.0, 0.0).astype(BF16)
        xs_ref[...] = gather(h_ref[...]).astype(BF16)
        comb = comb_ref[...]
        hi = comb.astype(BF16)
        r1 = comb - hi.astype(F32)
        mid = r1.astype(BF16)
        lo = (r1 - mid.astype(F32)).astype(BF16)
        cs_ref[...] = gather(hi) + gather(mid) + gather(lo)
        ys_ref[...] = jnp.zeros_like(ys_ref)

    g = (e * MOE_EXPERTS_PER_STEP) // EXPERTS_PER_GROUP
    cnt = [cnt_ref[i * N_GROUPS + gg] for gg in range(N_GROUPS)]
    seg = [(c + (MOE_ALIGN - 1)) // MOE_ALIGN * MOE_ALIGN for c in cnt]
    start = (jnp.where(g > 0, seg[0], 0) + jnp.where(g > 1, seg[1], 0) + jnp.where(g > 2, seg[2], 0))
    cnt_g = jnp.where(g == 0, cnt[0], jnp.where(g == 1, cnt[1], jnp.where(g == 2, cnt[2], cnt[3])))
    lane = lax.broadcasted_iota(jnp.int32, (MOE_TILE, ROUTER_W), 1)

    def tile(t, carry):
        r0 = pl.multiple_of(start + t * MOE_TILE, MOE_ALIGN)
        xt = xs_ref[pl.ds(r0, MOE_TILE), :]
        ct = cs_ref[pl.ds(r0, MOE_TILE), :]
        acts = []
        for k in range(MOE_EXPERTS_PER_STEP):
            ce = jnp.sum(jnp.where(lane == e * MOE_EXPERTS_PER_STEP + k, ct, 0.0), axis=-1, keepdims=True)
            h13 = jnp.dot(xt, w13_ref[k], preferred_element_type=F32)
            a1 = h13[:, :D_EXPERT]
            acts.append((a1 * jax.nn.sigmoid(a1) * h13[:, D_EXPERT:] * ce).astype(BF16))
        ys_ref[pl.ds(r0, MOE_TILE), :] += jnp.dot(jnp.concatenate(acts, axis=-1), w2_ref[...],
                                                   preferred_element_type=F32)
        return carry

    lax.fori_loop(0, (cnt_g + MOE_TILE - 1) // MOE_TILE, tile, 0)

    @pl.when(e == pl.num_programs(1) - 1)
    def _():
        y = jnp.dot(pt_ref[...], ys_ref[...].astype(BF16), preferred_element_type=F32)
        xn = x_ref[...] + mod_ref[5:6, :] * y
        o_ref[...] = _rms(xn, gf_ref[...]) if final_norm else xn


def _moe(h2, x, mod, wr, w13, w2, g_final, dims, need_ctx, final_norm):
    B, S, L = dims
    n = x.shape[0]
    tm = min(MOE_TOKEN_BLOCK, S, B * L)
    spb = S // tm
    nblk = (n if need_ctx else B * S) // tm
    comb, pos, cnt = pl.pallas_call(
        _route_kernel,
        grid=(nblk,),
        in_specs=[pl.BlockSpec((tm, D_MODEL), lambda i: (i, 0)), pl.BlockSpec((D_MODEL, ROUTER_W), lambda i: (0, 0))],
        out_specs=[pl.BlockSpec((tm, ROUTER_W), lambda i: (i, 0)), pl.BlockSpec((tm, ROUTER_W), lambda i: (i, 0)),
                   pl.BlockSpec((8, ROUTER_W), lambda i: (i, 0))],
        out_shape=[jax.ShapeDtypeStruct((nblk * tm, ROUTER_W), F32), jax.ShapeDtypeStruct((nblk * tm, ROUTER_W), F32),
                   jax.ShapeDtypeStruct((nblk * 8, ROUTER_W), jnp.int32)],
        compiler_params=_cparams(("parallel",)),
        name="moe_route",
    )(h2, wr)
    cnt = cnt.reshape(nblk, 8, ROUTER_W)[:, 0, :N_GROUPS].reshape(nblk * N_GROUPS)

    row = lambda i, e, c: (i, 0)
    eps = MOE_EXPERTS_PER_STEP
    assert MOE_SORT_PAD >= (N_GROUPS - 1) * (MOE_ALIGN - 1) + MOE_TILE - 1 and EXPERTS_PER_GROUP % eps == 0
    slots = tm + MOE_SORT_PAD
    return pl.pallas_call(
        functools.partial(_moe_kernel, final_norm=final_norm),
        grid_spec=pltpu.PrefetchScalarGridSpec(
            num_scalar_prefetch=1,
            grid=(nblk, N_EXPERTS // eps),
            in_specs=[
                pl.BlockSpec((tm, D_MODEL), row),
                pl.BlockSpec((tm, D_MODEL), row),
                pl.BlockSpec((None, 6, D_MODEL), lambda i, e, c: (jnp.minimum(i // spb, B), 0, 0)),
                pl.BlockSpec((tm, ROUTER_W), row),
                pl.BlockSpec((tm, ROUTER_W), row),
                pl.BlockSpec((eps, D_MODEL, 2 * D_EXPERT), lambda i, e, c: (e, 0, 0)),
                pl.BlockSpec((eps * D_EXPERT, D_MODEL), lambda i, e, c: (e, 0)),
                pl.BlockSpec((1, D_MODEL), lambda i, e, c: (0, 0)),
            ],
            out_specs=pl.BlockSpec((tm, D_MODEL), row),
            scratch_shapes=[pltpu.VMEM((tm, slots), BF16), pltpu.VMEM((slots, D_MODEL), BF16),
                            pltpu.VMEM((slots, ROUTER_W), F32), pltpu.VMEM((slots, D_MODEL), F32)],
        ),
        out_shape=jax.ShapeDtypeStruct((nblk * tm, D_MODEL), F32),
        compiler_params=_cparams(("parallel", "arbitrary")),
        name="hier_moe",
    )(cnt, h2, x, mod, pos, comb, w13, w2.reshape(N_EXPERTS * D_EXPERT, D_MODEL), g_final)


def _rope_tables(S, L):
    rows = S // GRID_W
    pos_r = np.repeat(np.arange(rows, dtype=np.float32), GRID_W)
    pos_c = np.tile(np.arange(GRID_W, dtype=np.float32), rows)

    def angles(rot_dim):
        nf = rot_dim // 4
        inv = jnp.asarray(ROPE_BASE, F32) ** (-jnp.arange(nf, dtype=F32) / nf)
        ang = jnp.concatenate([pos_r[:, None] * inv, pos_c[:, None] * inv], axis=-1)
        return jnp.cos(ang), jnp.sin(ang)

    def with_ctx(cos, s_lo, s_hi):
        ident = jnp.concatenate([jnp.ones((PROJ_BLOCK, 128), F32), jnp.zeros((PROJ_BLOCK, 256), F32)], axis=-1)
        return jnp.concatenate([jnp.concatenate([cos, s_lo, s_hi], axis=-1), ident], axis=0)

    cos, sin = angles(HEAD_DIM)
    z = jnp.zeros_like(sin)
    taba = with_ctx(jnp.tile(cos, (1, 4)), jnp.tile(jnp.concatenate([-sin, z], -1), (1, 2)),
                    jnp.tile(jnp.concatenate([z, sin], -1), (1, 2)))
    cos, sin = angles(C_ROPE)
    z = jnp.zeros_like(sin)
    one64, zero64, zero32 = jnp.ones((S, 64), F32), jnp.zeros((S, 64), F32), jnp.zeros((S, 32), F32)
    tabc = with_ctx(jnp.concatenate([one64, cos, cos, one64[:, :32]], -1),
                    jnp.concatenate([zero64, -sin, z, zero32], -1),
                    jnp.concatenate([zero64, z, sin, zero32], -1))
    return taba, tabc


def _pack_w_in(w):
    pad = lambda k: jnp.zeros((w.shape[0], k), w.dtype)
    return jnp.concatenate([w[:, :3712], pad(64), w[:, 3712:3744], pad(32), w[:, 3744:]], axis=-1).astype(BF16)


def _pack_w_uq(w):
    w = w.reshape(C_Q_LORA, C_HEADS, C_NOPE + C_ROPE)
    w = jnp.pad(w, ((0, 0), (0, 0), (0, C_HEAD_PAD - C_NOPE - C_ROPE)))
    return w.reshape(C_Q_LORA, C_HEADS * C_HEAD_PAD).astype(BF16)


def _pack_w_ukv(w):
    w = w.reshape(C_KV_LORA, C_HEADS, C_NOPE + C_V)
    wk = jnp.pad(w[:, :, :C_NOPE], ((0, 0), (0, 0), (0, C_HEAD_PAD - C_NOPE))).reshape(C_KV_LORA, -1)
    wv = w[:, :, C_NOPE:].reshape(C_KV_LORA, -1)
    return jnp.concatenate([wk, wv], axis=-1).astype(BF16)


def kernel(x, c, ctx, c_ctx, w_mod, b_mod, g_norm1, g_norm2, w_in, a_sink, b_lb_logits, b_onorm, c_qnorm, c_kvnorm,
           w_uq, w_ukv, w_br, w_out, w_rg, w_re, w1, w3, w2, g_final):
    B, S, _ = x.shape
    L = ctx.shape[1]
    depth = w_in.shape[0]
    assert L == TOKEN_BLOCK and S % min(MLA_Q_BLOCK, S) == 0 and S % min(MLA_K_BLOCK, S) == 0 and S % GRID_W == 0
    assert S % PROJ_BLOCK == 0 and (B * L) % PROJ_BLOCK == 0
    dims = (B, S, L)

    xs = jnp.concatenate([x.reshape(B * S, D_MODEL), ctx.reshape(B * L, D_MODEL)], axis=0)
    cc = jnp.zeros((8, D_MODEL), F32).at[:B].set(c).at[B].set(c_ctx)
    mod_all = _modulation(cc, w_mod, b_mod).reshape(depth, 8, 6, D_MODEL)

    lb_all = jnp.cumsum(jax.nn.softmax(b_lb_logits.astype(F32), axis=0), axis=0)
    lb_all = (lb_all - lb_all[0:1]).reshape(depth, 1, 2 * B_W)
    lbp_all = jnp.concatenate([jnp.log(lb_all), jnp.log1p(-lb_all), 1.0 - lb_all,
                               jnp.zeros((depth, 5, 2 * B_W), F32)], axis=1)

    taba, tabc = _rope_tables(S, L)
    ones = jnp.kron(jnp.eye(B_HEADS, dtype=F32), jnp.ones((B_DK, B_DK), F32)).astype(BF16)

    for l in range(depth):
        need_ctx = l < depth - 1
        mod = mod_all[l]
        wr = jnp.concatenate([w_re[l], w_rg[l], jnp.zeros((D_MODEL, ROUTER_W - N_EXPERTS - N_GROUPS), F32)],
                             axis=-1).astype(BF16)
        w13 = jnp.concatenate([w1[l], w3[l]], axis=-1).astype(BF16)
        sink = jnp.repeat(a_sink[l].astype(F32).reshape(A_KV_HEADS, 1, A_GROUP) * LOG2E, A_QBLOCK, axis=-1)

        qa, ka, va, bqig, gates, qc, kc, vc, gl = _projection(
            xs, mod, g_norm1[l][None], _pack_w_in(w_in[l]), _pack_w_uq(w_uq[l]), _pack_w_ukv(w_ukv[l]),
            c_qnorm[l][None], c_kvnorm[l][None], lbp_all[l], taba, tabc, dims)
        oa = _window_gqa(qa, ka, va, sink, dims, need_ctx)
        ohg = _hgrn2_scan(bqig, gates, dims)
        oc = _mla_attention(qc, kc, vc, dims, need_ctx)
        xs, h2 = _merge(xs, oa, ohg, bqig, oc, gl, w_br[l].astype(BF16), w_out[l].astype(BF16), b_onorm[l][None],
                        mod, g_norm2[l][None], ones, dims, need_ctx)
        xs = _moe(h2, xs, mod, wr, w13, w2[l].astype(BF16), g_final[None], dims, need_ctx, final_norm=not need_ctx)

    return xs.reshape(B, S, D_MODEL)
```

```python
import functools

import jax
import jax.numpy as jnp
import numpy as np
from jax import lax
from jax.experimental import pallas as pl
from jax.experimental.pallas import tpu as pltpu

F32 = jnp.float32
BF16 = jnp.bfloat16
HIGHEST = lax.Precision.HIGHEST

D_MODEL = 1024
GRID_W = 64
HEAD_DIM = 64
ROPE_BASE = 10000.0
EPS = 1e-6
A_HEADS = 8
A_KV_HEADS = 2
A_GROUP = A_HEADS // A_KV_HEADS
A_WINDOW = 128
A_BLOCK = A_WINDOW
A_QBLOCK = 2 * A_BLOCK
A_STEP_BLOCKS = 4
B_HEADS = 8
B_DK = 64
B_W = B_HEADS * B_DK
B_CHUNK = 64
B_BLOCK = 256
B_EXP_LIMIT = 80.0
C_HEADS = 8
C_NOPE = 64
C_ROPE = 32
C_V = 64
C_Q_LORA = 256
C_KV_LORA = 128
C_HEAD_PAD = 128
N_BRANCH = 3
BRANCH_W = 512
N_GROUPS = 4
EXPERTS_PER_GROUP = 8
N_EXPERTS = N_GROUPS * EXPERTS_PER_GROUP
D_EXPERT = 256
ROUTER_W = 128
V_SLAB = 80
LOG2E = 1.4426950408889634

OFF_AQ, OFF_AK, OFF_AV = 0, 512, 640
OFF_BQ, OFF_BI, OFF_BZF, OFF_BZB, OFF_BG = 768, 1280, 1792, 2304, 2816
OFF_CQ, OFF_CKV, OFF_CKR, OFF_GL = 3328, 3584, 3712, 3840
IN_W_PACKED = OFF_GL + N_BRANCH * D_MODEL

TOKEN_BLOCK = 256
PROJ_BLOCK = 512
MOE_TOKEN_BLOCK = 1024
MOE_EXPERTS_PER_STEP = 4
MOE_TILE = 288
MOE_ALIGN = 16
MOE_SORT_PAD = 384
MLA_Q_BLOCK = 2048
MLA_K_BLOCK = 1024
MLA_COL_GROUP = 512
MLA_KEY_CHUNK = 256
MLA_LAG_LIMIT = 64.0
VMEM_LIMIT = 56 * 1024 * 1024
PROJ_VMEM_LIMIT = 61 * 1024 * 1024


def _cparams(sem, vmem_limit=VMEM_LIMIT, **kw):
    return pltpu.CompilerParams(dimension_semantics=sem, vmem_limit_bytes=vmem_limit, **kw)


def _mod_kernel(cc_ref, w_ref, b_ref, o_ref):
    cc = cc_ref[...]
    a = cc * jax.nn.sigmoid(cc)
    o_ref[...] = jnp.dot(a, w_ref[...], preferred_element_type=F32, precision=HIGHEST) + b_ref[...]


def _modulation(cc, w_mod, b_mod):
    depth = w_mod.shape[0]
    nj = 6
    return pl.pallas_call(
        _mod_kernel,
        grid=(depth, nj),
        in_specs=[
            pl.BlockSpec((8, D_MODEL), lambda l, j: (0, 0)),
            pl.BlockSpec((None, D_MODEL, D_MODEL), lambda l, j: (l, 0, j)),
            pl.BlockSpec((None, 1, D_MODEL), lambda l, j: (l, 0, j)),
        ],
        out_specs=pl.BlockSpec((None, 8, D_MODEL), lambda l, j: (l, 0, j)),
        out_shape=jax.ShapeDtypeStruct((depth, 8, 6 * D_MODEL), F32),
        compiler_params=_cparams(("arbitrary", "arbitrary")),
        name="adaln_mod",
    )(cc, w_mod, b_mod.reshape(depth, 1, 6 * D_MODEL))


def _rms(x, g):
    return x * lax.rsqrt(jnp.mean(x * x, axis=-1, keepdims=True) + EPS) * g


def _rope(v, tab_ref, half):
    n = v.shape[-1]
    cos = tab_ref[:, 0:128]
    s_lo = tab_ref[:, 128:256]
    s_hi = tab_ref[:, 256:384]
    return v * cos + pltpu.roll(v, n - half, 1) * s_lo + pltpu.roll(v, half, 1) * s_hi


def _store_v_slabs(ref, vt, heads):
    ones = jnp.ones((V_SLAB - HEAD_DIM, vt.shape[1]), BF16)
    for hd in range(heads):
        ref[V_SLAB * hd:V_SLAB * hd + HEAD_DIM, :] = vt[HEAD_DIM * hd:HEAD_DIM * (hd + 1), :].astype(BF16)
        ref[V_SLAB * hd + HEAD_DIM:V_SLAB * (hd + 1), :] = ones


def _proj_kernel(x_ref, mod_ref, g1_ref, w_ref, wuq_ref, wukv_ref, gq_ref, gkv_ref, lbp_ref, taba_ref, tabc_ref,
                 qa_ref, ka_ref, va_ref, bqig_ref, gates_ref, qc_ref, kc_ref, vc_ref, gl_ref):
    x = x_ref[...]
    h = _rms(x, g1_ref[...]) * (1.0 + mod_ref[1:2, :]) + mod_ref[0:1, :]
    hb = h.astype(BF16)

    def seg(off, width):
        return jnp.dot(hb, w_ref[:, off:off + width], preferred_element_type=F32)

    aq = seg(OFF_AQ, 512) * (HEAD_DIM ** -0.5 * LOG2E)
    for j in range(4):
        qa_ref[128 * j:128 * (j + 1), :] = _rope(aq[:, 128 * j:128 * (j + 1)], taba_ref, 32).T.astype(BF16)
    ka_ref[...] = _rope(seg(OFF_AK, 128), taba_ref, 32).astype(BF16)
    _store_v_slabs(va_ref, seg(OFF_AV, 128).T, A_KV_HEADS)

    bqig_ref[:, 0:512] = seg(OFF_BQ, 512).astype(BF16)
    bqig_ref[:, 512:1024] = seg(OFF_BI, 512).astype(BF16)
    bqig_ref[:, 1024:1536] = seg(OFF_BG, 512).astype(BF16)
    for d, off in enumerate((OFF_BZF, OFF_BZB)):
        z = seg(off, 512)
        log_lb = lbp_ref[0:1, 512 * d:512 * (d + 1)]
        log1m_lb = lbp_ref[1:2, 512 * d:512 * (d + 1)]
        one_m_lb = lbp_ref[2:3, 512 * d:512 * (d + 1)]
        e = jnp.exp(-jnp.abs(z))
        log_sig = jnp.minimum(z, 0.0) - jnp.log(1.0 + e)
        b = log1m_lb + log_sig
        mx = jnp.maximum(log_lb, b)
        logf = mx + jnp.log(1.0 + jnp.exp(-jnp.abs(log_lb - b)))
        r = 1.0 / (1.0 + e)
        key = one_m_lb * jnp.where(z >= 0.0, e * r, r)
        gates_ref[:, 512 * d:512 * (d + 1)] = logf
        gates_ref[:, 1024 + 512 * d:1024 + 512 * (d + 1)] = key

    cq = _rms(seg(OFF_CQ, C_Q_LORA), gq_ref[...]).astype(BF16)
    qh = jnp.dot(cq, wuq_ref[...], preferred_element_type=F32) * ((C_NOPE + C_ROPE) ** -0.5 * LOG2E)
    ckv = _rms(seg(OFF_CKV, C_KV_LORA), gkv_ref[...]).astype(BF16)
    kvh = jnp.dot(ckv, wukv_ref[...], preferred_element_type=F32)
    kr = _rope(seg(OFF_CKR, 128), tabc_ref, 16)
    for j in range(C_HEADS):
        sl = slice(C_HEAD_PAD * j, C_HEAD_PAD * (j + 1))
        qc_ref[sl, :] = _rope(qh[:, sl], tabc_ref, 16).T.astype(BF16)
        kc_ref[:, sl] = (kvh[:, sl] + kr).astype(BF16)
    _store_v_slabs(vc_ref, kvh[:, C_HEADS * C_HEAD_PAD:].T, C_HEADS)

    for j in range(6):
        gl_ref[:, 512 * j:512 * (j + 1)] = seg(OFF_GL + 512 * j, 512).astype(BF16)


def _projection(x, mod, g1, w_in_p, wuq_p, wukv_p, gq, gkv, lbp, taba, tabc, dims):
    B, S, L = dims
    n = x.shape[0]
    tm = PROJ_BLOCK
    nlat = B * S // tm
    spb = S // tm

    def row(i):
        return (i, 0)

    def mod_row(i):
        return (jnp.minimum(i // spb, B), 0, 0)

    def tab_row(i):
        return (jnp.where(i < nlat, i % spb, spb), 0)

    const = lambda i: (0, 0)
    resident = functools.partial(pl.BlockSpec, index_map=const, pipeline_mode=pl.Buffered(1))
    widths = (512, 128, A_KV_HEADS * V_SLAB, 1536, 2048, 1024, 1024, C_HEADS * V_SLAB, 3072)
    dtypes = (BF16, BF16, BF16, BF16, F32, BF16, BF16, BF16, BF16)
    transposed = (0, 2, 5, 7)
    return pl.pallas_call(
        _proj_kernel,
        grid=(n // tm,),
        in_specs=[
            pl.BlockSpec((tm, D_MODEL), row),
            pl.BlockSpec((None, 6, D_MODEL), mod_row),
            pl.BlockSpec((1, D_MODEL), const),
            resident((D_MODEL, IN_W_PACKED)),
            resident(wuq_p.shape),
            resident(wukv_p.shape),
            pl.BlockSpec((1, C_Q_LORA), const),
            pl.BlockSpec((1, C_KV_LORA), const),
            pl.BlockSpec((8, 2 * B_W), const),
            pl.BlockSpec((tm, 384), tab_row),
            pl.BlockSpec((tm, 384), tab_row),
        ],
        out_specs=[pl.BlockSpec((w, tm), lambda i: (0, i)) if k in transposed else pl.BlockSpec((tm, w), row)
                   for k, w in enumerate(widths)],
        out_shape=[jax.ShapeDtypeStruct((w, n) if k in transposed else (n, w), dt)
                   for k, (w, dt) in enumerate(zip(widths, dtypes))],
        compiler_params=_cparams(("parallel",), PROJ_VMEM_LIMIT),
        name="in_proj",
    )(x, mod, g1, w_in_p, wuq_p, wukv_p, gq, gkv, lbp, taba, tabc)


def _gqa_ctx_kernel(qt_ref, kx_ref, vx_ref, sink_ref, _, o_ref):
    for g in range(A_KV_HEADS):
        qg = jnp.concatenate([qt_ref[HEAD_DIM * hd:HEAD_DIM * (hd + 1), :]
                              for hd in range(A_GROUP * g, A_GROUP * (g + 1))], axis=1)
        s = jnp.dot(kx_ref[:, HEAD_DIM * g:HEAD_DIM * (g + 1)], qg, preferred_element_type=F32)
        m = jnp.maximum(jnp.max(s, axis=0, keepdims=True), sink_ref[g])
        p = jnp.exp2(s - m).astype(BF16)
        pv = jnp.dot(vx_ref[V_SLAB * g:V_SLAB * (g + 1), :], p, preferred_element_type=F32)
        o = pv[0:HEAD_DIM, :] / (pv[HEAD_DIM:HEAD_DIM + 1, :] + jnp.exp2(sink_ref[g] - m))
        for hh in range(A_GROUP):
            hd = A_GROUP * g + hh
            o_ref[HEAD_DIM * hd:HEAD_DIM * (hd + 1), :] = o[:, A_QBLOCK * hh:A_QBLOCK * (hh + 1)].astype(BF16)


def _gqa_lat_kernel(qt_ref, kp_ref, kc_ref, kn_ref, kx_ref, vp_ref, vc_ref, vn_ref, vx_ref, sink_ref, o_ref,
                    bias_ref, s0_ref, s1_ref, p0_ref, p1_ref, *, seq):
    j = pl.program_id(1)
    nband = 4 * A_BLOCK
    nsb, nk = bias_ref.shape[0], bias_ref.shape[1]
    step_q = nsb * A_QBLOCK
    rows = lax.broadcasted_iota(jnp.int32, (nk, A_QBLOCK), 0)
    cols = lax.broadcasted_iota(jnp.int32, (nk, A_QBLOCK), 1)
    for qb in range(nsb):
        qpos = j * step_q + qb * A_QBLOCK + cols
        kpos = j * step_q + qb * A_QBLOCK - A_BLOCK + rows
        valid = ((kpos >= 0) & (kpos < seq) & (jnp.abs(qpos - kpos) <= A_WINDOW)) | (rows >= nband)
        bias_ref[qb] = jnp.where(valid, 0.0, -jnp.inf)
    k = jnp.concatenate([kp_ref[...], kc_ref[...], kn_ref[...]], axis=0)
    vt = jnp.concatenate([vp_ref[...], vc_ref[...], vn_ref[...]], axis=1)
    s_refs, p_refs = (s0_ref, s1_ref), (p0_ref, p1_ref)
    nchunk = nk // A_QBLOCK
    krows = lambda r: slice(A_QBLOCK * r, A_QBLOCK * (r + 1))
    items = [(qb, g) for qb in range(nsb) for g in range(A_KV_HEADS)]
    n = len(items)
    smax, m, pv = [None] * n, [None] * n, [None] * n

    def keys(i, r):
        qb, g = items[i]
        if A_QBLOCK * (r + 1) <= nband:
            sl = slice(A_QBLOCK * (qb + r), A_QBLOCK * (qb + r + 1))
            return k[sl, HEAD_DIM * g:HEAD_DIM * (g + 1)], vt[V_SLAB * g:V_SLAB * (g + 1), sl]
        return kx_ref[:, HEAD_DIM * g:HEAD_DIM * (g + 1)], vx_ref[V_SLAB * g:V_SLAB * (g + 1), :]

    def stage_scores(i, r):
        qb, g = items[i]
        qcols = slice(A_QBLOCK * qb, A_QBLOCK * (qb + 1))
        qg = jnp.concatenate([qt_ref[HEAD_DIM * hd:HEAD_DIM * (hd + 1), qcols]
                              for hd in range(A_GROUP * g, A_GROUP * (g + 1))], axis=1)
        bias = bias_ref[qb, krows(r), :]
        sc = (jnp.dot(keys(i, r)[0], qg, preferred_element_type=F32) + jnp.concatenate([bias] * A_GROUP, axis=1))
        s_refs[i % 2][krows(r), :] = sc
        cm = jnp.max(sc, axis=0, keepdims=True)
        smax[i] = cm if smax[i] is None else jnp.maximum(smax[i], cm)

    def stage_exp(i, r):
        p_refs[i % 2][krows(r), :] = jnp.exp2(s_refs[i % 2][krows(r), :] - m[i]).astype(BF16)

    def stage_pv(i, r):
        qb, g = items[i]
        t = jnp.dot(keys(i, r)[1], p_refs[i % 2][krows(r), :], preferred_element_type=F32)
        pv[i] = t if pv[i] is None else pv[i] + t
        if r == nchunk - 1:
            denom = pv[i][HEAD_DIM:HEAD_DIM + 1, :] + jnp.exp2(sink_ref[g] - m[i])
            o = pv[i][0:HEAD_DIM, :] / denom
            for hh in range(A_GROUP):
                hd = A_GROUP * g + hh
                o_ref[HEAD_DIM * hd:HEAD_DIM * (hd + 1), A_QBLOCK * qb:A_QBLOCK * (qb + 1)] = (
                    o[:, A_QBLOCK * hh:A_QBLOCK * (hh + 1)].astype(BF16))

    for r in range(nchunk):
        stage_scores(0, r)
    for i in range(n + 1):
        if i < n:
            m[i] = jnp.maximum(smax[i], sink_ref[items[i][1]])
        for r in range(nchunk):
            if i + 1 < n:
                stage_scores(i + 1, r)
            if i < n:
                stage_exp(i, r)
            if i >= 1:
                stage_pv(i - 1, r)


def _window_gqa(qat, ka, vat, sink, dims, need_ctx):
    B, S, L = dims
    n = ka.shape[0]
    assert L == A_QBLOCK
    nb = S // A_BLOCK
    nqb = S // A_QBLOCK
    nk = 4 * A_BLOCK + L
    width = A_GROUP * A_QBLOCK
    vs = A_KV_HEADS * V_SLAB
    step_q = min(A_STEP_BLOCKS * A_QBLOCK, S)
    nstep = S // step_q
    bps = step_q // A_BLOCK
    ctx_row = lambda b, j: (B * S // L + b, 0)
    ctx_col = lambda b, j: (0, B * S // L + b)
    sink_spec = pl.BlockSpec((A_KV_HEADS, 1, width), lambda b, j: (0, 0, 0))
    stage = [pltpu.VMEM((nk, width), F32), pltpu.VMEM((nk, width), F32),
             pltpu.VMEM((nk, width), BF16), pltpu.VMEM((nk, width), BF16)]
    prev_blk = lambda b, j: b * nb + jnp.maximum(bps * j - 1, 0)
    next_blk = lambda b, j: b * nb + jnp.minimum(bps * (j + 1), nb - 1)
    o_lat = pl.pallas_call(
        functools.partial(_gqa_lat_kernel, seq=S),
        grid=(B, nstep),
        in_specs=[
            pl.BlockSpec((512, step_q), lambda b, j: (0, b * nstep + j)),
            pl.BlockSpec((A_BLOCK, 128), lambda b, j: (prev_blk(b, j), 0)),
            pl.BlockSpec((step_q, 128), lambda b, j: (b * nstep + j, 0)),
            pl.BlockSpec((A_BLOCK, 128), lambda b, j: (next_blk(b, j), 0)),
            pl.BlockSpec((L, 128), ctx_row),
            pl.BlockSpec((vs, A_BLOCK), lambda b, j: (0, prev_blk(b, j))),
            pl.BlockSpec((vs, step_q), lambda b, j: (0, b * nstep + j)),
            pl.BlockSpec((vs, A_BLOCK), lambda b, j: (0, next_blk(b, j))),
            pl.BlockSpec((vs, L), ctx_col),
            sink_spec,
        ],
        out_specs=pl.BlockSpec((512, step_q), lambda b, j: (0, b * nstep + j)),
        out_shape=jax.ShapeDtypeStruct((512, n if need_ctx else B * S), BF16),
        scratch_shapes=[pltpu.VMEM((step_q // A_QBLOCK, nk, A_QBLOCK), F32)] + stage,
        compiler_params=_cparams(("parallel", "parallel")),
        name="window_gqa",
    )(qat, ka, ka, ka, ka, vat, vat, vat, vat, sink)
    if not need_ctx:
        return o_lat
    return pl.pallas_call(
        _gqa_ctx_kernel,
        grid=(B,),
        in_specs=[pl.BlockSpec((512, L), lambda b: (0, B * S // L + b)),
                  pl.BlockSpec((L, 128), lambda b: (B * S // L + b, 0)),
                  pl.BlockSpec((vs, L), lambda b: (0, B * S // L + b)),
                  pl.BlockSpec((A_KV_HEADS, 1, width), lambda b: (0, 0, 0)),
                  pl.BlockSpec(memory_space=pl.ANY)],
        out_specs=pl.BlockSpec((512, L), lambda b: (0, B * S // L + b)),
        out_shape=jax.ShapeDtypeStruct((512, n), BF16),
        input_output_aliases={4: 0},
        compiler_params=_cparams(("parallel",)),
        name="window_gqa_context",
    )(qat, ka, vat, sink, o_lat)


def _hgrn_scan(g_ref, r0, reverse):
    C = B_CHUNK
    rows = lax.broadcasted_iota(jnp.int32, (C, C), 0)
    cols = lax.broadcasted_iota(jnp.int32, (C, C), 1)
    causal = (rows <= cols) if reverse else (rows >= cols)
    g = g_ref[r0:r0 + C, :]
    bc = jnp.dot(causal.astype(F32), g, preferred_element_type=F32, precision=HIGHEST)
    tot = jnp.sum(g, axis=0, keepdims=True)
    mid = C // 2 if reverse else C // 2 - 1
    rho = bc[mid:mid + 1, :]
    return causal, bc, tot, rho


def _hgrn_prep(q_ref, v_ref, k_ref, r0, scan, factored):
    causal, bc, tot, rho = scan
    rs = slice(r0, r0 + B_CHUNK)
    q = q_ref[rs, :].astype(F32)
    key = k_ref[rs, :]
    v = v_ref[rs, :]
    qe = (q * jnp.exp(bc - rho)).astype(BF16) if factored else None
    ke = (key * jnp.exp(rho - bc)).astype(BF16) if factored else None
    qs = (q * jnp.exp(bc)).astype(BF16)
    ks = (key * jnp.exp(tot - bc)).astype(BF16)
    dec = jnp.exp(tot)
    return rs, causal, v, qe, ke, qs, ks, dec


def _hgrn_intra_exact(q_ref, v_ref, k_ref, r0, scan, reverse, ones_ref, bcs_ref, vs_ref):
    C = B_CHUNK
    causal, bc, tot, rho = scan
    q = q_ref[r0:r0 + C, :].astype(F32)
    bcs_ref[...] = bc
    vs_ref[...] = v_ref[r0:r0 + C, :].astype(F32)
    trow = lax.broadcasted_iota(jnp.int32, (C, 1), 0)

    def body(s, acc):
        later = (trow <= s) if reverse else (trow >= s)
        w = jnp.where(later, jnp.exp(jnp.minimum(bc - bcs_ref[pl.ds(s, 1), :], 0.0)), 0.0)
        att = _group_sum(q * w * k_ref[pl.ds(r0 + s, 1), :], ones_ref)
        return acc + att * vs_ref[pl.ds(s, 1), :]

    return lax.fori_loop(0, C, body, jnp.zeros((C, B_W), F32))


def _hgrn_heads(prep, o_ref, st_ref, intra=None):
    rs, causal, v, qe, ke, qs, ks, dec = prep
    for hd in range(B_HEADS):
        sl = slice(B_DK * hd, B_DK * (hd + 1))
        st = st_ref[hd]
        o = lax.dot_general(qs[:, sl], st.astype(BF16), (((1,), (1,)), ((), ())), preferred_element_type=F32)
        if intra is None:
            att = lax.dot_general(qe[:, sl], ke[:, sl], (((1,), (1,)), ((), ())), preferred_element_type=F32)
            att = jnp.where(causal, att, 0.0).astype(BF16)
            o = o + jnp.dot(att, v[:, sl], preferred_element_type=F32)
        else:
            o = o + intra[:, sl]
        o_ref[rs, sl] = o
        upd = lax.dot_general(v[:, sl], ks[:, sl], (((0,), (0,)), ((), ())), preferred_element_type=F32)
        st_ref[hd] = st * dec[:, sl] + upd


def _hgrn_kernel(qf_ref, vf_ref, gf_ref, kf_ref, qb_ref, vb_ref, gb_ref, kb_ref, ones_ref, of_ref, ob_ref,
                 stf_ref, stb_ref, bcs_ref, vs_ref):
    @pl.when(pl.program_id(1) == 0)
    def _():
        stf_ref[...] = jnp.zeros_like(stf_ref)
        stb_ref[...] = jnp.zeros_like(stb_ref)

    nchunk = qf_ref.shape[0] // B_CHUNK
    chunks = []
    for i in range(nchunk):
        chunks.append((qf_ref, vf_ref, gf_ref, kf_ref, of_ref, stf_ref, B_CHUNK * i, False))
        chunks.append((qb_ref, vb_ref, gb_ref, kb_ref, ob_ref, stb_ref, B_CHUNK * (nchunk - 1 - i), True))
    scans = [_hgrn_scan(g_ref, r0, rev) for (_, _, g_ref, _, _, _, r0, rev) in chunks]
    spread = jnp.abs(scans[0][1] - scans[0][3])
    for _, bc, _, rho in scans[1:]:
        spread = jnp.maximum(spread, jnp.abs(bc - rho))
    factorable = jnp.max(spread) <= B_EXP_LIMIT

    @pl.when(factorable)
    def _():
        preps = [_hgrn_prep(q_ref, v_ref, k_ref, r0, scan, True)
                 for (q_ref, v_ref, _, k_ref, _, _, r0, _), scan in zip(chunks, scans)]
        for prep, (_, _, _, _, o_ref, st_ref, _, _) in zip(preps, chunks):
            _hgrn_heads(prep, o_ref, st_ref)

    @pl.when(jnp.logical_not(factorable))
    def _():
        for (q_ref, v_ref, _, k_ref, o_ref, st_ref, r0, rev), scan in zip(chunks, scans):
            intra = _hgrn_intra_exact(q_ref, v_ref, k_ref, r0, scan, rev, ones_ref, bcs_ref, vs_ref)
            _hgrn_heads(_hgrn_prep(q_ref, v_ref, k_ref, r0, scan, False), o_ref, st_ref, intra)


def _hgrn2_scan(bqig, gates, ones, dims):
    B, S, L = dims
    n = bqig.shape[0]
    T = B_BLOCK
    assert L == T
    ns = S // T

    def fwd_blk(b, c):
        return jnp.where(c == 0, B * ns + b, b * ns + c - 1)

    def bwd_blk(b, c):
        return jnp.where(c == 0, B * ns + b, b * ns + ns - c)

    def specs(blk, d):
        return [pl.BlockSpec((T, B_W), lambda b, c: (blk(b, c), 0)),
                pl.BlockSpec((T, B_W), lambda b, c: (blk(b, c), 1)),
                pl.BlockSpec((T, B_W), lambda b, c: (blk(b, c), d)),
                pl.BlockSpec((T, B_W), lambda b, c: (blk(b, c), 2 + d))]

    return pl.pallas_call(
        _hgrn_kernel,
        grid=(B, ns + 1),
        in_specs=specs(fwd_blk, 0) + specs(bwd_blk, 1) + [pl.BlockSpec((B_W, B_W), lambda b, c: (0, 0))],
        out_specs=[pl.BlockSpec((T, B_W), lambda b, c: (fwd_blk(b, c), 0)),
                   pl.BlockSpec((T, B_W), lambda b, c: (bwd_blk(b, c), 0))],
        out_shape=[jax.ShapeDtypeStruct((n, B_W), F32), jax.ShapeDtypeStruct((n, B_W), F32)],
        scratch_shapes=[pltpu.VMEM((B_HEADS, B_DK, B_DK), F32), pltpu.VMEM((B_HEADS, B_DK, B_DK), F32),
                        pltpu.VMEM((B_CHUNK, B_W), F32), pltpu.VMEM((B_CHUNK, B_W), F32)],
        compiler_params=_cparams(("parallel", "arbitrary")),
        name="hgrn2_scan",
    )(bqig, bqig, gates, gates, bqig, bqig, gates, gates, ones)


def _mla_kernel(qt_ref, k_ref, vt_ref, *rest, with_ctx):
    if with_ctx:
        kx_ref, vxt_ref, o_ref, m_ref, acc_ref, *bufs = rest
    else:
        _, o_ref, m_ref, acc_ref, *bufs = rest
    s_refs, p_refs = bufs[0:2], bufs[2:4]
    kstep = pl.program_id(2)
    tq = qt_ref.shape[1]

    def kv_pass(k_ref, vt_ref, first):
        nkeys = k_ref.shape[0]
        nchunk = nkeys // MLA_KEY_CHUNK
        krows = lambda r: slice(MLA_KEY_CHUNK * r, MLA_KEY_CHUNK * (r + 1))

        group = s_refs[0].shape[1]
        items = [(c, hd) for c in range(tq // group) for hd in range(C_HEADS)]
        n = len(items)
        smax = [None] * n
        m_new = [None] * n
        alpha = [None] * n
        pv = [None] * n
        cols = lambda i: slice(group * items[i][0], group * (items[i][0] + 1))

        def stage_scores(i, r):
            hd = items[i][1]
            sl = slice(C_HEAD_PAD * hd, C_HEAD_PAD * (hd + 1))
            sc = jnp.dot(k_ref[krows(r), sl], qt_ref[sl, cols(i)], preferred_element_type=F32)
            s_refs[i % 2][krows(r), :] = sc
            cm = jnp.max(sc, axis=0, keepdims=True)
            smax[i] = cm if smax[i] is None else jnp.maximum(smax[i], cm)

        def stage_stats(i):
            hd = items[i][1]
            if first:
                m_new[i] = smax[i]
            else:
                m_old = m_ref[hd, :, cols(i)]
                m_new[i] = jnp.maximum(m_old, smax[i])
                alpha[i] = jnp.exp2(m_old - m_new[i])
            m_ref[hd, :, cols(i)] = m_new[i]

        def stage_exp(i, r):
            p_refs[i % 2][krows(r), :] = jnp.exp2(s_refs[i % 2][krows(r), :] - m_new[i]).astype(BF16)

        def stage_pv(i, r):
            hd = items[i][1]
            t = jnp.dot(vt_ref[V_SLAB * hd:V_SLAB * (hd + 1), krows(r)], p_refs[i % 2][krows(r), :],
                        preferred_element_type=F32)
            pv[i] = t if pv[i] is None else pv[i] + t
            if r == nchunk - 1:
                acc_ref[hd, :, cols(i)] = pv[i] if first else alpha[i] * acc_ref[hd, :, cols(i)] + pv[i]

        for r in range(nchunk):
            stage_scores(0, r)
        for i in range(n + 1):
            if i < n:
                stage_stats(i)
            for r in range(nchunk):
                if i + 1 < n:
                    stage_scores(i + 1, r)
                if i < n:
                    stage_exp(i, r)
                if i >= 1:
                    stage_pv(i - 1, r)

    def kv_pass_lagged(k_ref, vt_ref, tmp_ref):
        nkeys = k_ref.shape[0]
        nchunk = nkeys // MLA_KEY_CHUNK
        krows = lambda r: slice(MLA_KEY_CHUNK * r, MLA_KEY_CHUNK * (r + 1))
        group = p_refs[0].shape[1]
        items = [(c, hd) for c in range(tq // group) for hd in range(C_HEADS)]
        n = len(items)
        cols = lambda i: slice(group * items[i][0], group * (items[i][0] + 1))
        base, smax, pv = [None] * n, [None] * n, [None] * n

        def stage_scores(i, r):
            hd = items[i][1]
            sl = slice(C_HEAD_PAD * hd, C_HEAD_PAD * (hd + 1))
            if base[i] is None:
                base[i] = m_ref[hd, :, cols(i)]
            sc = jnp.dot(k_ref[krows(r), sl], qt_ref[sl, cols(i)], preferred_element_type=F32)
            p_refs[i % 2][krows(r), :] = jnp.exp2(sc - base[i]).astype(BF16)
            cm = jnp.max(sc, axis=0, keepdims=True)
            smax[i] = cm if smax[i] is None else jnp.maximum(smax[i], cm)

        def stage_pv(i, r):
            hd = items[i][1]
            t = jnp.dot(vt_ref[V_SLAB * hd:V_SLAB * (hd + 1), krows(r)], p_refs[i % 2][krows(r), :],
                        preferred_element_type=F32)
            pv[i] = t if pv[i] is None else pv[i] + t
            if r == nchunk - 1:
                tmp_ref[hd, :, cols(i)] = pv[i]

        for r in range(nchunk):
            stage_scores(0, r)
        for i in range(n):
            for r in range(nchunk):
                if i + 1 < n:
                    stage_scores(i + 1, r)
                stage_pv(i, r)
        excess = smax[0] - base[0]
        for i in range(1, n):
            excess = jnp.maximum(excess, smax[i] - base[i])
        return jnp.max(excess)

    if with_ctx:
        @pl.when(kstep == 0)
        def _():
            kv_pass(kx_ref, vxt_ref, True)

        tmp_ref = bufs[4]
        safe = kv_pass_lagged(k_ref, vt_ref, tmp_ref) <= MLA_LAG_LIMIT

        @pl.when(safe)
        def _():
            for hd in range(C_HEADS):
                acc_ref[hd] = acc_ref[hd] + tmp_ref[hd]

        @pl.when(jnp.logical_not(safe))
        def _():
            kv_pass(k_ref, vt_ref, False)
    else:
        kv_pass(k_ref, vt_ref, True)

    @pl.when(kstep == pl.num_programs(2) - 1)
    def _():
        for hd in range(C_HEADS):
            o_ref[C_V * hd:C_V * (hd + 1), :] = (acc_ref[hd, 0:C_V, :] / acc_ref[hd, C_V:C_V + 1, :]).astype(BF16)


def _mla_attention(qct, kc, vct, dims, need_ctx):
    B, S, L = dims
    n = kc.shape[0]
    tq = min(MLA_Q_BLOCK, S)
    tk = min(MLA_K_BLOCK, S)
    nq, nk = S // tq, S // tk
    hw = C_HEADS * C_HEAD_PAD
    vw = C_HEADS * C_V
    vs = C_HEADS * V_SLAB
    scratch = lambda t, nkeys: [
        pltpu.VMEM((C_HEADS, 1, t), F32), pltpu.VMEM((C_HEADS, V_SLAB, t), F32),
        pltpu.VMEM((nkeys, min(t, MLA_COL_GROUP)), F32), pltpu.VMEM((nkeys, min(t, MLA_COL_GROUP)), F32),
        pltpu.VMEM((nkeys, min(t, MLA_COL_GROUP)), BF16), pltpu.VMEM((nkeys, min(t, MLA_COL_GROUP)), BF16)]
    ctx_row = lambda b, i, k: (B * S // L + b, 0)
    ctx_col = lambda b, i, k: (0, B * S // L + b)
    o_lat = pl.pallas_call(
        functools.partial(_mla_kernel, with_ctx=True),
        grid=(B, nq, nk),
        in_specs=[
            pl.BlockSpec((hw, tq), lambda b, i, k: (0, b * nq + i)),
            pl.BlockSpec((tk, hw), lambda b, i, k: (b * nk + k, 0)),
            pl.BlockSpec((vs, tk), lambda b, i, k: (0, b * nk + k)),
            pl.BlockSpec((L, hw), ctx_row),
            pl.BlockSpec((vs, L), ctx_col),
        ],
        out_specs=pl.BlockSpec((vw, tq), lambda b, i, k: (0, b * nq + i)),
        out_shape=jax.ShapeDtypeStruct((vw, n if need_ctx else B * S), BF16),
        scratch_shapes=scratch(tq, tk) + [pltpu.VMEM((C_HEADS, V_SLAB, tq), F32)],
        compiler_params=_cparams(("parallel", "parallel", "arbitrary")),
        name="mla_latent",
    )(qct, kc, vct, kc, vct)
    if not need_ctx:
        return o_lat
    return pl.pallas_call(
        functools.partial(_mla_kernel, with_ctx=False),
        grid=(B, 1, 1),
        in_specs=[
            pl.BlockSpec((hw, L), ctx_col),
            pl.BlockSpec((L, hw), ctx_row),
            pl.BlockSpec((vs, L), ctx_col),
            pl.BlockSpec(memory_space=pl.ANY),
        ],
        out_specs=pl.BlockSpec((vw, L), ctx_col),
        out_shape=jax.ShapeDtypeStruct((vw, n), BF16),
        scratch_shapes=scratch(L, L),
        input_output_aliases={3: 0},
        compiler_params=_cparams(("parallel", "arbitrary", "arbitrary")),
        name="mla_context",
    )(qct, kc, vct, o_lat)


def _group_sum(x, ones_ref):
    hi = x.astype(BF16)
    lo = (x - hi.astype(F32)).astype(BF16)
    return (jnp.dot(hi, ones_ref[...], preferred_element_type=F32)
            + jnp.dot(lo, ones_ref[...], preferred_element_type=F32))


def _merge_kernel(x_ref, oa_ref, of_ref, ob_ref, bg_ref, oc_ref, gl_ref, wbr_ref, wout_ref, gn_ref, mod_ref,
                  g2_ref, ones_ref, xo_ref, h2_ref):
    ob = of_ref[...] + ob_ref[...]
    ms = _group_sum(ob * ob, ones_ref) * (1.0 / B_DK)
    obn = ob * lax.rsqrt(ms + EPS) * gn_ref[...]
    bg = bg_ref[...].astype(F32)
    bb = (obn * (bg * jax.nn.sigmoid(bg))).astype(BF16)
    branches = ((oa_ref[...], 0), (bb, 1), (oc_ref[...], 0))
    y = None
    for nbr, (br, axis) in enumerate(branches):
        gate = jax.nn.sigmoid(gl_ref[:, D_MODEL * nbr:D_MODEL * (nbr + 1)].astype(F32))
        t = gate * lax.dot_general(br, wbr_ref[nbr], (((axis,), (0,)), ((), ())), preferred_element_type=F32)
        y = t if y is None else y + t
    upd = jnp.dot(y.astype(BF16), wout_ref[...], preferred_element_type=F32)
    xn = x_ref[...] + mod_ref[2:3, :] * upd
    xo_ref[...] = xn
    h2 = _rms(xn, g2_ref[...]) * (1.0 + mod_ref[4:5, :]) + mod_ref[3:4, :]
    h2_ref[...] = h2.astype(BF16)


def _merge(x, oa, ohg, bqig, oc, gl, wbr, wout, gn, mod, g2, ones, dims, need_ctx):
    B, S, L = dims
    n = x.shape[0]
    tm = PROJ_BLOCK
    spb = S // tm
    nblk = (n if need_ctx else B * S) // tm
    row = lambda i: (i, 0)
    const2 = lambda i: (0, 0)
    return pl.pallas_call(
        _merge_kernel,
        grid=(nblk,),
        in_specs=[
            pl.BlockSpec((tm, D_MODEL), row),
            pl.BlockSpec((512, tm), lambda i: (0, i)),
            pl.BlockSpec((tm, B_W), row),
            pl.BlockSpec((tm, B_W), row),
            pl.BlockSpec((tm, B_W), lambda i: (i, 2)),
            pl.BlockSpec((512, tm), lambda i: (0, i)),
            pl.BlockSpec((tm, N_BRANCH * D_MODEL), row),
            pl.BlockSpec((N_BRANCH, BRANCH_W, D_MODEL), lambda i: (0, 0, 0)),
            pl.BlockSpec((D_MODEL, D_MODEL), const2),
            pl.BlockSpec((1, B_W), const2),
            pl.BlockSpec((None, 6, D_MODEL), lambda i: (jnp.minimum(i // spb, B), 0, 0)),
            pl.BlockSpec((1, D_MODEL), const2),
            pl.BlockSpec((B_W, B_W), const2),
        ],
        out_specs=[pl.BlockSpec((tm, D_MODEL), row), pl.BlockSpec((tm, D_MODEL), row)],
        out_shape=[jax.ShapeDtypeStruct((nblk * tm, D_MODEL), F32), jax.ShapeDtypeStruct((nblk * tm, D_MODEL), BF16)],
        compiler_params=_cparams(("parallel",)),
        name="branch_merge",
    )(x, oa, ohg[0], ohg[1], bqig, oc, gl, wbr, wout, gn, mod, g2, ones)


def _route_kernel(h_ref, wr_ref, comb_ref, pos_ref, cnt_ref):
    h = h_ref[...]
    T = h.shape[0]
    lane = lax.broadcasted_iota(jnp.int32, (T, ROUTER_W), 1)
    logits = jnp.dot(h, wr_ref[...], preferred_element_type=F32)
    big = jnp.int32(ROUTER_W)
    is_grp = (lane >= N_EXPERTS) & (lane < N_EXPERTS + N_GROUPS)
    gl = jnp.where(is_grp, logits, -jnp.inf)
    gmax = jnp.max(gl, axis=-1, keepdims=True)
    gsel = jnp.min(jnp.where(gl == gmax, lane, big), axis=-1, keepdims=True) - N_EXPERTS
    gw = 1.0 / jnp.sum(jnp.exp(gl - gmax), axis=-1, keepdims=True)
    in_grp = (lane >= gsel * EXPERTS_PER_GROUP) & (lane < (gsel + 1) * EXPERTS_PER_GROUP)
    el = jnp.where(in_grp, logits, -jnp.inf)
    m1 = jnp.max(el, axis=-1, keepdims=True)
    i1 = jnp.min(jnp.where(el == m1, lane, big), axis=-1, keepdims=True)
    el2 = jnp.where(lane == i1, -jnp.inf, el)
    m2 = jnp.max(el2, axis=-1, keepdims=True)
    i2 = jnp.min(jnp.where(el2 == m2, lane, big), axis=-1, keepdims=True)
    e2 = jnp.exp(m2 - m1)
    w1 = gw / (1.0 + e2)
    w2 = gw * e2 / (1.0 + e2)
    comb_ref[...] = jnp.where(lane == i1, w1, 0.0) + jnp.where(lane == i2, w2, 0.0)

    onehot = lane == gsel
    ones = jnp.where(onehot, 1.0, 0.0)
    rows = lax.broadcasted_iota(jnp.int32, (T, T), 0)
    cols = lax.broadcasted_iota(jnp.int32, (T, T), 1)
    earlier = jnp.where(rows > cols, 1.0, 0.0).astype(BF16)
    before = jnp.dot(earlier, ones.astype(BF16), preferred_element_type=F32)
    rank = jnp.sum(jnp.where(onehot, before, 0.0), axis=-1, keepdims=True)
    cnt = jnp.sum(ones, axis=0, keepdims=True)
    padded = jnp.floor((cnt + (MOE_ALIGN - 1)) * (1.0 / MOE_ALIGN)) * MOE_ALIGN
    seg = [jnp.sum(jnp.where(lane[0:1] == g, padded, 0.0), axis=-1, keepdims=True) for g in range(N_GROUPS - 1)]
    start = jnp.where(gsel == 0, 0.0, jnp.where(gsel == 1, seg[0], jnp.where(gsel == 2, seg[0] + seg[1],
                                                                             seg[0] + seg[1] + seg[2])))
    pos_ref[...] = jnp.broadcast_to(start + rank, (T, ROUTER_W))
    cnt_ref[...] = jnp.broadcast_to(cnt, (8, ROUTER_W)).astype(jnp.int32)


def _moe_kernel(cnt_ref, h_ref, x_ref, mod_ref, pos_ref, comb_ref, w13_ref, w2_ref, gf_ref, o_ref,
                pt_ref, xs_ref, cs_ref, ys_ref, *, final_norm):
    i = pl.program_id(0)
    e = pl.program_id(1)
    T = h_ref.shape[0]
    R = xs_ref.shape[0]
    gather = lambda a: lax.dot_general(pt_ref[...], a, (((0,), (0,)), ((), ())), preferred_element_type=F32)

    @pl.when(e == 0)
    def _():
        slot = lax.broadcasted_iota(jnp.int32, (T, R), 1).astype(F32)
        pt_ref[...] = jnp.where(pos_ref[:, 0:1] == slot, 1.0, 0.0).astype(BF16)
        xs_ref[...] = gather(h_ref[...]).astype(BF16)
        comb = comb_ref[...]
        hi = comb.astype(BF16)
        r1 = comb - hi.astype(F32)
        mid = r1.astype(BF16)
        lo = (r1 - mid.astype(F32)).astype(BF16)
        cs_ref[...] = gather(hi) + gather(mid) + gather(lo)
        ys_ref[...] = jnp.zeros_like(ys_ref)

    g = (e * MOE_EXPERTS_PER_STEP) // EXPERTS_PER_GROUP
    cnt = [cnt_ref[i * N_GROUPS + gg] for gg in range(N_GROUPS)]
    seg = [(c + (MOE_ALIGN - 1)) // MOE_ALIGN * MOE_ALIGN for c in cnt]
    start = (jnp.where(g > 0, seg[0], 0) + jnp.where(g > 1, seg[1], 0) + jnp.where(g > 2, seg[2], 0))
    cnt_g = jnp.where(g == 0, cnt[0], jnp.where(g == 1, cnt[1], jnp.where(g == 2, cnt[2], cnt[3])))
    lane = lax.broadcasted_iota(jnp.int32, (MOE_TILE, ROUTER_W), 1)

    def tile(t, carry):
        r0 = pl.multiple_of(start + t * MOE_TILE, MOE_ALIGN)
        xt = xs_ref[pl.ds(r0, MOE_TILE), :]
        ct = cs_ref[pl.ds(r0, MOE_TILE), :]
        acts = []
        for k in range(MOE_EXPERTS_PER_STEP):
            ce = jnp.sum(jnp.where(lane == e * MOE_EXPERTS_PER_STEP + k, ct, 0.0), axis=-1, keepdims=True)
            h13 = jnp.dot(xt, w13_ref[k], preferred_element_type=F32)
            a1 = h13[:, :D_EXPERT]
            acts.append((a1 * jax.nn.sigmoid(a1) * h13[:, D_EXPERT:] * ce).astype(BF16))
        ys_ref[pl.ds(r0, MOE_TILE), :] += jnp.dot(jnp.concatenate(acts, axis=-1), w2_ref[...],
                                                   preferred_element_type=F32)
        return carry

    lax.fori_loop(0, (cnt_g + MOE_TILE - 1) // MOE_TILE, tile, 0)

    @pl.when(e == pl.num_programs(1) - 1)
    def _():
        y = jnp.dot(pt_ref[...], ys_ref[...].astype(BF16), preferred_element_type=F32)
        xn = x_ref[...] + mod_ref[5:6, :] * y
        o_ref[...] = _rms(xn, gf_ref[...]) if final_norm else xn


def _moe(h2, x, mod, wr, w13, w2, g_final, dims, need_ctx, final_norm):
    B, S, L = dims
    n = x.shape[0]
    tm = min(MOE_TOKEN_BLOCK, S, B * L)
    spb = S // tm
    nblk = (n if need_ctx else B * S) // tm
    comb, pos, cnt = pl.pallas_call(
        _route_kernel,
        grid=(nblk,),
        in_specs=[pl.BlockSpec((tm, D_MODEL), lambda i: (i, 0)), pl.BlockSpec((D_MODEL, ROUTER_W), lambda i: (0, 0))],
        out_specs=[pl.BlockSpec((tm, ROUTER_W), lambda i: (i, 0)), pl.BlockSpec((tm, ROUTER_W), lambda i: (i, 0)),
                   pl.BlockSpec((8, ROUTER_W), lambda i: (i, 0))],
        out_shape=[jax.ShapeDtypeStruct((nblk * tm, ROUTER_W), F32), jax.ShapeDtypeStruct((nblk * tm, ROUTER_W), F32),
                   jax.ShapeDtypeStruct((nblk * 8, ROUTER_W), jnp.int32)],
        compiler_params=_cparams(("parallel",)),
        name="moe_route",
    )(h2, wr)
    cnt = cnt.reshape(nblk, 8, ROUTER_W)[:, 0, :N_GROUPS].reshape(nblk * N_GROUPS)

    row = lambda i, e, c: (i, 0)
    eps = MOE_EXPERTS_PER_STEP
    assert MOE_SORT_PAD >= (N_GROUPS - 1) * (MOE_ALIGN - 1) + MOE_TILE - 1 and EXPERTS_PER_GROUP % eps == 0
    slots = tm + MOE_SORT_PAD
    return pl.pallas_call(
        functools.partial(_moe_kernel, final_norm=final_norm),
        grid_spec=pltpu.PrefetchScalarGridSpec(
            num_scalar_prefetch=1,
            grid=(nblk, N_EXPERTS // eps),
            in_specs=[
                pl.BlockSpec((tm, D_MODEL), row),
                pl.BlockSpec((tm, D_MODEL), row),
                pl.BlockSpec((None, 6, D_MODEL), lambda i, e, c: (jnp.minimum(i // spb, B), 0, 0)),
                pl.BlockSpec((tm, ROUTER_W), row),
                pl.BlockSpec((tm, ROUTER_W), row),
                pl.BlockSpec((eps, D_MODEL, 2 * D_EXPERT), lambda i, e, c: (e, 0, 0)),
                pl.BlockSpec((eps * D_EXPERT, D_MODEL), lambda i, e, c: (e, 0)),
                pl.BlockSpec((1, D_MODEL), lambda i, e, c: (0, 0)),
            ],
            out_specs=pl.BlockSpec((tm, D_MODEL), row),
            scratch_shapes=[pltpu.VMEM((tm, slots), BF16), pltpu.VMEM((slots, D_MODEL), BF16),
                            pltpu.VMEM((slots, ROUTER_W), F32), pltpu.VMEM((slots, D_MODEL), F32)],
        ),
        out_shape=jax.ShapeDtypeStruct((nblk * tm, D_MODEL), F32),
        compiler_params=_cparams(("parallel", "arbitrary")),
        name="hier_moe",
    )(cnt, h2, x, mod, pos, comb, w13, w2.reshape(N_EXPERTS * D_EXPERT, D_MODEL), g_final)


def _rope_tables(S, L):
    rows = S // GRID_W
    pos_r = np.repeat(np.arange(rows, dtype=np.float32), GRID_W)
    pos_c = np.tile(np.arange(GRID_W, dtype=np.float32), rows)

    def angles(rot_dim):
        nf = rot_dim // 4
        inv = jnp.asarray(ROPE_BASE, F32) ** (-jnp.arange(nf, dtype=F32) / nf)
        ang = jnp.concatenate([pos_r[:, None] * inv, pos_c[:, None] * inv], axis=-1)
        return jnp.cos(ang), jnp.sin(ang)

    def with_ctx(cos, s_lo, s_hi):
        ident = jnp.concatenate([jnp.ones((PROJ_BLOCK, 128), F32), jnp.zeros((PROJ_BLOCK, 256), F32)], axis=-1)
        return jnp.concatenate([jnp.concatenate([cos, s_lo, s_hi], axis=-1), ident], axis=0)

    cos, sin = angles(HEAD_DIM)
    z = jnp.zeros_like(sin)
    taba = with_ctx(jnp.tile(cos, (1, 4)), jnp.tile(jnp.concatenate([-sin, z], -1), (1, 2)),
                    jnp.tile(jnp.concatenate([z, sin], -1), (1, 2)))
    cos, sin = angles(C_ROPE)
    z = jnp.zeros_like(sin)
    one64, zero64, zero32 = jnp.ones((S, 64), F32), jnp.zeros((S, 64), F32), jnp.zeros((S, 32), F32)
    tabc = with_ctx(jnp.concatenate([one64, cos, cos, one64[:, :32]], -1),
                    jnp.concatenate([zero64, -sin, z, zero32], -1),
                    jnp.concatenate([zero64, z, sin, zero32], -1))
    return taba, tabc


def _pack_w_in(w):
    pad = lambda k: jnp.zeros((w.shape[0], k), w.dtype)
    return jnp.concatenate([w[:, :3712], pad(64), w[:, 3712:3744], pad(32), w[:, 3744:]], axis=-1).astype(BF16)


def _pack_w_uq(w):
    w = w.reshape(C_Q_LORA, C_HEADS, C_NOPE + C_ROPE)
    w = jnp.pad(w, ((0, 0), (0, 0), (0, C_HEAD_PAD - C_NOPE - C_ROPE)))
    return w.reshape(C_Q_LORA, C_HEADS * C_HEAD_PAD).astype(BF16)


def _pack_w_ukv(w):
    w = w.reshape(C_KV_LORA, C_HEADS, C_NOPE + C_V)
    wk = jnp.pad(w[:, :, :C_NOPE], ((0, 0), (0, 0), (0, C_HEAD_PAD - C_NOPE))).reshape(C_KV_LORA, -1)
    wv = w[:, :, C_NOPE:].reshape(C_KV_LORA, -1)
    return jnp.concatenate([wk, wv], axis=-1).astype(BF16)


def kernel(x, c, ctx, c_ctx, w_mod, b_mod, g_norm1, g_norm2, w_in, a_sink, b_lb_logits, b_onorm, c_qnorm, c_kvnorm,
           w_uq, w_ukv, w_br, w_out, w_rg, w_re, w1, w3, w2, g_final):
    B, S, _ = x.shape
    L = ctx.shape[1]
    depth = w_in.shape[0]
    assert L == TOKEN_BLOCK and S % min(MLA_Q_BLOCK, S) == 0 and S % min(MLA_K_BLOCK, S) == 0 and S % GRID_W == 0
    assert S % PROJ_BLOCK == 0 and (B * L) % PROJ_BLOCK == 0
    dims = (B, S, L)

    xs = jnp.concatenate([x.reshape(B * S, D_MODEL), ctx.reshape(B * L, D_MODEL)], axis=0)
    cc = jnp.zeros((8, D_MODEL), F32).at[:B].set(c).at[B].set(c_ctx)
    mod_all = _modulation(cc, w_mod, b_mod).reshape(depth, 8, 6, D_MODEL)

    lb_all = jnp.cumsum(jax.nn.softmax(b_lb_logits.astype(F32), axis=0), axis=0)
    lb_all = (lb_all - lb_all[0:1]).reshape(depth, 1, 2 * B_W)
    lbp_all = jnp.concatenate([jnp.log(lb_all), jnp.log1p(-lb_all), 1.0 - lb_all,
                               jnp.zeros((depth, 5, 2 * B_W), F32)], axis=1)

    taba, tabc = _rope_tables(S, L)
    ones = jnp.kron(jnp.eye(B_HEADS, dtype=F32), jnp.ones((B_DK, B_DK), F32)).astype(BF16)

    for l in range(depth):
        need_ctx = l < depth - 1
        mod = mod_all[l]
        wr = jnp.concatenate([w_re[l], w_rg[l], jnp.zeros((D_MODEL, ROUTER_W - N_EXPERTS - N_GROUPS), F32)],
                             axis=-1).astype(BF16)
        w13 = jnp.concatenate([w1[l], w3[l]], axis=-1).astype(BF16)
        sink = jnp.repeat(a_sink[l].astype(F32).reshape(A_KV_HEADS, 1, A_GROUP) * LOG2E, A_QBLOCK, axis=-1)

        qa, ka, va, bqig, gates, qc, kc, vc, gl = _projection(
            xs, mod, g_norm1[l][None], _pack_w_in(w_in[l]), _pack_w_uq(w_uq[l]), _pack_w_ukv(w_ukv[l]),
            c_qnorm[l][None], c_kvnorm[l][None], lbp_all[l], taba, tabc, dims)
        oa = _window_gqa(qa, ka, va, sink, dims, need_ctx)
        ohg = _hgrn2_scan(bqig, gates, ones, dims)
        oc = _mla_attention(qc, kc, vc, dims, need_ctx)
        xs, h2 = _merge(xs, oa, ohg, bqig, oc, gl, w_br[l].astype(BF16), w_out[l].astype(BF16), b_onorm[l][None],
                        mod, g_norm2[l][None], ones, dims, need_ctx)
        xs = _moe(h2, xs, mod, wr, w13, w2[l].astype(BF16), g_final[None], dims, need_ctx, final_norm=not need_ctx)

    return xs.reshape(B, S, D_MODEL)
```

```python
import functools

import jax
import jax.numpy as jnp
import numpy as np
from jax import lax
from jax.experimental import pallas as pl
from jax.experimental.pallas import tpu as pltpu

F32 = jnp.float32
BF16 = jnp.bfloat16
HIGHEST = lax.Precision.HIGHEST

D_MODEL = 1024
GRID_W = 64
HEAD_DIM = 64
ROPE_BASE = 10000.0
EPS = 1e-6
A_HEADS = 8
A_KV_HEADS = 2
A_GROUP = A_HEADS // A_KV_HEADS
A_WINDOW = 128
A_BLOCK = A_WINDOW
A_QBLOCK = 2 * A_BLOCK
A_STEP_BLOCKS = 4
B_HEADS = 8
B_DK = 64
B_W = B_HEADS * B_DK
B_CHUNK = 64
B_BLOCK = 256
B_EXP_LIMIT = 80.0
C_HEADS = 8
C_NOPE = 64
C_ROPE = 32
C_V = 64
C_Q_LORA = 256
C_KV_LORA = 128
C_HEAD_PAD = 128
N_BRANCH = 3
BRANCH_W = 512
N_GROUPS = 4
EXPERTS_PER_GROUP = 8
N_EXPERTS = N_GROUPS * EXPERTS_PER_GROUP
D_EXPERT = 256
ROUTER_W = 128
V_SLAB = 80
LOG2E = 1.4426950408889634

OFF_AQ, OFF_AK, OFF_AV = 0, 512, 640
OFF_BQ, OFF_BI, OFF_BZF, OFF_BZB, OFF_BG = 768, 1280, 1792, 2304, 2816
OFF_CQ, OFF_CKV, OFF_CKR, OFF_GL = 3328, 3584, 3712, 3840
IN_W_PACKED = OFF_GL + N_BRANCH * D_MODEL

TOKEN_BLOCK = 256
PROJ_BLOCK = 512
MOE_TOKEN_BLOCK = 1024
MOE_EXPERTS_PER_STEP = 4
MOE_TILE = 288
MOE_ALIGN = 16
MOE_SORT_PAD = 384
MLA_Q_BLOCK = 2048
MLA_K_BLOCK = 1024
MLA_COL_GROUP = 512
MLA_KEY_CHUNK = 256
MLA_LAG_LIMIT = 64.0
VMEM_LIMIT = 56 * 1024 * 1024
PROJ_VMEM_LIMIT = 61 * 1024 * 1024


def _cparams(sem, vmem_limit=VMEM_LIMIT, **kw):
    return pltpu.CompilerParams(dimension_semantics=sem, vmem_limit_bytes=vmem_limit, **kw)


def _mod_kernel(cc_ref, w_ref, b_ref, o_ref):
    cc = cc_ref[...]
    a = cc * jax.nn.sigmoid(cc)
    o_ref[...] = jnp.dot(a, w_ref[...], preferred_element_type=F32, precision=HIGHEST) + b_ref[...]


def _modulation(cc, w_mod, b_mod):
    depth = w_mod.shape[0]
    nj = 6
    return pl.pallas_call(
        _mod_kernel,
        grid=(depth, nj),
        in_specs=[
            pl.BlockSpec((8, D_MODEL), lambda l, j: (0, 0)),
            pl.BlockSpec((None, D_MODEL, D_MODEL), lambda l, j: (l, 0, j)),
            pl.BlockSpec((None, 1, D_MODEL), lambda l, j: (l, 0, j)),
        ],
        out_specs=pl.BlockSpec((None, 8, D_MODEL), lambda l, j: (l, 0, j)),
        out_shape=jax.ShapeDtypeStruct((depth, 8, 6 * D_MODEL), F32),
        compiler_params=_cparams(("arbitrary", "arbitrary")),
        name="adaln_mod",
    )(cc, w_mod, b_mod.reshape(depth, 1, 6 * D_MODEL))


def _rms(x, g):
    return x * lax.rsqrt(jnp.mean(x * x, axis=-1, keepdims=True) + EPS) * g


def _rope(v, tab_ref, half):
    n = v.shape[-1]
    cos = tab_ref[:, 0:128]
    s_lo = tab_ref[:, 128:256]
    s_hi = tab_ref[:, 256:384]
    return v * cos + pltpu.roll(v, n - half, 1) * s_lo + pltpu.roll(v, half, 1) * s_hi


def _store_v_slabs(ref, vt, heads):
    ones = jnp.ones((V_SLAB - HEAD_DIM, vt.shape[1]), BF16)
    for hd in range(heads):
        ref[V_SLAB * hd:V_SLAB * hd + HEAD_DIM, :] = vt[HEAD_DIM * hd:HEAD_DIM * (hd + 1), :].astype(BF16)
        ref[V_SLAB * hd + HEAD_DIM:V_SLAB * (hd + 1), :] = ones


def _proj_kernel(x_ref, mod_ref, g1_ref, w_ref, wuq_ref, wukv_ref, gq_ref, gkv_ref, lbp_ref, taba_ref, tabc_ref,
                 qa_ref, ka_ref, va_ref, bqig_ref, gates_ref, qc_ref, kc_ref, vc_ref, gl_ref):
    x = x_ref[...]
    h = _rms(x, g1_ref[...]) * (1.0 + mod_ref[1:2, :]) + mod_ref[0:1, :]
    hb = h.astype(BF16)

    def seg(off, width):
        return jnp.dot(hb, w_ref[:, off:off + width], preferred_element_type=F32)

    aq = seg(OFF_AQ, 512) * (HEAD_DIM ** -0.5 * LOG2E)
    for j in range(4):
        qa_ref[128 * j:128 * (j + 1), :] = _rope(aq[:, 128 * j:128 * (j + 1)], taba_ref, 32).T.astype(BF16)
    ka_ref[...] = _rope(seg(OFF_AK, 128), taba_ref, 32).astype(BF16)
    _store_v_slabs(va_ref, seg(OFF_AV, 128).T, A_KV_HEADS)

    bqig_ref[:, 0:512] = seg(OFF_BQ, 512).astype(BF16)
    bqig_ref[:, 512:1024] = seg(OFF_BI, 512).astype(BF16)
    bqig_ref[:, 1024:1536] = seg(OFF_BG, 512).astype(BF16)
    for d, off in enumerate((OFF_BZF, OFF_BZB)):
        z = seg(off, 512)
        log_lb = lbp_ref[0:1, 512 * d:512 * (d + 1)]
        log1m_lb = lbp_ref[1:2, 512 * d:512 * (d + 1)]
        one_m_lb = lbp_ref[2:3, 512 * d:512 * (d + 1)]
        e = jnp.exp(-jnp.abs(z))
        log_sig = jnp.minimum(z, 0.0) - jnp.log(1.0 + e)
        b = log1m_lb + log_sig
        mx = jnp.maximum(log_lb, b)
        logf = mx + jnp.log(1.0 + jnp.exp(-jnp.abs(log_lb - b)))
        r = 1.0 / (1.0 + e)
        key = one_m_lb * jnp.where(z >= 0.0, e * r, r)
        gates_ref[:, 512 * d:512 * (d + 1)] = logf
        gates_ref[:, 1024 + 512 * d:1024 + 512 * (d + 1)] = key

    cq = _rms(seg(OFF_CQ, C_Q_LORA), gq_ref[...]).astype(BF16)
    qh = jnp.dot(cq, wuq_ref[...], preferred_element_type=F32) * ((C_NOPE + C_ROPE) ** -0.5 * LOG2E)
    ckv = _rms(seg(OFF_CKV, C_KV_LORA), gkv_ref[...]).astype(BF16)
    kvh = jnp.dot(ckv, wukv_ref[...], preferred_element_type=F32)
    kr = _rope(seg(OFF_CKR, 128), tabc_ref, 16)
    for j in range(C_HEADS):
        sl = slice(C_HEAD_PAD * j, C_HEAD_PAD * (j + 1))
        qc_ref[sl, :] = _rope(qh[:, sl], tabc_ref, 16).T.astype(BF16)
        kc_ref[:, sl] = (kvh[:, sl] + kr).astype(BF16)
    _store_v_slabs(vc_ref, kvh[:, C_HEADS * C_HEAD_PAD:].T, C_HEADS)

    for j in range(6):
        gl_ref[:, 512 * j:512 * (j + 1)] = seg(OFF_GL + 512 * j, 512).astype(BF16)


def _projection(x, mod, g1, w_in_p, wuq_p, wukv_p, gq, gkv, lbp, taba, tabc, dims):
    B, S, L = dims
    n = x.shape[0]
    tm = PROJ_BLOCK
    nlat = B * S // tm
    spb = S // tm

    def row(i):
        return (i, 0)

    def mod_row(i):
        return (jnp.minimum(i // spb, B), 0, 0)

    def tab_row(i):
        return (jnp.where(i < nlat, i % spb, spb), 0)

    const = lambda i: (0, 0)
    resident = functools.partial(pl.BlockSpec, index_map=const, pipeline_mode=pl.Buffered(1))
    widths = (512, 128, A_KV_HEADS * V_SLAB, 1536, 2048, 1024, 1024, C_HEADS * V_SLAB, 3072)
    dtypes = (BF16, BF16, BF16, BF16, F32, BF16, BF16, BF16, BF16)
    transposed = (0, 2, 5, 7)
    return pl.pallas_call(
        _proj_kernel,
        grid=(n // tm,),
        in_specs=[
            pl.BlockSpec((tm, D_MODEL), row),
            pl.BlockSpec((None, 6, D_MODEL), mod_row),
            pl.BlockSpec((1, D_MODEL), const),
            resident((D_MODEL, IN_W_PACKED)),
            resident(wuq_p.shape),
            resident(wukv_p.shape),
            pl.BlockSpec((1, C_Q_LORA), const),
            pl.BlockSpec((1, C_KV_LORA), const),
            pl.BlockSpec((8, 2 * B_W), const),
            pl.BlockSpec((tm, 384), tab_row),
            pl.BlockSpec((tm, 384), tab_row),
        ],
        out_specs=[pl.BlockSpec((w, tm), lambda i: (0, i)) if k in transposed else pl.BlockSpec((tm, w), row)
                   for k, w in enumerate(widths)],
        out_shape=[jax.ShapeDtypeStruct((w, n) if k in transposed else (n, w), dt)
                   for k, (w, dt) in enumerate(zip(widths, dtypes))],
        compiler_params=_cparams(("parallel",), PROJ_VMEM_LIMIT),
        name="in_proj",
    )(x, mod, g1, w_in_p, wuq_p, wukv_p, gq, gkv, lbp, taba, tabc)


def _gqa_ctx_kernel(qt_ref, kx_ref, vx_ref, sink_ref, _, o_ref):
    for g in range(A_KV_HEADS):
        qg = jnp.concatenate([qt_ref[HEAD_DIM * hd:HEAD_DIM * (hd + 1), :]
                              for hd in range(A_GROUP * g, A_GROUP * (g + 1))], axis=1)
        s = jnp.dot(kx_ref[:, HEAD_DIM * g:HEAD_DIM * (g + 1)], qg, preferred_element_type=F32)
        m = jnp.maximum(jnp.max(s, axis=0, keepdims=True), sink_ref[g])
        p = jnp.exp2(s - m).astype(BF16)
        pv = jnp.dot(vx_ref[V_SLAB * g:V_SLAB * (g + 1), :], p, preferred_element_type=F32)
        o = pv[0:HEAD_DIM, :] / (pv[HEAD_DIM:HEAD_DIM + 1, :] + jnp.exp2(sink_ref[g] - m))
        for hh in range(A_GROUP):
            hd = A_GROUP * g + hh
            o_ref[HEAD_DIM * hd:HEAD_DIM * (hd + 1), :] = o[:, A_QBLOCK * hh:A_QBLOCK * (hh + 1)].astype(BF16)


def _gqa_lat_kernel(qt_ref, kp_ref, kc_ref, kn_ref, kx_ref, vp_ref, vc_ref, vn_ref, vx_ref, sink_ref, o_ref,
                    bias_ref, s0_ref, s1_ref, p0_ref, p1_ref, *, seq):
    j = pl.program_id(1)
    nband = 4 * A_BLOCK
    nsb, nk = bias_ref.shape[0], bias_ref.shape[1]
    step_q = nsb * A_QBLOCK
    rows = lax.broadcasted_iota(jnp.int32, (nk, A_QBLOCK), 0)
    cols = lax.broadcasted_iota(jnp.int32, (nk, A_QBLOCK), 1)
    for qb in range(nsb):
        qpos = j * step_q + qb * A_QBLOCK + cols
        kpos = j * step_q + qb * A_QBLOCK - A_BLOCK + rows
        valid = ((kpos >= 0) & (kpos < seq) & (jnp.abs(qpos - kpos) <= A_WINDOW)) | (rows >= nband)
        bias_ref[qb] = jnp.where(valid, 0.0, -jnp.inf)
    k = jnp.concatenate([kp_ref[...], kc_ref[...], kn_ref[...]], axis=0)
    vt = jnp.concatenate([vp_ref[...], vc_ref[...], vn_ref[...]], axis=1)
    s_refs, p_refs = (s0_ref, s1_ref), (p0_ref, p1_ref)
    nchunk = nk // A_QBLOCK
    krows = lambda r: slice(A_QBLOCK * r, A_QBLOCK * (r + 1))
    items = [(qb, g) for qb in range(nsb) for g in range(A_KV_HEADS)]
    n = len(items)
    smax, m, pv = [None] * n, [None] * n, [None] * n

    def keys(i, r):
        qb, g = items[i]
        if A_QBLOCK * (r + 1) <= nband:
            sl = slice(A_QBLOCK * (qb + r), A_QBLOCK * (qb + r + 1))
            return k[sl, HEAD_DIM * g:HEAD_DIM * (g + 1)], vt[V_SLAB * g:V_SLAB * (g + 1), sl]
        return kx_ref[:, HEAD_DIM * g:HEAD_DIM * (g + 1)], vx_ref[V_SLAB * g:V_SLAB * (g + 1), :]

    def stage_scores(i, r):
        qb, g = items[i]
        qcols = slice(A_QBLOCK * qb, A_QBLOCK * (qb + 1))
        qg = jnp.concatenate([qt_ref[HEAD_DIM * hd:HEAD_DIM * (hd + 1), qcols]
                              for hd in range(A_GROUP * g, A_GROUP * (g + 1))], axis=1)
        bias = bias_ref[qb, krows(r), :]
        sc = (jnp.dot(keys(i, r)[0], qg, preferred_element_type=F32) + jnp.concatenate([bias] * A_GROUP, axis=1))
        s_refs[i % 2][krows(r), :] = sc
        cm = jnp.max(sc, axis=0, keepdims=True)
        smax[i] = cm if smax[i] is None else jnp.maximum(smax[i], cm)

    def stage_exp(i, r):
        p_refs[i % 2][krows(r), :] = jnp.exp2(s_refs[i % 2][krows(r), :] - m[i]).astype(BF16)

    def stage_pv(i, r):
        qb, g = items[i]
        t = jnp.dot(keys(i, r)[1], p_refs[i % 2][krows(r), :], preferred_element_type=F32)
        pv[i] = t if pv[i] is None else pv[i] + t
        if r == nchunk - 1:
            denom = pv[i][HEAD_DIM:HEAD_DIM + 1, :] + jnp.exp2(sink_ref[g] - m[i])
            o = pv[i][0:HEAD_DIM, :] / denom
            for hh in range(A_GROUP):
                hd = A_GROUP * g + hh
                o_ref[HEAD_DIM * hd:HEAD_DIM * (hd + 1), A_QBLOCK * qb:A_QBLOCK * (qb + 1)] = (
                    o[:, A_QBLOCK * hh:A_QBLOCK * (hh + 1)].astype(BF16))

    for r in range(nchunk):
        stage_scores(0, r)
    for i in range(n + 1):
        if i < n:
            m[i] = jnp.maximum(smax[i], sink_ref[items[i][1]])
        for r in range(nchunk):
            if i + 1 < n:
                stage_scores(i + 1, r)
            if i < n:
                stage_exp(i, r)
            if i >= 1:
                stage_pv(i - 1, r)


def _window_gqa(qat, ka, vat, sink, dims, need_ctx):
    B, S, L = dims
    n = ka.shape[0]
    assert L == A_QBLOCK
    nb = S // A_BLOCK
    nqb = S // A_QBLOCK
    nk = 4 * A_BLOCK + L
    width = A_GROUP * A_QBLOCK
    vs = A_KV_HEADS * V_SLAB
    step_q = min(A_STEP_BLOCKS * A_QBLOCK, S)
    nstep = S // step_q
    bps = step_q // A_BLOCK
    ctx_row = lambda b, j: (B * S // L + b, 0)
    ctx_col = lambda b, j: (0, B * S // L + b)
    sink_spec = pl.BlockSpec((A_KV_HEADS, 1, width), lambda b, j: (0, 0, 0))
    stage = [pltpu.VMEM((nk, width), F32), pltpu.VMEM((nk, width), F32),
             pltpu.VMEM((nk, width), BF16), pltpu.VMEM((nk, width), BF16)]
    prev_blk = lambda b, j: b * nb + jnp.maximum(bps * j - 1, 0)
    next_blk = lambda b, j: b * nb + jnp.minimum(bps * (j + 1), nb - 1)
    o_lat = pl.pallas_call(
        functools.partial(_gqa_lat_kernel, seq=S),
        grid=(B, nstep),
        in_specs=[
            pl.BlockSpec((512, step_q), lambda b, j: (0, b * nstep + j)),
            pl.BlockSpec((A_BLOCK, 128), lambda b, j: (prev_blk(b, j), 0)),
            pl.BlockSpec((step_q, 128), lambda b, j: (b * nstep + j, 0)),
            pl.BlockSpec((A_BLOCK, 128), lambda b, j: (next_blk(b, j), 0)),
            pl.BlockSpec((L, 128), ctx_row),
            pl.BlockSpec((vs, A_BLOCK), lambda b, j: (0, prev_blk(b, j))),
            pl.BlockSpec((vs, step_q), lambda b, j: (0, b * nstep + j)),
            pl.BlockSpec((vs, A_BLOCK), lambda b, j: (0, next_blk(b, j))),
            pl.BlockSpec((vs, L), ctx_col),
            sink_spec,
        ],
        out_specs=pl.BlockSpec((512, step_q), lambda b, j: (0, b * nstep + j)),
        out_shape=jax.ShapeDtypeStruct((512, n if need_ctx else B * S), BF16),
        scratch_shapes=[pltpu.VMEM((step_q // A_QBLOCK, nk, A_QBLOCK), F32)] + stage,
        compiler_params=_cparams(("parallel", "parallel")),
        name="window_gqa",
    )(qat, ka, ka, ka, ka, vat, vat, vat, vat, sink)
    if not need_ctx:
        return o_lat
    return pl.pallas_call(
        _gqa_ctx_kernel,
        grid=(B,),
        in_specs=[pl.BlockSpec((512, L), lambda b: (0, B * S // L + b)),
                  pl.BlockSpec((L, 128), lambda b: (B * S // L + b, 0)),
                  pl.BlockSpec((vs, L), lambda b: (0, B * S // L + b)),
                  pl.BlockSpec((A_KV_HEADS, 1, width), lambda b: (0, 0, 0)),
                  pl.BlockSpec(memory_space=pl.ANY)],
        out_specs=pl.BlockSpec((512, L), lambda b: (0, B * S // L + b)),
        out_shape=jax.ShapeDtypeStruct((512, n), BF16),
        input_output_aliases={4: 0},
        compiler_params=_cparams(("parallel",)),
        name="window_gqa_context",
    )(qat, ka, vat, sink, o_lat)


def _hgrn_scan(g_ref, r0, reverse):
    C = B_CHUNK
    rows = lax.broadcasted_iota(jnp.int32, (C, C), 0)
    cols = lax.broadcasted_iota(jnp.int32, (C, C), 1)
    causal = (rows <= cols) if reverse else (rows >= cols)
    g = g_ref[r0:r0 + C, :]
    bc = jnp.dot(causal.astype(F32), g, preferred_element_type=F32, precision=HIGHEST)
    tot = jnp.sum(g, axis=0, keepdims=True)
    mid = C // 2 if reverse else C // 2 - 1
    rho = bc[mid:mid + 1, :]
    return causal, bc, tot, rho


def _hgrn_prep(q_ref, v_ref, k_ref, r0, scan, factored):
    causal, bc, tot, rho = scan
    rs = slice(r0, r0 + B_CHUNK)
    q = q_ref[rs, :].astype(F32)
    key = k_ref[rs, :]
    v = v_ref[rs, :]
    qe = (q * jnp.exp(bc - rho)).astype(BF16) if factored else None
    ke = (key * jnp.exp(rho - bc)).astype(BF16) if factored else None
    qs = (q * jnp.exp(bc)).astype(BF16)
    ks = (key * jnp.exp(tot - bc)).astype(BF16)
    dec = jnp.exp(tot)
    return rs, causal, v, qe, ke, qs, ks, dec


def _hgrn_intra_exact(q_ref, v_ref, k_ref, r0, scan, reverse, ones_ref, bcs_ref, vs_ref):
    C = B_CHUNK
    causal, bc, tot, rho = scan
    q = q_ref[r0:r0 + C, :].astype(F32)
    bcs_ref[...] = bc
    vs_ref[...] = v_ref[r0:r0 + C, :].astype(F32)
    trow = lax.broadcasted_iota(jnp.int32, (C, 1), 0)

    def body(s, acc):
        later = (trow <= s) if reverse else (trow >= s)
        w = jnp.where(later, jnp.exp(jnp.minimum(bc - bcs_ref[pl.ds(s, 1), :], 0.0)), 0.0)
        att = _group_sum(q * w * k_ref[pl.ds(r0 + s, 1), :], ones_ref)
        return acc + att * vs_ref[pl.ds(s, 1), :]

    return lax.fori_loop(0, C, body, jnp.zeros((C, B_W), F32))


def _hgrn_heads(prep, o_ref, st_ref, intra=None):
    rs, causal, v, qe, ke, qs, ks, dec = prep
    for hd in range(B_HEADS):
        sl = slice(B_DK * hd, B_DK * (hd + 1))
        st = st_ref[hd]
        o = lax.dot_general(qs[:, sl], st.astype(BF16), (((1,), (1,)), ((), ())), preferred_element_type=F32)
        if intra is None:
            att = lax.dot_general(qe[:, sl], ke[:, sl], (((1,), (1,)), ((), ())), preferred_element_type=F32)
            att = jnp.where(causal, att, 0.0).astype(BF16)
            o = o + jnp.dot(att, v[:, sl], preferred_element_type=F32)
        else:
            o = o + intra[:, sl]
        o_ref[rs, sl] = o
        upd = lax.dot_general(v[:, sl], ks[:, sl], (((0,), (0,)), ((), ())), preferred_element_type=F32)
        st_ref[hd] = st * dec[:, sl] + upd


def _hgrn_kernel(qf_ref, vf_ref, gf_ref, kf_ref, qb_ref, vb_ref, gb_ref, kb_ref, ones_ref, of_ref, ob_ref,
                 stf_ref, stb_ref, bcs_ref, vs_ref):
    @pl.when(pl.program_id(1) == 0)
    def _():
        stf_ref[...] = jnp.zeros_like(stf_ref)
        stb_ref[...] = jnp.zeros_like(stb_ref)

    nchunk = qf_ref.shape[0] // B_CHUNK
    chunks = []
    for i in range(nchunk):
        chunks.append((qf_ref, vf_ref, gf_ref, kf_ref, of_ref, stf_ref, B_CHUNK * i, False))
        chunks.append((qb_ref, vb_ref, gb_ref, kb_ref, ob_ref, stb_ref, B_CHUNK * (nchunk - 1 - i), True))
    half = B_CHUNK // 2
    spread = None
    for (_, _, g_ref, _, _, _, r0, _) in chunks:
        for h0 in (r0, r0 + half):
            tot = jnp.sum(jnp.abs(g_ref[h0:h0 + half, :]), axis=0, keepdims=True)
            spread = tot if spread is None else jnp.maximum(spread, tot)
    factorable = jnp.max(spread) <= B_EXP_LIMIT

    @pl.when(factorable)
    def _():
        preps = [_hgrn_prep(q_ref, v_ref, k_ref, r0, _hgrn_scan(g_ref, r0, rev), True)
                 for (q_ref, v_ref, g_ref, k_ref, _, _, r0, rev) in chunks]
        for prep, (_, _, _, _, o_ref, st_ref, _, _) in zip(preps, chunks):
            _hgrn_heads(prep, o_ref, st_ref)

    @pl.when(jnp.logical_not(factorable))
    def _():
        for (q_ref, v_ref, g_ref, k_ref, o_ref, st_ref, r0, rev) in chunks:
            scan = _hgrn_scan(g_ref, r0, rev)
            intra = _hgrn_intra_exact(q_ref, v_ref, k_ref, r0, scan, rev, ones_ref, bcs_ref, vs_ref)
            _hgrn_heads(_hgrn_prep(q_ref, v_ref, k_ref, r0, scan, False), o_ref, st_ref, intra)


def _hgrn2_scan(bqig, gates, ones, dims):
    B, S, L = dims
    n = bqig.shape[0]
    T = B_BLOCK
    assert L == T
    ns = S // T

    def fwd_blk(b, c):
        return jnp.where(c == 0, B * ns + b, b * ns + c - 1)

    def bwd_blk(b, c):
        return jnp.where(c == 0, B * ns + b, b * ns + ns - c)

    def specs(blk, d):
        return [pl.BlockSpec((T, B_W), lambda b, c: (blk(b, c), 0)),
                pl.BlockSpec((T, B_W), lambda b, c: (blk(b, c), 1)),
                pl.BlockSpec((T, B_W), lambda b, c: (blk(b, c), d)),
                pl.BlockSpec((T, B_W), lambda b, c: (blk(b, c), 2 + d))]

    return pl.pallas_call(
        _hgrn_kernel,
        grid=(B, ns + 1),
        in_specs=specs(fwd_blk, 0) + specs(bwd_blk, 1) + [pl.BlockSpec((B_W, B_W), lambda b, c: (0, 0))],
        out_specs=[pl.BlockSpec((T, B_W), lambda b, c: (fwd_blk(b, c), 0)),
                   pl.BlockSpec((T, B_W), lambda b, c: (bwd_blk(b, c), 0))],
        out_shape=[jax.ShapeDtypeStruct((n, B_W), F32), jax.ShapeDtypeStruct((n, B_W), F32)],
        scratch_shapes=[pltpu.VMEM((B_HEADS, B_DK, B_DK), F32), pltpu.VMEM((B_HEADS, B_DK, B_DK), F32),
                        pltpu.VMEM((B_CHUNK, B_W), F32), pltpu.VMEM((B_CHUNK, B_W), F32)],
        compiler_params=_cparams(("parallel", "arbitrary")),
        name="hgrn2_scan",
    )(bqig, bqig, gates, gates, bqig, bqig, gates, gates, ones)


def _mla_kernel(qt_ref, k_ref, vt_ref, *rest, with_ctx):
    if with_ctx:
        kx_ref, vxt_ref, o_ref, m_ref, acc_ref, *bufs = rest
    else:
        _, o_ref, m_ref, acc_ref, *bufs = rest
    s_refs, p_refs = bufs[0:2], bufs[2:4]
    kstep = pl.program_id(2)
    tq = qt_ref.shape[1]

    def kv_pass(k_ref, vt_ref, first):
        nkeys = k_ref.shape[0]
        nchunk = nkeys // MLA_KEY_CHUNK
        krows = lambda r: slice(MLA_KEY_CHUNK * r, MLA_KEY_CHUNK * (r + 1))

        group = s_refs[0].shape[1]
        items = [(c, hd) for c in range(tq // group) for hd in range(C_HEADS)]
        n = len(items)
        smax = [None] * n
        m_new = [None] * n
        alpha = [None] * n
        pv = [None] * n
        cols = lambda i: slice(group * items[i][0], group * (items[i][0] + 1))

        def stage_scores(i, r):
            hd = items[i][1]
            sl = slice(C_HEAD_PAD * hd, C_HEAD_PAD * (hd + 1))
            sc = jnp.dot(k_ref[krows(r), sl], qt_ref[sl, cols(i)], preferred_element_type=F32)
            s_refs[i % 2][krows(r), :] = sc
            cm = jnp.max(sc, axis=0, keepdims=True)
            smax[i] = cm if smax[i] is None else jnp.maximum(smax[i], cm)

        def stage_stats(i):
            hd = items[i][1]
            if first:
                m_new[i] = smax[i]
            else:
                m_old = m_ref[hd, :, cols(i)]
                m_new[i] = jnp.maximum(m_old, smax[i])
                alpha[i] = jnp.exp2(m_old - m_new[i])
            m_ref[hd, :, cols(i)] = m_new[i]

        def stage_exp(i, r):
            p_refs[i % 2][krows(r), :] = jnp.exp2(s_refs[i % 2][krows(r), :] - m_new[i]).astype(BF16)

        def stage_pv(i, r):
            hd = items[i][1]
            t = jnp.dot(vt_ref[V_SLAB * hd:V_SLAB * (hd + 1), krows(r)], p_refs[i % 2][krows(r), :],
                        preferred_element_type=F32)
            pv[i] = t if pv[i] is None else pv[i] + t
            if r == nchunk - 1:
                acc_ref[hd, :, cols(i)] = pv[i] if first else alpha[i] * acc_ref[hd, :, cols(i)] + pv[i]

        for r in range(nchunk):
            stage_scores(0, r)
        for i in range(n + 1):
            if i < n:
                stage_stats(i)
            for r in range(nchunk):
                if i + 1 < n:
                    stage_scores(i + 1, r)
                if i < n:
                    stage_exp(i, r)
                if i >= 1:
                    stage_pv(i - 1, r)

    def kv_pass_lagged(k_ref, vt_ref, tmp_ref):
        nkeys = k_ref.shape[0]
        nchunk = nkeys // MLA_KEY_CHUNK
        krows = lambda r: slice(MLA_KEY_CHUNK * r, MLA_KEY_CHUNK * (r + 1))
        group = p_refs[0].shape[1]
        items = [(c, hd) for c in range(tq // group) for hd in range(C_HEADS)]
        n = len(items)
        cols = lambda i: slice(group * items[i][0], group * (items[i][0] + 1))
        base, smax, pv = [None] * n, [None] * n, [None] * n

        def stage_scores(i, r):
            hd = items[i][1]
            sl = slice(C_HEAD_PAD * hd, C_HEAD_PAD * (hd + 1))
            if base[i] is None:
                base[i] = m_ref[hd, :, cols(i)]
            sc = jnp.dot(k_ref[krows(r), sl], qt_ref[sl, cols(i)], preferred_element_type=F32)
            p_refs[i % 2][krows(r), :] = jnp.exp2(sc - base[i]).astype(BF16)
            cm = jnp.max(sc, axis=0, keepdims=True)
            smax[i] = cm if smax[i] is None else jnp.maximum(smax[i], cm)

        def stage_pv(i, r):
            hd = items[i][1]
            t = jnp.dot(vt_ref[V_SLAB * hd:V_SLAB * (hd + 1), krows(r)], p_refs[i % 2][krows(r), :],
                        preferred_element_type=F32)
            pv[i] = t if pv[i] is None else pv[i] + t
            if r == nchunk - 1:
                tmp_ref[hd, :, cols(i)] = pv[i]

        for r in range(nchunk):
            stage_scores(0, r)
        for i in range(n):
            for r in range(nchunk):
                if i + 1 < n:
                    stage_scores(i + 1, r)
                stage_pv(i, r)
        excess = smax[0] - base[0]
        for i in range(1, n):
            excess = jnp.maximum(excess, smax[i] - base[i])
        return jnp.max(excess)

    if with_ctx:
        @pl.when(kstep == 0)
        def _():
            kv_pass(kx_ref, vxt_ref, True)

        tmp_ref = bufs[4]
        safe = kv_pass_lagged(k_ref, vt_ref, tmp_ref) <= MLA_LAG_LIMIT

        @pl.when(safe)
        def _():
            for hd in range(C_HEADS):
                acc_ref[hd] = acc_ref[hd] + tmp_ref[hd]

        @pl.when(jnp.logical_not(safe))
        def _():
            kv_pass(k_ref, vt_ref, False)
    else:
        kv_pass(k_ref, vt_ref, True)

    @pl.when(kstep == pl.num_programs(2) - 1)
    def _():
        for hd in range(C_HEADS):
            o_ref[C_V * hd:C_V * (hd + 1), :] = (acc_ref[hd, 0:C_V, :] / acc_ref[hd, C_V:C_V + 1, :]).astype(BF16)


def _mla_attention(qct, kc, vct, dims, need_ctx):
    B, S, L = dims
    n = kc.shape[0]
    tq = min(MLA_Q_BLOCK, S)
    tk = min(MLA_K_BLOCK, S)
    nq, nk = S // tq, S // tk
    hw = C_HEADS * C_HEAD_PAD
    vw = C_HEADS * C_V
    vs = C_HEADS * V_SLAB
    scratch = lambda t, nkeys: [
        pltpu.VMEM((C_HEADS, 1, t), F32), pltpu.VMEM((C_HEADS, V_SLAB, t), F32),
        pltpu.VMEM((nkeys, min(t, MLA_COL_GROUP)), F32), pltpu.VMEM((nkeys, min(t, MLA_COL_GROUP)), F32),
        pltpu.VMEM((nkeys, min(t, MLA_COL_GROUP)), BF16), pltpu.VMEM((nkeys, min(t, MLA_COL_GROUP)), BF16)]
    ctx_row = lambda b, i, k: (B * S // L + b, 0)
    ctx_col = lambda b, i, k: (0, B * S // L + b)
    o_lat = pl.pallas_call(
        functools.partial(_mla_kernel, with_ctx=True),
        grid=(B, nq, nk),
        in_specs=[
            pl.BlockSpec((hw, tq), lambda b, i, k: (0, b * nq + i)),
            pl.BlockSpec((tk, hw), lambda b, i, k: (b * nk + k, 0)),
            pl.BlockSpec((vs, tk), lambda b, i, k: (0, b * nk + k)),
            pl.BlockSpec((L, hw), ctx_row),
            pl.BlockSpec((vs, L), ctx_col),
        ],
        out_specs=pl.BlockSpec((vw, tq), lambda b, i, k: (0, b * nq + i)),
        out_shape=jax.ShapeDtypeStruct((vw, n if need_ctx else B * S), BF16),
        scratch_shapes=scratch(tq, tk) + [pltpu.VMEM((C_HEADS, V_SLAB, tq), F32)],
        compiler_params=_cparams(("parallel", "parallel", "arbitrary")),
        name="mla_latent",
    )(qct, kc, vct, kc, vct)
    if not need_ctx:
        return o_lat
    return pl.pallas_call(
        functools.partial(_mla_kernel, with_ctx=False),
        grid=(B, 1, 1),
        in_specs=[
            pl.BlockSpec((hw, L), ctx_col),
            pl.BlockSpec((L, hw), ctx_row),
            pl.BlockSpec((vs, L), ctx_col),
            pl.BlockSpec(memory_space=pl.ANY),
        ],
        out_specs=pl.BlockSpec((vw, L), ctx_col),
        out_shape=jax.ShapeDtypeStruct((vw, n), BF16),
        scratch_shapes=scratch(L, L),
        input_output_aliases={3: 0},
        compiler_params=_cparams(("parallel", "arbitrary", "arbitrary")),
        name="mla_context",
    )(qct, kc, vct, o_lat)


def _group_sum(x, ones_ref):
    hi = x.astype(BF16)
    lo = (x - hi.astype(F32)).astype(BF16)
    return (jnp.dot(hi, ones_ref[...], preferred_element_type=F32)
            + jnp.dot(lo, ones_ref[...], preferred_element_type=F32))


def _merge_kernel(x_ref, oa_ref, of_ref, ob_ref, bg_ref, oc_ref, gl_ref, wbr_ref, wout_ref, gn_ref, mod_ref,
                  g2_ref, ones_ref, xo_ref, h2_ref):
    ob = of_ref[...] + ob_ref[...]
    ms = _group_sum(ob * ob, ones_ref) * (1.0 / B_DK)
    obn = ob * lax.rsqrt(ms + EPS) * gn_ref[...]
    bg = bg_ref[...].astype(F32)
    bb = (obn * (bg * jax.nn.sigmoid(bg))).astype(BF16)
    branches = ((oa_ref[...], 0), (bb, 1), (oc_ref[...], 0))
    y = None
    for nbr, (br, axis) in enumerate(branches):
        gate = jax.nn.sigmoid(gl_ref[:, D_MODEL * nbr:D_MODEL * (nbr + 1)].astype(F32))
        t = gate * lax.dot_general(br, wbr_ref[nbr], (((axis,), (0,)), ((), ())), preferred_element_type=F32)
        y = t if y is None else y + t
    upd = jnp.dot(y.astype(BF16), wout_ref[...], preferred_element_type=F32)
    xn = x_ref[...] + mod_ref[2:3, :] * upd
    xo_ref[...] = xn
    h2 = _rms(xn, g2_ref[...]) * (1.0 + mod_ref[4:5, :]) + mod_ref[3:4, :]
    h2_ref[...] = h2.astype(BF16)


def _merge(x, oa, ohg, bqig, oc, gl, wbr, wout, gn, mod, g2, ones, dims, need_ctx):
    B, S, L = dims
    n = x.shape[0]
    tm = PROJ_BLOCK
    spb = S // tm
    nblk = (n if need_ctx else B * S) // tm
    row = lambda i: (i, 0)
    const2 = lambda i: (0, 0)
    return pl.pallas_call(
        _merge_kernel,
        grid=(nblk,),
        in_specs=[
            pl.BlockSpec((tm, D_MODEL), row),
            pl.BlockSpec((512, tm), lambda i: (0, i)),
            pl.BlockSpec((tm, B_W), row),
            pl.BlockSpec((tm, B_W), row),
            pl.BlockSpec((tm, B_W), lambda i: (i, 2)),
            pl.BlockSpec((512, tm), lambda i: (0, i)),
            pl.BlockSpec((tm, N_BRANCH * D_MODEL), row),
            pl.BlockSpec((N_BRANCH, BRANCH_W, D_MODEL), lambda i: (0, 0, 0)),
            pl.BlockSpec((D_MODEL, D_MODEL), const2),
            pl.BlockSpec((1, B_W), const2),
            pl.BlockSpec((None, 6, D_MODEL), lambda i: (jnp.minimum(i // spb, B), 0, 0)),
            pl.BlockSpec((1, D_MODEL), const2),
            pl.BlockSpec((B_W, B_W), const2),
        ],
        out_specs=[pl.BlockSpec((tm, D_MODEL), row), pl.BlockSpec((tm, D_MODEL), row)],
        out_shape=[jax.ShapeDtypeStruct((nblk * tm, D_MODEL), F32), jax.ShapeDtypeStruct((nblk * tm, D_MODEL), BF16)],
        compiler_params=_cparams(("parallel",)),
        name="branch_merge",
    )(x, oa, ohg[0], ohg[1], bqig, oc, gl, wbr, wout, gn, mod, g2, ones)


def _route_kernel(h_ref, wr_ref, comb_ref, pos_ref, cnt_ref):
    h = h_ref[...]
    T = h.shape[0]
    lane = lax.broadcasted_iota(jnp.int32, (T, ROUTER_W), 1)
    logits = jnp.dot(h, wr_ref[...], preferred_element_type=F32)
    big = jnp.int32(ROUTER_W)
    is_grp = (lane >= N_EXPERTS) & (lane < N_EXPERTS + N_GROUPS)
    gl = jnp.where(is_grp, logits, -jnp.inf)
    gmax = jnp.max(gl, axis=-1, keepdims=True)
    gsel = jnp.min(jnp.where(gl == gmax, lane, big), axis=-1, keepdims=True) - N_EXPERTS
    gw = 1.0 / jnp.sum(jnp.exp(gl - gmax), axis=-1, keepdims=True)
    in_grp = (lane >= gsel * EXPERTS_PER_GROUP) & (lane < (gsel + 1) * EXPERTS_PER_GROUP)
    el = jnp.where(in_grp, logits, -jnp.inf)
    m1 = jnp.max(el, axis=-1, keepdims=True)
    i1 = jnp.min(jnp.where(el == m1, lane, big), axis=-1, keepdims=True)
    el2 = jnp.where(lane == i1, -jnp.inf, el)
    m2 = jnp.max(el2, axis=-1, keepdims=True)
    i2 = jnp.min(jnp.where(el2 == m2, lane, big), axis=-1, keepdims=True)
    e2 = jnp.exp(m2 - m1)
    w1 = gw / (1.0 + e2)
    w2 = gw * e2 / (1.0 + e2)
    comb_ref[...] = jnp.where(lane == i1, w1, 0.0) + jnp.where(lane == i2, w2, 0.0)

    onehot = lane == gsel
    ones = jnp.where(onehot, 1.0, 0.0)
    rows = lax.broadcasted_iota(jnp.int32, (T, T), 0)
    cols = lax.broadcasted_iota(jnp.int32, (T, T), 1)
    earlier = jnp.where(rows > cols, 1.0, 0.0).astype(BF16)
    before = jnp.dot(earlier, ones.astype(BF16), preferred_element_type=F32)
    rank = jnp.sum(jnp.where(onehot, before, 0.0), axis=-1, keepdims=True)
    cnt = jnp.sum(ones, axis=0, keepdims=True)
    padded = jnp.floor((cnt + (MOE_ALIGN - 1)) * (1.0 / MOE_ALIGN)) * MOE_ALIGN
    seg = [jnp.sum(jnp.where(lane[0:1] == g, padded, 0.0), axis=-1, keepdims=True) for g in range(N_GROUPS - 1)]
    start = jnp.where(gsel == 0, 0.0, jnp.where(gsel == 1, seg[0], jnp.where(gsel == 2, seg[0] + seg[1],
                                                                             seg[0] + seg[1] + seg[2])))
    pos_ref[...] = jnp.broadcast_to(start + rank, (T, ROUTER_W))
    cnt_ref[...] = jnp.broadcast_to(cnt, (8, ROUTER_W)).astype(jnp.int32)


def _moe_kernel(cnt_ref, h_ref, x_ref, mod_ref, pos_ref, comb_ref, w13_ref, w2_ref, gf_ref, o_ref,
                pt_ref, xs_ref, cs_ref, ys_ref, *, final_norm):
    i = pl.program_id(0)
    e = pl.program_id(1)
    T = h_ref.shape[0]
    R = xs_ref.shape[0]
    gather = lambda a: lax.dot_general(pt_ref[...], a, (((0,), (0,)), ((), ())), preferred_element_type=F32)

    @pl.when(e == 0)
    def _():
        slot = lax.broadcasted_iota(jnp.int32, (T, R), 1).astype(F32)
        pt_ref[...] = jnp.where(pos_ref[:, 0:1] == slot, 1.0, 0.0).astype(BF16)
        xs_ref[...] = gather(h_ref[...]).astype(BF16)
        comb = comb_ref[...]
        hi = comb.astype(BF16)
        r1 = comb - hi.astype(F32)
        mid = r1.astype(BF16)
        lo = (r1 - mid.astype(F32)).astype(BF16)
        cs_ref[...] = gather(hi) + gather(mid) + gather(lo)
        ys_ref[...] = jnp.zeros_like(ys_ref)

    g = (e * MOE_EXPERTS_PER_STEP) // EXPERTS_PER_GROUP
    cnt = [cnt_ref[i * N_GROUPS + gg] for gg in range(N_GROUPS)]
    seg = [(c + (MOE_ALIGN - 1)) // MOE_ALIGN * MOE_ALIGN for c in cnt]
    start = (jnp.where(g > 0, seg[0], 0) + jnp.where(g > 1, seg[1], 0) + jnp.where(g > 2, seg[2], 0))
    cnt_g = jnp.where(g == 0, cnt[0], jnp.where(g == 1, cnt[1], jnp.where(g == 2, cnt[2], cnt[3])))
    lane = lax.broadcasted_iota(jnp.int32, (MOE_TILE, ROUTER_W), 1)

    def tile(t, carry):
        r0 = pl.multiple_of(start + t * MOE_TILE, MOE_ALIGN)
        xt = xs_ref[pl.ds(r0, MOE_TILE), :]
        ct = cs_ref[pl.ds(r0, MOE_TILE), :]
        acts = []
        for k in range(MOE_EXPERTS_PER_STEP):
            ce = jnp.sum(jnp.where(lane == e * MOE_EXPERTS_PER_STEP + k, ct, 0.0), axis=-1, keepdims=True)
            h13 = jnp.dot(xt, w13_ref[k], preferred_element_type=F32)
            a1 = h13[:, :D_EXPERT]
            acts.append((a1 * jax.nn.sigmoid(a1) * h13[:, D_EXPERT:] * ce).astype(BF16))
        ys_ref[pl.ds(r0, MOE_TILE), :] += jnp.dot(jnp.concatenate(acts, axis=-1), w2_ref[...],
                                                   preferred_element_type=F32)
        return carry

    lax.fori_loop(0, (cnt_g + MOE_TILE - 1) // MOE_TILE, tile, 0)

    @pl.when(e == pl.num_programs(1) - 1)
    def _():
        y = jnp.dot(pt_ref[...], ys_ref[...].astype(BF16), preferred_element_type=F32)
        xn = x_ref[...] + mod_ref[5:6, :] * y
        o_ref[...] = _rms(xn, gf_ref[...]) if final_norm else xn


def _moe(h2, x, mod, wr, w13, w2, g_final, dims, need_ctx, final_norm):
    B, S, L = dims
    n = x.shape[0]
    tm = min(MOE_TOKEN_BLOCK, S, B * L)
    spb = S // tm
    nblk = (n if need_ctx else B * S) // tm
    comb, pos, cnt = pl.pallas_call(
        _route_kernel,
        grid=(nblk,),
        in_specs=[pl.BlockSpec((tm, D_MODEL), lambda i: (i, 0)), pl.BlockSpec((D_MODEL, ROUTER_W), lambda i: (0, 0))],
        out_specs=[pl.BlockSpec((tm, ROUTER_W), lambda i: (i, 0)), pl.BlockSpec((tm, ROUTER_W), lambda i: (i, 0)),
                   pl.BlockSpec((8, ROUTER_W), lambda i: (i, 0))],
        out_shape=[jax.ShapeDtypeStruct((nblk * tm, ROUTER_W), F32), jax.ShapeDtypeStruct((nblk * tm, ROUTER_W), F32),
                   jax.ShapeDtypeStruct((nblk * 8, ROUTER_W), jnp.int32)],
        compiler_params=_cparams(("parallel",)),
        name="moe_route",
    )(h2, wr)
    cnt = cnt.reshape(nblk, 8, ROUTER_W)[:, 0, :N_GROUPS].reshape(nblk * N_GROUPS)

    row = lambda i, e, c: (i, 0)
    eps = MOE_EXPERTS_PER_STEP
    assert MOE_SORT_PAD >= (N_GROUPS - 1) * (MOE_ALIGN - 1) + MOE_TILE - 1 and EXPERTS_PER_GROUP % eps == 0
    slots = tm + MOE_SORT_PAD
    return pl.pallas_call(
        functools.partial(_moe_kernel, final_norm=final_norm),
        grid_spec=pltpu.PrefetchScalarGridSpec(
            num_scalar_prefetch=1,
            grid=(nblk, N_EXPERTS // eps),
            in_specs=[
                pl.BlockSpec((tm, D_MODEL), row),
                pl.BlockSpec((tm, D_MODEL), row),
                pl.BlockSpec((None, 6, D_MODEL), lambda i, e, c: (jnp.minimum(i // spb, B), 0, 0)),
                pl.BlockSpec((tm, ROUTER_W), row),
                pl.BlockSpec((tm, ROUTER_W), row),
                pl.BlockSpec((eps, D_MODEL, 2 * D_EXPERT), lambda i, e, c: (e, 0, 0)),
                pl.BlockSpec((eps * D_EXPERT, D_MODEL), lambda i, e, c: (e, 0)),
                pl.BlockSpec((1, D_MODEL), lambda i, e, c: (0, 0)),
            ],
            out_specs=pl.BlockSpec((tm, D_MODEL), row),
            scratch_shapes=[pltpu.VMEM((tm, slots), BF16), pltpu.VMEM((slots, D_MODEL), BF16),
                            pltpu.VMEM((slots, ROUTER_W), F32), pltpu.VMEM((slots, D_MODEL), F32)],
        ),
        out_shape=jax.ShapeDtypeStruct((nblk * tm, D_MODEL), F32),
        compiler_params=_cparams(("parallel", "arbitrary")),
        name="hier_moe",
    )(cnt, h2, x, mod, pos, comb, w13, w2.reshape(N_EXPERTS * D_EXPERT, D_MODEL), g_final)


def _rope_tables(S, L):
    rows = S // GRID_W
    pos_r = np.repeat(np.arange(rows, dtype=np.float32), GRID_W)
    pos_c = np.tile(np.arange(GRID_W, dtype=np.float32), rows)

    def angles(rot_dim):
        nf = rot_dim // 4
        inv = jnp.asarray(ROPE_BASE, F32) ** (-jnp.arange(nf, dtype=F32) / nf)
        ang = jnp.concatenate([pos_r[:, None] * inv, pos_c[:, None] * inv], axis=-1)
        return jnp.cos(ang), jnp.sin(ang)

    def with_ctx(cos, s_lo, s_hi):
        ident = jnp.concatenate([jnp.ones((PROJ_BLOCK, 128), F32), jnp.zeros((PROJ_BLOCK, 256), F32)], axis=-1)
        return jnp.concatenate([jnp.concatenate([cos, s_lo, s_hi], axis=-1), ident], axis=0)

    cos, sin = angles(HEAD_DIM)
    z = jnp.zeros_like(sin)
    taba = with_ctx(jnp.tile(cos, (1, 4)), jnp.tile(jnp.concatenate([-sin, z], -1), (1, 2)),
                    jnp.tile(jnp.concatenate([z, sin], -1), (1, 2)))
    cos, sin = angles(C_ROPE)
    z = jnp.zeros_like(sin)
    one64, zero64, zero32 = jnp.ones((S, 64), F32), jnp.zeros((S, 64), F32), jnp.zeros((S, 32), F32)
    tabc = with_ctx(jnp.concatenate([one64, cos, cos, one64[:, :32]], -1),
                    jnp.concatenate([zero64, -sin, z, zero32], -1),
                    jnp.concatenate([zero64, z, sin, zero32], -1))
    return taba, tabc


def _pack_w_in(w):
    pad = lambda k: jnp.zeros((w.shape[0], k), w.dtype)
    return jnp.concatenate([w[:, :3712], pad(64), w[:, 3712:3744], pad(32), w[:, 3744:]], axis=-1).astype(BF16)


def _pack_w_uq(w):
    w = w.reshape(C_Q_LORA, C_HEADS, C_NOPE + C_ROPE)
    w = jnp.pad(w, ((0, 0), (0, 0), (0, C_HEAD_PAD - C_NOPE - C_ROPE)))
    return w.reshape(C_Q_LORA, C_HEADS * C_HEAD_PAD).astype(BF16)


def _pack_w_ukv(w):
    w = w.reshape(C_KV_LORA, C_HEADS, C_NOPE + C_V)
    wk = jnp.pad(w[:, :, :C_NOPE], ((0, 0), (0, 0), (0, C_HEAD_PAD - C_NOPE))).reshape(C_KV_LORA, -1)
    wv = w[:, :, C_NOPE:].reshape(C_KV_LORA, -1)
    return jnp.concatenate([wk, wv], axis=-1).astype(BF16)


def kernel(x, c, ctx, c_ctx, w_mod, b_mod, g_norm1, g_norm2, w_in, a_sink, b_lb_logits, b_onorm, c_qnorm, c_kvnorm,
           w_uq, w_ukv, w_br, w_out, w_rg, w_re, w1, w3, w2, g_final):
    B, S, _ = x.shape
    L = ctx.shape[1]
    depth = w_in.shape[0]
    assert L == TOKEN_BLOCK and S % min(MLA_Q_BLOCK, S) == 0 and S % min(MLA_K_BLOCK, S) == 0 and S % GRID_W == 0
    assert S % PROJ_BLOCK == 0 and (B * L) % PROJ_BLOCK == 0
    dims = (B, S, L)

    xs = jnp.concatenate([x.reshape(B * S, D_MODEL), ctx.reshape(B * L, D_MODEL)], axis=0)
    cc = jnp.zeros((8, D_MODEL), F32).at[:B].set(c).at[B].set(c_ctx)
    mod_all = _modulation(cc, w_mod, b_mod).reshape(depth, 8, 6, D_MODEL)

    lb_all = jnp.cumsum(jax.nn.softmax(b_lb_logits.astype(F32), axis=0), axis=0)
    lb_all = (lb_all - lb_all[0:1]).reshape(depth, 1, 2 * B_W)
    lbp_all = jnp.concatenate([jnp.log(lb_all), jnp.log1p(-lb_all), 1.0 - lb_all,
                               jnp.zeros((depth, 5, 2 * B_W), F32)], axis=1)

    taba, tabc = _rope_tables(S, L)
    ones = jnp.kron(jnp.eye(B_HEADS, dtype=F32), jnp.ones((B_DK, B_DK), F32)).astype(BF16)

    for l in range(depth):
        need_ctx = l < depth - 1
        mod = mod_all[l]
        wr = jnp.concatenate([w_re[l], w_rg[l], jnp.zeros((D_MODEL, ROUTER_W - N_EXPERTS - N_GROUPS), F32)],
                             axis=-1).astype(BF16)
        w13 = jnp.concatenate([w1[l], w3[l]], axis=-1).astype(BF16)
        sink = jnp.repeat(a_sink[l].astype(F32).reshape(A_KV_HEADS, 1, A_GROUP) * LOG2E, A_QBLOCK, axis=-1)

        qa, ka, va, bqig, gates, qc, kc, vc, gl = _projection(
            xs, mod, g_norm1[l][None], _pack_w_in(w_in[l]), _pack_w_uq(w_uq[l]), _pack_w_ukv(w_ukv[l]),
            c_qnorm[l][None], c_kvnorm[l][None], lbp_all[l], taba, tabc, dims)
        oa = _window_gqa(qa, ka, va, sink, dims, need_ctx)
        ohg = _hgrn2_scan(bqig, gates, ones, dims)
        oc = _mla_attention(qc, kc, vc, dims, need_ctx)
        xs, h2 = _merge(xs, oa, ohg, bqig, oc, gl, w_br[l].astype(BF16), w_out[l].astype(BF16), b_onorm[l][None],
                        mod, g_norm2[l][None], ones, dims, need_ctx)
        xs = _moe(h2, xs, mod, wr, w13, w2[l].astype(BF16), g_final[None], dims, need_ctx, final_norm=not need_ctx)

    return xs.reshape(B, S, D_MODEL)
```

```python
import functools

import jax
import jax.numpy as jnp
import numpy as np
from jax import lax
from jax.experimental import pallas as pl
from jax.experimental.pallas import tpu as pltpu

F32 = jnp.float32
BF16 = jnp.bfloat16
HIGHEST = lax.Precision.HIGHEST

D_MODEL = 1024
GRID_W = 64
HEAD_DIM = 64
ROPE_BASE = 10000.0
EPS = 1e-6
A_HEADS = 8
A_KV_HEADS = 2
A_GROUP = A_HEADS // A_KV_HEADS
A_WINDOW = 128
A_BLOCK = A_WINDOW
A_QBLOCK = 2 * A_BLOCK
A_STEP_BLOCKS = 4
B_HEADS = 8
B_DK = 64
B_W = B_HEADS * B_DK
B_CHUNK = 64
B_BLOCK = 256
B_EXP_LIMIT = 80.0
C_HEADS = 8
C_NOPE = 64
C_ROPE = 32
C_V = 64
C_Q_LORA = 256
C_KV_LORA = 128
C_HEAD_PAD = 128
N_BRANCH = 3
BRANCH_W = 512
N_GROUPS = 4
EXPERTS_PER_GROUP = 8
N_EXPERTS = N_GROUPS * EXPERTS_PER_GROUP
D_EXPERT = 256
ROUTER_W = 128
V_SLAB = 80
LOG2E = 1.4426950408889634

OFF_AQ, OFF_AK, OFF_AV = 0, 512, 640
OFF_BQ, OFF_BI, OFF_BZF, OFF_BZB, OFF_BG = 768, 1280, 1792, 2304, 2816
OFF_CQ, OFF_CKV, OFF_CKR, OFF_GL = 3328, 3584, 3712, 3840
IN_W_PACKED = OFF_GL + N_BRANCH * D_MODEL

TOKEN_BLOCK = 256
PROJ_BLOCK = 512
MOE_TOKEN_BLOCK = 1024
MOE_EXPERTS_PER_STEP = 4
MOE_TILE = 288
MOE_ALIGN = 16
MOE_SORT_PAD = 384
MLA_Q_BLOCK = 2048
MLA_K_BLOCK = 1024
MLA_COL_GROUP = 512
MLA_KEY_CHUNK = 256
MLA_LAG_LIMIT = 64.0
VMEM_LIMIT = 56 * 1024 * 1024
PROJ_VMEM_LIMIT = 61 * 1024 * 1024


def _cparams(sem, vmem_limit=VMEM_LIMIT, **kw):
    return pltpu.CompilerParams(dimension_semantics=sem, vmem_limit_bytes=vmem_limit, **kw)


def _mod_kernel(cc_ref, w_ref, b_ref, o_ref):
    cc = cc_ref[...]
    a = cc * jax.nn.sigmoid(cc)
    o_ref[...] = jnp.dot(a, w_ref[...], preferred_element_type=F32, precision=HIGHEST) + b_ref[...]


def _modulation(cc, w_mod, b_mod):
    depth = w_mod.shape[0]
    nj = 6
    return pl.pallas_call(
        _mod_kernel,
        grid=(depth, nj),
        in_specs=[
            pl.BlockSpec((8, D_MODEL), lambda l, j: (0, 0)),
            pl.BlockSpec((None, D_MODEL, D_MODEL), lambda l, j: (l, 0, j)),
            pl.BlockSpec((None, 1, D_MODEL), lambda l, j: (l, 0, j)),
        ],
        out_specs=pl.BlockSpec((None, 8, D_MODEL), lambda l, j: (l, 0, j)),
        out_shape=jax.ShapeDtypeStruct((depth, 8, 6 * D_MODEL), F32),
        compiler_params=_cparams(("arbitrary", "arbitrary")),
        name="adaln_mod",
    )(cc, w_mod, b_mod.reshape(depth, 1, 6 * D_MODEL))


def _rms(x, g):
    return x * lax.rsqrt(jnp.mean(x * x, axis=-1, keepdims=True) + EPS) * g


def _rope(v, tab_ref, half):
    n = v.shape[-1]
    cos = tab_ref[:, 0:128]
    s_lo = tab_ref[:, 128:256]
    s_hi = tab_ref[:, 256:384]
    return v * cos + pltpu.roll(v, n - half, 1) * s_lo + pltpu.roll(v, half, 1) * s_hi


def _store_v_slabs(ref, vt, heads):
    ones = jnp.ones((V_SLAB - HEAD_DIM, vt.shape[1]), BF16)
    for hd in range(heads):
        ref[V_SLAB * hd:V_SLAB * hd + HEAD_DIM, :] = vt[HEAD_DIM * hd:HEAD_DIM * (hd + 1), :].astype(BF16)
        ref[V_SLAB * hd + HEAD_DIM:V_SLAB * (hd + 1), :] = ones


def _proj_kernel(x_ref, mod_ref, g1_ref, w_ref, wuq_ref, wukv_ref, gq_ref, gkv_ref, lbp_ref, taba_ref, tabc_ref,
                 qa_ref, ka_ref, va_ref, bqig_ref, gates_ref, qc_ref, kc_ref, vc_ref, gl_ref):
    x = x_ref[...]
    h = _rms(x, g1_ref[...]) * (1.0 + mod_ref[1:2, :]) + mod_ref[0:1, :]
    hb = h.astype(BF16)

    def seg(off, width):
        return jnp.dot(hb, w_ref[:, off:off + width], preferred_element_type=F32)

    aq = seg(OFF_AQ, 512) * (HEAD_DIM ** -0.5 * LOG2E)
    for j in range(4):
        qa_ref[128 * j:128 * (j + 1), :] = _rope(aq[:, 128 * j:128 * (j + 1)], taba_ref, 32).T.astype(BF16)
    ka_ref[...] = _rope(seg(OFF_AK, 128), taba_ref, 32).astype(BF16)
    _store_v_slabs(va_ref, seg(OFF_AV, 128).T, A_KV_HEADS)

    bqig_ref[:, 0:512] = seg(OFF_BQ, 512).astype(BF16)
    bqig_ref[:, 512:1024] = seg(OFF_BI, 512).astype(BF16)
    bqig_ref[:, 1024:1536] = seg(OFF_BG, 512).astype(BF16)
    for d, off in enumerate((OFF_BZF, OFF_BZB)):
        z = seg(off, 512)
        log_lb = lbp_ref[0:1, 512 * d:512 * (d + 1)]
        log1m_lb = lbp_ref[1:2, 512 * d:512 * (d + 1)]
        one_m_lb = lbp_ref[2:3, 512 * d:512 * (d + 1)]
        e = jnp.exp(-jnp.abs(z))
        log_sig = jnp.minimum(z, 0.0) - jnp.log(1.0 + e)
        b = log1m_lb + log_sig
        mx = jnp.maximum(log_lb, b)
        logf = mx + jnp.log(1.0 + jnp.exp(-jnp.abs(log_lb - b)))
        r = 1.0 / (1.0 + e)
        key = one_m_lb * jnp.where(z >= 0.0, e * r, r)
        gates_ref[:, 512 * d:512 * (d + 1)] = logf
        gates_ref[:, 1024 + 512 * d:1024 + 512 * (d + 1)] = key

    cq = _rms(seg(OFF_CQ, C_Q_LORA), gq_ref[...]).astype(BF16)
    qh = jnp.dot(cq, wuq_ref[...], preferred_element_type=F32) * ((C_NOPE + C_ROPE) ** -0.5 * LOG2E)
    ckv = _rms(seg(OFF_CKV, C_KV_LORA), gkv_ref[...]).astype(BF16)
    kvh = jnp.dot(ckv, wukv_ref[...], preferred_element_type=F32)
    kr = _rope(seg(OFF_CKR, 128), tabc_ref, 16)
    for j in range(C_HEADS):
        sl = slice(C_HEAD_PAD * j, C_HEAD_PAD * (j + 1))
        qc_ref[sl, :] = _rope(qh[:, sl], tabc_ref, 16).T.astype(BF16)
        kc_ref[:, sl] = (kvh[:, sl] + kr).astype(BF16)
    _store_v_slabs(vc_ref, kvh[:, C_HEADS * C_HEAD_PAD:].T, C_HEADS)

    for j in range(6):
        gl_ref[:, 512 * j:512 * (j + 1)] = seg(OFF_GL + 512 * j, 512).astype(BF16)


def _projection(x, mod, g1, w_in_p, wuq_p, wukv_p, gq, gkv, lbp, taba, tabc, dims):
    B, S, L = dims
    n = x.shape[0]
    tm = PROJ_BLOCK
    nlat = B * S // tm
    spb = S // tm

    def row(i):
        return (i, 0)

    def mod_row(i):
        return (jnp.minimum(i // spb, B), 0, 0)

    def tab_row(i):
        return (jnp.where(i < nlat, i % spb, spb), 0)

    const = lambda i: (0, 0)
    resident = functools.partial(pl.BlockSpec, index_map=const, pipeline_mode=pl.Buffered(1))
    widths = (512, 128, A_KV_HEADS * V_SLAB, 1536, 2048, 1024, 1024, C_HEADS * V_SLAB, 3072)
    dtypes = (BF16, BF16, BF16, BF16, F32, BF16, BF16, BF16, BF16)
    transposed = (0, 2, 5, 7)
    return pl.pallas_call(
        _proj_kernel,
        grid=(n // tm,),
        in_specs=[
            pl.BlockSpec((tm, D_MODEL), row),
            pl.BlockSpec((None, 6, D_MODEL), mod_row),
            pl.BlockSpec((1, D_MODEL), const),
            resident((D_MODEL, IN_W_PACKED)),
            resident(wuq_p.shape),
            resident(wukv_p.shape),
            pl.BlockSpec((1, C_Q_LORA), const),
            pl.BlockSpec((1, C_KV_LORA), const),
            pl.BlockSpec((8, 2 * B_W), const),
            pl.BlockSpec((tm, 384), tab_row),
            pl.BlockSpec((tm, 384), tab_row),
        ],
        out_specs=[pl.BlockSpec((w, tm), lambda i: (0, i)) if k in transposed else pl.BlockSpec((tm, w), row)
                   for k, w in enumerate(widths)],
        out_shape=[jax.ShapeDtypeStruct((w, n) if k in transposed else (n, w), dt)
                   for k, (w, dt) in enumerate(zip(widths, dtypes))],
        compiler_params=_cparams(("parallel",), PROJ_VMEM_LIMIT),
        name="in_proj",
    )(x, mod, g1, w_in_p, wuq_p, wukv_p, gq, gkv, lbp, taba, tabc)


def _gqa_ctx_kernel(qt_ref, kx_ref, vx_ref, sink_ref, _, o_ref):
    for g in range(A_KV_HEADS):
        qg = jnp.concatenate([qt_ref[HEAD_DIM * hd:HEAD_DIM * (hd + 1), :]
                              for hd in range(A_GROUP * g, A_GROUP * (g + 1))], axis=1)
        s = jnp.dot(kx_ref[:, HEAD_DIM * g:HEAD_DIM * (g + 1)], qg, preferred_element_type=F32)
        m = jnp.maximum(jnp.max(s, axis=0, keepdims=True), sink_ref[g])
        p = jnp.exp2(s - m).astype(BF16)
        pv = jnp.dot(vx_ref[V_SLAB * g:V_SLAB * (g + 1), :], p, preferred_element_type=F32)
        o = pv[0:HEAD_DIM, :] / (pv[HEAD_DIM:HEAD_DIM + 1, :] + jnp.exp2(sink_ref[g] - m))
        for hh in range(A_GROUP):
            hd = A_GROUP * g + hh
            o_ref[HEAD_DIM * hd:HEAD_DIM * (hd + 1), :] = o[:, A_QBLOCK * hh:A_QBLOCK * (hh + 1)].astype(BF16)


def _gqa_lat_kernel(qt_ref, kp_ref, kc_ref, kn_ref, kx_ref, vp_ref, vc_ref, vn_ref, vx_ref, sink_ref, o_ref,
                    bias_ref, s0_ref, s1_ref, p0_ref, p1_ref, *, seq):
    j = pl.program_id(1)
    nband = 4 * A_BLOCK
    nsb, nk = bias_ref.shape[0], bias_ref.shape[1]
    step_q = nsb * A_QBLOCK
    rows = lax.broadcasted_iota(jnp.int32, (nk, A_QBLOCK), 0)
    cols = lax.broadcasted_iota(jnp.int32, (nk, A_QBLOCK), 1)
    for qb in range(nsb):
        qpos = j * step_q + qb * A_QBLOCK + cols
        kpos = j * step_q + qb * A_QBLOCK - A_BLOCK + rows
        valid = ((kpos >= 0) & (kpos < seq) & (jnp.abs(qpos - kpos) <= A_WINDOW)) | (rows >= nband)
        bias_ref[qb] = jnp.where(valid, 0.0, -jnp.inf)
    k = jnp.concatenate([kp_ref[...], kc_ref[...], kn_ref[...]], axis=0)
    vt = jnp.concatenate([vp_ref[...], vc_ref[...], vn_ref[...]], axis=1)
    s_refs, p_refs = (s0_ref, s1_ref), (p0_ref, p1_ref)
    nchunk = nk // A_QBLOCK
    krows = lambda r: slice(A_QBLOCK * r, A_QBLOCK * (r + 1))
    items = [(qb, g) for qb in range(nsb) for g in range(A_KV_HEADS)]
    n = len(items)
    smax, m, pv = [None] * n, [None] * n, [None] * n

    def keys(i, r):
        qb, g = items[i]
        if A_QBLOCK * (r + 1) <= nband:
            sl = slice(A_QBLOCK * (qb + r), A_QBLOCK * (qb + r + 1))
            return k[sl, HEAD_DIM * g:HEAD_DIM * (g + 1)], vt[V_SLAB * g:V_SLAB * (g + 1), sl]
        return kx_ref[:, HEAD_DIM * g:HEAD_DIM * (g + 1)], vx_ref[V_SLAB * g:V_SLAB * (g + 1), :]

    def stage_scores(i, r):
        qb, g = items[i]
        qcols = slice(A_QBLOCK * qb, A_QBLOCK * (qb + 1))
        qg = jnp.concatenate([qt_ref[HEAD_DIM * hd:HEAD_DIM * (hd + 1), qcols]
                              for hd in range(A_GROUP * g, A_GROUP * (g + 1))], axis=1)
        bias = bias_ref[qb, krows(r), :]
        sc = (jnp.dot(keys(i, r)[0], qg, preferred_element_type=F32) + jnp.concatenate([bias] * A_GROUP, axis=1))
        s_refs[i % 2][krows(r), :] = sc
        cm = jnp.max(sc, axis=0, keepdims=True)
        smax[i] = cm if smax[i] is None else jnp.maximum(smax[i], cm)

    def stage_exp(i, r):
        p_refs[i % 2][krows(r), :] = jnp.exp2(s_refs[i % 2][krows(r), :] - m[i]).astype(BF16)

    def stage_pv(i, r):
        qb, g = items[i]
        t = jnp.dot(keys(i, r)[1], p_refs[i % 2][krows(r), :], preferred_element_type=F32)
        pv[i] = t if pv[i] is None else pv[i] + t
        if r == nchunk - 1:
            denom = pv[i][HEAD_DIM:HEAD_DIM + 1, :] + jnp.exp2(sink_ref[g] - m[i])
            o = pv[i][0:HEAD_DIM, :] / denom
            for hh in range(A_GROUP):
                hd = A_GROUP * g + hh
                o_ref[HEAD_DIM * hd:HEAD_DIM * (hd + 1), A_QBLOCK * qb:A_QBLOCK * (qb + 1)] = (
                    o[:, A_QBLOCK * hh:A_QBLOCK * (hh + 1)].astype(BF16))

    for r in range(nchunk):
        stage_scores(0, r)
    for i in range(n + 1):
        if i < n:
            m[i] = jnp.maximum(smax[i], sink_ref[items[i][1]])
        for r in range(nchunk):
            if i + 1 < n:
                stage_scores(i + 1, r)
            if i < n:
                stage_exp(i, r)
            if i >= 1:
                stage_pv(i - 1, r)


def _window_gqa(qat, ka, vat, sink, dims, need_ctx):
    B, S, L = dims
    n = ka.shape[0]
    assert L == A_QBLOCK
    nb = S // A_BLOCK
    nqb = S // A_QBLOCK
    nk = 4 * A_BLOCK + L
    width = A_GROUP * A_QBLOCK
    vs = A_KV_HEADS * V_SLAB
    step_q = min(A_STEP_BLOCKS * A_QBLOCK, S)
    nstep = S // step_q
    bps = step_q // A_BLOCK
    ctx_row = lambda b, j: (B * S // L + b, 0)
    ctx_col = lambda b, j: (0, B * S // L + b)
    sink_spec = pl.BlockSpec((A_KV_HEADS, 1, width), lambda b, j: (0, 0, 0))
    stage = [pltpu.VMEM((nk, width), F32), pltpu.VMEM((nk, width), F32),
             pltpu.VMEM((nk, width), BF16), pltpu.VMEM((nk, width), BF16)]
    prev_blk = lambda b, j: b * nb + jnp.maximum(bps * j - 1, 0)
    next_blk = lambda b, j: b * nb + jnp.minimum(bps * (j + 1), nb - 1)
    o_lat = pl.pallas_call(
        functools.partial(_gqa_lat_kernel, seq=S),
        grid=(B, nstep),
        in_specs=[
            pl.BlockSpec((512, step_q), lambda b, j: (0, b * nstep + j)),
            pl.BlockSpec((A_BLOCK, 128), lambda b, j: (prev_blk(b, j), 0)),
            pl.BlockSpec((step_q, 128), lambda b, j: (b * nstep + j, 0)),
            pl.BlockSpec((A_BLOCK, 128), lambda b, j: (next_blk(b, j), 0)),
            pl.BlockSpec((L, 128), ctx_row),
            pl.BlockSpec((vs, A_BLOCK), lambda b, j: (0, prev_blk(b, j))),
            pl.BlockSpec((vs, step_q), lambda b, j: (0, b * nstep + j)),
            pl.BlockSpec((vs, A_BLOCK), lambda b, j: (0, next_blk(b, j))),
            pl.BlockSpec((vs, L), ctx_col),
            sink_spec,
        ],
        out_specs=pl.BlockSpec((512, step_q), lambda b, j: (0, b * nstep + j)),
        out_shape=jax.ShapeDtypeStruct((512, n if need_ctx else B * S), BF16),
        scratch_shapes=[pltpu.VMEM((step_q // A_QBLOCK, nk, A_QBLOCK), F32)] + stage,
        compiler_params=_cparams(("parallel", "parallel")),
        name="window_gqa",
    )(qat, ka, ka, ka, ka, vat, vat, vat, vat, sink)
    if not need_ctx:
        return o_lat
    return pl.pallas_call(
        _gqa_ctx_kernel,
        grid=(B,),
        in_specs=[pl.BlockSpec((512, L), lambda b: (0, B * S // L + b)),
                  pl.BlockSpec((L, 128), lambda b: (B * S // L + b, 0)),
                  pl.BlockSpec((vs, L), lambda b: (0, B * S // L + b)),
                  pl.BlockSpec((A_KV_HEADS, 1, width), lambda b: (0, 0, 0)),
                  pl.BlockSpec(memory_space=pl.ANY)],
        out_specs=pl.BlockSpec((512, L), lambda b: (0, B * S // L + b)),
        out_shape=jax.ShapeDtypeStruct((512, n), BF16),
        input_output_aliases={4: 0},
        compiler_params=_cparams(("parallel",)),
        name="window_gqa_context",
    )(qat, ka, vat, sink, o_lat)


def _hgrn_scan(g_ref, r0, reverse):
    C = B_CHUNK
    rows = lax.broadcasted_iota(jnp.int32, (C, C), 0)
    cols = lax.broadcasted_iota(jnp.int32, (C, C), 1)
    causal = (rows <= cols) if reverse else (rows >= cols)
    g = g_ref[r0:r0 + C, :]
    tri = jnp.where(causal, 1.0, 0.0).astype(BF16)
    hi = g.astype(BF16)
    r1 = g - hi.astype(F32)
    mid = r1.astype(BF16)
    lo = (r1 - mid.astype(F32)).astype(BF16)
    bc = (jnp.dot(tri, hi, preferred_element_type=F32) + jnp.dot(tri, mid, preferred_element_type=F32)
          + jnp.dot(tri, lo, preferred_element_type=F32))
    tot = jnp.sum(g, axis=0, keepdims=True)
    mid = C // 2 if reverse else C // 2 - 1
    rho = bc[mid:mid + 1, :]
    return causal, bc, tot, rho


def _hgrn_prep(q_ref, v_ref, k_ref, r0, scan, factored):
    causal, bc, tot, rho = scan
    rs = slice(r0, r0 + B_CHUNK)
    q = q_ref[rs, :].astype(F32)
    key = k_ref[rs, :]
    v = v_ref[rs, :]
    qe = (q * jnp.exp(bc - rho)).astype(BF16) if factored else None
    ke = (key * jnp.exp(rho - bc)).astype(BF16) if factored else None
    qs = (q * jnp.exp(bc)).astype(BF16)
    ks = (key * jnp.exp(tot - bc)).astype(BF16)
    dec = jnp.exp(tot)
    return rs, causal, v, qe, ke, qs, ks, dec


def _hgrn_intra_exact(q_ref, v_ref, k_ref, r0, scan, reverse, ones_ref, bcs_ref, vs_ref):
    C = B_CHUNK
    causal, bc, tot, rho = scan
    q = q_ref[r0:r0 + C, :].astype(F32)
    bcs_ref[...] = bc
    vs_ref[...] = v_ref[r0:r0 + C, :].astype(F32)
    trow = lax.broadcasted_iota(jnp.int32, (C, 1), 0)

    def body(s, acc):
        later = (trow <= s) if reverse else (trow >= s)
        w = jnp.where(later, jnp.exp(jnp.minimum(bc - bcs_ref[pl.ds(s, 1), :], 0.0)), 0.0)
        att = _group_sum(q * w * k_ref[pl.ds(r0 + s, 1), :], ones_ref)
        return acc + att * vs_ref[pl.ds(s, 1), :]

    return lax.fori_loop(0, C, body, jnp.zeros((C, B_W), F32))


def _hgrn_heads(prep, o_ref, st_ref, intra=None):
    rs, causal, v, qe, ke, qs, ks, dec = prep
    for hd in range(B_HEADS):
        sl = slice(B_DK * hd, B_DK * (hd + 1))
        st = st_ref[hd]
        o = lax.dot_general(qs[:, sl], st.astype(BF16), (((1,), (1,)), ((), ())), preferred_element_type=F32)
        if intra is None:
            att = lax.dot_general(qe[:, sl], ke[:, sl], (((1,), (1,)), ((), ())), preferred_element_type=F32)
            att = jnp.where(causal, att, 0.0).astype(BF16)
            o = o + jnp.dot(att, v[:, sl], preferred_element_type=F32)
        else:
            o = o + intra[:, sl]
        o_ref[rs, sl] = o
        upd = lax.dot_general(v[:, sl], ks[:, sl], (((0,), (0,)), ((), ())), preferred_element_type=F32)
        st_ref[hd] = st * dec[:, sl] + upd


def _hgrn_kernel(qf_ref, vf_ref, gf_ref, kf_ref, qb_ref, vb_ref, gb_ref, kb_ref, ones_ref, of_ref, ob_ref,
                 stf_ref, stb_ref, bcs_ref, vs_ref):
    @pl.when(pl.program_id(1) == 0)
    def _():
        stf_ref[...] = jnp.zeros_like(stf_ref)
        stb_ref[...] = jnp.zeros_like(stb_ref)

    nchunk = qf_ref.shape[0] // B_CHUNK
    chunks = []
    for i in range(nchunk):
        chunks.append((qf_ref, vf_ref, gf_ref, kf_ref, of_ref, stf_ref, B_CHUNK * i, False))
        chunks.append((qb_ref, vb_ref, gb_ref, kb_ref, ob_ref, stb_ref, B_CHUNK * (nchunk - 1 - i), True))
    half = B_CHUNK // 2
    spread = None
    for (_, _, g_ref, _, _, _, r0, _) in chunks:
        for h0 in (r0, r0 + half):
            tot = jnp.sum(jnp.abs(g_ref[h0:h0 + half, :]), axis=0, keepdims=True)
            spread = tot if spread is None else jnp.maximum(spread, tot)
    factorable = jnp.max(spread) <= B_EXP_LIMIT

    @pl.when(factorable)
    def _():
        preps = [_hgrn_prep(q_ref, v_ref, k_ref, r0, _hgrn_scan(g_ref, r0, rev), True)
                 for (q_ref, v_ref, g_ref, k_ref, _, _, r0, rev) in chunks]
        for prep, (_, _, _, _, o_ref, st_ref, _, _) in zip(preps, chunks):
            _hgrn_heads(prep, o_ref, st_ref)

    @pl.when(jnp.logical_not(factorable))
    def _():
        for (q_ref, v_ref, g_ref, k_ref, o_ref, st_ref, r0, rev) in chunks:
            scan = _hgrn_scan(g_ref, r0, rev)
            intra = _hgrn_intra_exact(q_ref, v_ref, k_ref, r0, scan, rev, ones_ref, bcs_ref, vs_ref)
            _hgrn_heads(_hgrn_prep(q_ref, v_ref, k_ref, r0, scan, False), o_ref, st_ref, intra)


def _hgrn2_scan(bqig, gates, ones, dims):
    B, S, L = dims
    n = bqig.shape[0]
    T = B_BLOCK
    assert L == T
    ns = S // T

    def fwd_blk(b, c):
        return jnp.where(c == 0, B * ns + b, b * ns + c - 1)

    def bwd_blk(b, c):
        return jnp.where(c == 0, B * ns + b, b * ns + ns - c)

    def specs(blk, d):
        return [pl.BlockSpec((T, B_W), lambda b, c: (blk(b, c), 0)),
                pl.BlockSpec((T, B_W), lambda b, c: (blk(b, c), 1)),
                pl.BlockSpec((T, B_W), lambda b, c: (blk(b, c), d)),
                pl.BlockSpec((T, B_W), lambda b, c: (blk(b, c), 2 + d))]

    return pl.pallas_call(
        _hgrn_kernel,
        grid=(B, ns + 1),
        in_specs=specs(fwd_blk, 0) + specs(bwd_blk, 1) + [pl.BlockSpec((B_W, B_W), lambda b, c: (0, 0))],
        out_specs=[pl.BlockSpec((T, B_W), lambda b, c: (fwd_blk(b, c), 0)),
                   pl.BlockSpec((T, B_W), lambda b, c: (bwd_blk(b, c), 0))],
        out_shape=[jax.ShapeDtypeStruct((n, B_W), F32), jax.ShapeDtypeStruct((n, B_W), F32)],
        scratch_shapes=[pltpu.VMEM((B_HEADS, B_DK, B_DK), F32), pltpu.VMEM((B_HEADS, B_DK, B_DK), F32),
                        pltpu.VMEM((B_CHUNK, B_W), F32), pltpu.VMEM((B_CHUNK, B_W), F32)],
        compiler_params=_cparams(("parallel", "arbitrary")),
        name="hgrn2_scan",
    )(bqig, bqig, gates, gates, bqig, bqig, gates, gates, ones)


def _mla_kernel(qt_ref, k_ref, vt_ref, *rest, with_ctx):
    if with_ctx:
        kx_ref, vxt_ref, o_ref, m_ref, acc_ref, *bufs = rest
    else:
        _, o_ref, m_ref, acc_ref, *bufs = rest
    s_refs, p_refs = bufs[0:2], bufs[2:4]
    kstep = pl.program_id(2)
    tq = qt_ref.shape[1]

    def kv_pass(k_ref, vt_ref, first):
        nkeys = k_ref.shape[0]
        nchunk = nkeys // MLA_KEY_CHUNK
        krows = lambda r: slice(MLA_KEY_CHUNK * r, MLA_KEY_CHUNK * (r + 1))

        group = s_refs[0].shape[1]
        items = [(c, hd) for c in range(tq // group) for hd in range(C_HEADS)]
        n = len(items)
        smax = [None] * n
        m_new = [None] * n
        alpha = [None] * n
        pv = [None] * n
        cols = lambda i: slice(group * items[i][0], group * (items[i][0] + 1))

        def stage_scores(i, r):
            hd = items[i][1]
            sl = slice(C_HEAD_PAD * hd, C_HEAD_PAD * (hd + 1))
            sc = jnp.dot(k_ref[krows(r), sl], qt_ref[sl, cols(i)], preferred_element_type=F32)
            s_refs[i % 2][krows(r), :] = sc
            cm = jnp.max(sc, axis=0, keepdims=True)
            smax[i] = cm if smax[i] is None else jnp.maximum(smax[i], cm)

        def stage_stats(i):
            hd = items[i][1]
            if first:
                m_new[i] = smax[i]
            else:
                m_old = m_ref[hd, :, cols(i)]
                m_new[i] = jnp.maximum(m_old, smax[i])
                alpha[i] = jnp.exp2(m_old - m_new[i])
            m_ref[hd, :, cols(i)] = m_new[i]

        def stage_exp(i, r):
            p_refs[i % 2][krows(r), :] = jnp.exp2(s_refs[i % 2][krows(r), :] - m_new[i]).astype(BF16)

        def stage_pv(i, r):
            hd = items[i][1]
            t = jnp.dot(vt_ref[V_SLAB * hd:V_SLAB * (hd + 1), krows(r)], p_refs[i % 2][krows(r), :],
                        preferred_element_type=F32)
            pv[i] = t if pv[i] is None else pv[i] + t
            if r == nchunk - 1:
                acc_ref[hd, :, cols(i)] = pv[i] if first else alpha[i] * acc_ref[hd, :, cols(i)] + pv[i]

        for r in range(nchunk):
            stage_scores(0, r)
        for i in range(n + 1):
            if i < n:
                stage_stats(i)
            for r in range(nchunk):
                if i + 1 < n:
                    stage_scores(i + 1, r)
                if i < n:
                    stage_exp(i, r)
                if i >= 1:
                    stage_pv(i - 1, r)

    def kv_pass_lagged(k_ref, vt_ref, tmp_ref):
        nkeys = k_ref.shape[0]
        nchunk = nkeys // MLA_KEY_CHUNK
        krows = lambda r: slice(MLA_KEY_CHUNK * r, MLA_KEY_CHUNK * (r + 1))
        group = p_refs[0].shape[1]
        items = [(c, hd) for c in range(tq // group) for hd in range(C_HEADS)]
        n = len(items)
        cols = lambda i: slice(group * items[i][0], group * (items[i][0] + 1))
        base, smax, pv = [None] * n, [None] * n, [None] * n

        def stage_scores(i, r):
            hd = items[i][1]
            sl = slice(C_HEAD_PAD * hd, C_HEAD_PAD * (hd + 1))
            if base[i] is None:
                base[i] = m_ref[hd, :, cols(i)]
            sc = jnp.dot(k_ref[krows(r), sl], qt_ref[sl, cols(i)], preferred_element_type=F32)
            p_refs[i % 2][krows(r), :] = jnp.exp2(sc - base[i]).astype(BF16)
            cm = jnp.max(sc, axis=0, keepdims=True)
            smax[i] = cm if smax[i] is None else jnp.maximum(smax[i], cm)

        def stage_pv(i, r):
            hd = items[i][1]
            t = jnp.dot(vt_ref[V_SLAB * hd:V_SLAB * (hd + 1), krows(r)], p_refs[i % 2][krows(r), :],
                        preferred_element_type=F32)
            pv[i] = t if pv[i] is None else pv[i] + t
            if r == nchunk - 1:
                tmp_ref[hd, :, cols(i)] = pv[i]

        for r in range(nchunk):
            stage_scores(0, r)
        for i in range(n):
            for r in range(nchunk):
                if i + 1 < n:
                    stage_scores(i + 1, r)
                stage_pv(i, r)
        excess = smax[0] - base[0]
        for i in range(1, n):
            excess = jnp.maximum(excess, smax[i] - base[i])
        return jnp.max(excess)

    if with_ctx:
        @pl.when(kstep == 0)
        def _():
            kv_pass(kx_ref, vxt_ref, True)

        tmp_ref = bufs[4]
        safe = kv_pass_lagged(k_ref, vt_ref, tmp_ref) <= MLA_LAG_LIMIT

        @pl.when(safe)
        def _():
            for hd in range(C_HEADS):
                acc_ref[hd] = acc_ref[hd] + tmp_ref[hd]

        @pl.when(jnp.logical_not(safe))
        def _():
            kv_pass(k_ref, vt_ref, False)
    else:
        kv_pass(k_ref, vt_ref, True)

    @pl.when(kstep == pl.num_programs(2) - 1)
    def _():
        for hd in range(C_HEADS):
            o_ref[C_V * hd:C_V * (hd + 1), :] = (acc_ref[hd, 0:C_V, :] / acc_ref[hd, C_V:C_V + 1, :]).astype(BF16)


def _mla_attention(qct, kc, vct, dims, need_ctx):
    B, S, L = dims
    n = kc.shape[0]
    tq = min(MLA_Q_BLOCK, S)
    tk = min(MLA_K_BLOCK, S)
    nq, nk = S // tq, S // tk
    hw = C_HEADS * C_HEAD_PAD
    vw = C_HEADS * C_V
    vs = C_HEADS * V_SLAB
    scratch = lambda t, nkeys: [
        pltpu.VMEM((C_HEADS, 1, t), F32), pltpu.VMEM((C_HEADS, V_SLAB, t), F32),
        pltpu.VMEM((nkeys, min(t, MLA_COL_GROUP)), F32), pltpu.VMEM((nkeys, min(t, MLA_COL_GROUP)), F32),
        pltpu.VMEM((nkeys, min(t, MLA_COL_GROUP)), BF16), pltpu.VMEM((nkeys, min(t, MLA_COL_GROUP)), BF16)]
    ctx_row = lambda b, i, k: (B * S // L + b, 0)
    ctx_col = lambda b, i, k: (0, B * S // L + b)
    o_lat = pl.pallas_call(
        functools.partial(_mla_kernel, with_ctx=True),
        grid=(B, nq, nk),
        in_specs=[
            pl.BlockSpec((hw, tq), lambda b, i, k: (0, b * nq + i)),
            pl.BlockSpec((tk, hw), lambda b, i, k: (b * nk + k, 0)),
            pl.BlockSpec((vs, tk), lambda b, i, k: (0, b * nk + k)),
            pl.BlockSpec((L, hw), ctx_row),
            pl.BlockSpec((vs, L), ctx_col),
        ],
        out_specs=pl.BlockSpec((vw, tq), lambda b, i, k: (0, b * nq + i)),
        out_shape=jax.ShapeDtypeStruct((vw, n if need_ctx else B * S), BF16),
        scratch_shapes=scratch(tq, tk) + [pltpu.VMEM((C_HEADS, V_SLAB, tq), F32)],
        compiler_params=_cparams(("parallel", "parallel", "arbitrary")),
        name="mla_latent",
    )(qct, kc, vct, kc, vct)
    if not need_ctx:
        return o_lat
    return pl.pallas_call(
        functools.partial(_mla_kernel, with_ctx=False),
        grid=(B, 1, 1),
        in_specs=[
            pl.BlockSpec((hw, L), ctx_col),
            pl.BlockSpec((L, hw), ctx_row),
            pl.BlockSpec((vs, L), ctx_col),
            pl.BlockSpec(memory_space=pl.ANY),
        ],
        out_specs=pl.BlockSpec((vw, L), ctx_col),
        out_shape=jax.ShapeDtypeStruct((vw, n), BF16),
        scratch_shapes=scratch(L, L),
        input_output_aliases={3: 0},
        compiler_params=_cparams(("parallel", "arbitrary", "arbitrary")),
        name="mla_context",
    )(qct, kc, vct, o_lat)


def _group_sum(x, ones_ref):
    hi = x.astype(BF16)
    lo = (x - hi.astype(F32)).astype(BF16)
    return (jnp.dot(hi, ones_ref[...], preferred_element_type=F32)
            + jnp.dot(lo, ones_ref[...], preferred_element_type=F32))


def _merge_kernel(x_ref, oa_ref, of_ref, ob_ref, bg_ref, oc_ref, gl_ref, wbr_ref, wout_ref, gn_ref, mod_ref,
                  g2_ref, ones_ref, xo_ref, h2_ref):
    ob = of_ref[...] + ob_ref[...]
    ms = _group_sum(ob * ob, ones_ref) * (1.0 / B_DK)
    obn = ob * lax.rsqrt(ms + EPS) * gn_ref[...]
    bg = bg_ref[...].astype(F32)
    bb = (obn * (bg * jax.nn.sigmoid(bg))).astype(BF16)
    branches = ((oa_ref[...], 0), (bb, 1), (oc_ref[...], 0))
    y = None
    for nbr, (br, axis) in enumerate(branches):
        gate = jax.nn.sigmoid(gl_ref[:, D_MODEL * nbr:D_MODEL * (nbr + 1)].astype(F32))
        t = gate * lax.dot_general(br, wbr_ref[nbr], (((axis,), (0,)), ((), ())), preferred_element_type=F32)
        y = t if y is None else y + t
    upd = jnp.dot(y.astype(BF16), wout_ref[...], preferred_element_type=F32)
    xn = x_ref[...] + mod_ref[2:3, :] * upd
    xo_ref[...] = xn
    h2 = _rms(xn, g2_ref[...]) * (1.0 + mod_ref[4:5, :]) + mod_ref[3:4, :]
    h2_ref[...] = h2.astype(BF16)


def _merge(x, oa, ohg, bqig, oc, gl, wbr, wout, gn, mod, g2, ones, dims, need_ctx):
    B, S, L = dims
    n = x.shape[0]
    tm = PROJ_BLOCK
    spb = S // tm
    nblk = (n if need_ctx else B * S) // tm
    row = lambda i: (i, 0)
    const2 = lambda i: (0, 0)
    return pl.pallas_call(
        _merge_kernel,
        grid=(nblk,),
        in_specs=[
            pl.BlockSpec((tm, D_MODEL), row),
            pl.BlockSpec((512, tm), lambda i: (0, i)),
            pl.BlockSpec((tm, B_W), row),
            pl.BlockSpec((tm, B_W), row),
            pl.BlockSpec((tm, B_W), lambda i: (i, 2)),
            pl.BlockSpec((512, tm), lambda i: (0, i)),
            pl.BlockSpec((tm, N_BRANCH * D_MODEL), row),
            pl.BlockSpec((N_BRANCH, BRANCH_W, D_MODEL), lambda i: (0, 0, 0)),
            pl.BlockSpec((D_MODEL, D_MODEL), const2),
            pl.BlockSpec((1, B_W), const2),
            pl.BlockSpec((None, 6, D_MODEL), lambda i: (jnp.minimum(i // spb, B), 0, 0)),
            pl.BlockSpec((1, D_MODEL), const2),
            pl.BlockSpec((B_W, B_W), const2),
        ],
        out_specs=[pl.BlockSpec((tm, D_MODEL), row), pl.BlockSpec((tm, D_MODEL), row)],
        out_shape=[jax.ShapeDtypeStruct((nblk * tm, D_MODEL), F32), jax.ShapeDtypeStruct((nblk * tm, D_MODEL), BF16)],
        compiler_params=_cparams(("parallel",)),
        name="branch_merge",
    )(x, oa, ohg[0], ohg[1], bqig, oc, gl, wbr, wout, gn, mod, g2, ones)


def _route_kernel(h_ref, wr_ref, comb_ref, pos_ref, cnt_ref):
    h = h_ref[...]
    T = h.shape[0]
    lane = lax.broadcasted_iota(jnp.int32, (T, ROUTER_W), 1)
    logits = jnp.dot(h, wr_ref[...], preferred_element_type=F32)
    big = jnp.int32(ROUTER_W)
    is_grp = (lane >= N_EXPERTS) & (lane < N_EXPERTS + N_GROUPS)
    gl = jnp.where(is_grp, logits, -jnp.inf)
    gmax = jnp.max(gl, axis=-1, keepdims=True)
    gsel = jnp.min(jnp.where(gl == gmax, lane, big), axis=-1, keepdims=True) - N_EXPERTS
    gw = 1.0 / jnp.sum(jnp.exp(gl - gmax), axis=-1, keepdims=True)
    in_grp = (lane >= gsel * EXPERTS_PER_GROUP) & (lane < (gsel + 1) * EXPERTS_PER_GROUP)
    el = jnp.where(in_grp, logits, -jnp.inf)
    m1 = jnp.max(el, axis=-1, keepdims=True)
    i1 = jnp.min(jnp.where(el == m1, lane, big), axis=-1, keepdims=True)
    el2 = jnp.where(lane == i1, -jnp.inf, el)
    m2 = jnp.max(el2, axis=-1, keepdims=True)
    i2 = jnp.min(jnp.where(el2 == m2, lane, big), axis=-1, keepdims=True)
    e2 = jnp.exp(m2 - m1)
    w1 = gw / (1.0 + e2)
    w2 = gw * e2 / (1.0 + e2)
    comb_ref[...] = jnp.where(lane == i1, w1, 0.0) + jnp.where(lane == i2, w2, 0.0)

    onehot = lane == gsel
    ones = jnp.where(onehot, 1.0, 0.0)
    rows = lax.broadcasted_iota(jnp.int32, (T, T), 0)
    cols = lax.broadcasted_iota(jnp.int32, (T, T), 1)
    earlier = jnp.where(rows > cols, 1.0, 0.0).astype(BF16)
    before = jnp.dot(earlier, ones.astype(BF16), preferred_element_type=F32)
    rank = jnp.sum(jnp.where(onehot, before, 0.0), axis=-1, keepdims=True)
    cnt = jnp.sum(ones, axis=0, keepdims=True)
    padded = jnp.floor((cnt + (MOE_ALIGN - 1)) * (1.0 / MOE_ALIGN)) * MOE_ALIGN
    seg = [jnp.sum(jnp.where(lane[0:1] == g, padded, 0.0), axis=-1, keepdims=True) for g in range(N_GROUPS - 1)]
    start = jnp.where(gsel == 0, 0.0, jnp.where(gsel == 1, seg[0], jnp.where(gsel == 2, seg[0] + seg[1],
                                                                             seg[0] + seg[1] + seg[2])))
    pos_ref[...] = jnp.broadcast_to(start + rank, (T, ROUTER_W))
    cnt_ref[...] = jnp.broadcast_to(cnt, (8, ROUTER_W)).astype(jnp.int32)


def _moe_kernel(cnt_ref, h_ref, x_ref, mod_ref, pos_ref, comb_ref, w13_ref, w2_ref, gf_ref, o_ref,
                pt_ref, xs_ref, cs_ref, ys_ref, *, final_norm):
    i = pl.program_id(0)
    e = pl.program_id(1)
    T = h_ref.shape[0]
    R = xs_ref.shape[0]
    gather = lambda a: lax.dot_general(pt_ref[...], a, (((0,), (0,)), ((), ())), preferred_element_type=F32)

    @pl.when(e == 0)
    def _():
        slot = lax.broadcasted_iota(jnp.int32, (T, R), 1).astype(F32)
        pt_ref[...] = jnp.where(pos_ref[:, 0:1] == slot, 1.0, 0.0).astype(BF16)
        xs_ref[...] = gather(h_ref[...]).astype(BF16)
        comb = comb_ref[...]
        hi = comb.astype(BF16)
        r1 = comb - hi.astype(F32)
        mid = r1.astype(BF16)
        lo = (r1 - mid.astype(F32)).astype(BF16)
        cs_ref[...] = gather(hi) + gather(mid) + gather(lo)
        ys_ref[...] = jnp.zeros_like(ys_ref)

    g = (e * MOE_EXPERTS_PER_STEP) // EXPERTS_PER_GROUP
    cnt = [cnt_ref[i * N_GROUPS + gg] for gg in range(N_GROUPS)]
    seg = [(c + (MOE_ALIGN - 1)) // MOE_ALIGN * MOE_ALIGN for c in cnt]
    start = (jnp.where(g > 0, seg[0], 0) + jnp.where(g > 1, seg[1], 0) + jnp.where(g > 2, seg[2], 0))
    cnt_g = jnp.where(g == 0, cnt[0], jnp.where(g == 1, cnt[1], jnp.where(g == 2, cnt[2], cnt[3])))
    lane = lax.broadcasted_iota(jnp.int32, (MOE_TILE, ROUTER_W), 1)

    def tile(t, carry):
        r0 = pl.multiple_of(start + t * MOE_TILE, MOE_ALIGN)
        xt = xs_ref[pl.ds(r0, MOE_TILE), :]
        ct = cs_ref[pl.ds(r0, MOE_TILE), :]
        acts = []
        for k in range(MOE_EXPERTS_PER_STEP):
            ce = jnp.sum(jnp.where(lane == e * MOE_EXPERTS_PER_STEP + k, ct, 0.0), axis=-1, keepdims=True)
            h13 = jnp.dot(xt, w13_ref[k], preferred_element_type=F32)
            a1 = h13[:, :D_EXPERT]
            acts.append((a1 * jax.nn.sigmoid(a1) * h13[:, D_EXPERT:] * ce).astype(BF16))
        ys_ref[pl.ds(r0, MOE_TILE), :] += jnp.dot(jnp.concatenate(acts, axis=-1), w2_ref[...],
                                                   preferred_element_type=F32)
        return carry

    lax.fori_loop(0, (cnt_g + MOE_TILE - 1) // MOE_TILE, tile, 0)

    @pl.when(e == pl.num_programs(1) - 1)
    def _():
        y = jnp.dot(pt_ref[...], ys_ref[...].astype(BF16), preferred_element_type=F32)
        xn = x_ref[...] + mod_ref[5:6, :] * y
        o_ref[...] = _rms(xn, gf_ref[...]) if final_norm else xn


def _moe(h2, x, mod, wr, w13, w2, g_final, dims, need_ctx, final_norm):
    B, S, L = dims
    n = x.shape[0]
    tm = min(MOE_TOKEN_BLOCK, S, B * L)
    spb = S // tm
    nblk = (n if need_ctx else B * S) // tm
    comb, pos, cnt = pl.pallas_call(
        _route_kernel,
        grid=(nblk,),
        in_specs=[pl.BlockSpec((tm, D_MODEL), lambda i: (i, 0)), pl.BlockSpec((D_MODEL, ROUTER_W), lambda i: (0, 0))],
        out_specs=[pl.BlockSpec((tm, ROUTER_W), lambda i: (i, 0)), pl.BlockSpec((tm, ROUTER_W), lambda i: (i, 0)),
                   pl.BlockSpec((8, ROUTER_W), lambda i: (i, 0))],
        out_shape=[jax.ShapeDtypeStruct((nblk * tm, ROUTER_W), F32), jax.ShapeDtypeStruct((nblk * tm, ROUTER_W), F32),
                   jax.ShapeDtypeStruct((nblk * 8, ROUTER_W), jnp.int32)],
        compiler_params=_cparams(("parallel",)),
        name="moe_route",
    )(h2, wr)
    cnt = cnt.reshape(nblk, 8, ROUTER_W)[:, 0, :N_GROUPS].reshape(nblk * N_GROUPS)

    row = lambda i, e, c: (i, 0)
    eps = MOE_EXPERTS_PER_STEP
    assert MOE_SORT_PAD >= (N_GROUPS - 1) * (MOE_ALIGN - 1) + MOE_TILE - 1 and EXPERTS_PER_GROUP % eps == 0
    slots = tm + MOE_SORT_PAD
    return pl.pallas_call(
        functools.partial(_moe_kernel, final_norm=final_norm),
        grid_spec=pltpu.PrefetchScalarGridSpec(
            num_scalar_prefetch=1,
            grid=(nblk, N_EXPERTS // eps),
            in_specs=[
                pl.BlockSpec((tm, D_MODEL), row),
                pl.BlockSpec((tm, D_MODEL), row),
                pl.BlockSpec((None, 6, D_MODEL), lambda i, e, c: (jnp.minimum(i // spb, B), 0, 0)),
                pl.BlockSpec((tm, ROUTER_W), row),
                pl.BlockSpec((tm, ROUTER_W), row),
                pl.BlockSpec((eps, D_MODEL, 2 * D_EXPERT), lambda i, e, c: (e, 0, 0)),
                pl.BlockSpec((eps * D_EXPERT, D_MODEL), lambda i, e, c: (e, 0)),
                pl.BlockSpec((1, D_MODEL), lambda i, e, c: (0, 0)),
            ],
            out_specs=pl.BlockSpec((tm, D_MODEL), row),
            scratch_shapes=[pltpu.VMEM((tm, slots), BF16), pltpu.VMEM((slots, D_MODEL), BF16),
                            pltpu.VMEM((slots, ROUTER_W), F32), pltpu.VMEM((slots, D_MODEL), F32)],
        ),
        out_shape=jax.ShapeDtypeStruct((nblk * tm, D_MODEL), F32),
        compiler_params=_cparams(("parallel", "arbitrary")),
        name="hier_moe",
    )(cnt, h2, x, mod, pos, comb, w13, w2.reshape(N_EXPERTS * D_EXPERT, D_MODEL), g_final)


def _rope_tables(S, L):
    rows = S // GRID_W
    pos_r = np.repeat(np.arange(rows, dtype=np.float32), GRID_W)
    pos_c = np.tile(np.arange(GRID_W, dtype=np.float32), rows)

    def angles(rot_dim):
        nf = rot_dim // 4
        inv = jnp.asarray(ROPE_BASE, F32) ** (-jnp.arange(nf, dtype=F32) / nf)
        ang = jnp.concatenate([pos_r[:, None] * inv, pos_c[:, None] * inv], axis=-1)
        return jnp.cos(ang), jnp.sin(ang)

    def with_ctx(cos, s_lo, s_hi):
        ident = jnp.concatenate([jnp.ones((PROJ_BLOCK, 128), F32), jnp.zeros((PROJ_BLOCK, 256), F32)], axis=-1)
        return jnp.concatenate([jnp.concatenate([cos, s_lo, s_hi], axis=-1), ident], axis=0)

    cos, sin = angles(HEAD_DIM)
    z = jnp.zeros_like(sin)
    taba = with_ctx(jnp.tile(cos, (1, 4)), jnp.tile(jnp.concatenate([-sin, z], -1), (1, 2)),
                    jnp.tile(jnp.concatenate([z, sin], -1), (1, 2)))
    cos, sin = angles(C_ROPE)
    z = jnp.zeros_like(sin)
    one64, zero64, zero32 = jnp.ones((S, 64), F32), jnp.zeros((S, 64), F32), jnp.zeros((S, 32), F32)
    tabc = with_ctx(jnp.concatenate([one64, cos, cos, one64[:, :32]], -1),
                    jnp.concatenate([zero64, -sin, z, zero32], -1),
                    jnp.concatenate([zero64, z, sin, zero32], -1))
    return taba, tabc


def _pack_w_in(w):
    pad = lambda k: jnp.zeros((w.shape[0], k), w.dtype)
    return jnp.concatenate([w[:, :3712], pad(64), w[:, 3712:3744], pad(32), w[:, 3744:]], axis=-1).astype(BF16)


def _pack_w_uq(w):
    w = w.reshape(C_Q_LORA, C_HEADS, C_NOPE + C_ROPE)
    w = jnp.pad(w, ((0, 0), (0, 0), (0, C_HEAD_PAD - C_NOPE - C_ROPE)))
    return w.reshape(C_Q_LORA, C_HEADS * C_HEAD_PAD).astype(BF16)


def _pack_w_ukv(w):
    w = w.reshape(C_KV_LORA, C_HEADS, C_NOPE + C_V)
    wk = jnp.pad(w[:, :, :C_NOPE], ((0, 0), (0, 0), (0, C_HEAD_PAD - C_NOPE))).reshape(C_KV_LORA, -1)
    wv = w[:, :, C_NOPE:].reshape(C_KV_LORA, -1)
    return jnp.concatenate([wk, wv], axis=-1).astype(BF16)


def kernel(x, c, ctx, c_ctx, w_mod, b_mod, g_norm1, g_norm2, w_in, a_sink, b_lb_logits, b_onorm, c_qnorm, c_kvnorm,
           w_uq, w_ukv, w_br, w_out, w_rg, w_re, w1, w3, w2, g_final):
    B, S, _ = x.shape
    L = ctx.shape[1]
    depth = w_in.shape[0]
    assert L == TOKEN_BLOCK and S % min(MLA_Q_BLOCK, S) == 0 and S % min(MLA_K_BLOCK, S) == 0 and S % GRID_W == 0
    assert S % PROJ_BLOCK == 0 and (B * L) % PROJ_BLOCK == 0
    dims = (B, S, L)

    xs = jnp.concatenate([x.reshape(B * S, D_MODEL), ctx.reshape(B * L, D_MODEL)], axis=0)
    cc = jnp.zeros((8, D_MODEL), F32).at[:B].set(c).at[B].set(c_ctx)
    mod_all = _modulation(cc, w_mod, b_mod).reshape(depth, 8, 6, D_MODEL)

    lb_all = jnp.cumsum(jax.nn.softmax(b_lb_logits.astype(F32), axis=0), axis=0)
    lb_all = (lb_all - lb_all[0:1]).reshape(depth, 1, 2 * B_W)
    lbp_all = jnp.concatenate([jnp.log(lb_all), jnp.log1p(-lb_all), 1.0 - lb_all,
                               jnp.zeros((depth, 5, 2 * B_W), F32)], axis=1)

    taba, tabc = _rope_tables(S, L)
    ones = jnp.kron(jnp.eye(B_HEADS, dtype=F32), jnp.ones((B_DK, B_DK), F32)).astype(BF16)

    for l in range(depth):
        need_ctx = l < depth - 1
        mod = mod_all[l]
        wr = jnp.concatenate([w_re[l], w_rg[l], jnp.zeros((D_MODEL, ROUTER_W - N_EXPERTS - N_GROUPS), F32)],
                             axis=-1).astype(BF16)
        w13 = jnp.concatenate([w1[l], w3[l]], axis=-1).astype(BF16)
        sink = jnp.repeat(a_sink[l].astype(F32).reshape(A_KV_HEADS, 1, A_GROUP) * LOG2E, A_QBLOCK, axis=-1)

        qa, ka, va, bqig, gates, qc, kc, vc, gl = _projection(
            xs, mod, g_norm1[l][None], _pack_w_in(w_in[l]), _pack_w_uq(w_uq[l]), _pack_w_ukv(w_ukv[l]),
            c_qnorm[l][None], c_kvnorm[l][None], lbp_all[l], taba, tabc, dims)
        oa = _window_gqa(qa, ka, va, sink, dims, need_ctx)
        ohg = _hgrn2_scan(bqig, gates, ones, dims)
        oc = _mla_attention(qc, kc, vc, dims, need_ctx)
        xs, h2 = _merge(xs, oa, ohg, bqig, oc, gl, w_br[l].astype(BF16), w_out[l].astype(BF16), b_onorm[l][None],
                        mod, g_norm2[l][None], ones, dims, need_ctx)
        xs = _moe(h2, xs, mod, wr, w13, w2[l].astype(BF16), g_final[None], dims, need_ctx, final_norm=not need_ctx)

    return xs.reshape(B, S, D_MODEL)
```

```python
import functools

import jax
import jax.numpy as jnp
import numpy as np
from jax import lax
from jax.experimental import pallas as pl
from jax.experimental.pallas import tpu as pltpu

F32 = jnp.float32
BF16 = jnp.bfloat16
HIGHEST = lax.Precision.HIGHEST

D_MODEL = 1024
GRID_W = 64
HEAD_DIM = 64
ROPE_BASE = 10000.0
EPS = 1e-6
A_HEADS = 8
A_KV_HEADS = 2
A_GROUP = A_HEADS // A_KV_HEADS
A_WINDOW = 128
A_BLOCK = A_WINDOW
A_QBLOCK = 2 * A_BLOCK
A_STEP_BLOCKS = 4
B_HEADS = 8
B_DK = 64
B_W = B_HEADS * B_DK
B_CHUNK = 64
B_BLOCK = 256
B_EXP_LIMIT = 80.0
C_HEADS = 8
C_NOPE = 64
C_ROPE = 32
C_V = 64
C_Q_LORA = 256
C_KV_LORA = 128
C_HEAD_PAD = 128
N_BRANCH = 3
BRANCH_W = 512
N_GROUPS = 4
EXPERTS_PER_GROUP = 8
N_EXPERTS = N_GROUPS * EXPERTS_PER_GROUP
D_EXPERT = 256
ROUTER_W = 128
V_SLAB = 80
LOG2E = 1.4426950408889634

OFF_AQ, OFF_AK, OFF_AV = 0, 512, 640
OFF_BQ, OFF_BI, OFF_BZF, OFF_BZB, OFF_BG = 768, 1280, 1792, 2304, 2816
OFF_CQ, OFF_CKV, OFF_CKR, OFF_GL = 3328, 3584, 3712, 3840
IN_W_PACKED = OFF_GL + N_BRANCH * D_MODEL

TOKEN_BLOCK = 256
PROJ_BLOCK = 512
MOE_TOKEN_BLOCK = 1024
MOE_EXPERTS_PER_STEP = 4
MOE_TILE = 288
MOE_ALIGN = 16
MOE_SORT_PAD = 384
MLA_Q_BLOCK = 2048
MLA_K_BLOCK = 1024
MLA_COL_GROUP = 512
MLA_KEY_CHUNK = 256
MLA_LAG_LIMIT = 64.0
VMEM_LIMIT = 56 * 1024 * 1024
PROJ_VMEM_LIMIT = 61 * 1024 * 1024


def _cparams(sem, vmem_limit=VMEM_LIMIT, **kw):
    return pltpu.CompilerParams(dimension_semantics=sem, vmem_limit_bytes=vmem_limit, **kw)


def _mod_kernel(cc_ref, w_ref, b_ref, o_ref):
    cc = cc_ref[...]
    a = cc * jax.nn.sigmoid(cc)
    o_ref[...] = jnp.dot(a, w_ref[...], preferred_element_type=F32, precision=HIGHEST) + b_ref[...]


def _modulation(cc, w_mod, b_mod):
    depth = w_mod.shape[0]
    nj = 6
    return pl.pallas_call(
        _mod_kernel,
        grid=(depth, nj),
        in_specs=[
            pl.BlockSpec((8, D_MODEL), lambda l, j: (0, 0)),
            pl.BlockSpec((None, D_MODEL, D_MODEL), lambda l, j: (l, 0, j)),
            pl.BlockSpec((None, 1, D_MODEL), lambda l, j: (l, 0, j)),
        ],
        out_specs=pl.BlockSpec((None, 8, D_MODEL), lambda l, j: (l, 0, j)),
        out_shape=jax.ShapeDtypeStruct((depth, 8, 6 * D_MODEL), F32),
        compiler_params=_cparams(("arbitrary", "arbitrary")),
        name="adaln_mod",
    )(cc, w_mod, b_mod.reshape(depth, 1, 6 * D_MODEL))


def _rms(x, g):
    return x * lax.rsqrt(jnp.mean(x * x, axis=-1, keepdims=True) + EPS) * g


def _rope(v, tab_ref, half):
    n = v.shape[-1]
    cos = tab_ref[:, 0:128]
    s_lo = tab_ref[:, 128:256]
    s_hi = tab_ref[:, 256:384]
    return v * cos + pltpu.roll(v, n - half, 1) * s_lo + pltpu.roll(v, half, 1) * s_hi


def _store_v_slabs(ref, vt, heads):
    ones = jnp.ones((V_SLAB - HEAD_DIM, vt.shape[1]), BF16)
    for hd in range(heads):
        ref[V_SLAB * hd:V_SLAB * hd + HEAD_DIM, :] = vt[HEAD_DIM * hd:HEAD_DIM * (hd + 1), :].astype(BF16)
        ref[V_SLAB * hd + HEAD_DIM:V_SLAB * (hd + 1), :] = ones


def _proj_kernel(x_ref, mod_ref, g1_ref, w_ref, wuq_ref, wukv_ref, gq_ref, gkv_ref, lbp_ref, taba_ref, tabc_ref,
                 qa_ref, ka_ref, va_ref, bqig_ref, gates_ref, qc_ref, kc_ref, vc_ref, gl_ref):
    x = x_ref[...]
    h = _rms(x, g1_ref[...]) * (1.0 + mod_ref[1:2, :]) + mod_ref[0:1, :]
    hb = h.astype(BF16)

    def seg(off, width):
        return jnp.dot(hb, w_ref[:, off:off + width], preferred_element_type=F32)

    aq = seg(OFF_AQ, 512) * (HEAD_DIM ** -0.5 * LOG2E)
    for j in range(4):
        qa_ref[128 * j:128 * (j + 1), :] = _rope(aq[:, 128 * j:128 * (j + 1)], taba_ref, 32).T.astype(BF16)
    ka_ref[...] = _rope(seg(OFF_AK, 128), taba_ref, 32).astype(BF16)
    _store_v_slabs(va_ref, seg(OFF_AV, 128).T, A_KV_HEADS)

    bqig_ref[:, 0:512] = seg(OFF_BQ, 512).astype(BF16)
    bqig_ref[:, 512:1024] = seg(OFF_BI, 512).astype(BF16)
    bqig_ref[:, 1024:1536] = seg(OFF_BG, 512).astype(BF16)
    for d, off in enumerate((OFF_BZF, OFF_BZB)):
        z = seg(off, 512)
        log_lb = lbp_ref[0:1, 512 * d:512 * (d + 1)]
        log1m_lb = lbp_ref[1:2, 512 * d:512 * (d + 1)]
        one_m_lb = lbp_ref[2:3, 512 * d:512 * (d + 1)]
        e = jnp.exp(-jnp.abs(z))
        log_sig = jnp.minimum(z, 0.0) - jnp.log(1.0 + e)
        b = log1m_lb + log_sig
        mx = jnp.maximum(log_lb, b)
        logf = mx + jnp.log(1.0 + jnp.exp(-jnp.abs(log_lb - b)))
        r = 1.0 / (1.0 + e)
        key = one_m_lb * jnp.where(z >= 0.0, e * r, r)
        gates_ref[:, 512 * d:512 * (d + 1)] = logf
        gates_ref[:, 1024 + 512 * d:1024 + 512 * (d + 1)] = key

    cq = _rms(seg(OFF_CQ, C_Q_LORA), gq_ref[...]).astype(BF16)
    qh = jnp.dot(cq, wuq_ref[...], preferred_element_type=F32) * ((C_NOPE + C_ROPE) ** -0.5 * LOG2E)
    ckv = _rms(seg(OFF_CKV, C_KV_LORA), gkv_ref[...]).astype(BF16)
    kvh = jnp.dot(ckv, wukv_ref[...], preferred_element_type=F32)
    kr = _rope(seg(OFF_CKR, 128), tabc_ref, 16)
    for j in range(C_HEADS):
        sl = slice(C_HEAD_PAD * j, C_HEAD_PAD * (j + 1))
        qc_ref[sl, :] = _rope(qh[:, sl], tabc_ref, 16).T.astype(BF16)
        kc_ref[:, sl] = (kvh[:, sl] + kr).astype(BF16)
    _store_v_slabs(vc_ref, kvh[:, C_HEADS * C_HEAD_PAD:].T, C_HEADS)

    for j in range(6):
        gl_ref[:, 512 * j:512 * (j + 1)] = seg(OFF_GL + 512 * j, 512).astype(BF16)


def _projection(x, mod, g1, w_in_p, wuq_p, wukv_p, gq, gkv, lbp, taba, tabc, dims):
    B, S, L = dims
    n = x.shape[0]
    tm = PROJ_BLOCK
    nlat = B * S // tm
    spb = S // tm

    def row(i):
        return (i, 0)

    def mod_row(i):
        return (jnp.minimum(i // spb, B), 0, 0)

    def tab_row(i):
        return (jnp.where(i < nlat, i % spb, spb), 0)

    const = lambda i: (0, 0)
    resident = functools.partial(pl.BlockSpec, index_map=const, pipeline_mode=pl.Buffered(1))
    widths = (512, 128, A_KV_HEADS * V_SLAB, 1536, 2048, 1024, 1024, C_HEADS * V_SLAB, 3072)
    dtypes = (BF16, BF16, BF16, BF16, F32, BF16, BF16, BF16, BF16)
    transposed = (0, 2, 5, 7)
    return pl.pallas_call(
        _proj_kernel,
        grid=(n // tm,),
        in_specs=[
            pl.BlockSpec((tm, D_MODEL), row),
            pl.BlockSpec((None, 6, D_MODEL), mod_row),
            pl.BlockSpec((1, D_MODEL), const),
            resident((D_MODEL, IN_W_PACKED)),
            resident(wuq_p.shape),
            resident(wukv_p.shape),
            pl.BlockSpec((1, C_Q_LORA), const),
            pl.BlockSpec((1, C_KV_LORA), const),
            pl.BlockSpec((8, 2 * B_W), const),
            pl.BlockSpec((tm, 384), tab_row),
            pl.BlockSpec((tm, 384), tab_row),
        ],
        out_specs=[pl.BlockSpec((w, tm), lambda i: (0, i)) if k in transposed else pl.BlockSpec((tm, w), row)
                   for k, w in enumerate(widths)],
        out_shape=[jax.ShapeDtypeStruct((w, n) if k in transposed else (n, w), dt)
                   for k, (w, dt) in enumerate(zip(widths, dtypes))],
        compiler_params=_cparams(("parallel",), PROJ_VMEM_LIMIT),
        name="in_proj",
    )(x, mod, g1, w_in_p, wuq_p, wukv_p, gq, gkv, lbp, taba, tabc)


def _gqa_ctx_kernel(qt_ref, kx_ref, vx_ref, sink_ref, o_ref):
    for g in range(A_KV_HEADS):
        qg = jnp.concatenate([qt_ref[HEAD_DIM * hd:HEAD_DIM * (hd + 1), :]
                              for hd in range(A_GROUP * g, A_GROUP * (g + 1))], axis=1)
        s = jnp.dot(kx_ref[:, HEAD_DIM * g:HEAD_DIM * (g + 1)], qg, preferred_element_type=F32)
        m = jnp.maximum(jnp.max(s, axis=0, keepdims=True), sink_ref[g])
        p = jnp.exp2(s - m).astype(BF16)
        pv = jnp.dot(vx_ref[V_SLAB * g:V_SLAB * (g + 1), :], p, preferred_element_type=F32)
        o = pv[0:HEAD_DIM, :] / (pv[HEAD_DIM:HEAD_DIM + 1, :] + jnp.exp2(sink_ref[g] - m))
        for hh in range(A_GROUP):
            hd = A_GROUP * g + hh
            o_ref[HEAD_DIM * hd:HEAD_DIM * (hd + 1), :] = o[:, A_QBLOCK * hh:A_QBLOCK * (hh + 1)].astype(BF16)


def _gqa_lat_kernel(qt_ref, kp_ref, kc_ref, kn_ref, kx_ref, vp_ref, vc_ref, vn_ref, vx_ref, sink_ref, o_ref,
                    bias_ref, s0_ref, s1_ref, p0_ref, p1_ref, *, seq):
    j = pl.program_id(1)
    nband = 4 * A_BLOCK
    nsb, nk = bias_ref.shape[0], bias_ref.shape[1]
    step_q = nsb * A_QBLOCK
    rows = lax.broadcasted_iota(jnp.int32, (nk, A_QBLOCK), 0)
    cols = lax.broadcasted_iota(jnp.int32, (nk, A_QBLOCK), 1)
    for qb in range(nsb):
        qpos = j * step_q + qb * A_QBLOCK + cols
        kpos = j * step_q + qb * A_QBLOCK - A_BLOCK + rows
        valid = ((kpos >= 0) & (kpos < seq) & (jnp.abs(qpos - kpos) <= A_WINDOW)) | (rows >= nband)
        bias_ref[qb] = jnp.where(valid, 0.0, -jnp.inf)
    k = jnp.concatenate([kp_ref[...], kc_ref[...], kn_ref[...]], axis=0)
    vt = jnp.concatenate([vp_ref[...], vc_ref[...], vn_ref[...]], axis=1)
    s_refs, p_refs = (s0_ref, s1_ref), (p0_ref, p1_ref)
    nchunk = nk // A_QBLOCK
    krows = lambda r: slice(A_QBLOCK * r, A_QBLOCK * (r + 1))
    items = [(qb, g) for qb in range(nsb) for g in range(A_KV_HEADS)]
    n = len(items)
    smax, m, pv = [None] * n, [None] * n, [None] * n

    def keys(i, r):
        qb, g = items[i]
        if A_QBLOCK * (r + 1) <= nband:
            sl = slice(A_QBLOCK * (qb + r), A_QBLOCK * (qb + r + 1))
            return k[sl, HEAD_DIM * g:HEAD_DIM * (g + 1)], vt[V_SLAB * g:V_SLAB * (g + 1), sl]
        return kx_ref[:, HEAD_DIM * g:HEAD_DIM * (g + 1)], vx_ref[V_SLAB * g:V_SLAB * (g + 1), :]

    def stage_scores(i, r):
        qb, g = items[i]
        qcols = slice(A_QBLOCK * qb, A_QBLOCK * (qb + 1))
        qg = jnp.concatenate([qt_ref[HEAD_DIM * hd:HEAD_DIM * (hd + 1), qcols]
                              for hd in range(A_GROUP * g, A_GROUP * (g + 1))], axis=1)
        bias = bias_ref[qb, krows(r), :]
        sc = (jnp.dot(keys(i, r)[0], qg, preferred_element_type=F32) + jnp.concatenate([bias] * A_GROUP, axis=1))
        s_refs[i % 2][krows(r), :] = sc
        cm = jnp.max(sc, axis=0, keepdims=True)
        smax[i] = cm if smax[i] is None else jnp.maximum(smax[i], cm)

    def stage_exp(i, r):
        p_refs[i % 2][krows(r), :] = jnp.exp2(s_refs[i % 2][krows(r), :] - m[i]).astype(BF16)

    def stage_pv(i, r):
        qb, g = items[i]
        t = jnp.dot(keys(i, r)[1], p_refs[i % 2][krows(r), :], preferred_element_type=F32)
        pv[i] = t if pv[i] is None else pv[i] + t
        if r == nchunk - 1:
            denom = pv[i][HEAD_DIM:HEAD_DIM + 1, :] + jnp.exp2(sink_ref[g] - m[i])
            o = pv[i][0:HEAD_DIM, :] / denom
            for hh in range(A_GROUP):
                hd = A_GROUP * g + hh
                o_ref[HEAD_DIM * hd:HEAD_DIM * (hd + 1), A_QBLOCK * qb:A_QBLOCK * (qb + 1)] = (
                    o[:, A_QBLOCK * hh:A_QBLOCK * (hh + 1)].astype(BF16))

    for r in range(nchunk):
        stage_scores(0, r)
    for i in range(n + 1):
        if i < n:
            m[i] = jnp.maximum(smax[i], sink_ref[items[i][1]])
        for r in range(nchunk):
            if i + 1 < n:
                stage_scores(i + 1, r)
            if i < n:
                stage_exp(i, r)
            if i >= 1:
                stage_pv(i - 1, r)


def _window_gqa(qat, ka, vat, sink, dims, need_ctx):
    B, S, L = dims
    n = ka.shape[0]
    assert L == A_QBLOCK
    nb = S // A_BLOCK
    nqb = S // A_QBLOCK
    nk = 4 * A_BLOCK + L
    width = A_GROUP * A_QBLOCK
    vs = A_KV_HEADS * V_SLAB
    step_q = min(A_STEP_BLOCKS * A_QBLOCK, S)
    nstep = S // step_q
    bps = step_q // A_BLOCK
    ctx_row = lambda b, j: (B * S // L + b, 0)
    ctx_col = lambda b, j: (0, B * S // L + b)
    sink_spec = pl.BlockSpec((A_KV_HEADS, 1, width), lambda b, j: (0, 0, 0))
    stage = [pltpu.VMEM((nk, width), F32), pltpu.VMEM((nk, width), F32),
             pltpu.VMEM((nk, width), BF16), pltpu.VMEM((nk, width), BF16)]
    prev_blk = lambda b, j: b * nb + jnp.maximum(bps * j - 1, 0)
    next_blk = lambda b, j: b * nb + jnp.minimum(bps * (j + 1), nb - 1)
    o_lat = pl.pallas_call(
        functools.partial(_gqa_lat_kernel, seq=S),
        grid=(B, nstep),
        in_specs=[
            pl.BlockSpec((512, step_q), lambda b, j: (0, b * nstep + j)),
            pl.BlockSpec((A_BLOCK, 128), lambda b, j: (prev_blk(b, j), 0)),
            pl.BlockSpec((step_q, 128), lambda b, j: (b * nstep + j, 0)),
            pl.BlockSpec((A_BLOCK, 128), lambda b, j: (next_blk(b, j), 0)),
            pl.BlockSpec((L, 128), ctx_row),
            pl.BlockSpec((vs, A_BLOCK), lambda b, j: (0, prev_blk(b, j))),
            pl.BlockSpec((vs, step_q), lambda b, j: (0, b * nstep + j)),
            pl.BlockSpec((vs, A_BLOCK), lambda b, j: (0, next_blk(b, j))),
            pl.BlockSpec((vs, L), ctx_col),
            sink_spec,
        ],
        out_specs=pl.BlockSpec((512, step_q), lambda b, j: (0, b * nstep + j)),
        out_shape=jax.ShapeDtypeStruct((512, B * S), BF16),
        scratch_shapes=[pltpu.VMEM((step_q // A_QBLOCK, nk, A_QBLOCK), F32)] + stage,
        compiler_params=_cparams(("parallel", "parallel")),
        name="window_gqa",
    )(qat, ka, ka, ka, ka, vat, vat, vat, vat, sink)
    if not need_ctx:
        return o_lat, None
    o_ctx = pl.pallas_call(
        _gqa_ctx_kernel,
        grid=(B,),
        in_specs=[pl.BlockSpec((512, L), lambda b: (0, B * S // L + b)),
                  pl.BlockSpec((L, 128), lambda b: (B * S // L + b, 0)),
                  pl.BlockSpec((vs, L), lambda b: (0, B * S // L + b)),
                  pl.BlockSpec((A_KV_HEADS, 1, width), lambda b: (0, 0, 0))],
        out_specs=pl.BlockSpec((512, L), lambda b: (0, b)),
        out_shape=jax.ShapeDtypeStruct((512, B * L), BF16),
        compiler_params=_cparams(("parallel",)),
        name="window_gqa_context",
    )(qat, ka, vat, sink)
    return o_lat, o_ctx


def _hgrn_scan(g_ref, r0, reverse):
    C = B_CHUNK
    rows = lax.broadcasted_iota(jnp.int32, (C, C), 0)
    cols = lax.broadcasted_iota(jnp.int32, (C, C), 1)
    causal = (rows <= cols) if reverse else (rows >= cols)
    g = g_ref[r0:r0 + C, :]
    tri = jnp.where(causal, 1.0, 0.0).astype(BF16)
    hi = g.astype(BF16)
    r1 = g - hi.astype(F32)
    mid = r1.astype(BF16)
    lo = (r1 - mid.astype(F32)).astype(BF16)
    bc = (jnp.dot(tri, hi, preferred_element_type=F32) + jnp.dot(tri, mid, preferred_element_type=F32)
          + jnp.dot(tri, lo, preferred_element_type=F32))
    tot = jnp.sum(g, axis=0, keepdims=True)
    mid = C // 2 if reverse else C // 2 - 1
    rho = bc[mid:mid + 1, :]
    return causal, bc, tot, rho


def _hgrn_prep(q_ref, v_ref, k_ref, r0, scan, factored):
    causal, bc, tot, rho = scan
    rs = slice(r0, r0 + B_CHUNK)
    q = q_ref[rs, :].astype(F32)
    key = k_ref[rs, :]
    v = v_ref[rs, :]
    qe = (q * jnp.exp(bc - rho)).astype(BF16) if factored else None
    ke = (key * jnp.exp(rho - bc)).astype(BF16) if factored else None
    qs = (q * jnp.exp(bc)).astype(BF16)
    ks = (key * jnp.exp(tot - bc)).astype(BF16)
    dec = jnp.exp(tot)
    return rs, causal, v, qe, ke, qs, ks, dec


def _hgrn_intra_exact(q_ref, v_ref, k_ref, r0, scan, reverse, ones_ref, bcs_ref, vs_ref):
    C = B_CHUNK
    causal, bc, tot, rho = scan
    q = q_ref[r0:r0 + C, :].astype(F32)
    bcs_ref[...] = bc
    vs_ref[...] = v_ref[r0:r0 + C, :].astype(F32)
    trow = lax.broadcasted_iota(jnp.int32, (C, 1), 0)

    def body(s, acc):
        later = (trow <= s) if reverse else (trow >= s)
        w = jnp.where(later, jnp.exp(jnp.minimum(bc - bcs_ref[pl.ds(s, 1), :], 0.0)), 0.0)
        att = _group_sum(q * w * k_ref[pl.ds(r0 + s, 1), :], ones_ref)
        return acc + att * vs_ref[pl.ds(s, 1), :]

    return lax.fori_loop(0, C, body, jnp.zeros((C, B_W), F32))


def _hgrn_heads(prep, o_ref, st_ref, intra=None):
    rs, causal, v, qe, ke, qs, ks, dec = prep
    for hd in range(B_HEADS):
        sl = slice(B_DK * hd, B_DK * (hd + 1))
        st = st_ref[hd]
        o = lax.dot_general(qs[:, sl], st.astype(BF16), (((1,), (1,)), ((), ())), preferred_element_type=F32)
        if intra is None:
            att = lax.dot_general(qe[:, sl], ke[:, sl], (((1,), (1,)), ((), ())), preferred_element_type=F32)
            att = jnp.where(causal, att, 0.0).astype(BF16)
            o = o + jnp.dot(att, v[:, sl], preferred_element_type=F32)
        else:
            o = o + intra[:, sl]
        o_ref[rs, sl] = o
        upd = lax.dot_general(v[:, sl], ks[:, sl], (((0,), (0,)), ((), ())), preferred_element_type=F32)
        st_ref[hd] = st * dec[:, sl] + upd


def _hgrn_kernel(qf_ref, vf_ref, gf_ref, kf_ref, qb_ref, vb_ref, gb_ref, kb_ref, ones_ref, of_ref, ob_ref,
                 stf_ref, stb_ref, bcs_ref, vs_ref):
    @pl.when(pl.program_id(1) == 0)
    def _():
        stf_ref[...] = jnp.zeros_like(stf_ref)
        stb_ref[...] = jnp.zeros_like(stb_ref)

    nchunk = qf_ref.shape[0] // B_CHUNK
    chunks = []
    for i in range(nchunk):
        chunks.append((qf_ref, vf_ref, gf_ref, kf_ref, of_ref, stf_ref, B_CHUNK * i, False))
        chunks.append((qb_ref, vb_ref, gb_ref, kb_ref, ob_ref, stb_ref, B_CHUNK * (nchunk - 1 - i), True))
    half = B_CHUNK // 2
    spread = None
    for (_, _, g_ref, _, _, _, r0, _) in chunks:
        for h0 in (r0, r0 + half):
            tot = jnp.sum(jnp.abs(g_ref[h0:h0 + half, :]), axis=0, keepdims=True)
            spread = tot if spread is None else jnp.maximum(spread, tot)
    factorable = jnp.max(spread) <= B_EXP_LIMIT

    @pl.when(factorable)
    def _():
        preps = [_hgrn_prep(q_ref, v_ref, k_ref, r0, _hgrn_scan(g_ref, r0, rev), True)
                 for (q_ref, v_ref, g_ref, k_ref, _, _, r0, rev) in chunks]
        for prep, (_, _, _, _, o_ref, st_ref, _, _) in zip(preps, chunks):
            _hgrn_heads(prep, o_ref, st_ref)

    @pl.when(jnp.logical_not(factorable))
    def _():
        for (q_ref, v_ref, g_ref, k_ref, o_ref, st_ref, r0, rev) in chunks:
            scan = _hgrn_scan(g_ref, r0, rev)
            intra = _hgrn_intra_exact(q_ref, v_ref, k_ref, r0, scan, rev, ones_ref, bcs_ref, vs_ref)
            _hgrn_heads(_hgrn_prep(q_ref, v_ref, k_ref, r0, scan, False), o_ref, st_ref, intra)


def _hgrn2_scan(bqig, gates, ones, dims):
    B, S, L = dims
    n = bqig.shape[0]
    T = B_BLOCK
    assert L == T
    ns = S // T

    def fwd_blk(b, c):
        return jnp.where(c == 0, B * ns + b, b * ns + c - 1)

    def bwd_blk(b, c):
        return jnp.where(c == 0, B * ns + b, b * ns + ns - c)

    def specs(blk, d):
        return [pl.BlockSpec((T, B_W), lambda b, c: (blk(b, c), 0)),
                pl.BlockSpec((T, B_W), lambda b, c: (blk(b, c), 1)),
                pl.BlockSpec((T, B_W), lambda b, c: (blk(b, c), d)),
                pl.BlockSpec((T, B_W), lambda b, c: (blk(b, c), 2 + d))]

    return pl.pallas_call(
        _hgrn_kernel,
        grid=(B, ns + 1),
        in_specs=specs(fwd_blk, 0) + specs(bwd_blk, 1) + [pl.BlockSpec((B_W, B_W), lambda b, c: (0, 0))],
        out_specs=[pl.BlockSpec((T, B_W), lambda b, c: (fwd_blk(b, c), 0)),
                   pl.BlockSpec((T, B_W), lambda b, c: (bwd_blk(b, c), 0))],
        out_shape=[jax.ShapeDtypeStruct((n, B_W), F32), jax.ShapeDtypeStruct((n, B_W), F32)],
        scratch_shapes=[pltpu.VMEM((B_HEADS, B_DK, B_DK), F32), pltpu.VMEM((B_HEADS, B_DK, B_DK), F32),
                        pltpu.VMEM((B_CHUNK, B_W), F32), pltpu.VMEM((B_CHUNK, B_W), F32)],
        compiler_params=_cparams(("parallel", "arbitrary")),
        name="hgrn2_scan",
    )(bqig, bqig, gates, gates, bqig, bqig, gates, gates, ones)


def _mla_kernel(qt_ref, k_ref, vt_ref, *rest, with_ctx):
    if with_ctx:
        kx_ref, vxt_ref, o_ref, m_ref, acc_ref, *bufs = rest
    else:
        o_ref, m_ref, acc_ref, *bufs = rest
    s_refs, p_refs = bufs[0:2], bufs[2:4]
    kstep = pl.program_id(2)
    tq = qt_ref.shape[1]

    def kv_pass(k_ref, vt_ref, first):
        nkeys = k_ref.shape[0]
        nchunk = nkeys // MLA_KEY_CHUNK
        krows = lambda r: slice(MLA_KEY_CHUNK * r, MLA_KEY_CHUNK * (r + 1))

        group = s_refs[0].shape[1]
        items = [(c, hd) for c in range(tq // group) for hd in range(C_HEADS)]
        n = len(items)
        smax = [None] * n
        m_new = [None] * n
        alpha = [None] * n
        pv = [None] * n
        cols = lambda i: slice(group * items[i][0], group * (items[i][0] + 1))

        def stage_scores(i, r):
            hd = items[i][1]
            sl = slice(C_HEAD_PAD * hd, C_HEAD_PAD * (hd + 1))
            sc = jnp.dot(k_ref[krows(r), sl], qt_ref[sl, cols(i)], preferred_element_type=F32)
            s_refs[i % 2][krows(r), :] = sc
            cm = jnp.max(sc, axis=0, keepdims=True)
            smax[i] = cm if smax[i] is None else jnp.maximum(smax[i], cm)

        def stage_stats(i):
            hd = items[i][1]
            if first:
                m_new[i] = smax[i]
            else:
                m_old = m_ref[hd, :, cols(i)]
                m_new[i] = jnp.maximum(m_old, smax[i])
                alpha[i] = jnp.exp2(m_old - m_new[i])
            m_ref[hd, :, cols(i)] = m_new[i]

        def stage_exp(i, r):
            p_refs[i % 2][krows(r), :] = jnp.exp2(s_refs[i % 2][krows(r), :] - m_new[i]).astype(BF16)

        def stage_pv(i, r):
            hd = items[i][1]
            t = jnp.dot(vt_ref[V_SLAB * hd:V_SLAB * (hd + 1), krows(r)], p_refs[i % 2][krows(r), :],
                        preferred_element_type=F32)
            pv[i] = t if pv[i] is None else pv[i] + t
            if r == nchunk - 1:
                acc_ref[hd, :, cols(i)] = pv[i] if first else alpha[i] * acc_ref[hd, :, cols(i)] + pv[i]

        for r in range(nchunk):
            stage_scores(0, r)
        for i in range(n + 1):
            if i < n:
                stage_stats(i)
            for r in range(nchunk):
                if i + 1 < n:
                    stage_scores(i + 1, r)
                if i < n:
                    stage_exp(i, r)
                if i >= 1:
                    stage_pv(i - 1, r)

    def kv_pass_lagged(k_ref, vt_ref, tmp_ref):
        nkeys = k_ref.shape[0]
        nchunk = nkeys // MLA_KEY_CHUNK
        krows = lambda r: slice(MLA_KEY_CHUNK * r, MLA_KEY_CHUNK * (r + 1))
        group = p_refs[0].shape[1]
        items = [(c, hd) for c in range(tq // group) for hd in range(C_HEADS)]
        n = len(items)
        cols = lambda i: slice(group * items[i][0], group * (items[i][0] + 1))
        base, smax, pv = [None] * n, [None] * n, [None] * n

        def stage_scores(i, r):
            hd = items[i][1]
            sl = slice(C_HEAD_PAD * hd, C_HEAD_PAD * (hd + 1))
            if base[i] is None:
                base[i] = m_ref[hd, :, cols(i)]
            sc = jnp.dot(k_ref[krows(r), sl], qt_ref[sl, cols(i)], preferred_element_type=F32)
            p_refs[i % 2][krows(r), :] = jnp.exp2(sc - base[i]).astype(BF16)
            cm = jnp.max(sc, axis=0, keepdims=True)
            smax[i] = cm if smax[i] is None else jnp.maximum(smax[i], cm)

        def stage_pv(i, r):
            hd = items[i][1]
            t = jnp.dot(vt_ref[V_SLAB * hd:V_SLAB * (hd + 1), krows(r)], p_refs[i % 2][krows(r), :],
                        preferred_element_type=F32)
            pv[i] = t if pv[i] is None else pv[i] + t
            if r == nchunk - 1:
                tmp_ref[hd, :, cols(i)] = pv[i]

        for r in range(nchunk):
            stage_scores(0, r)
        for i in range(n):
            for r in range(nchunk):
                if i + 1 < n:
                    stage_scores(i + 1, r)
                stage_pv(i, r)
        excess = smax[0] - base[0]
        for i in range(1, n):
            excess = jnp.maximum(excess, smax[i] - base[i])
        return jnp.max(excess)

    if with_ctx:
        @pl.when(kstep == 0)
        def _():
            kv_pass(kx_ref, vxt_ref, True)

        tmp_ref = bufs[4]
        safe = kv_pass_lagged(k_ref, vt_ref, tmp_ref) <= MLA_LAG_LIMIT

        @pl.when(safe)
        def _():
            for hd in range(C_HEADS):
                acc_ref[hd] = acc_ref[hd] + tmp_ref[hd]

        @pl.when(jnp.logical_not(safe))
        def _():
            kv_pass(k_ref, vt_ref, False)
    else:
        kv_pass(k_ref, vt_ref, True)

    @pl.when(kstep == pl.num_programs(2) - 1)
    def _():
        for hd in range(C_HEADS):
            o_ref[C_V * hd:C_V * (hd + 1), :] = (acc_ref[hd, 0:C_V, :] / acc_ref[hd, C_V:C_V + 1, :]).astype(BF16)


def _mla_attention(qct, kc, vct, dims, need_ctx):
    B, S, L = dims
    n = kc.shape[0]
    tq = min(MLA_Q_BLOCK, S)
    tk = min(MLA_K_BLOCK, S)
    nq, nk = S // tq, S // tk
    hw = C_HEADS * C_HEAD_PAD
    vw = C_HEADS * C_V
    vs = C_HEADS * V_SLAB
    scratch = lambda t, nkeys: [
        pltpu.VMEM((C_HEADS, 1, t), F32), pltpu.VMEM((C_HEADS, V_SLAB, t), F32),
        pltpu.VMEM((nkeys, min(t, MLA_COL_GROUP)), F32), pltpu.VMEM((nkeys, min(t, MLA_COL_GROUP)), F32),
        pltpu.VMEM((nkeys, min(t, MLA_COL_GROUP)), BF16), pltpu.VMEM((nkeys, min(t, MLA_COL_GROUP)), BF16)]
    ctx_row = lambda b, i, k: (B * S // L + b, 0)
    ctx_col = lambda b, i, k: (0, B * S // L + b)
    o_lat = pl.pallas_call(
        functools.partial(_mla_kernel, with_ctx=True),
        grid=(B, nq, nk),
        in_specs=[
            pl.BlockSpec((hw, tq), lambda b, i, k: (0, b * nq + i)),
            pl.BlockSpec((tk, hw), lambda b, i, k: (b * nk + k, 0)),
            pl.BlockSpec((vs, tk), lambda b, i, k: (0, b * nk + k)),
            pl.BlockSpec((L, hw), ctx_row),
            pl.BlockSpec((vs, L), ctx_col),
        ],
        out_specs=pl.BlockSpec((vw, tq), lambda b, i, k: (0, b * nq + i)),
        out_shape=jax.ShapeDtypeStruct((vw, B * S), BF16),
        scratch_shapes=scratch(tq, tk) + [pltpu.VMEM((C_HEADS, V_SLAB, tq), F32)],
        compiler_params=_cparams(("parallel", "parallel", "arbitrary")),
        name="mla_latent",
    )(qct, kc, vct, kc, vct)
    if not need_ctx:
        return o_lat, None
    o_ctx = pl.pallas_call(
        functools.partial(_mla_kernel, with_ctx=False),
        grid=(B, 1, 1),
        in_specs=[
            pl.BlockSpec((hw, L), ctx_col),
            pl.BlockSpec((L, hw), ctx_row),
            pl.BlockSpec((vs, L), ctx_col),
        ],
        out_specs=pl.BlockSpec((vw, L), lambda b, i, k: (0, b)),
        out_shape=jax.ShapeDtypeStruct((vw, B * L), BF16),
        scratch_shapes=scratch(L, L),
        compiler_params=_cparams(("parallel", "arbitrary", "arbitrary")),
        name="mla_context",
    )(qct, kc, vct)
    return o_lat, o_ctx


def _group_sum(x, ones_ref):
    hi = x.astype(BF16)
    lo = (x - hi.astype(F32)).astype(BF16)
    return (jnp.dot(hi, ones_ref[...], preferred_element_type=F32)
            + jnp.dot(lo, ones_ref[...], preferred_element_type=F32))


def _merge_kernel(x_ref, oa_ref, oax_ref, of_ref, ob_ref, bg_ref, oc_ref, ocx_ref, gl_ref, wbr_ref, wout_ref, gn_ref,
                  mod_ref, g2_ref, ones_ref, xo_ref, h2_ref, *, nlat):
    is_lat = pl.program_id(0) < nlat
    oa = jnp.where(is_lat, oa_ref[...], oax_ref[...])
    oc = jnp.where(is_lat, oc_ref[...], ocx_ref[...])
    ob = of_ref[...] + ob_ref[...]
    ms = _group_sum(ob * ob, ones_ref) * (1.0 / B_DK)
    obn = ob * lax.rsqrt(ms + EPS) * gn_ref[...]
    bg = bg_ref[...].astype(F32)
    bb = (obn * (bg * jax.nn.sigmoid(bg))).astype(BF16)
    branches = ((oa, 0), (bb, 1), (oc, 0))
    y = None
    for nbr, (br, axis) in enumerate(branches):
        gate = jax.nn.sigmoid(gl_ref[:, D_MODEL * nbr:D_MODEL * (nbr + 1)].astype(F32))
        t = gate * lax.dot_general(br, wbr_ref[nbr], (((axis,), (0,)), ((), ())), preferred_element_type=F32)
        y = t if y is None else y + t
    upd = jnp.dot(y.astype(BF16), wout_ref[...], preferred_element_type=F32)
    xn = x_ref[...] + mod_ref[2:3, :] * upd
    xo_ref[...] = xn
    h2 = _rms(xn, g2_ref[...]) * (1.0 + mod_ref[4:5, :]) + mod_ref[3:4, :]
    h2_ref[...] = h2.astype(BF16)


def _merge(x, oa, ohg, bqig, oc, gl, wbr, wout, gn, mod, g2, ones, dims, need_ctx):
    B, S, L = dims
    n = x.shape[0]
    tm = PROJ_BLOCK
    spb = S // tm
    nblk = (n if need_ctx else B * S) // tm
    nlat = B * S // tm
    row = lambda i: (i, 0)
    const2 = lambda i: (0, 0)
    (oa_lat, oa_ctx), (oc_lat, oc_ctx) = oa, oc
    lat_col = pl.BlockSpec((512, tm), lambda i: (0, jnp.minimum(i, nlat - 1)))
    ctx_col = pl.BlockSpec((512, tm), lambda i: (0, jnp.maximum(i - nlat, 0)))
    if not need_ctx:
        oa_ctx, oc_ctx, ctx_col = oa_lat, oc_lat, lat_col
    return pl.pallas_call(
        functools.partial(_merge_kernel, nlat=nlat),
        grid=(nblk,),
        in_specs=[
            pl.BlockSpec((tm, D_MODEL), row),
            lat_col,
            ctx_col,
            pl.BlockSpec((tm, B_W), row),
            pl.BlockSpec((tm, B_W), row),
            pl.BlockSpec((tm, B_W), lambda i: (i, 2)),
            lat_col,
            ctx_col,
            pl.BlockSpec((tm, N_BRANCH * D_MODEL), row),
            pl.BlockSpec((N_BRANCH, BRANCH_W, D_MODEL), lambda i: (0, 0, 0)),
            pl.BlockSpec((D_MODEL, D_MODEL), const2),
            pl.BlockSpec((1, B_W), const2),
            pl.BlockSpec((None, 6, D_MODEL), lambda i: (jnp.minimum(i // spb, B), 0, 0)),
            pl.BlockSpec((1, D_MODEL), const2),
            pl.BlockSpec((B_W, B_W), const2),
        ],
        out_specs=[pl.BlockSpec((tm, D_MODEL), row), pl.BlockSpec((tm, D_MODEL), row)],
        out_shape=[jax.ShapeDtypeStruct((nblk * tm, D_MODEL), F32), jax.ShapeDtypeStruct((nblk * tm, D_MODEL), BF16)],
        compiler_params=_cparams(("parallel",)),
        name="branch_merge",
    )(x, oa_lat, oa_ctx, ohg[0], ohg[1], bqig, oc_lat, oc_ctx, gl, wbr, wout, gn, mod, g2, ones)


def _route_kernel(h_ref, wr_ref, comb_ref, pos_ref, cnt_ref):
    h = h_ref[...]
    T = h.shape[0]
    lane = lax.broadcasted_iota(jnp.int32, (T, ROUTER_W), 1)
    logits = jnp.dot(h, wr_ref[...], preferred_element_type=F32)
    big = jnp.int32(ROUTER_W)
    is_grp = (lane >= N_EXPERTS) & (lane < N_EXPERTS + N_GROUPS)
    gl = jnp.where(is_grp, logits, -jnp.inf)
    gmax = jnp.max(gl, axis=-1, keepdims=True)
    gsel = jnp.min(jnp.where(gl == gmax, lane, big), axis=-1, keepdims=True) - N_EXPERTS
    gw = 1.0 / jnp.sum(jnp.exp(gl - gmax), axis=-1, keepdims=True)
    in_grp = (lane >= gsel * EXPERTS_PER_GROUP) & (lane < (gsel + 1) * EXPERTS_PER_GROUP)
    el = jnp.where(in_grp, logits, -jnp.inf)
    m1 = jnp.max(el, axis=-1, keepdims=True)
    i1 = jnp.min(jnp.where(el == m1, lane, big), axis=-1, keepdims=True)
    el2 = jnp.where(lane == i1, -jnp.inf, el)
    m2 = jnp.max(el2, axis=-1, keepdims=True)
    i2 = jnp.min(jnp.where(el2 == m2, lane, big), axis=-1, keepdims=True)
    e2 = jnp.exp(m2 - m1)
    w1 = gw / (1.0 + e2)
    w2 = gw * e2 / (1.0 + e2)
    comb_ref[...] = jnp.where(lane == i1, w1, 0.0) + jnp.where(lane == i2, w2, 0.0)

    onehot = lane == gsel
    ones = jnp.where(onehot, 1.0, 0.0)
    rows = lax.broadcasted_iota(jnp.int32, (T, T), 0)
    cols = lax.broadcasted_iota(jnp.int32, (T, T), 1)
    earlier = jnp.where(rows > cols, 1.0, 0.0).astype(BF16)
    before = jnp.dot(earlier, ones.astype(BF16), preferred_element_type=F32)
    rank = jnp.sum(jnp.where(onehot, before, 0.0), axis=-1, keepdims=True)
    cnt = jnp.sum(ones, axis=0, keepdims=True)
    padded = jnp.floor((cnt + (MOE_ALIGN - 1)) * (1.0 / MOE_ALIGN)) * MOE_ALIGN
    seg = [jnp.sum(jnp.where(lane[0:1] == g, padded, 0.0), axis=-1, keepdims=True) for g in range(N_GROUPS - 1)]
    start = jnp.where(gsel == 0, 0.0, jnp.where(gsel == 1, seg[0], jnp.where(gsel == 2, seg[0] + seg[1],
                                                                             seg[0] + seg[1] + seg[2])))
    pos_ref[...] = jnp.broadcast_to(start + rank, (T, ROUTER_W))
    cnt_ref[...] = jnp.broadcast_to(cnt, (8, ROUTER_W)).astype(jnp.int32)


def _moe_kernel(cnt_ref, h_ref, x_ref, mod_ref, pos_ref, comb_ref, w13_ref, w2_ref, gf_ref, o_ref,
                pt_ref, xs_ref, cs_ref, ys_ref, *, final_norm):
    i = pl.program_id(0)
    e = pl.program_id(1)
    T = h_ref.shape[0]
    R = xs_ref.shape[0]
    gather = lambda a: lax.dot_general(pt_ref[...], a, (((0,), (0,)), ((), ())), preferred_element_type=F32)

    @pl.when(e == 0)
    def _():
        slot = lax.broadcasted_iota(jnp.int32, (T, R), 1).astype(F32)
        pt_ref[...] = jnp.where(pos_ref[:, 0:1] == slot, 1.0, 0.0).astype(BF16)
        xs_ref[...] = gather(h_ref[...]).astype(BF16)
        comb = comb_ref[...]
        hi = comb.astype(BF16)
        r1 = comb - hi.astype(F32)
        mid = r1.astype(BF16)
        lo = (r1 - mid.astype(F32)).astype(BF16)
        cs_ref[...] = gather(hi) + gather(mid) + gather(lo)
        ys_ref[...] = jnp.zeros_like(ys_ref)

    g = (e * MOE_EXPERTS_PER_STEP) // EXPERTS_PER_GROUP
    cnt = [cnt_ref[i * N_GROUPS + gg] for gg in range(N_GROUPS)]
    seg = [(c + (MOE_ALIGN - 1)) // MOE_ALIGN * MOE_ALIGN for c in cnt]
    start = (jnp.where(g > 0, seg[0], 0) + jnp.where(g > 1, seg[1], 0) + jnp.where(g > 2, seg[2], 0))
    cnt_g = jnp.where(g == 0, cnt[0], jnp.where(g == 1, cnt[1], jnp.where(g == 2, cnt[2], cnt[3])))
    lane = lax.broadcasted_iota(jnp.int32, (MOE_TILE, ROUTER_W), 1)

    def tile(t, carry):
        r0 = pl.multiple_of(start + t * MOE_TILE, MOE_ALIGN)
        xt = xs_ref[pl.ds(r0, MOE_TILE), :]
        ct = cs_ref[pl.ds(r0, MOE_TILE), :]
        acts = []
        for k in range(MOE_EXPERTS_PER_STEP):
            ce = jnp.sum(jnp.where(lane == e * MOE_EXPERTS_PER_STEP + k, ct, 0.0), axis=-1, keepdims=True)
            h13 = jnp.dot(xt, w13_ref[k], preferred_element_type=F32)
            a1 = h13[:, :D_EXPERT]
            acts.append((a1 * jax.nn.sigmoid(a1) * h13[:, D_EXPERT:] * ce).astype(BF16))
        ys_ref[pl.ds(r0, MOE_TILE), :] += jnp.dot(jnp.concatenate(acts, axis=-1), w2_ref[...],
                                                   preferred_element_type=F32)
        return carry

    lax.fori_loop(0, (cnt_g + MOE_TILE - 1) // MOE_TILE, tile, 0)

    @pl.when(e == pl.num_programs(1) - 1)
    def _():
        y = jnp.dot(pt_ref[...], ys_ref[...].astype(BF16), preferred_element_type=F32)
        xn = x_ref[...] + mod_ref[5:6, :] * y
        o_ref[...] = _rms(xn, gf_ref[...]) if final_norm else xn


def _moe(h2, x, mod, wr, w13, w2, g_final, dims, need_ctx, final_norm):
    B, S, L = dims
    n = x.shape[0]
    tm = min(MOE_TOKEN_BLOCK, S, B * L)
    spb = S // tm
    nblk = (n if need_ctx else B * S) // tm
    comb, pos, cnt = pl.pallas_call(
        _route_kernel,
        grid=(nblk,),
        in_specs=[pl.BlockSpec((tm, D_MODEL), lambda i: (i, 0)), pl.BlockSpec((D_MODEL, ROUTER_W), lambda i: (0, 0))],
        out_specs=[pl.BlockSpec((tm, ROUTER_W), lambda i: (i, 0)), pl.BlockSpec((tm, ROUTER_W), lambda i: (i, 0)),
                   pl.BlockSpec((8, ROUTER_W), lambda i: (i, 0))],
        out_shape=[jax.ShapeDtypeStruct((nblk * tm, ROUTER_W), F32), jax.ShapeDtypeStruct((nblk * tm, ROUTER_W), F32),
                   jax.ShapeDtypeStruct((nblk * 8, ROUTER_W), jnp.int32)],
        compiler_params=_cparams(("parallel",)),
        name="moe_route",
    )(h2, wr)
    cnt = cnt.reshape(nblk, 8, ROUTER_W)[:, 0, :N_GROUPS].reshape(nblk * N_GROUPS)

    row = lambda i, e, c: (i, 0)
    eps = MOE_EXPERTS_PER_STEP
    assert MOE_SORT_PAD >= (N_GROUPS - 1) * (MOE_ALIGN - 1) + MOE_TILE - 1 and EXPERTS_PER_GROUP % eps == 0
    slots = tm + MOE_SORT_PAD
    return pl.pallas_call(
        functools.partial(_moe_kernel, final_norm=final_norm),
        grid_spec=pltpu.PrefetchScalarGridSpec(
            num_scalar_prefetch=1,
            grid=(nblk, N_EXPERTS // eps),
            in_specs=[
                pl.BlockSpec((tm, D_MODEL), row),
                pl.BlockSpec((tm, D_MODEL), row),
                pl.BlockSpec((None, 6, D_MODEL), lambda i, e, c: (jnp.minimum(i // spb, B), 0, 0)),
                pl.BlockSpec((tm, ROUTER_W), row),
                pl.BlockSpec((tm, ROUTER_W), row),
                pl.BlockSpec((eps, D_MODEL, 2 * D_EXPERT), lambda i, e, c: (e, 0, 0)),
                pl.BlockSpec((eps * D_EXPERT, D_MODEL), lambda i, e, c: (e, 0)),
                pl.BlockSpec((1, D_MODEL), lambda i, e, c: (0, 0)),
            ],
            out_specs=pl.BlockSpec((tm, D_MODEL), row),
            scratch_shapes=[pltpu.VMEM((tm, slots), BF16), pltpu.VMEM((slots, D_MODEL), BF16),
                            pltpu.VMEM((slots, ROUTER_W), F32), pltpu.VMEM((slots, D_MODEL), F32)],
        ),
        out_shape=jax.ShapeDtypeStruct((nblk * tm, D_MODEL), F32),
        compiler_params=_cparams(("parallel", "arbitrary")),
        name="hier_moe",
    )(cnt, h2, x, mod, pos, comb, w13, w2.reshape(N_EXPERTS * D_EXPERT, D_MODEL), g_final)


def _rope_tables(S, L):
    rows = S // GRID_W
    pos_r = np.repeat(np.arange(rows, dtype=np.float32), GRID_W)
    pos_c = np.tile(np.arange(GRID_W, dtype=np.float32), rows)

    def angles(rot_dim):
        nf = rot_dim // 4
        inv = jnp.asarray(ROPE_BASE, F32) ** (-jnp.arange(nf, dtype=F32) / nf)
        ang = jnp.concatenate([pos_r[:, None] * inv, pos_c[:, None] * inv], axis=-1)
        return jnp.cos(ang), jnp.sin(ang)

    def with_ctx(cos, s_lo, s_hi):
        ident = jnp.concatenate([jnp.ones((PROJ_BLOCK, 128), F32), jnp.zeros((PROJ_BLOCK, 256), F32)], axis=-1)
        return jnp.concatenate([jnp.concatenate([cos, s_lo, s_hi], axis=-1), ident], axis=0)

    cos, sin = angles(HEAD_DIM)
    z = jnp.zeros_like(sin)
    taba = with_ctx(jnp.tile(cos, (1, 4)), jnp.tile(jnp.concatenate([-sin, z], -1), (1, 2)),
                    jnp.tile(jnp.concatenate([z, sin], -1), (1, 2)))
    cos, sin = angles(C_ROPE)
    z = jnp.zeros_like(sin)
    one64, zero64, zero32 = jnp.ones((S, 64), F32), jnp.zeros((S, 64), F32), jnp.zeros((S, 32), F32)
    tabc = with_ctx(jnp.concatenate([one64, cos, cos, one64[:, :32]], -1),
                    jnp.concatenate([zero64, -sin, z, zero32], -1),
                    jnp.concatenate([zero64, z, sin, zero32], -1))
    return taba, tabc


def _pack_w_in(w):
    pad = lambda k: jnp.zeros((w.shape[0], k), w.dtype)
    return jnp.concatenate([w[:, :3712], pad(64), w[:, 3712:3744], pad(32), w[:, 3744:]], axis=-1).astype(BF16)


def _pack_w_uq(w):
    w = w.reshape(C_Q_LORA, C_HEADS, C_NOPE + C_ROPE)
    w = jnp.pad(w, ((0, 0), (0, 0), (0, C_HEAD_PAD - C_NOPE - C_ROPE)))
    return w.reshape(C_Q_LORA, C_HEADS * C_HEAD_PAD).astype(BF16)


def _pack_w_ukv(w):
    w = w.reshape(C_KV_LORA, C_HEADS, C_NOPE + C_V)
    wk = jnp.pad(w[:, :, :C_NOPE], ((0, 0), (0, 0), (0, C_HEAD_PAD - C_NOPE))).reshape(C_KV_LORA, -1)
    wv = w[:, :, C_NOPE:].reshape(C_KV_LORA, -1)
    return jnp.concatenate([wk, wv], axis=-1).astype(BF16)


def kernel(x, c, ctx, c_ctx, w_mod, b_mod, g_norm1, g_norm2, w_in, a_sink, b_lb_logits, b_onorm, c_qnorm, c_kvnorm,
           w_uq, w_ukv, w_br, w_out, w_rg, w_re, w1, w3, w2, g_final):
    B, S, _ = x.shape
    L = ctx.shape[1]
    depth = w_in.shape[0]
    assert L == TOKEN_BLOCK and S % min(MLA_Q_BLOCK, S) == 0 and S % min(MLA_K_BLOCK, S) == 0 and S % GRID_W == 0
    assert S % PROJ_BLOCK == 0 and (B * L) % PROJ_BLOCK == 0
    dims = (B, S, L)

    xs = jnp.concatenate([x.reshape(B * S, D_MODEL), ctx.reshape(B * L, D_MODEL)], axis=0)
    cc = jnp.zeros((8, D_MODEL), F32).at[:B].set(c).at[B].set(c_ctx)
    mod_all = _modulation(cc, w_mod, b_mod).reshape(depth, 8, 6, D_MODEL)

    lb_all = jnp.cumsum(jax.nn.softmax(b_lb_logits.astype(F32), axis=0), axis=0)
    lb_all = (lb_all - lb_all[0:1]).reshape(depth, 1, 2 * B_W)
    lbp_all = jnp.concatenate([jnp.log(lb_all), jnp.log1p(-lb_all), 1.0 - lb_all,
                               jnp.zeros((depth, 5, 2 * B_W), F32)], axis=1)

    taba, tabc = _rope_tables(S, L)
    ones = jnp.kron(jnp.eye(B_HEADS, dtype=F32), jnp.ones((B_DK, B_DK), F32)).astype(BF16)

    for l in range(depth):
        need_ctx = l < depth - 1
        mod = mod_all[l]
        wr = jnp.concatenate([w_re[l], w_rg[l], jnp.zeros((D_MODEL, ROUTER_W - N_EXPERTS - N_GROUPS), F32)],
                             axis=-1).astype(BF16)
        w13 = jnp.concatenate([w1[l], w3[l]], axis=-1).astype(BF16)
        sink = jnp.repeat(a_sink[l].astype(F32).reshape(A_KV_HEADS, 1, A_GROUP) * LOG2E, A_QBLOCK, axis=-1)

        qa, ka, va, bqig, gates, qc, kc, vc, gl = _projection(
            xs, mod, g_norm1[l][None], _pack_w_in(w_in[l]), _pack_w_uq(w_uq[l]), _pack_w_ukv(w_ukv[l]),
            c_qnorm[l][None], c_kvnorm[l][None], lbp_all[l], taba, tabc, dims)
        oa = _window_gqa(qa, ka, va, sink, dims, need_ctx)
        ohg = _hgrn2_scan(bqig, gates, ones, dims)
        oc = _mla_attention(qc, kc, vc, dims, need_ctx)
        xs, h2 = _merge(xs, oa, ohg, bqig, oc, gl, w_br[l].astype(BF16), w_out[l].astype(BF16), b_onorm[l][None],
                        mod, g_norm2[l][None], ones, dims, need_ctx)
        xs = _moe(h2, xs, mod, wr, w13, w2[l].astype(BF16), g_final[None], dims, need_ctx, final_norm=not need_ctx)

    return xs.reshape(B, S, D_MODEL)
```
